```python
import jax, jax.numpy as jnp
from jax import lax
import numpy as np

D_MODEL = 1024
BATCH = 8
SEQ = 4096
DEPTH = 1

HEAD_DIM = 64
ATT_GROUPS = ((128, 1), (512, 4), (2048, 16))
N_DIL = len(ATT_GROUPS)
HEADS_PER_GROUP = 4
N_ATT_HEADS = N_DIL * HEADS_PER_GROUP
ATT_W = N_ATT_HEADS * HEAD_DIM
ATT_OUT_W = HEADS_PER_GROUP * HEAD_DIM
BLK = 128
REL_BUCKETS = 32
REL_MAX_DIST = 2048
CHUNK = 128
GMLP_W = 768
GMLP_GROUPS = 12
GMLP_GD = GMLP_W // GMLP_GROUPS
N_BRANCH = 2
IN_W = 3 * ATT_W + 2 * GMLP_W + N_BRANCH * D_MODEL
N_EXPERTS = 32
TOP_K = 4
D_EXPERT = D_MODEL
SWIGLU_LIMIT = 7.0
SWIGLU_ALPHA = 1.702
MOE_BLK = 128
PLE_DIM = 256
EPS = 1e-6

kernel_name = "hybrid_dilated_gmlp_moe_block"


def _rmsnorm(x, g):
    xf = x.astype(jnp.float32)
    y = xf * lax.rsqrt(jnp.mean(xf * xf, axis=-1, keepdims=True) + EPS) * g.astype(jnp.float32)
    return y.astype(x.dtype)


def _t5_bucket(n):
    exact = REL_BUCKETS // 2
    nf = np.maximum(n, 1).astype(np.float32)
    large = exact + (np.log(nf / exact) / np.log(REL_MAX_DIST / exact) * (REL_BUCKETS - exact)).astype(np.int32)
    large = np.minimum(large, REL_BUCKETS - 1)
    return np.where(n < exact, n, large).astype(np.int32)


def _dilated_group(q, k, v, rel_bias_g, window, dilation):
    B, S, H, E = q.shape
    span = window // dilation
    unit = dilation * BLK
    Sp = -(-S // unit) * unit
    nb = Sp // unit

    def to_blocks(t):
        t = jnp.pad(t, ((0, 0), (0, Sp - S), (0, 0), (0, 0)))
        return t.reshape(B, nb, BLK, dilation, H, E)

    qb, kb, vb = to_blocks(q), to_blocks(k), to_blocks(v)

    def with_prev(t):
        prev = jnp.pad(t, ((0, 0), (1, 0), (0, 0), (0, 0), (0, 0), (0, 0)))[:, :-1]
        return jnp.concatenate([prev, t], axis=2)

    kk, vv = with_prev(kb), with_prev(vb)

    i = np.arange(BLK)[:, None]
    j = np.arange(2 * BLK)[None, :]
    dist = i + BLK - j
    band = (dist >= 0) & (dist <= span)
    bucket = _t5_bucket(np.clip(dist, 0, None) * dilation)
    bias = jnp.transpose(rel_bias_g[bucket], (2, 0, 1)).astype(jnp.float32)
    key_ok = (np.arange(nb)[:, None] * BLK + np.arange(2 * BLK)[None, :] - BLK) >= 0
    mask = jnp.asarray(band[None, :, :] & key_ok[:, None, :])

    logits = jnp.einsum('bnqrhe,bnkrhe->bnrhqk', qb, kk).astype(jnp.float32) * (E ** -0.5)
    logits = logits + bias[None, None, None]
    logits = jnp.where(mask[None, :, None, None], logits, -jnp.inf)
    m = jnp.max(logits, axis=-1, keepdims=True)
    w = jnp.exp(logits - m)
    den = jnp.sum(w, axis=-1)
    num = jnp.einsum('bnrhqk,bnkrhe->bnqrhe', w.astype(vv.dtype), vv).astype(jnp.float32)
    den_t = jnp.transpose(den, (0, 1, 4, 2, 3))
    lse = jnp.transpose(m[..., 0] + jnp.log(den), (0, 1, 4, 2, 3))
    out = (num / den_t[..., None]).reshape(B, Sp, H, E)[:, :S]
    lse = lse.reshape(B, Sp, H)[:, :S]
    return out, lse


def _dilated_attention(q, k, v, rel_bias):
    outs, lses = [], []
    for g, (window, dilation) in enumerate(ATT_GROUPS):
        o, l = _dilated_group(q[:, :, g], k[:, :, g], v[:, :, g],
                              rel_bias[:, g * HEADS_PER_GROUP:(g + 1) * HEADS_PER_GROUP], window, dilation)
        outs.append(o)
        lses.append(l)
    outs = jnp.stack(outs, axis=2)
    alpha = jax.nn.softmax(jnp.stack(lses, axis=2), axis=2)
    return jnp.sum(alpha[..., None] * outs, axis=2)


def _spatial_gating(u_raw, v_raw, ln_g, ln_b, w_s, b_s):
    B, S, _ = u_raw.shape
    zu = jax.nn.gelu(u_raw, approximate=False)
    zv = jax.nn.gelu(v_raw, approximate=False).astype(jnp.float32)
    mu = jnp.mean(zv, axis=-1, keepdims=True)
    var = jnp.mean(jnp.square(zv - mu), axis=-1, keepdims=True)
    vn = (zv - mu) * lax.rsqrt(var + EPS) * ln_g.astype(jnp.float32) + ln_b.astype(jnp.float32)
    vc = vn.reshape(B, S // CHUNK, CHUNK, GMLP_GROUPS, GMLP_GD)
    causal = jnp.asarray(np.tril(np.ones((CHUNK, CHUNK), np.float32)))
    w_c = w_s.astype(jnp.float32) * causal[None]
    mixed = jnp.einsum('gij,bcjgd->bcigd', w_c, vc) + b_s.astype(jnp.float32).T[None, None, :, :, None]
    return zu * mixed.reshape(B, S, GMLP_W).astype(zu.dtype)


def _moe(hn, w_router, b_router, w_gate_up, b_gate_up, w_down, b_down):
    B, S, D = hn.shape
    T = B * S
    xt = hn.reshape(T, D)
    logits = (xt @ w_router + b_router).astype(jnp.float32)
    top_val, top_idx = lax.top_k(logits, TOP_K)
    gate = jax.nn.softmax(top_val, axis=-1)
    A = T * TOP_K
    flat_e = top_idx.reshape(A).astype(jnp.int32)
    flat_tok = jnp.repeat(jnp.arange(T, dtype=jnp.int32), TOP_K)
    flat_gate = gate.reshape(A)
    order = jnp.argsort(flat_e)
    e_sorted = flat_e[order]
    counts = jnp.bincount(flat_e, length=N_EXPERTS).astype(jnp.int32)
    start = jnp.cumsum(counts) - counts
    blk_counts = (counts + MOE_BLK - 1) // MOE_BLK
    blk_end = jnp.cumsum(blk_counts)
    pad_start = (blk_end - blk_counts) * MOE_BLK
    rank = jnp.arange(A, dtype=jnp.int32) - start[e_sorted]
    dest = pad_start[e_sorted] + rank
    n_blocks = -(-A // MOE_BLK) + N_EXPERTS
    P = n_blocks * MOE_BLK
    slot_tok = jnp.full((P,), T, jnp.int32).at[dest].set(flat_tok[order])
    slot_gate = jnp.zeros((P,), jnp.float32).at[dest].set(flat_gate[order])
    block_expert = jnp.minimum(
        jnp.searchsorted(blk_end, jnp.arange(n_blocks, dtype=jnp.int32), side='right'), N_EXPERTS - 1)
    x_pad = jnp.concatenate([xt, jnp.zeros((1, D), xt.dtype)], axis=0)
    xb = x_pad[slot_tok].reshape(n_blocks, MOE_BLK, D)

    def expert_block(args):
        xe, e = args
        gu = xe @ w_gate_up[e] + b_gate_up[e]
        glu = jnp.minimum(gu[:, :D_EXPERT], SWIGLU_LIMIT)
        lin = jnp.clip(gu[:, D_EXPERT:], -SWIGLU_LIMIT, SWIGLU_LIMIT)
        act = glu * jax.nn.sigmoid(SWIGLU_ALPHA * glu) * (lin + 1.0)
        return act @ w_down[e] + b_down[e]

    yb = lax.map(expert_block, (xb, block_expert))
    y = yb.reshape(P, D) * slot_gate[:, None].astype(yb.dtype)
    out = jnp.zeros((T + 1, D), yb.dtype).at[slot_tok].add(y)[:T]
    return out.reshape(B, S, D)


def setup_inputs(seed: int = 0) -> dict:
    key = jax.random.key(seed)
    ks = jax.random.split(key, 24)
    f32 = jnp.float32
    nrm = lambda k, shape, s: jax.random.normal(k, shape, f32) * s
    L, D, E = DEPTH, D_MODEL, N_EXPERTS
    return {
        "x": nrm(ks[0], (BATCH, SEQ, D), 1.0),
        "p": nrm(ks[1], (DEPTH, BATCH, SEQ, PLE_DIM), 1.0),
        "g_mix": 1.0 + nrm(ks[2], (L, D), 0.05),
        "w_in": nrm(ks[3], (L, D, IN_W), D ** -0.5),
        "rel_bias": nrm(ks[4], (REL_BUCKETS, N_ATT_HEADS), 0.5),
        "w_att_out": nrm(ks[5], (L, ATT_OUT_W, D), ATT_OUT_W ** -0.5),
        "ln_v_g": 1.0 + nrm(ks[6], (L, GMLP_W), 0.05),
        "ln_v_b": nrm(ks[7], (L, GMLP_W), 0.02),
        "w_spatial": nrm(ks[8], (L, GMLP_GROUPS, CHUNK, CHUNK), CHUNK ** -0.5),
        "b_spatial": 1.0 + nrm(ks[9], (L, GMLP_GROUPS, CHUNK), 0.1),
        "w_gmlp_out": nrm(ks[10], (L, GMLP_W, D), GMLP_W ** -0.5),
        "w_out": nrm(ks[11], (L, D, D), D ** -0.5),
        "g_moe": 1.0 + nrm(ks[12], (L, D), 0.05),
        "w_router": nrm(ks[13], (L, D, E), D ** -0.5),
        "b_router": nrm(ks[14], (L, E), 0.01),
        "w_gate_up": nrm(ks[15], (L, E, D, 2 * D_EXPERT), D ** -0.5),
        "b_gate_up": nrm(ks[16], (L, E, 2 * D_EXPERT), 0.02),
        "w_down": nrm(ks[17], (L, E, D_EXPERT, D), D_EXPERT ** -0.5),
        "b_down": nrm(ks[18], (L, E, D), 0.02),
        "g_ple": 1.0 + nrm(ks[19], (L, D), 0.05),
        "w_ple_gate": nrm(ks[20], (L, D, D), D ** -0.5),
        "w_ple_proj": nrm(ks[21], (L, PLE_DIM, D), PLE_DIM ** -0.5),
        "g_final": 1.0 + nrm(ks[22], (D,), 0.05),
    }


def reference(x, p, g_mix, w_in, rel_bias, w_att_out, ln_v_g, ln_v_b, w_spatial, b_spatial,
              w_gmlp_out, w_out, g_moe, w_router, b_router, w_gate_up, b_gate_up, w_down, b_down,
              g_ple, w_ple_gate, w_ple_proj, g_final):
    B, S, D = x.shape
    splits = [int(c) for c in np.cumsum([ATT_W, ATT_W, ATT_W, GMLP_W, GMLP_W])]
    h = x
    for i in range(DEPTH):
        n1 = _rmsnorm(h, g_mix[i])
        z = n1 @ w_in[i]
        q, k, v, u_raw, v_raw, gate_logits = jnp.split(z, splits, axis=-1)
        hs = (B, S, N_DIL, HEADS_PER_GROUP, HEAD_DIM)
        att = _dilated_attention(q.reshape(hs), k.reshape(hs), v.reshape(hs), rel_bias)
        y_att = att.reshape(B, S, ATT_OUT_W).astype(h.dtype) @ w_att_out[i]
        y_gm = _spatial_gating(u_raw, v_raw, ln_v_g[i], ln_v_b[i], w_spatial[i], b_spatial[i]) @ w_gmlp_out[i]
        gates = jax.nn.sigmoid(gate_logits.reshape(B, S, N_BRANCH, D))
        merged = gates[:, :, 0] * y_att + gates[:, :, 1] * y_gm
        h = h + merged @ w_out[i]
        h = h + _moe(_rmsnorm(h, g_moe[i]), w_router[i], b_router[i], w_gate_up[i], b_gate_up[i],
                     w_down[i], b_down[i])
        ple_gate = jax.nn.sigmoid(_rmsnorm(h, g_ple[i]) @ w_ple_gate[i])
        h = h + ple_gate * (p[i] @ w_ple_proj[i])
    return _rmsnorm(h, g_final)
```

```python
import functools

import jax
import jax.numpy as jnp
import numpy as np
from jax import lax
from jax.experimental import pallas as pl
from jax.experimental.pallas import tpu as pltpu

F32 = jnp.float32
BF16 = jnp.bfloat16

D_MODEL = 1024
HEAD_DIM = 64
ATT_GROUPS = ((128, 1), (512, 4), (2048, 16))
HEADS_PER_GROUP = 4
GROUP_W = HEADS_PER_GROUP * HEAD_DIM
N_DIL = len(ATT_GROUPS)
ATT_W = N_DIL * GROUP_W
BLK = 128
REL_BUCKETS = 32
REL_MAX_DIST = 2048
CHUNK = 128
GMLP_W = 768
GMLP_GD = 64
N_BRANCH = 2
IN_W = 3 * ATT_W + 2 * GMLP_W + N_BRANCH * D_MODEL
N_EXPERTS = 32
TOP_K = 4
D_EXPERT = D_MODEL
SWIGLU_LIMIT = 7.0
SWIGLU_ALPHA = 1.702
PLE_DIM = 256
EPS = 1e-6
MASKED = -1e30

MXU_N = 256
VMEM_LIMIT = 56 * 1024 * 1024

TM_PROJ = 512
TM_MOE = 256


def _cparams(*sem):
    return pltpu.CompilerParams(dimension_semantics=sem, vmem_limit_bytes=VMEM_LIMIT)


def _resident(shape):
    nd = len(shape)
    return pl.BlockSpec(shape, lambda *_: (0,) * nd, pipeline_mode=pl.Buffered(1))


def _rms(x, g):
    return x * lax.rsqrt(jnp.mean(x * x, axis=-1, keepdims=True) + EPS) * g


def _inproj_kernel(x_ref, g_ref, w_ref, qkv_ref, uv_ref, gl_ref):
    n = _rms(x_ref[...], g_ref[...]).astype(BF16)
    n_qkv, n_uv = 3 * ATT_W // MXU_N, 2 * GMLP_W // MXU_N
    for c in range(IN_W // MXU_N):
        z = jnp.dot(n, w_ref[:, c * MXU_N:(c + 1) * MXU_N], preferred_element_type=F32).astype(BF16)
        if c < n_qkv:
            qkv_ref[:, c * MXU_N:(c + 1) * MXU_N] = z
        elif c < n_qkv + n_uv:
            uv_ref[:, (c - n_qkv) * MXU_N:(c - n_qkv + 1) * MXU_N] = z
        else:
            gl_ref[:, (c - n_qkv - n_uv) * MXU_N:(c - n_qkv - n_uv + 1) * MXU_N] = z


def _in_proj(x2, g, w_bf):
    T = x2.shape[0]
    tm = TM_PROJ
    row = lambda w: pl.BlockSpec((tm, w), lambda i: (i, 0))
    return pl.pallas_call(
        _inproj_kernel,
        grid=(T // tm,),
        in_specs=[row(D_MODEL), _resident((1, D_MODEL)), _resident((D_MODEL, IN_W))],
        out_specs=[row(3 * ATT_W), row(2 * GMLP_W), row(N_BRANCH * D_MODEL)],
        out_shape=[jax.ShapeDtypeStruct((T, 3 * ATT_W), BF16),
                   jax.ShapeDtypeStruct((T, 2 * GMLP_W), BF16),
                   jax.ShapeDtypeStruct((T, N_BRANCH * D_MODEL), BF16)],
        compiler_params=_cparams("parallel"),
        name="in_proj",
    )(x2, g, w_bf)


def _t5_bucket(n):
    exact = REL_BUCKETS // 2
    nf = np.maximum(n, 1).astype(np.float32)
    large = exact + (np.log(nf / exact) / np.log(REL_MAX_DIST / exact) * (REL_BUCKETS - exact)).astype(np.int32)
    large = np.minimum(large, REL_BUCKETS - 1)
    return np.where(n < exact, n, large).astype(np.int32)


def _bias_table(rel_bias_g, dilation):
    i = np.arange(BLK)[:, None]
    j = np.arange(2 * BLK)[None, :]
    dist = i + BLK - j
    band = (dist >= 0) & (dist <= BLK)
    bucket = _t5_bucket(np.clip(dist, 0, None) * dilation)
    bias = jnp.transpose(rel_bias_g[bucket], (2, 0, 1)).astype(F32)
    return jnp.where(jnp.asarray(band)[None], bias, MASKED)


def _attn_kernel(q_ref, kp_ref, kc_ref, vp_ref, vc_ref, bias_ref, o_ref, lse_ref):
    first = pl.program_id(2) == 0
    q, kp, kc, vp, vc = q_ref[0], kp_ref[0], kc_ref[0], vp_ref[0], vc_ref[0]
    nt = (((1,), (1,)), ((), ()))
    scale = HEAD_DIM ** -0.5
    for h in range(HEADS_PER_GROUP):
        sl = slice(h * HEAD_DIM, (h + 1) * HEAD_DIM)
        qh = q[:, sl]
        s_p = lax.dot_general(qh, kp[:, sl], nt, preferred_element_type=F32) * scale + bias_ref[h, :, :BLK]
        s_c = lax.dot_general(qh, kc[:, sl], nt, preferred_element_type=F32) * scale + bias_ref[h, :, BLK:]
        s_p = jnp.where(first, MASKED, s_p)
        m = jnp.maximum(jnp.max(s_p, axis=-1, keepdims=True), jnp.max(s_c, axis=-1, keepdims=True))
        w_p = jnp.exp(s_p - m)
        w_c = jnp.exp(s_c - m)
        den = jnp.sum(w_p, axis=-1, keepdims=True) + jnp.sum(w_c, axis=-1, keepdims=True)
        num = (jnp.dot(w_p.astype(BF16), vp[:, sl], preferred_element_type=F32)
               + jnp.dot(w_c.astype(BF16), vc[:, sl], preferred_element_type=F32))
        o_ref[0, :, sl] = (num / den).astype(BF16)
        lse_ref[0, :, sl] = jnp.broadcast_to(m + jnp.log(den), (BLK, HEAD_DIM))


def _attention_group(qkv, bias, g, dilation, B, S):
    sd = S // dilation
    nb = sd // BLK
    ncol = 3 * ATT_W // GROUP_W
    qkv_v = qkv.reshape(B, sd, dilation * 3 * ATT_W)

    def col(which, prev):
        def index(b, r, n):
            return (b, jnp.maximum(n - 1, 0) if prev else n, r * ncol + which * N_DIL + g)
        return pl.BlockSpec((1, BLK, GROUP_W), index)

    out_spec = pl.BlockSpec((1, BLK, GROUP_W), lambda b, r, n: (b, n, r))
    o, lse = pl.pallas_call(
        _attn_kernel,
        grid=(B, dilation, nb),
        in_specs=[col(0, False), col(1, True), col(1, False), col(2, True), col(2, False),
                  _resident((HEADS_PER_GROUP, BLK, 2 * BLK))],
        out_specs=[out_spec, out_spec],
        out_shape=[jax.ShapeDtypeStruct((B, sd, dilation * GROUP_W), BF16),
                   jax.ShapeDtypeStruct((B, sd, dilation * GROUP_W), F32)],
        compiler_params=_cparams("parallel", "parallel", "arbitrary"),
        name=f"attn_d{dilation}",
    )(qkv_v, qkv_v, qkv_v, qkv_v, qkv_v, bias)
    return o.reshape(B * S, GROUP_W), lse.reshape(B * S, GROUP_W)


def _gelu(x):
    return x * (lax.erf(x * (2.0 ** -0.5)) + 1.0) * 0.5


def _mix_kernel(x_ref, o1_ref, o2_ref, o3_ref, l1_ref, l2_ref, l3_ref, uv_ref, gl_ref,
                wa_ref, wg_ref, wo_ref, wc_ref, bs_ref, lng_ref, lnb_ref, h_ref, g_scr):
    tm = x_ref.shape[0]
    l1, l2, l3 = l1_ref[...], l2_ref[...], l3_ref[...]
    lm = jnp.maximum(jnp.maximum(l1, l2), l3)
    e1, e2, e3 = jnp.exp(l1 - lm), jnp.exp(l2 - lm), jnp.exp(l3 - lm)
    att = (e1 * o1_ref[...].astype(F32) + e2 * o2_ref[...].astype(F32) + e3 * o3_ref[...].astype(F32)) / (e1 + e2 + e3)
    y_att = jnp.dot(att.astype(BF16), wa_ref[...], preferred_element_type=F32)

    zu = _gelu(uv_ref[:, :GMLP_W].astype(F32))
    zv = _gelu(uv_ref[:, GMLP_W:].astype(F32))
    mu = jnp.mean(zv, axis=-1, keepdims=True)
    var = jnp.mean(jnp.square(zv - mu), axis=-1, keepdims=True)
    vn = (zv - mu) * lax.rsqrt(var + EPS) * lng_ref[...] + lnb_ref[...]
    low_half = lax.broadcasted_iota(jnp.int32, (CHUNK, 2 * GMLP_GD), 1) < GMLP_GD
    for c in range(tm // CHUNK):
        rows = slice(c * CHUNK, (c + 1) * CHUNK)
        for s in range(GMLP_W // (2 * GMLP_GD)):
            cols = slice(s * 2 * GMLP_GD, (s + 1) * 2 * GMLP_GD)
            v2 = vn[rows, cols]
            rhs = jnp.concatenate([jnp.where(low_half, v2, 0.0), jnp.where(low_half, 0.0, v2)], axis=0).astype(BF16)
            mixed = jnp.dot(wc_ref[s], rhs, preferred_element_type=F32) + bs_ref[:, cols]
            g_scr[rows, cols] = (zu[rows, cols] * mixed).astype(BF16)
    y_gm = jnp.dot(g_scr[...], wg_ref[...], preferred_element_type=F32)

    gate_a = jax.nn.sigmoid(gl_ref[:, :D_MODEL].astype(F32))
    gate_g = jax.nn.sigmoid(gl_ref[:, D_MODEL:].astype(F32))
    merged = (gate_a * y_att + gate_g * y_gm).astype(BF16)
    h_ref[...] = x_ref[...] + jnp.dot(merged, wo_ref[...], preferred_element_type=F32)


def _mix(x2, outs, lses, uv, gl, wa, wg, wo, wc2, bs, lng, lnb):
    T = x2.shape[0]
    tm = TM_PROJ
    row = lambda w: pl.BlockSpec((tm, w), lambda i: (i, 0))
    return pl.pallas_call(
        _mix_kernel,
        grid=(T // tm,),
        in_specs=[row(D_MODEL)] + [row(GROUP_W)] * 6 + [row(2 * GMLP_W), row(N_BRANCH * D_MODEL),
                  _resident(wa.shape), _resident(wg.shape), _resident(wo.shape), _resident(wc2.shape),
                  _resident(bs.shape), _resident(lng.shape), _resident(lnb.shape)],
        out_specs=row(D_MODEL),
        out_shape=jax.ShapeDtypeStruct((T, D_MODEL), F32),
        scratch_shapes=[pltpu.VMEM((tm, GMLP_W), BF16)],
        compiler_params=_cparams("parallel"),
        name="mix",
    )(x2, *outs, *lses, uv, gl, wa, wg, wo, wc2, bs, lng, lnb)


def _router_kernel(h_ref, g_ref, wr_ref, br_ref, eidx_ref, gate_ref, rank_ref, cnt_ref, carry):
    tm = h_ref.shape[0]

    @pl.when(pl.program_id(0) == 0)
    def _():
        carry[...] = jnp.zeros_like(carry)

    hn = _rms(h_ref[...], g_ref[...])
    logits = jnp.dot(hn, wr_ref[...], preferred_element_type=F32, precision=lax.Precision.HIGHEST) + br_ref[...]
    lane = lax.broadcasted_iota(jnp.int32, (tm, N_EXPERTS), 1)
    vals, hots = [], []
    l = logits
    for k in range(TOP_K):
        m = jnp.max(l, axis=-1, keepdims=True)
        idx = jnp.min(jnp.where(l == m, lane, N_EXPERTS), axis=-1, keepdims=True)
        hot = lane == idx
        eidx_ref[:, k:k + 1] = idx
        vals.append(m)
        hots.append(hot)
        l = jnp.where(hot, -jnp.inf, l)
    ex = [jnp.exp(v - vals[0]) for v in vals]
    tot = ex[0] + ex[1] + ex[2] + ex[3]
    for k in range(TOP_K):
        gate_ref[:, k:k + 1] = ex[k] / tot
    multi = jnp.zeros((tm, N_EXPERTS), F32)
    for hot in hots:
        multi = multi + hot.astype(F32)
    r = lax.broadcasted_iota(jnp.int32, (tm, tm), 0)
    c = lax.broadcasted_iota(jnp.int32, (tm, tm), 1)
    strict_lower = (c < r).astype(BF16)
    before = jnp.dot(strict_lower, multi.astype(BF16), preferred_element_type=F32) + carry[...]
    for k in range(TOP_K):
        rank_ref[:, k:k + 1] = jnp.sum(jnp.where(hots[k], before, 0.0), axis=-1, keepdims=True).astype(jnp.int32)
    carry[...] += jnp.sum(multi, axis=0, keepdims=True)
    cnt_ref[...] = carry[...]


def _router(h1, g, wr, br):
    T = h1.shape[0]
    tm = TM_PROJ
    col4 = pl.BlockSpec((tm, TOP_K), lambda i: (i, 0))
    return pl.pallas_call(
        _router_kernel,
        grid=(T // tm,),
        in_specs=[pl.BlockSpec((tm, D_MODEL), lambda i: (i, 0)), _resident((1, D_MODEL)),
                  _resident((D_MODEL, N_EXPERTS)), _resident((1, N_EXPERTS))],
        out_specs=[col4, col4, col4, pl.BlockSpec((1, N_EXPERTS), lambda i: (0, 0))],
        out_shape=[jax.ShapeDtypeStruct((T, TOP_K), jnp.int32), jax.ShapeDtypeStruct((T, TOP_K), F32),
                   jax.ShapeDtypeStruct((T, TOP_K), jnp.int32), jax.ShapeDtypeStruct((1, N_EXPERTS), F32)],
        scratch_shapes=[pltpu.VMEM((1, N_EXPERTS), F32)],
        compiler_params=_cparams("arbitrary"),
        name="router",
    )(h1, g, wr, br)


def _row_copy_wait(buf, hbm, sem, n_bursts):
    for _ in range(n_bursts):
        pltpu.make_async_copy(buf, hbm.at[pl.ds(0, buf.shape[0])], sem).wait()


def _dispatch_kernel(dest_ref, h_ref, g_ref, xs_in, xs_ref, buf, sem):
    del xs_in
    tm = h_ref.shape[0]
    i = pl.program_id(0)
    slot = lax.rem(i, 2)
    buf[slot] = _rms(h_ref[...], g_ref[...])

    def issue(t, carry):
        for k in range(TOP_K):
            d = dest_ref[0, 0, t * TOP_K + k]
            pltpu.make_async_copy(buf.at[slot, pl.ds(t, 1)], xs_ref.at[pl.ds(d, 1)], sem.at[slot]).start()
        return carry

    lax.fori_loop(0, tm, issue, 0, unroll=4)

    @pl.when(i > 0)
    def _():
        _row_copy_wait(buf.at[1 - slot], xs_ref, sem.at[1 - slot], TOP_K)

    @pl.when(i == pl.num_programs(0) - 1)
    def _():
        _row_copy_wait(buf.at[slot], xs_ref, sem.at[slot], TOP_K)


def _dispatch(dest3, h1, g, n_slots):
    T = h1.shape[0]
    tm = TM_MOE
    xs0 = jnp.zeros((n_slots, D_MODEL), F32)
    return pl.pallas_call(
        _dispatch_kernel,
        grid=(T // tm,),
        in_specs=[pl.BlockSpec((1, 1, TOP_K * tm), lambda i: (i, 0, 0), memory_space=pltpu.SMEM),
                  pl.BlockSpec((tm, D_MODEL), lambda i: (i, 0)), _resident((1, D_MODEL)),
                  pl.BlockSpec(memory_space=pl.ANY)],
        out_specs=pl.BlockSpec(memory_space=pl.ANY),
        out_shape=jax.ShapeDtypeStruct((n_slots, D_MODEL), F32),
        scratch_shapes=[pltpu.VMEM((2, tm, D_MODEL), F32), pltpu.SemaphoreType.DMA((2,))],
        input_output_aliases={3: 0},
        compiler_params=_cparams("arbitrary"),
        name="dispatch",
    )(dest3, h1, g, xs0)


def _experts_kernel(be_ref, nv_ref, x_ref, wgu_ref, bgu_ref, wd_ref, bd_ref, y_ref):
    del be_ref

    @pl.when(pl.program_id(0) < nv_ref[0])
    def _():
        x = x_ref[...].astype(BF16)
        gu = jnp.dot(x, wgu_ref[0], preferred_element_type=F32) + bgu_ref[0]
        glu = jnp.minimum(gu[:, :D_EXPERT], SWIGLU_LIMIT)
        lin = jnp.clip(gu[:, D_EXPERT:], -SWIGLU_LIMIT, SWIGLU_LIMIT)
        act = glu * jax.nn.sigmoid(SWIGLU_ALPHA * glu) * (lin + 1.0)
        y_ref[...] = jnp.dot(act.astype(BF16), wd_ref[0], preferred_element_type=F32) + bd_ref[0]

    @pl.when(pl.program_id(0) >= nv_ref[0])
    def _():
        y_ref[...] = jnp.zeros_like(y_ref)


def _experts(block_expert, n_valid, xs, wgu, bgu, wd, bd):
    n_slots = xs.shape[0]
    tm = TM_MOE
    live = lambda b, be, nv: jnp.minimum(b, nv[0] - 1)
    grid_spec = pltpu.PrefetchScalarGridSpec(
        num_scalar_prefetch=2,
        grid=(n_slots // tm,),
        in_specs=[pl.BlockSpec((tm, D_MODEL), lambda b, be, nv: (live(b, be, nv), 0)),
                  pl.BlockSpec((1, D_MODEL, 2 * D_EXPERT), lambda b, be, nv: (be[b], 0, 0)),
                  pl.BlockSpec((1, 1, 2 * D_EXPERT), lambda b, be, nv: (be[b], 0, 0)),
                  pl.BlockSpec((1, D_EXPERT, D_MODEL), lambda b, be, nv: (be[b], 0, 0)),
                  pl.BlockSpec((1, 1, D_MODEL), lambda b, be, nv: (be[b], 0, 0))],
        out_specs=pl.BlockSpec((tm, D_MODEL), lambda b, be, nv: (b, 0)),
    )
    return pl.pallas_call(
        _experts_kernel,
        grid_spec=grid_spec,
        out_shape=jax.ShapeDtypeStruct((n_slots, D_MODEL), F32),
        compiler_params=_cparams("arbitrary"),
        name="experts",
    )(block_expert, n_valid, xs, wgu, bgu, wd, bd)


def _combine_kernel(dest_ref, h_ref, gate_ref, p_ref, gp_ref, wpg_ref, wpp_ref, gf_ref, ys_ref, o_ref, buf, sem):
    tm = h_ref.shape[0]

    def issue(t, carry):
        for k in range(TOP_K):
            d = dest_ref[0, 0, t * TOP_K + k]
            pltpu.make_async_copy(ys_ref.at[pl.ds(d, 1)], buf.at[k, pl.ds(t, 1)], sem).start()
        return carry

    lax.fori_loop(0, tm, issue, 0, unroll=4)
    proj = jnp.dot(p_ref[...].astype(BF16), wpp_ref[...], preferred_element_type=F32)
    for k in range(TOP_K):
        pltpu.make_async_copy(ys_ref.at[pl.ds(0, tm)], buf.at[k], sem).wait()
    h = h_ref[...]
    for k in range(TOP_K):
        h = h + gate_ref[:, k:k + 1] * buf[k]
    ple_gate = jax.nn.sigmoid(jnp.dot(_rms(h, gp_ref[...]).astype(BF16), wpg_ref[...], preferred_element_type=F32))
    h = h + ple_gate * proj
    o_ref[...] = _rms(h, gf_ref[...])


def _combine(dest3, h1, gate, p2, gp, wpg, wpp, gf, ys):
    T = h1.shape[0]
    tm = TM_MOE
    row = lambda w: pl.BlockSpec((tm, w), lambda i: (i, 0))
    return pl.pallas_call(
        _combine_kernel,
        grid=(T // tm,),
        in_specs=[pl.BlockSpec((1, 1, TOP_K * tm), lambda i: (i, 0, 0), memory_space=pltpu.SMEM),
                  row(D_MODEL), row(TOP_K), row(PLE_DIM), _resident((1, D_MODEL)),
                  _resident((D_MODEL, D_MODEL)), _resident((PLE_DIM, D_MODEL)), _resident((1, D_MODEL)),
                  pl.BlockSpec(memory_space=pl.ANY)],
        out_specs=row(D_MODEL),
        out_shape=jax.ShapeDtypeStruct((T, D_MODEL), F32),
        scratch_shapes=[pltpu.VMEM((TOP_K, tm, D_MODEL), F32), pltpu.SemaphoreType.DMA(())],
        compiler_params=_cparams("arbitrary"),
        name="combine",
    )(dest3, h1, gate, p2, gp, wpg, wpp, gf, ys)


def _layer(h, p_i, g_mix, w_in, rel_bias, w_att_out, ln_v_g, ln_v_b, w_spatial, b_spatial, w_gmlp_out, w_out,
           g_moe, w_router, b_router, w_gate_up, b_gate_up, w_down, b_down, g_ple, w_ple_gate, w_ple_proj,
           g_final, B, S):
    T = B * S
    row = lambda v: v.reshape(1, -1).astype(F32)

    qkv, uv, gl = _in_proj(h, row(g_mix), w_in.astype(BF16))

    outs, lses = [], []
    for g, (window, dilation) in enumerate(ATT_GROUPS):
        assert window // dilation == BLK and S % (dilation * BLK) == 0
        bias = _bias_table(rel_bias[:, g * HEADS_PER_GROUP:(g + 1) * HEADS_PER_GROUP], dilation)
        o, lse = _attention_group(qkv, bias, g, dilation, B, S)
        outs.append(o)
        lses.append(lse)

    causal = jnp.asarray(np.tril(np.ones((CHUNK, CHUNK), np.float32)))
    w_c = (w_spatial.astype(F32) * causal[None]).astype(BF16)
    wc2 = jnp.concatenate([w_c[0::2], w_c[1::2]], axis=2)
    bs = jnp.repeat(b_spatial.astype(F32).T, GMLP_GD, axis=1)
    h1 = _mix(h, outs, lses, uv, gl, w_att_out.astype(BF16), w_gmlp_out.astype(BF16), w_out.astype(BF16),
              wc2, bs, row(ln_v_g), row(ln_v_b))

    eidx, gate, rank, counts = _router(h1, row(g_moe), w_router.astype(F32), row(b_router))
    cnt = counts[0].astype(jnp.int32)
    blk_counts = (cnt + TM_MOE - 1) // TM_MOE
    blk_end = jnp.cumsum(blk_counts)
    pad_start = (blk_end - blk_counts) * TM_MOE
    n_blocks = T * TOP_K // TM_MOE + N_EXPERTS
    n_valid = blk_end[-1:].astype(jnp.int32)
    blk = jnp.minimum(jnp.arange(n_blocks, dtype=jnp.int32), n_valid[0] - 1)
    block_expert = jnp.minimum(jnp.searchsorted(blk_end, blk, side='right'), N_EXPERTS - 1).astype(jnp.int32)
    dest = pad_start[eidx] + rank
    dest3 = dest.reshape(T // TM_MOE, 1, TM_MOE * TOP_K)

    xs = _dispatch(dest3, h1, row(g_moe), n_blocks * TM_MOE)
    ys = _experts(block_expert, n_valid, xs, w_gate_up.astype(BF16), b_gate_up.reshape(N_EXPERTS, 1, -1).astype(F32),
                  w_down.astype(BF16), b_down.reshape(N_EXPERTS, 1, -1).astype(F32))
    return _combine(dest3, h1, gate, p_i, row(g_ple), w_ple_gate.astype(BF16), w_ple_proj.astype(BF16),
                    row(g_final), ys)


def kernel(x, p, g_mix, w_in, rel_bias, w_att_out, ln_v_g, ln_v_b, w_spatial, b_spatial, w_gmlp_out, w_out, g_moe, w_router, b_router, w_gate_up, b_gate_up, w_down, b_down, g_ple, w_ple_gate, w_ple_proj, g_final):
    B, S, D = x.shape
    depth = p.shape[0]
    assert depth == 1, "the final RMSNorm is fused into the (single) layer's last kernel"
    out = _layer(x.reshape(B * S, D), p[0].reshape(B * S, PLE_DIM), g_mix[0], w_in[0], rel_bias, w_att_out[0],
                 ln_v_g[0], ln_v_b[0], w_spatial[0], b_spatial[0], w_gmlp_out[0], w_out[0], g_moe[0], w_router[0],
                 b_router[0], w_gate_up[0], b_gate_up[0], w_down[0], b_down[0], g_ple[0], w_ple_gate[0],
                 w_ple_proj[0], g_final, B, S)
    return out.reshape(B, S, D)
```

```python
import functools

import jax
import jax.numpy as jnp
import numpy as np
from jax import lax
from jax.experimental import pallas as pl
from jax.experimental.pallas import tpu as pltpu

F32 = jnp.float32
BF16 = jnp.bfloat16

D_MODEL = 1024
HEAD_DIM = 64
ATT_GROUPS = ((128, 1), (512, 4), (2048, 16))
HEADS_PER_GROUP = 4
GROUP_W = HEADS_PER_GROUP * HEAD_DIM
N_DIL = len(ATT_GROUPS)
ATT_W = N_DIL * GROUP_W
BLK = 128
REL_BUCKETS = 32
REL_MAX_DIST = 2048
CHUNK = 128
GMLP_W = 768
GMLP_GD = 64
N_BRANCH = 2
IN_W = 3 * ATT_W + 2 * GMLP_W + N_BRANCH * D_MODEL
N_EXPERTS = 32
TOP_K = 4
D_EXPERT = D_MODEL
SWIGLU_LIMIT = 7.0
SWIGLU_ALPHA = 1.702
PLE_DIM = 256
EPS = 1e-6
MASKED = -1e30

QKV_G = 3 * GROUP_W

LANES = 128
MXU_N = 256
VMEM_LIMIT = 56 * 1024 * 1024

TM_PROJ = 512
TM_MOE = 256


def _cparams(*sem):
    return pltpu.CompilerParams(dimension_semantics=sem, vmem_limit_bytes=VMEM_LIMIT)


def _resident(shape):
    nd = len(shape)
    return pl.BlockSpec(shape, lambda *_: (0,) * nd, pipeline_mode=pl.Buffered(1))


def _rms(x, g):
    return x * lax.rsqrt(jnp.mean(x * x, axis=-1, keepdims=True) + EPS) * g


def _inproj_kernel(x_ref, g_ref, w_ref, a1_ref, a2_ref, a3_ref, uv_ref, gl_ref, scr):
    tm = x_ref.shape[0]
    n = _rms(x_ref[...], g_ref[...]).astype(BF16)
    att_refs = (a1_ref, a2_ref, a3_ref)
    n_att, n_uv = 3 * ATT_W // MXU_N, 2 * GMLP_W // MXU_N
    for c in range(IN_W // MXU_N):
        z = jnp.dot(n, w_ref[:, c * MXU_N:(c + 1) * MXU_N], preferred_element_type=F32)
        if c < n_att:
            which, g = divmod(c, N_DIL)
            d = ATT_GROUPS[g][1]
            dst = att_refs[g]
            if d == 1:
                dst[:, which * GROUP_W:(which + 1) * GROUP_W] = z.astype(BF16)
                continue
            scr[0] = z[:, :LANES]
            scr[1] = z[:, LANES:]
            for r in range(d):
                zr = jnp.concatenate([scr[0, pl.ds(r, tm // d, stride=d), :],
                                      scr[1, pl.ds(r, tm // d, stride=d), :]], axis=1)
                dst[:, r * QKV_G + which * GROUP_W:r * QKV_G + (which + 1) * GROUP_W] = zr.astype(BF16)
        elif c < n_att + n_uv:
            uv_ref[:, (c - n_att) * MXU_N:(c - n_att + 1) * MXU_N] = z.astype(BF16)
        else:
            gl_ref[:, (c - n_att - n_uv) * MXU_N:(c - n_att - n_uv + 1) * MXU_N] = z.astype(BF16)


def _in_proj(x2, g, w_bf):
    T = x2.shape[0]
    tm = TM_PROJ
    row = lambda w: pl.BlockSpec((tm, w), lambda i: (i, 0))
    att_spec = lambda d: pl.BlockSpec((tm // d, d * QKV_G), lambda i: (i, 0))
    dils = [d for _, d in ATT_GROUPS]
    return pl.pallas_call(
        _inproj_kernel,
        grid=(T // tm,),
        in_specs=[row(D_MODEL), _resident((1, D_MODEL)), _resident((D_MODEL, IN_W))],
        out_specs=[att_spec(d) for d in dils] + [row(2 * GMLP_W), row(N_BRANCH * D_MODEL)],
        out_shape=[jax.ShapeDtypeStruct((T // d, d * QKV_G), BF16) for d in dils]
                  + [jax.ShapeDtypeStruct((T, 2 * GMLP_W), BF16),
                     jax.ShapeDtypeStruct((T, N_BRANCH * D_MODEL), BF16)],
        scratch_shapes=[pltpu.VMEM((2, tm, LANES), F32)],
        compiler_params=_cparams("parallel"),
        name="in_proj",
    )(x2, g, w_bf)


def _t5_bucket(n):
    exact = REL_BUCKETS // 2
    nf = np.maximum(n, 1).astype(np.float32)
    large = exact + (np.log(nf / exact) / np.log(REL_MAX_DIST / exact) * (REL_BUCKETS - exact)).astype(np.int32)
    large = np.minimum(large, REL_BUCKETS - 1)
    return np.where(n < exact, n, large).astype(np.int32)


def _bias_table(rel_bias_g, dilation):
    n = 3 * BLK
    dist = 2 * BLK - 1 - np.arange(n)
    valid = (dist >= 0) & (dist <= BLK)
    bucket = _t5_bucket(np.clip(dist, 0, BLK) * dilation)
    c = jnp.where(jnp.asarray(valid)[None, :], rel_bias_g.astype(F32)[bucket].T, MASKED)
    shifted = jnp.tile(c, (1, BLK))[:, :BLK * (n - 1)].reshape(HEADS_PER_GROUP, BLK, n - 1)
    return shifted[:, :, BLK - 1:].reshape(HEADS_PER_GROUP * BLK, 2 * BLK)


def _attn_kernel(cur_ref, prev_ref, bias_ref, o_ref, lse_ref):
    rb = cur_ref.shape[1] // BLK
    starts_sequence = pl.program_id(1) == 0
    lane_head = lax.broadcasted_iota(jnp.int32, (1, GROUP_W), 1) // HEAD_DIM
    head_bf = [(lane_head == h).astype(BF16) for h in range(HEADS_PER_GROUP)]
    head_f = [(lane_head == h).astype(F32) for h in range(HEADS_PER_GROUP)]
    key_is_prev = lax.broadcasted_iota(jnp.int32, (1, 2 * BLK), 1) < BLK
    nt = (((1,), (1,)), ((), ()))
    scale = HEAD_DIM ** -0.5
    qc, kc_, vc_ = slice(0, GROUP_W), slice(GROUP_W, 2 * GROUP_W), slice(2 * GROUP_W, 3 * GROUP_W)
    for j in range(rb):
        rows = slice(j * BLK, (j + 1) * BLK)
        prev = prev_ref if j == 0 else cur_ref
        prows = slice(0, BLK) if j == 0 else slice((j - 1) * BLK, j * BLK)
        q = cur_ref[0, rows, qc]
        k = jnp.concatenate([prev[0, prows, kc_], cur_ref[0, rows, kc_]], axis=0)
        v = jnp.concatenate([prev[0, prows, vc_], cur_ref[0, rows, vc_]], axis=0)
        q_bd = jnp.concatenate([q * head_bf[h] for h in range(HEADS_PER_GROUP)], axis=0)
        s = lax.dot_general(q_bd, k, nt, preferred_element_type=F32) * scale + bias_ref[...]
        if j == 0:
            s = jnp.where(jnp.logical_and(starts_sequence, key_is_prev), MASKED, s)
        m = jnp.max(s, axis=-1, keepdims=True)
        p = jnp.exp(s - m)
        den = jnp.sum(p, axis=-1, keepdims=True)
        o = jnp.dot(p.astype(BF16), v, preferred_element_type=F32) / den
        l = m + jnp.log(den)
        out = jnp.zeros((BLK, GROUP_W), F32)
        lse = jnp.zeros((BLK, GROUP_W), F32)
        for h in range(HEADS_PER_GROUP):
            hr = slice(h * BLK, (h + 1) * BLK)
            out = out + o[hr] * head_f[h]
            lse = lse + l[hr] * head_f[h]
        o_ref[0, rows, :] = out.astype(BF16)
        lse_ref[0, rows, :] = lse


def _attention_group(a, bias, dilation, B, S):
    sd = S // dilation
    rb = min(8, sd // BLK)
    o, lse = pl.pallas_call(
        _attn_kernel,
        grid=(B, sd // (rb * BLK), dilation),
        in_specs=[pl.BlockSpec((1, rb * BLK, QKV_G), lambda b, n, r: (b, n, r)),
                  pl.BlockSpec((1, BLK, QKV_G), lambda b, n, r: (b, jnp.maximum(n * rb - 1, 0), r)),
                  _resident((HEADS_PER_GROUP * BLK, 2 * BLK))],
        out_specs=[pl.BlockSpec((1, rb * BLK, GROUP_W), lambda b, n, r: (b, n, r))] * 2,
        out_shape=[jax.ShapeDtypeStruct((B, sd, dilation * GROUP_W), BF16),
                   jax.ShapeDtypeStruct((B, sd, dilation * GROUP_W), F32)],
        compiler_params=_cparams("parallel", "parallel", "parallel"),
        name=f"attn_d{dilation}",
    )(a.reshape(B, sd, dilation * QKV_G), a.reshape(B, sd, dilation * QKV_G), bias)
    return o.reshape(B * sd, dilation * GROUP_W), lse.reshape(B * sd, dilation * GROUP_W)


def _gelu(x):
    return x * (lax.erf(x * (2.0 ** -0.5)) + 1.0) * 0.5


def _token_major(src_ref, d, scr, slot, tm):
    if d == 1:
        return src_ref[...].astype(F32)
    for r in range(d):
        piece = src_ref[:, r * GROUP_W:(r + 1) * GROUP_W].astype(F32)
        scr[slot, pl.ds(r, tm // d, stride=d), :] = piece[:, :LANES]
        scr[slot + 1, pl.ds(r, tm // d, stride=d), :] = piece[:, LANES:]
    return jnp.concatenate([scr[slot], scr[slot + 1]], axis=1)


def _mix_kernel(x_ref, o1_ref, o2_ref, o3_ref, l1_ref, l2_ref, l3_ref, uv_ref, gl_ref,
                wa_ref, wg_ref, wo_ref, wc_ref, bs_ref, lng_ref, lnb_ref, h_ref, g_scr, t_scr):
    tm = x_ref.shape[0]
    dils = [d for _, d in ATT_GROUPS]
    o1, o2, o3 = [_token_major(ref, d, t_scr, 4 * i, tm) for i, (ref, d) in enumerate(zip((o1_ref, o2_ref, o3_ref), dils))]
    l1, l2, l3 = [_token_major(ref, d, t_scr, 4 * i + 2, tm) for i, (ref, d) in enumerate(zip((l1_ref, l2_ref, l3_ref), dils))]
    lm = jnp.maximum(jnp.maximum(l1, l2), l3)
    e1, e2, e3 = jnp.exp(l1 - lm), jnp.exp(l2 - lm), jnp.exp(l3 - lm)
    att = (e1 * o1 + e2 * o2 + e3 * o3) / (e1 + e2 + e3)
    y_att = jnp.dot(att.astype(BF16), wa_ref[...], preferred_element_type=F32)

    zu = _gelu(uv_ref[:, :GMLP_W].astype(F32))
    zv = _gelu(uv_ref[:, GMLP_W:].astype(F32))
    mu = jnp.mean(zv, axis=-1, keepdims=True)
    var = jnp.mean(jnp.square(zv - mu), axis=-1, keepdims=True)
    vn = (zv - mu) * lax.rsqrt(var + EPS) * lng_ref[...] + lnb_ref[...]
    low_half = lax.broadcasted_iota(jnp.int32, (CHUNK, 2 * GMLP_GD), 1) < GMLP_GD
    for c in range(tm // CHUNK):
        rows = slice(c * CHUNK, (c + 1) * CHUNK)
        for s in range(GMLP_W // (2 * GMLP_GD)):
            cols = slice(s * 2 * GMLP_GD, (s + 1) * 2 * GMLP_GD)
            v2 = vn[rows, cols]
            rhs = jnp.concatenate([jnp.where(low_half, v2, 0.0), jnp.where(low_half, 0.0, v2)], axis=0).astype(BF16)
            mixed = jnp.dot(wc_ref[s], rhs, preferred_element_type=F32) + bs_ref[:, cols]
            g_scr[rows, cols] = (zu[rows, cols] * mixed).astype(BF16)
    y_gm = jnp.dot(g_scr[...], wg_ref[...], preferred_element_type=F32)

    gate_a = jax.nn.sigmoid(gl_ref[:, :D_MODEL].astype(F32))
    gate_g = jax.nn.sigmoid(gl_ref[:, D_MODEL:].astype(F32))
    merged = (gate_a * y_att + gate_g * y_gm).astype(BF16)
    h_ref[...] = x_ref[...] + jnp.dot(merged, wo_ref[...], preferred_element_type=F32)


def _mix(x2, outs, lses, uv, gl, wa, wg, wo, wc2, bs, lng, lnb):
    T = x2.shape[0]
    tm = TM_PROJ
    row = lambda w: pl.BlockSpec((tm, w), lambda i: (i, 0))
    att = [pl.BlockSpec((tm // d, d * GROUP_W), lambda i: (i, 0)) for _, d in ATT_GROUPS]
    return pl.pallas_call(
        _mix_kernel,
        grid=(T // tm,),
        in_specs=[row(D_MODEL)] + att + att + [row(2 * GMLP_W), row(N_BRANCH * D_MODEL),
                  _resident(wa.shape), _resident(wg.shape), _resident(wo.shape), _resident(wc2.shape),
                  _resident(bs.shape), _resident(lng.shape), _resident(lnb.shape)],
        out_specs=row(D_MODEL),
        out_shape=jax.ShapeDtypeStruct((T, D_MODEL), F32),
        scratch_shapes=[pltpu.VMEM((tm, GMLP_W), BF16), pltpu.VMEM((4 * N_DIL, tm, LANES), F32)],
        compiler_params=_cparams("parallel"),
        name="mix",
    )(x2, *outs, *lses, uv, gl, wa, wg, wo, wc2, bs, lng, lnb)


def _router_kernel(h_ref, g_ref, wr_ref, br_ref, eidx_ref, gate_ref, rank_ref, cnt_ref, carry):
    tm = h_ref.shape[0]

    @pl.when(pl.program_id(0) == 0)
    def _():
        carry[...] = jnp.zeros_like(carry)

    hn = _rms(h_ref[...], g_ref[...])
    logits = jnp.dot(hn, wr_ref[...], preferred_element_type=F32, precision=lax.Precision.HIGHEST) + br_ref[...]
    lane = lax.broadcasted_iota(jnp.int32, (tm, N_EXPERTS), 1)
    vals, hots = [], []
    l = logits
    for k in range(TOP_K):
        m = jnp.max(l, axis=-1, keepdims=True)
        idx = jnp.min(jnp.where(l == m, lane, N_EXPERTS), axis=-1, keepdims=True)
        hot = lane == idx
        eidx_ref[:, k:k + 1] = idx
        vals.append(m)
        hots.append(hot)
        l = jnp.where(hot, -jnp.inf, l)
    ex = [jnp.exp(v - vals[0]) for v in vals]
    tot = ex[0] + ex[1] + ex[2] + ex[3]
    for k in range(TOP_K):
        gate_ref[:, k:k + 1] = ex[k] / tot
    multi = jnp.zeros((tm, N_EXPERTS), F32)
    for hot in hots:
        multi = multi + hot.astype(F32)
    r = lax.broadcasted_iota(jnp.int32, (tm, tm), 0)
    c = lax.broadcasted_iota(jnp.int32, (tm, tm), 1)
    strict_lower = (c < r).astype(BF16)
    before = jnp.dot(strict_lower, multi.astype(BF16), preferred_element_type=F32) + carry[...]
    for k in range(TOP_K):
        rank_ref[:, k:k + 1] = jnp.sum(jnp.where(hots[k], before, 0.0), axis=-1, keepdims=True).astype(jnp.int32)
    carry[...] += jnp.sum(multi, axis=0, keepdims=True)
    cnt_ref[...] = carry[...]


def _router(h1, g, wr, br):
    T = h1.shape[0]
    tm = TM_PROJ
    col4 = pl.BlockSpec((tm, TOP_K), lambda i: (i, 0))
    return pl.pallas_call(
        _router_kernel,
        grid=(T // tm,),
        in_specs=[pl.BlockSpec((tm, D_MODEL), lambda i: (i, 0)), _resident((1, D_MODEL)),
                  _resident((D_MODEL, N_EXPERTS)), _resident((1, N_EXPERTS))],
        out_specs=[col4, col4, col4, pl.BlockSpec((1, N_EXPERTS), lambda i: (0, 0))],
        out_shape=[jax.ShapeDtypeStruct((T, TOP_K), jnp.int32), jax.ShapeDtypeStruct((T, TOP_K), F32),
                   jax.ShapeDtypeStruct((T, TOP_K), jnp.int32), jax.ShapeDtypeStruct((1, N_EXPERTS), F32)],
        scratch_shapes=[pltpu.VMEM((1, N_EXPERTS), F32)],
        compiler_params=_cparams("arbitrary"),
        name="router",
    )(h1, g, wr, br)


def _row_copy_wait(buf, hbm, sem, n_bursts):
    for _ in range(n_bursts):
        pltpu.make_async_copy(buf, hbm.at[pl.ds(0, buf.shape[0])], sem).wait()


def _dispatch_kernel(dest_ref, h_ref, g_ref, xs_in, xs_ref, buf, sem):
    del xs_in
    tm = h_ref.shape[0]
    i = pl.program_id(0)
    slot = lax.rem(i, 2)
    buf[slot] = _rms(h_ref[...], g_ref[...])

    def issue(t, carry):
        for k in range(TOP_K):
            d = dest_ref[0, 0, t * TOP_K + k]
            pltpu.make_async_copy(buf.at[slot, pl.ds(t, 1)], xs_ref.at[pl.ds(d, 1)], sem.at[slot]).start()
        return carry

    lax.fori_loop(0, tm, issue, 0, unroll=4)

    @pl.when(i > 0)
    def _():
        _row_copy_wait(buf.at[1 - slot], xs_ref, sem.at[1 - slot], TOP_K)

    @pl.when(i == pl.num_programs(0) - 1)
    def _():
        _row_copy_wait(buf.at[slot], xs_ref, sem.at[slot], TOP_K)


def _dispatch(dest3, h1, g, n_slots):
    T = h1.shape[0]
    tm = TM_MOE
    xs0 = jnp.zeros((n_slots, D_MODEL), F32)
    return pl.pallas_call(
        _dispatch_kernel,
        grid=(T // tm,),
        in_specs=[pl.BlockSpec((1, 1, TOP_K * tm), lambda i: (i, 0, 0), memory_space=pltpu.SMEM),
                  pl.BlockSpec((tm, D_MODEL), lambda i: (i, 0)), _resident((1, D_MODEL)),
                  pl.BlockSpec(memory_space=pl.ANY)],
        out_specs=pl.BlockSpec(memory_space=pl.ANY),
        out_shape=jax.ShapeDtypeStruct((n_slots, D_MODEL), F32),
        scratch_shapes=[pltpu.VMEM((2, tm, D_MODEL), F32), pltpu.SemaphoreType.DMA((2,))],
        input_output_aliases={3: 0},
        compiler_params=_cparams("arbitrary"),
        name="dispatch",
    )(dest3, h1, g, xs0)


def _experts_kernel(be_ref, nv_ref, x_ref, wgu_ref, bgu_ref, wd_ref, bd_ref, y_ref):
    del be_ref

    @pl.when(pl.program_id(0) < nv_ref[0])
    def _():
        x = x_ref[...].astype(BF16)
        gu = jnp.dot(x, wgu_ref[0], preferred_element_type=F32) + bgu_ref[0]
        glu = jnp.minimum(gu[:, :D_EXPERT], SWIGLU_LIMIT)
        lin = jnp.clip(gu[:, D_EXPERT:], -SWIGLU_LIMIT, SWIGLU_LIMIT)
        act = glu * jax.nn.sigmoid(SWIGLU_ALPHA * glu) * (lin + 1.0)
        y_ref[...] = jnp.dot(act.astype(BF16), wd_ref[0], preferred_element_type=F32) + bd_ref[0]

    @pl.when(pl.program_id(0) >= nv_ref[0])
    def _():
        y_ref[...] = jnp.zeros_like(y_ref)


def _experts(block_expert, n_valid, xs, wgu, bgu, wd, bd):
    n_slots = xs.shape[0]
    tm = TM_MOE
    live = lambda b, be, nv: jnp.minimum(b, nv[0] - 1)
    grid_spec = pltpu.PrefetchScalarGridSpec(
        num_scalar_prefetch=2,
        grid=(n_slots // tm,),
        in_specs=[pl.BlockSpec((tm, D_MODEL), lambda b, be, nv: (live(b, be, nv), 0)),
                  pl.BlockSpec((1, D_MODEL, 2 * D_EXPERT), lambda b, be, nv: (be[b], 0, 0)),
                  pl.BlockSpec((1, 1, 2 * D_EXPERT), lambda b, be, nv: (be[b], 0, 0)),
                  pl.BlockSpec((1, D_EXPERT, D_MODEL), lambda b, be, nv: (be[b], 0, 0)),
                  pl.BlockSpec((1, 1, D_MODEL), lambda b, be, nv: (be[b], 0, 0))],
        out_specs=pl.BlockSpec((tm, D_MODEL), lambda b, be, nv: (b, 0)),
    )
    return pl.pallas_call(
        _experts_kernel,
        grid_spec=grid_spec,
        out_shape=jax.ShapeDtypeStruct((n_slots, D_MODEL), F32),
        compiler_params=_cparams("arbitrary"),
        name="experts",
    )(block_expert, n_valid, xs, wgu, bgu, wd, bd)


def _combine_kernel(dest_ref, h_ref, gate_ref, p_ref, gp_ref, wpg_ref, wpp_ref, gf_ref, ys_ref, o_ref, buf, sem):
    tm = h_ref.shape[0]

    def issue(t, carry):
        for k in range(TOP_K):
            d = dest_ref[0, 0, t * TOP_K + k]
            pltpu.make_async_copy(ys_ref.at[pl.ds(d, 1)], buf.at[k, pl.ds(t, 1)], sem).start()
        return carry

    lax.fori_loop(0, tm, issue, 0, unroll=4)
    proj = jnp.dot(p_ref[...].astype(BF16), wpp_ref[...], preferred_element_type=F32)
    for k in range(TOP_K):
        pltpu.make_async_copy(ys_ref.at[pl.ds(0, tm)], buf.at[k], sem).wait()
    h = h_ref[...]
    for k in range(TOP_K):
        h = h + gate_ref[:, k:k + 1] * buf[k]
    ple_gate = jax.nn.sigmoid(jnp.dot(_rms(h, gp_ref[...]).astype(BF16), wpg_ref[...], preferred_element_type=F32))
    h = h + ple_gate * proj
    o_ref[...] = _rms(h, gf_ref[...])


def _combine(dest3, h1, gate, p2, gp, wpg, wpp, gf, ys):
    T = h1.shape[0]
    tm = TM_MOE
    row = lambda w: pl.BlockSpec((tm, w), lambda i: (i, 0))
    return pl.pallas_call(
        _combine_kernel,
        grid=(T // tm,),
        in_specs=[pl.BlockSpec((1, 1, TOP_K * tm), lambda i: (i, 0, 0), memory_space=pltpu.SMEM),
                  row(D_MODEL), row(TOP_K), row(PLE_DIM), _resident((1, D_MODEL)),
                  _resident((D_MODEL, D_MODEL)), _resident((PLE_DIM, D_MODEL)), _resident((1, D_MODEL)),
                  pl.BlockSpec(memory_space=pl.ANY)],
        out_specs=row(D_MODEL),
        out_shape=jax.ShapeDtypeStruct((T, D_MODEL), F32),
        scratch_shapes=[pltpu.VMEM((TOP_K, tm, D_MODEL), F32), pltpu.SemaphoreType.DMA(())],
        compiler_params=_cparams("arbitrary"),
        name="combine",
    )(dest3, h1, gate, p2, gp, wpg, wpp, gf, ys)


def _layer(h, p_i, g_mix, w_in, rel_bias, w_att_out, ln_v_g, ln_v_b, w_spatial, b_spatial, w_gmlp_out, w_out,
           g_moe, w_router, b_router, w_gate_up, b_gate_up, w_down, b_down, g_ple, w_ple_gate, w_ple_proj,
           g_final, B, S):
    T = B * S
    row = lambda v: v.reshape(1, -1).astype(F32)

    *att_in, uv, gl = _in_proj(h, row(g_mix), w_in.astype(BF16))

    outs, lses = [], []
    for g, (window, dilation) in enumerate(ATT_GROUPS):
        assert window // dilation == BLK and S % (dilation * BLK) == 0
        bias = _bias_table(rel_bias[:, g * HEADS_PER_GROUP:(g + 1) * HEADS_PER_GROUP], dilation)
        o, lse = _attention_group(att_in[g], bias, dilation, B, S)
        outs.append(o)
        lses.append(lse)

    causal = jnp.asarray(np.tril(np.ones((CHUNK, CHUNK), np.float32)))
    w_c = (w_spatial.astype(F32) * causal[None]).astype(BF16)
    wc2 = jnp.concatenate([w_c[0::2], w_c[1::2]], axis=2)
    bs = jnp.repeat(b_spatial.astype(F32).T, GMLP_GD, axis=1)
    h1 = _mix(h, outs, lses, uv, gl, w_att_out.astype(BF16), w_gmlp_out.astype(BF16), w_out.astype(BF16),
              wc2, bs, row(ln_v_g), row(ln_v_b))

    eidx, gate, rank, counts = _router(h1, row(g_moe), w_router.astype(F32), row(b_router))
    cnt = counts[0].astype(jnp.int32)
    blk_counts = (cnt + TM_MOE - 1) // TM_MOE
    blk_end = jnp.cumsum(blk_counts)
    pad_start = (blk_end - blk_counts) * TM_MOE
    n_blocks = T * TOP_K // TM_MOE + N_EXPERTS
    n_valid = blk_end[-1:].astype(jnp.int32)
    blk = jnp.minimum(jnp.arange(n_blocks, dtype=jnp.int32), n_valid[0] - 1)
    block_expert = jnp.minimum(jnp.sum((blk_end[None, :] <= blk[:, None]).astype(jnp.int32), axis=1), N_EXPERTS - 1)
    expert_ids = jnp.arange(N_EXPERTS, dtype=jnp.int32)
    dest = rank + jnp.sum(jnp.where(eidx[..., None] == expert_ids, pad_start, 0), axis=-1)
    dest3 = dest.reshape(T // TM_MOE, 1, TM_MOE * TOP_K)

    xs = _dispatch(dest3, h1, row(g_moe), n_blocks * TM_MOE)
    ys = _experts(block_expert, n_valid, xs, w_gate_up.astype(BF16), b_gate_up.reshape(N_EXPERTS, 1, -1).astype(F32),
                  w_down.astype(BF16), b_down.reshape(N_EXPERTS, 1, -1).astype(F32))
    return _combine(dest3, h1, gate, p_i, row(g_ple), w_ple_gate.astype(BF16), w_ple_proj.astype(BF16),
                    row(g_final), ys)


def kernel(x, p, g_mix, w_in, rel_bias, w_att_out, ln_v_g, ln_v_b, w_spatial, b_spatial, w_gmlp_out, w_out, g_moe, w_router, b_router, w_gate_up, b_gate_up, w_down, b_down, g_ple, w_ple_gate, w_ple_proj, g_final):
    B, S, D = x.shape
    depth = p.shape[0]
    assert depth == 1, "the final RMSNorm is fused into the (single) layer's last kernel"
    out = _layer(x.reshape(B * S, D), p[0].reshape(B * S, PLE_DIM), g_mix[0], w_in[0], rel_bias, w_att_out[0],
                 ln_v_g[0], ln_v_b[0], w_spatial[0], b_spatial[0], w_gmlp_out[0], w_out[0], g_moe[0], w_router[0],
                 b_router[0], w_gate_up[0], b_gate_up[0], w_down[0], b_down[0], g_ple[0], w_ple_gate[0],
                 w_ple_proj[0], g_final, B, S)
    return out.reshape(B, S, D)
```

```python
import functools

import jax
import jax.numpy as jnp
import numpy as np
from jax import lax
from jax.experimental import pallas as pl
from jax.experimental.pallas import tpu as pltpu

F32 = jnp.float32
BF16 = jnp.bfloat16

D_MODEL = 1024
HEAD_DIM = 64
ATT_GROUPS = ((128, 1), (512, 4), (2048, 16))
HEADS_PER_GROUP = 4
GROUP_W = HEADS_PER_GROUP * HEAD_DIM
N_DIL = len(ATT_GROUPS)
ATT_W = N_DIL * GROUP_W
BLK = 128
REL_BUCKETS = 32
REL_MAX_DIST = 2048
CHUNK = 128
GMLP_W = 768
GMLP_GD = 64
N_BRANCH = 2
IN_W = 3 * ATT_W + 2 * GMLP_W + N_BRANCH * D_MODEL
N_EXPERTS = 32
TOP_K = 4
D_EXPERT = D_MODEL
SWIGLU_LIMIT = 7.0
SWIGLU_ALPHA = 1.702
PLE_DIM = 256
EPS = 1e-6
MASKED = -1e30

QKV_G = 3 * GROUP_W

LANES = 128
ROW_SUBLANES = D_MODEL // LANES
assert ROW_SUBLANES == 8
MXU_N = 256
VMEM_LIMIT = 56 * 1024 * 1024

TM_PROJ = 512
TM_MOE = 256


def _cparams(*sem):
    return pltpu.CompilerParams(dimension_semantics=sem, vmem_limit_bytes=VMEM_LIMIT)


def _resident(shape):
    nd = len(shape)
    return pl.BlockSpec(shape, lambda *_: (0,) * nd, pipeline_mode=pl.Buffered(1))


def _rms(x, g):
    return x * lax.rsqrt(jnp.mean(x * x, axis=-1, keepdims=True) + EPS) * g


def _inproj_kernel(x_ref, g_ref, w_ref, a1_ref, a2_ref, a3_ref, uv_ref, gl_ref, scr):
    tm = x_ref.shape[0]
    n = _rms(x_ref[...], g_ref[...]).astype(BF16)
    att_refs = (a1_ref, a2_ref, a3_ref)
    n_att, n_uv = 3 * ATT_W // MXU_N, 2 * GMLP_W // MXU_N
    for c in range(IN_W // MXU_N):
        z = jnp.dot(n, w_ref[:, c * MXU_N:(c + 1) * MXU_N], preferred_element_type=F32)
        if c < n_att:
            which, g = divmod(c, N_DIL)
            d = ATT_GROUPS[g][1]
            dst = att_refs[g]
            if d == 1:
                dst[:, which * GROUP_W:(which + 1) * GROUP_W] = z.astype(BF16)
                continue
            scr[0] = z[:, :LANES]
            scr[1] = z[:, LANES:]
            for r in range(d):
                zr = jnp.concatenate([scr[0, pl.ds(r, tm // d, stride=d), :],
                                      scr[1, pl.ds(r, tm // d, stride=d), :]], axis=1)
                dst[:, r * QKV_G + which * GROUP_W:r * QKV_G + (which + 1) * GROUP_W] = zr.astype(BF16)
        elif c < n_att + n_uv:
            uv_ref[:, (c - n_att) * MXU_N:(c - n_att + 1) * MXU_N] = z.astype(BF16)
        else:
            gl_ref[:, (c - n_att - n_uv) * MXU_N:(c - n_att - n_uv + 1) * MXU_N] = z.astype(BF16)


def _in_proj(x2, g, w_bf):
    T = x2.shape[0]
    tm = TM_PROJ
    row = lambda w: pl.BlockSpec((tm, w), lambda i: (i, 0))
    att_spec = lambda d: pl.BlockSpec((tm // d, d * QKV_G), lambda i: (i, 0))
    dils = [d for _, d in ATT_GROUPS]
    return pl.pallas_call(
        _inproj_kernel,
        grid=(T // tm,),
        in_specs=[row(D_MODEL), _resident((1, D_MODEL)), _resident((D_MODEL, IN_W))],
        out_specs=[att_spec(d) for d in dils] + [row(2 * GMLP_W), row(N_BRANCH * D_MODEL)],
        out_shape=[jax.ShapeDtypeStruct((T // d, d * QKV_G), BF16) for d in dils]
                  + [jax.ShapeDtypeStruct((T, 2 * GMLP_W), BF16),
                     jax.ShapeDtypeStruct((T, N_BRANCH * D_MODEL), BF16)],
        scratch_shapes=[pltpu.VMEM((2, tm, LANES), F32)],
        compiler_params=_cparams("parallel"),
        name="in_proj",
    )(x2, g, w_bf)


def _t5_bucket(n):
    exact = REL_BUCKETS // 2
    nf = np.maximum(n, 1).astype(np.float32)
    large = exact + (np.log(nf / exact) / np.log(REL_MAX_DIST / exact) * (REL_BUCKETS - exact)).astype(np.int32)
    large = np.minimum(large, REL_BUCKETS - 1)
    return np.where(n < exact, n, large).astype(np.int32)


def _bias_table(rel_bias_g, dilation):
    n = 3 * BLK
    dist = 2 * BLK - 1 - np.arange(n)
    valid = (dist >= 0) & (dist <= BLK)
    bucket = _t5_bucket(np.clip(dist, 0, BLK) * dilation)
    c = jnp.where(jnp.asarray(valid)[None, :], rel_bias_g.astype(F32)[bucket].T, MASKED)
    shifted = jnp.tile(c, (1, BLK))[:, :BLK * (n - 1)].reshape(HEADS_PER_GROUP, BLK, n - 1)
    return shifted[:, :, BLK - 1:].reshape(HEADS_PER_GROUP * BLK, 2 * BLK)


def _attn_kernel(cur_ref, prev_ref, bias_ref, o_ref, lse_ref):
    rb = cur_ref.shape[1] // BLK
    starts_sequence = pl.program_id(1) == 0
    lane_head = lax.broadcasted_iota(jnp.int32, (1, GROUP_W), 1) // HEAD_DIM
    head_bf = [(lane_head == h).astype(BF16) for h in range(HEADS_PER_GROUP)]
    head_f = [(lane_head == h).astype(F32) for h in range(HEADS_PER_GROUP)]
    key_is_prev = lax.broadcasted_iota(jnp.int32, (1, 2 * BLK), 1) < BLK
    nt = (((1,), (1,)), ((), ()))
    scale = HEAD_DIM ** -0.5
    qc, kc_, vc_ = slice(0, GROUP_W), slice(GROUP_W, 2 * GROUP_W), slice(2 * GROUP_W, 3 * GROUP_W)
    for j in range(rb):
        rows = slice(j * BLK, (j + 1) * BLK)
        prev = prev_ref if j == 0 else cur_ref
        prows = slice(0, BLK) if j == 0 else slice((j - 1) * BLK, j * BLK)
        q = cur_ref[0, rows, qc]
        k = jnp.concatenate([prev[0, prows, kc_], cur_ref[0, rows, kc_]], axis=0)
        v = jnp.concatenate([prev[0, prows, vc_], cur_ref[0, rows, vc_]], axis=0)
        q_bd = jnp.concatenate([q * head_bf[h] for h in range(HEADS_PER_GROUP)], axis=0)
        s = lax.dot_general(q_bd, k, nt, preferred_element_type=F32) * scale + bias_ref[...]
        if j == 0:
            s = jnp.where(jnp.logical_and(starts_sequence, key_is_prev), MASKED, s)
        m = jnp.max(s, axis=-1, keepdims=True)
        p = jnp.exp(s - m)
        den = jnp.sum(p, axis=-1, keepdims=True)
        o = jnp.dot(p.astype(BF16), v, preferred_element_type=F32) / den
        l = m + jnp.log(den)
        out = jnp.zeros((BLK, GROUP_W), F32)
        lse = jnp.zeros((BLK, GROUP_W), F32)
        for h in range(HEADS_PER_GROUP):
            hr = slice(h * BLK, (h + 1) * BLK)
            out = out + o[hr] * head_f[h]
            lse = lse + l[hr] * head_f[h]
        o_ref[0, rows, :] = out.astype(BF16)
        lse_ref[0, rows, :] = lse


def _attention_group(a, bias, dilation, B, S):
    sd = S // dilation
    rb = min(8, sd // BLK)
    o, lse = pl.pallas_call(
        _attn_kernel,
        grid=(B, sd // (rb * BLK), dilation),
        in_specs=[pl.BlockSpec((1, rb * BLK, QKV_G), lambda b, n, r: (b, n, r)),
                  pl.BlockSpec((1, BLK, QKV_G), lambda b, n, r: (b, jnp.maximum(n * rb - 1, 0), r)),
                  _resident((HEADS_PER_GROUP * BLK, 2 * BLK))],
        out_specs=[pl.BlockSpec((1, rb * BLK, GROUP_W), lambda b, n, r: (b, n, r))] * 2,
        out_shape=[jax.ShapeDtypeStruct((B, sd, dilation * GROUP_W), BF16),
                   jax.ShapeDtypeStruct((B, sd, dilation * GROUP_W), F32)],
        compiler_params=_cparams("parallel", "parallel", "parallel"),
        name=f"attn_d{dilation}",
    )(a.reshape(B, sd, dilation * QKV_G), a.reshape(B, sd, dilation * QKV_G), bias)
    return o.reshape(B * sd, dilation * GROUP_W), lse.reshape(B * sd, dilation * GROUP_W)


def _gelu(x):
    return x * (lax.erf(x * (2.0 ** -0.5)) + 1.0) * 0.5


def _token_major(src_ref, d, scr, slot, tm):
    if d == 1:
        return src_ref[...].astype(F32)
    for r in range(d):
        piece = src_ref[:, r * GROUP_W:(r + 1) * GROUP_W].astype(F32)
        scr[slot, pl.ds(r, tm // d, stride=d), :] = piece[:, :LANES]
        scr[slot + 1, pl.ds(r, tm // d, stride=d), :] = piece[:, LANES:]
    return jnp.concatenate([scr[slot], scr[slot + 1]], axis=1)


def _mix_kernel(x_ref, o1_ref, o2_ref, o3_ref, l1_ref, l2_ref, l3_ref, uv_ref, gl_ref,
                wa_ref, wg_ref, wo_ref, wc_ref, bs_ref, lng_ref, lnb_ref, h_ref, g_scr, t_scr):
    tm = x_ref.shape[0]
    dils = [d for _, d in ATT_GROUPS]
    o1, o2, o3 = [_token_major(ref, d, t_scr, 4 * i, tm) for i, (ref, d) in enumerate(zip((o1_ref, o2_ref, o3_ref), dils))]
    l1, l2, l3 = [_token_major(ref, d, t_scr, 4 * i + 2, tm) for i, (ref, d) in enumerate(zip((l1_ref, l2_ref, l3_ref), dils))]
    lm = jnp.maximum(jnp.maximum(l1, l2), l3)
    e1, e2, e3 = jnp.exp(l1 - lm), jnp.exp(l2 - lm), jnp.exp(l3 - lm)
    att = (e1 * o1 + e2 * o2 + e3 * o3) / (e1 + e2 + e3)
    y_att = jnp.dot(att.astype(BF16), wa_ref[...], preferred_element_type=F32)

    zu = _gelu(uv_ref[:, :GMLP_W].astype(F32))
    zv = _gelu(uv_ref[:, GMLP_W:].astype(F32))
    mu = jnp.mean(zv, axis=-1, keepdims=True)
    var = jnp.mean(jnp.square(zv - mu), axis=-1, keepdims=True)
    vn = (zv - mu) * lax.rsqrt(var + EPS) * lng_ref[...] + lnb_ref[...]
    low_half = lax.broadcasted_iota(jnp.int32, (CHUNK, 2 * GMLP_GD), 1) < GMLP_GD
    for c in range(tm // CHUNK):
        rows = slice(c * CHUNK, (c + 1) * CHUNK)
        for s in range(GMLP_W // (2 * GMLP_GD)):
            cols = slice(s * 2 * GMLP_GD, (s + 1) * 2 * GMLP_GD)
            v2 = vn[rows, cols]
            rhs = jnp.concatenate([jnp.where(low_half, v2, 0.0), jnp.where(low_half, 0.0, v2)], axis=0).astype(BF16)
            mixed = jnp.dot(wc_ref[s], rhs, preferred_element_type=F32) + bs_ref[:, cols]
            g_scr[rows, cols] = (zu[rows, cols] * mixed).astype(BF16)
    y_gm = jnp.dot(g_scr[...], wg_ref[...], preferred_element_type=F32)

    gate_a = jax.nn.sigmoid(gl_ref[:, :D_MODEL].astype(F32))
    gate_g = jax.nn.sigmoid(gl_ref[:, D_MODEL:].astype(F32))
    merged = (gate_a * y_att + gate_g * y_gm).astype(BF16)
    h_ref[...] = x_ref[...] + jnp.dot(merged, wo_ref[...], preferred_element_type=F32)


def _mix(x2, outs, lses, uv, gl, wa, wg, wo, wc2, bs, lng, lnb):
    T = x2.shape[0]
    tm = TM_PROJ
    row = lambda w: pl.BlockSpec((tm, w), lambda i: (i, 0))
    att = [pl.BlockSpec((tm // d, d * GROUP_W), lambda i: (i, 0)) for _, d in ATT_GROUPS]
    return pl.pallas_call(
        _mix_kernel,
        grid=(T // tm,),
        in_specs=[row(D_MODEL)] + att + att + [row(2 * GMLP_W), row(N_BRANCH * D_MODEL),
                  _resident(wa.shape), _resident(wg.shape), _resident(wo.shape), _resident(wc2.shape),
                  _resident(bs.shape), _resident(lng.shape), _resident(lnb.shape)],
        out_specs=row(D_MODEL),
        out_shape=jax.ShapeDtypeStruct((T, D_MODEL), F32),
        scratch_shapes=[pltpu.VMEM((tm, GMLP_W), BF16), pltpu.VMEM((4 * N_DIL, tm, LANES), F32)],
        compiler_params=_cparams("parallel"),
        name="mix",
    )(x2, *outs, *lses, uv, gl, wa, wg, wo, wc2, bs, lng, lnb)


def _router_kernel(h_ref, g_ref, wr_ref, br_ref, eidx_ref, gate_ref, rank_ref, cnt_ref, carry):
    tm = h_ref.shape[0]

    @pl.when(pl.program_id(0) == 0)
    def _():
        carry[...] = jnp.zeros_like(carry)

    hn = _rms(h_ref[...], g_ref[...])
    logits = jnp.dot(hn, wr_ref[...], preferred_element_type=F32, precision=lax.Precision.HIGHEST) + br_ref[...]
    lane = lax.broadcasted_iota(jnp.int32, (tm, N_EXPERTS), 1)
    vals, hots = [], []
    l = logits
    for k in range(TOP_K):
        m = jnp.max(l, axis=-1, keepdims=True)
        idx = jnp.min(jnp.where(l == m, lane, N_EXPERTS), axis=-1, keepdims=True)
        hot = lane == idx
        eidx_ref[:, k:k + 1] = idx
        vals.append(m)
        hots.append(hot)
        l = jnp.where(hot, -jnp.inf, l)
    ex = [jnp.exp(v - vals[0]) for v in vals]
    tot = ex[0] + ex[1] + ex[2] + ex[3]
    for k in range(TOP_K):
        gate_ref[:, k:k + 1] = ex[k] / tot
    multi = jnp.zeros((tm, N_EXPERTS), F32)
    for hot in hots:
        multi = multi + hot.astype(F32)
    r = lax.broadcasted_iota(jnp.int32, (tm, tm), 0)
    c = lax.broadcasted_iota(jnp.int32, (tm, tm), 1)
    strict_lower = (c < r).astype(BF16)
    before = jnp.dot(strict_lower, multi.astype(BF16), preferred_element_type=F32) + carry[...]
    for k in range(TOP_K):
        rank_ref[:, k:k + 1] = jnp.sum(jnp.where(hots[k], before, 0.0), axis=-1, keepdims=True).astype(jnp.int32)
    carry[...] += jnp.sum(multi, axis=0, keepdims=True)
    cnt_ref[...] = carry[...]


def _router(h1, g, wr, br):
    T = h1.shape[0]
    tm = TM_PROJ
    col4 = pl.BlockSpec((tm, TOP_K), lambda i: (i, 0))
    return pl.pallas_call(
        _router_kernel,
        grid=(T // tm,),
        in_specs=[pl.BlockSpec((tm, D_MODEL), lambda i: (i, 0)), _resident((1, D_MODEL)),
                  _resident((D_MODEL, N_EXPERTS)), _resident((1, N_EXPERTS))],
        out_specs=[col4, col4, col4, pl.BlockSpec((1, N_EXPERTS), lambda i: (0, 0))],
        out_shape=[jax.ShapeDtypeStruct((T, TOP_K), jnp.int32), jax.ShapeDtypeStruct((T, TOP_K), F32),
                   jax.ShapeDtypeStruct((T, TOP_K), jnp.int32), jax.ShapeDtypeStruct((1, N_EXPERTS), F32)],
        scratch_shapes=[pltpu.VMEM((1, N_EXPERTS), F32)],
        compiler_params=_cparams("arbitrary"),
        name="router",
    )(h1, g, wr, br)


def _to_row_tiles(ref, lead, value):
    n = value.shape[0]
    for c in range(ROW_SUBLANES):
        ref[(*lead, pl.ds(c, n, stride=ROW_SUBLANES), slice(None))] = value[:, c * LANES:(c + 1) * LANES]


def _from_row_tiles(ref, lead, first, n):
    return jnp.concatenate(
        [ref[(*lead, pl.ds(first * ROW_SUBLANES + c, n, stride=ROW_SUBLANES), slice(None))] for c in range(ROW_SUBLANES)],
        axis=1)


def _tile_rows(idx, n=1):
    return pl.ds(pl.multiple_of(idx * ROW_SUBLANES, ROW_SUBLANES), n * ROW_SUBLANES)


def _dispatch_kernel(last_ref, nv_ref, dest_ref, h_ref, g_ref, xs_ref, buf, sem, zero_sem):
    tm = h_ref.shape[0]
    n_blocks = xs_ref.shape[0] // (tm * ROW_SUBLANES)
    i = pl.program_id(0)
    slot = lax.rem(i, 2)

    @pl.when(i == 0)
    def _():
        buf[1] = jnp.zeros(buf.shape[1:], F32)

        def zero_block(b):
            pltpu.make_async_copy(buf.at[1], xs_ref.at[_tile_rows(b * tm, tm)], zero_sem).start()

        def zero_done():
            pltpu.make_async_copy(buf.at[1], xs_ref.at[_tile_rows(0, tm)], zero_sem).wait()

        for e in range(N_EXPERTS):
            zero_block(last_ref[e])
        lax.fori_loop(nv_ref[0], n_blocks, lambda b, c: (zero_block(b), c)[1], 0)
        for e in range(N_EXPERTS):
            zero_done()
        lax.fori_loop(nv_ref[0], n_blocks, lambda b, c: (zero_done(), c)[1], 0)

    _to_row_tiles(buf, (slot,), _rms(h_ref[...], g_ref[...]))

    def issue(t, carry):
        for k in range(TOP_K):
            d = dest_ref[0, 0, t * TOP_K + k]
            pltpu.make_async_copy(buf.at[slot, _tile_rows(t)], xs_ref.at[_tile_rows(d)],
                                  sem.at[slot]).start(priority=k % 2)
        return carry

    lax.fori_loop(0, tm, issue, 0, unroll=8)

    def wait_slot(s):
        for _ in range(TOP_K):
            pltpu.make_async_copy(buf.at[s], xs_ref.at[_tile_rows(0, tm)], sem.at[s]).wait()

    @pl.when(i > 0)
    def _():
        wait_slot(1 - slot)

    @pl.when(i == pl.num_programs(0) - 1)
    def _():
        wait_slot(slot)


def _dispatch(last_block, n_valid, dest3, h1, g, n_slots):
    T = h1.shape[0]
    tm = TM_MOE
    grid_spec = pltpu.PrefetchScalarGridSpec(
        num_scalar_prefetch=2,
        grid=(T // tm,),
        in_specs=[pl.BlockSpec((1, 1, TOP_K * tm), lambda i, lb, nv: (i, 0, 0), memory_space=pltpu.SMEM),
                  pl.BlockSpec((tm, D_MODEL), lambda i, lb, nv: (i, 0)),
                  pl.BlockSpec((1, D_MODEL), lambda i, lb, nv: (0, 0), pipeline_mode=pl.Buffered(1))],
        out_specs=pl.BlockSpec(memory_space=pl.ANY),
        scratch_shapes=[pltpu.VMEM((2, tm * ROW_SUBLANES, LANES), F32), pltpu.SemaphoreType.DMA((2,)),
                        pltpu.SemaphoreType.DMA(())],
    )
    return pl.pallas_call(
        _dispatch_kernel,
        grid_spec=grid_spec,
        out_shape=jax.ShapeDtypeStruct((n_slots * ROW_SUBLANES, LANES), F32),
        compiler_params=_cparams("arbitrary"),
        name="dispatch",
    )(last_block, n_valid, dest3, h1, g)


def _experts_kernel(be_ref, nv_ref, x_ref, wgu_ref, bgu_ref, wd_ref, bd_ref, y_ref):
    del be_ref
    tm = x_ref.shape[0] // ROW_SUBLANES

    @pl.when(pl.program_id(0) < nv_ref[0])
    def _():
        x = _from_row_tiles(x_ref, (), 0, tm).astype(BF16)
        gu = jnp.dot(x, wgu_ref[0], preferred_element_type=F32) + bgu_ref[0]
        glu = jnp.minimum(gu[:, :D_EXPERT], SWIGLU_LIMIT)
        lin = jnp.clip(gu[:, D_EXPERT:], -SWIGLU_LIMIT, SWIGLU_LIMIT)
        act = glu * jax.nn.sigmoid(SWIGLU_ALPHA * glu) * (lin + 1.0)
        _to_row_tiles(y_ref, (), jnp.dot(act.astype(BF16), wd_ref[0], preferred_element_type=F32) + bd_ref[0])

    @pl.when(pl.program_id(0) >= nv_ref[0])
    def _():
        y_ref[...] = jnp.zeros_like(y_ref)


def _experts(block_expert, n_valid, xs, wgu, bgu, wd, bd):
    tm = TM_MOE
    n_blocks = xs.shape[0] // (tm * ROW_SUBLANES)
    live = lambda b, be, nv: jnp.maximum(jnp.minimum(b, nv[0] - 1), 0)
    grid_spec = pltpu.PrefetchScalarGridSpec(
        num_scalar_prefetch=2,
        grid=(n_blocks,),
        in_specs=[pl.BlockSpec((tm * ROW_SUBLANES, LANES), lambda b, be, nv: (live(b, be, nv), 0)),
                  pl.BlockSpec((1, D_MODEL, 2 * D_EXPERT), lambda b, be, nv: (be[b], 0, 0)),
                  pl.BlockSpec((1, 1, 2 * D_EXPERT), lambda b, be, nv: (be[b], 0, 0)),
                  pl.BlockSpec((1, D_EXPERT, D_MODEL), lambda b, be, nv: (be[b], 0, 0)),
                  pl.BlockSpec((1, 1, D_MODEL), lambda b, be, nv: (be[b], 0, 0))],
        out_specs=pl.BlockSpec((tm * ROW_SUBLANES, LANES), lambda b, be, nv: (b, 0)),
    )
    return pl.pallas_call(
        _experts_kernel,
        grid_spec=grid_spec,
        out_shape=jax.ShapeDtypeStruct(xs.shape, F32),
        compiler_params=_cparams("arbitrary"),
        name="experts",
    )(block_expert, n_valid, xs, wgu, bgu, wd, bd)


def _combine_kernel(dest_ref, next_dest_ref, h_ref, gate_ref, p_ref, gp_ref, wpg_ref, wpp_ref, gf_ref, ys_ref,
                    o_ref, buf, sem):
    tm = h_ref.shape[0]
    i = pl.program_id(0)
    slot = lax.rem(i, 2)

    def gather(dref, s):
        def issue(t, carry):
            for k in range(TOP_K):
                d = dref[0, 0, t * TOP_K + k]
                pltpu.make_async_copy(ys_ref.at[_tile_rows(d)], buf.at[s, _tile_rows(k * tm + t)],
                                      sem.at[s]).start(priority=k % 2)
            return carry
        lax.fori_loop(0, tm, issue, 0, unroll=8)

    @pl.when(i == 0)
    def _():
        gather(dest_ref, 0)

    @pl.when(i + 1 < pl.num_programs(0))
    def _():
        gather(next_dest_ref, 1 - slot)

    proj = jnp.dot(p_ref[...].astype(BF16), wpp_ref[...], preferred_element_type=F32)
    for _ in range(TOP_K):
        pltpu.make_async_copy(ys_ref.at[_tile_rows(0, tm)], buf.at[slot, _tile_rows(0, tm)], sem.at[slot]).wait()
    h = h_ref[...]
    for k in range(TOP_K):
        h = h + gate_ref[:, k:k + 1] * _from_row_tiles(buf, (slot,), k * tm, tm)
    ple_gate = jax.nn.sigmoid(jnp.dot(_rms(h, gp_ref[...]).astype(BF16), wpg_ref[...], preferred_element_type=F32))
    h = h + ple_gate * proj
    o_ref[...] = _rms(h, gf_ref[...])


def _combine(dest3, h1, gate, p2, gp, wpg, wpp, gf, ys):
    T = h1.shape[0]
    tm = TM_MOE
    n_tiles = T // tm
    row = lambda w: pl.BlockSpec((tm, w), lambda i: (i, 0))
    dest_spec = lambda ahead: pl.BlockSpec((1, 1, TOP_K * tm), lambda i: (jnp.minimum(i + ahead, n_tiles - 1), 0, 0),
                                           memory_space=pltpu.SMEM)
    return pl.pallas_call(
        _combine_kernel,
        grid=(n_tiles,),
        in_specs=[dest_spec(0), dest_spec(1),
                  row(D_MODEL), row(TOP_K), row(PLE_DIM), _resident((1, D_MODEL)),
                  _resident((D_MODEL, D_MODEL)), _resident((PLE_DIM, D_MODEL)), _resident((1, D_MODEL)),
                  pl.BlockSpec(memory_space=pl.ANY)],
        out_specs=row(D_MODEL),
        out_shape=jax.ShapeDtypeStruct((T, D_MODEL), F32),
        scratch_shapes=[pltpu.VMEM((2, TOP_K * tm * ROW_SUBLANES, LANES), F32), pltpu.SemaphoreType.DMA((2,))],
        compiler_params=_cparams("arbitrary"),
        name="combine",
    )(dest3, dest3, h1, gate, p2, gp, wpg, wpp, gf, ys)


def _layer(h, p_i, g_mix, w_in, rel_bias, w_att_out, ln_v_g, ln_v_b, w_spatial, b_spatial, w_gmlp_out, w_out,
           g_moe, w_router, b_router, w_gate_up, b_gate_up, w_down, b_down, g_ple, w_ple_gate, w_ple_proj,
           g_final, B, S):
    T = B * S
    row = lambda v: v.reshape(1, -1).astype(F32)

    *att_in, uv, gl = _in_proj(h, row(g_mix), w_in.astype(BF16))

    outs, lses = [], []
    for g, (window, dilation) in enumerate(ATT_GROUPS):
        assert window // dilation == BLK and S % (dilation * BLK) == 0
        bias = _bias_table(rel_bias[:, g * HEADS_PER_GROUP:(g + 1) * HEADS_PER_GROUP], dilation)
        o, lse = _attention_group(att_in[g], bias, dilation, B, S)
        outs.append(o)
        lses.append(lse)

    causal = jnp.asarray(np.tril(np.ones((CHUNK, CHUNK), np.float32)))
    w_c = (w_spatial.astype(F32) * causal[None]).astype(BF16)
    wc2 = jnp.concatenate([w_c[0::2], w_c[1::2]], axis=2)
    bs = jnp.repeat(b_spatial.astype(F32).T, GMLP_GD, axis=1)
    h1 = _mix(h, outs, lses, uv, gl, w_att_out.astype(BF16), w_gmlp_out.astype(BF16), w_out.astype(BF16),
              wc2, bs, row(ln_v_g), row(ln_v_b))

    eidx, gate, rank, counts = _router(h1, row(g_moe), w_router.astype(F32), row(b_router))
    cnt = counts[0].astype(jnp.int32)
    blk_counts = (cnt + TM_MOE - 1) // TM_MOE
    blk_end = jnp.cumsum(blk_counts)
    pad_start = (blk_end - blk_counts) * TM_MOE
    n_blocks = T * TOP_K // TM_MOE + N_EXPERTS
    n_valid = blk_end[-1:].astype(jnp.int32)
    blk = jnp.minimum(jnp.arange(n_blocks, dtype=jnp.int32), n_valid[0] - 1)
    block_expert = jnp.minimum(jnp.sum((blk_end[None, :] <= blk[:, None]).astype(jnp.int32), axis=1), N_EXPERTS - 1)
    expert_ids = jnp.arange(N_EXPERTS, dtype=jnp.int32)
    dest = rank + jnp.sum(jnp.where(eidx[..., None] == expert_ids, pad_start, 0), axis=-1)
    dest3 = dest.reshape(T // TM_MOE, 1, TM_MOE * TOP_K)

    last_block = jnp.maximum(blk_end - 1, 0).astype(jnp.int32)
    xs = _dispatch(last_block, n_valid, dest3, h1, row(g_moe), n_blocks * TM_MOE)
    ys = _experts(block_expert, n_valid, xs, w_gate_up.astype(BF16), b_gate_up.reshape(N_EXPERTS, 1, -1).astype(F32),
                  w_down.astype(BF16), b_down.reshape(N_EXPERTS, 1, -1).astype(F32))
    return _combine(dest3, h1, gate, p_i, row(g_ple), w_ple_gate.astype(BF16), w_ple_proj.astype(BF16),
                    row(g_final), ys)


def kernel(x, p, g_mix, w_in, rel_bias, w_att_out, ln_v_g, ln_v_b, w_spatial, b_spatial, w_gmlp_out, w_out, g_moe, w_router, b_router, w_gate_up, b_gate_up, w_down, b_down, g_ple, w_ple_gate, w_ple_proj, g_final):
    B, S, D = x.shape
    depth = p.shape[0]
    assert depth == 1, "the final RMSNorm is fused into the (single) layer's last kernel"
    out = _layer(x.reshape(B * S, D), p[0].reshape(B * S, PLE_DIM), g_mix[0], w_in[0], rel_bias, w_att_out[0],
                 ln_v_g[0], ln_v_b[0], w_spatial[0], b_spatial[0], w_gmlp_out[0], w_out[0], g_moe[0], w_router[0],
                 b_router[0], w_gate_up[0], b_gate_up[0], w_down[0], b_down[0], g_ple[0], w_ple_gate[0],
                 w_ple_proj[0], g_final, B, S)
    return out.reshape(B, S, D)
```

```python
import functools

import jax
import jax.numpy as jnp
import numpy as np
from jax import lax
from jax.experimental import pallas as pl
from jax.experimental.pallas import tpu as pltpu

F32 = jnp.float32
BF16 = jnp.bfloat16

D_MODEL = 1024
HEAD_DIM = 64
ATT_GROUPS = ((128, 1), (512, 4), (2048, 16))
HEADS_PER_GROUP = 4
GROUP_W = HEADS_PER_GROUP * HEAD_DIM
N_DIL = len(ATT_GROUPS)
ATT_W = N_DIL * GROUP_W
BLK = 128
REL_BUCKETS = 32
REL_MAX_DIST = 2048
CHUNK = 128
GMLP_W = 768
GMLP_GD = 64
N_BRANCH = 2
IN_W = 3 * ATT_W + 2 * GMLP_W + N_BRANCH * D_MODEL
N_EXPERTS = 32
TOP_K = 4
D_EXPERT = D_MODEL
SWIGLU_LIMIT = 7.0
SWIGLU_ALPHA = 1.702
PLE_DIM = 256
EPS = 1e-6
MASKED = -1e30

QKV_G = 3 * GROUP_W

LANES = 128
ROW_SUBLANES = D_MODEL // LANES
assert ROW_SUBLANES == 8
MXU_N = 256
VMEM_LIMIT = 56 * 1024 * 1024

TM_PROJ = 512
TM_TOK = 256
TM_EXP = 512
assert TM_EXP % TM_TOK == 0


def _cparams(*sem):
    return pltpu.CompilerParams(dimension_semantics=sem, vmem_limit_bytes=VMEM_LIMIT)


def _resident(shape):
    nd = len(shape)
    return pl.BlockSpec(shape, lambda *_: (0,) * nd, pipeline_mode=pl.Buffered(1))


def _rms(x, g):
    return x * lax.rsqrt(jnp.mean(x * x, axis=-1, keepdims=True) + EPS) * g


def _inproj_kernel(x_ref, g_ref, w_ref, a1_ref, a2_ref, a3_ref, uv_ref, gl_ref, scr):
    tm = x_ref.shape[0]
    n = _rms(x_ref[...], g_ref[...]).astype(BF16)
    att_refs = (a1_ref, a2_ref, a3_ref)
    n_att, n_uv = 3 * ATT_W // MXU_N, 2 * GMLP_W // MXU_N
    for c in range(IN_W // MXU_N):
        z = jnp.dot(n, w_ref[:, c * MXU_N:(c + 1) * MXU_N], preferred_element_type=F32)
        if c < n_att:
            which, g = divmod(c, N_DIL)
            d = ATT_GROUPS[g][1]
            dst = att_refs[g]
            if d == 1:
                dst[:, which * GROUP_W:(which + 1) * GROUP_W] = z.astype(BF16)
                continue
            scr[0] = z[:, :LANES]
            scr[1] = z[:, LANES:]
            for r in range(d):
                zr = jnp.concatenate([scr[0, pl.ds(r, tm // d, stride=d), :],
                                      scr[1, pl.ds(r, tm // d, stride=d), :]], axis=1)
                dst[:, r * QKV_G + which * GROUP_W:r * QKV_G + (which + 1) * GROUP_W] = zr.astype(BF16)
        elif c < n_att + n_uv:
            uv_ref[:, (c - n_att) * MXU_N:(c - n_att + 1) * MXU_N] = z.astype(BF16)
        else:
            gl_ref[:, (c - n_att - n_uv) * MXU_N:(c - n_att - n_uv + 1) * MXU_N] = z.astype(BF16)


def _in_proj(x2, g, w_bf):
    T = x2.shape[0]
    tm = TM_PROJ
    row = lambda w: pl.BlockSpec((tm, w), lambda i: (i, 0))
    att_spec = lambda d: pl.BlockSpec((tm // d, d * QKV_G), lambda i: (i, 0))
    dils = [d for _, d in ATT_GROUPS]
    return pl.pallas_call(
        _inproj_kernel,
        grid=(T // tm,),
        in_specs=[row(D_MODEL), _resident((1, D_MODEL)), _resident((D_MODEL, IN_W))],
        out_specs=[att_spec(d) for d in dils] + [row(2 * GMLP_W), row(N_BRANCH * D_MODEL)],
        out_shape=[jax.ShapeDtypeStruct((T // d, d * QKV_G), BF16) for d in dils]
                  + [jax.ShapeDtypeStruct((T, 2 * GMLP_W), BF16),
                     jax.ShapeDtypeStruct((T, N_BRANCH * D_MODEL), BF16)],
        scratch_shapes=[pltpu.VMEM((2, tm, LANES), F32)],
        compiler_params=_cparams("parallel"),
        name="in_proj",
    )(x2, g, w_bf)


def _t5_bucket(n):
    exact = REL_BUCKETS // 2
    nf = np.maximum(n, 1).astype(np.float32)
    large = exact + (np.log(nf / exact) / np.log(REL_MAX_DIST / exact) * (REL_BUCKETS - exact)).astype(np.int32)
    large = np.minimum(large, REL_BUCKETS - 1)
    return np.where(n < exact, n, large).astype(np.int32)


def _bias_table(rel_bias_g, dilation):
    n = 3 * BLK
    dist = 2 * BLK - 1 - np.arange(n)
    valid = (dist >= 0) & (dist <= BLK)
    bucket = _t5_bucket(np.clip(dist, 0, BLK) * dilation)
    c = jnp.where(jnp.asarray(valid)[None, :], rel_bias_g.astype(F32)[bucket].T, MASKED)
    shifted = jnp.tile(c, (1, BLK))[:, :BLK * (n - 1)].reshape(HEADS_PER_GROUP, BLK, n - 1)
    return shifted[:, :, BLK - 1:].reshape(HEADS_PER_GROUP * BLK, 2 * BLK)


def _attn_kernel(cur_ref, prev_ref, bias_ref, o_ref, lse_ref):
    rb = cur_ref.shape[1] // BLK
    starts_sequence = pl.program_id(1) == 0
    lane_head = lax.broadcasted_iota(jnp.int32, (1, GROUP_W), 1) // HEAD_DIM
    head_bf = [(lane_head == h).astype(BF16) for h in range(HEADS_PER_GROUP)]
    head_f = [(lane_head == h).astype(F32) for h in range(HEADS_PER_GROUP)]
    key_is_prev = lax.broadcasted_iota(jnp.int32, (1, 2 * BLK), 1) < BLK
    nt = (((1,), (1,)), ((), ()))
    scale = HEAD_DIM ** -0.5
    qc, kc_, vc_ = slice(0, GROUP_W), slice(GROUP_W, 2 * GROUP_W), slice(2 * GROUP_W, 3 * GROUP_W)
    for j in range(rb):
        rows = slice(j * BLK, (j + 1) * BLK)
        prev = prev_ref if j == 0 else cur_ref
        prows = slice(0, BLK) if j == 0 else slice((j - 1) * BLK, j * BLK)
        q = cur_ref[0, rows, qc]
        k = jnp.concatenate([prev[0, prows, kc_], cur_ref[0, rows, kc_]], axis=0)
        v = jnp.concatenate([prev[0, prows, vc_], cur_ref[0, rows, vc_]], axis=0)
        q_bd = jnp.concatenate([q * head_bf[h] for h in range(HEADS_PER_GROUP)], axis=0)
        s = lax.dot_general(q_bd, k, nt, preferred_element_type=F32) * scale + bias_ref[...]
        if j == 0:
            s = jnp.where(jnp.logical_and(starts_sequence, key_is_prev), MASKED, s)
        m = jnp.max(s, axis=-1, keepdims=True)
        p = jnp.exp(s - m)
        den = jnp.sum(p, axis=-1, keepdims=True)
        o = jnp.dot(p.astype(BF16), v, preferred_element_type=F32) / den
        l = m + jnp.log(den)
        out = jnp.zeros((BLK, GROUP_W), F32)
        lse = jnp.zeros((BLK, GROUP_W), F32)
        for h in range(HEADS_PER_GROUP):
            hr = slice(h * BLK, (h + 1) * BLK)
            out = out + o[hr] * head_f[h]
            lse = lse + l[hr] * head_f[h]
        o_ref[0, rows, :] = out.astype(BF16)
        lse_ref[0, rows, :] = lse


def _attention_group(a, bias, dilation, B, S):
    sd = S // dilation
    rb = min(8, sd // BLK)
    o, lse = pl.pallas_call(
        _attn_kernel,
        grid=(B, sd // (rb * BLK), dilation),
        in_specs=[pl.BlockSpec((1, rb * BLK, QKV_G), lambda b, n, r: (b, n, r)),
                  pl.BlockSpec((1, BLK, QKV_G), lambda b, n, r: (b, jnp.maximum(n * rb - 1, 0), r)),
                  _resident((HEADS_PER_GROUP * BLK, 2 * BLK))],
        out_specs=[pl.BlockSpec((1, rb * BLK, GROUP_W), lambda b, n, r: (b, n, r))] * 2,
        out_shape=[jax.ShapeDtypeStruct((B, sd, dilation * GROUP_W), BF16),
                   jax.ShapeDtypeStruct((B, sd, dilation * GROUP_W), F32)],
        compiler_params=_cparams("parallel", "parallel", "parallel"),
        name=f"attn_d{dilation}",
    )(a.reshape(B, sd, dilation * QKV_G), a.reshape(B, sd, dilation * QKV_G), bias)
    return o.reshape(B * sd, dilation * GROUP_W), lse.reshape(B * sd, dilation * GROUP_W)


def _gelu(x):
    return x * (lax.erf(x * (2.0 ** -0.5)) + 1.0) * 0.5


def _token_major(src_ref, d, scr, slot, tm):
    if d == 1:
        return src_ref[...].astype(F32)
    for r in range(d):
        piece = src_ref[:, r * GROUP_W:(r + 1) * GROUP_W].astype(F32)
        scr[slot, pl.ds(r, tm // d, stride=d), :] = piece[:, :LANES]
        scr[slot + 1, pl.ds(r, tm // d, stride=d), :] = piece[:, LANES:]
    return jnp.concatenate([scr[slot], scr[slot + 1]], axis=1)


def _mix_kernel(x_ref, o1_ref, o2_ref, o3_ref, l1_ref, l2_ref, l3_ref, uv_ref, gl_ref,
                wa_ref, wg_ref, wo_ref, wc_ref, bs_ref, lng_ref, lnb_ref, h_ref, g_scr, t_scr):
    tm = x_ref.shape[0]
    dils = [d for _, d in ATT_GROUPS]
    o1, o2, o3 = [_token_major(ref, d, t_scr, 4 * i, tm) for i, (ref, d) in enumerate(zip((o1_ref, o2_ref, o3_ref), dils))]
    l1, l2, l3 = [_token_major(ref, d, t_scr, 4 * i + 2, tm) for i, (ref, d) in enumerate(zip((l1_ref, l2_ref, l3_ref), dils))]
    lm = jnp.maximum(jnp.maximum(l1, l2), l3)
    e1, e2, e3 = jnp.exp(l1 - lm), jnp.exp(l2 - lm), jnp.exp(l3 - lm)
    att = (e1 * o1 + e2 * o2 + e3 * o3) / (e1 + e2 + e3)
    y_att = jnp.dot(att.astype(BF16), wa_ref[...], preferred_element_type=F32)

    zu = _gelu(uv_ref[:, :GMLP_W].astype(F32))
    zv = _gelu(uv_ref[:, GMLP_W:].astype(F32))
    mu = jnp.mean(zv, axis=-1, keepdims=True)
    var = jnp.mean(jnp.square(zv - mu), axis=-1, keepdims=True)
    vn = (zv - mu) * lax.rsqrt(var + EPS) * lng_ref[...] + lnb_ref[...]
    low_half = lax.broadcasted_iota(jnp.int32, (CHUNK, 2 * GMLP_GD), 1) < GMLP_GD
    for c in range(tm // CHUNK):
        rows = slice(c * CHUNK, (c + 1) * CHUNK)
        for s in range(GMLP_W // (2 * GMLP_GD)):
            cols = slice(s * 2 * GMLP_GD, (s + 1) * 2 * GMLP_GD)
            v2 = vn[rows, cols]
            rhs = jnp.concatenate([jnp.where(low_half, v2, 0.0), jnp.where(low_half, 0.0, v2)], axis=0).astype(BF16)
            mixed = jnp.dot(wc_ref[s], rhs, preferred_element_type=F32) + bs_ref[:, cols]
            g_scr[rows, cols] = (zu[rows, cols] * mixed).astype(BF16)
    y_gm = jnp.dot(g_scr[...], wg_ref[...], preferred_element_type=F32)

    gate_a = jax.nn.sigmoid(gl_ref[:, :D_MODEL].astype(F32))
    gate_g = jax.nn.sigmoid(gl_ref[:, D_MODEL:].astype(F32))
    merged = (gate_a * y_att + gate_g * y_gm).astype(BF16)
    h_ref[...] = x_ref[...] + jnp.dot(merged, wo_ref[...], preferred_element_type=F32)


def _mix(x2, outs, lses, uv, gl, wa, wg, wo, wc2, bs, lng, lnb):
    T = x2.shape[0]
    tm = TM_PROJ
    row = lambda w: pl.BlockSpec((tm, w), lambda i: (i, 0))
    att = [pl.BlockSpec((tm // d, d * GROUP_W), lambda i: (i, 0)) for _, d in ATT_GROUPS]
    return pl.pallas_call(
        _mix_kernel,
        grid=(T // tm,),
        in_specs=[row(D_MODEL)] + att + att + [row(2 * GMLP_W), row(N_BRANCH * D_MODEL),
                  _resident(wa.shape), _resident(wg.shape), _resident(wo.shape), _resident(wc2.shape),
                  _resident(bs.shape), _resident(lng.shape), _resident(lnb.shape)],
        out_specs=row(D_MODEL),
        out_shape=jax.ShapeDtypeStruct((T, D_MODEL), F32),
        scratch_shapes=[pltpu.VMEM((tm, GMLP_W), BF16), pltpu.VMEM((4 * N_DIL, tm, LANES), F32)],
        compiler_params=_cparams("parallel"),
        name="mix",
    )(x2, *outs, *lses, uv, gl, wa, wg, wo, wc2, bs, lng, lnb)


def _router_kernel(h_ref, g_ref, wr_ref, br_ref, eidx_ref, gate_ref, rank_ref, cnt_ref, carry):
    tm = h_ref.shape[0]

    @pl.when(pl.program_id(0) == 0)
    def _():
        carry[...] = jnp.zeros_like(carry)

    hn = _rms(h_ref[...], g_ref[...])
    logits = jnp.dot(hn, wr_ref[...], preferred_element_type=F32, precision=lax.Precision.HIGHEST) + br_ref[...]
    lane = lax.broadcasted_iota(jnp.int32, (tm, N_EXPERTS), 1)
    vals, hots = [], []
    l = logits
    for k in range(TOP_K):
        m = jnp.max(l, axis=-1, keepdims=True)
        idx = jnp.min(jnp.where(l == m, lane, N_EXPERTS), axis=-1, keepdims=True)
        hot = lane == idx
        eidx_ref[:, k:k + 1] = idx
        vals.append(m)
        hots.append(hot)
        l = jnp.where(hot, -jnp.inf, l)
    ex = [jnp.exp(v - vals[0]) for v in vals]
    tot = ex[0] + ex[1] + ex[2] + ex[3]
    for k in range(TOP_K):
        gate_ref[:, k:k + 1] = ex[k] / tot
    multi = jnp.zeros((tm, N_EXPERTS), F32)
    for hot in hots:
        multi = multi + hot.astype(F32)
    r = lax.broadcasted_iota(jnp.int32, (tm, tm), 0)
    c = lax.broadcasted_iota(jnp.int32, (tm, tm), 1)
    strict_lower = (c < r).astype(BF16)
    before = jnp.dot(strict_lower, multi.astype(BF16), preferred_element_type=F32) + carry[...]
    for k in range(TOP_K):
        rank_ref[:, k:k + 1] = jnp.sum(jnp.where(hots[k], before, 0.0), axis=-1, keepdims=True).astype(jnp.int32)
    carry[...] += jnp.sum(multi, axis=0, keepdims=True)
    cnt_ref[...] = carry[...]


def _router(h1, g, wr, br):
    T = h1.shape[0]
    tm = TM_PROJ
    col4 = pl.BlockSpec((tm, TOP_K), lambda i: (i, 0))
    return pl.pallas_call(
        _router_kernel,
        grid=(T // tm,),
        in_specs=[pl.BlockSpec((tm, D_MODEL), lambda i: (i, 0)), _resident((1, D_MODEL)),
                  _resident((D_MODEL, N_EXPERTS)), _resident((1, N_EXPERTS))],
        out_specs=[col4, col4, col4, pl.BlockSpec((1, N_EXPERTS), lambda i: (0, 0))],
        out_shape=[jax.ShapeDtypeStruct((T, TOP_K), jnp.int32), jax.ShapeDtypeStruct((T, TOP_K), F32),
                   jax.ShapeDtypeStruct((T, TOP_K), jnp.int32), jax.ShapeDtypeStruct((1, N_EXPERTS), F32)],
        scratch_shapes=[pltpu.VMEM((1, N_EXPERTS), F32)],
        compiler_params=_cparams("arbitrary"),
        name="router",
    )(h1, g, wr, br)


def _to_row_tiles(ref, lead, value):
    n = value.shape[0]
    for c in range(ROW_SUBLANES):
        ref[(*lead, pl.ds(c, n, stride=ROW_SUBLANES), slice(None))] = value[:, c * LANES:(c + 1) * LANES]


def _from_row_tiles(ref, lead, first, n):
    return jnp.concatenate(
        [ref[(*lead, pl.ds(first * ROW_SUBLANES + c, n, stride=ROW_SUBLANES), slice(None))] for c in range(ROW_SUBLANES)],
        axis=1)


def _tile_rows(idx, n=1):
    return pl.ds(pl.multiple_of(idx * ROW_SUBLANES, ROW_SUBLANES), n * ROW_SUBLANES)


def _dispatch_kernel(last_ref, nv_ref, dest_ref, h_ref, g_ref, xs_ref, buf, sem, zero_sem):
    tm = h_ref.shape[0]
    n_blocks = xs_ref.shape[0] // (TM_EXP * ROW_SUBLANES)
    i = pl.program_id(0)
    slot = lax.rem(i, 2)

    @pl.when(i == 0)
    def _():
        buf[1] = jnp.zeros(buf.shape[1:], F32)

        def zero_block(b):
            for part in range(TM_EXP // tm):
                pltpu.make_async_copy(buf.at[1], xs_ref.at[_tile_rows(b * TM_EXP + part * tm, tm)], zero_sem).start()

        def zero_done():
            for part in range(TM_EXP // tm):
                pltpu.make_async_copy(buf.at[1], xs_ref.at[_tile_rows(0, tm)], zero_sem).wait()

        for e in range(N_EXPERTS):
            zero_block(last_ref[e])
        lax.fori_loop(nv_ref[0], n_blocks, lambda b, c: (zero_block(b), c)[1], 0)
        for e in range(N_EXPERTS):
            zero_done()
        lax.fori_loop(nv_ref[0], n_blocks, lambda b, c: (zero_done(), c)[1], 0)

    _to_row_tiles(buf, (slot,), _rms(h_ref[...], g_ref[...]))

    def issue(t, carry):
        for k in range(TOP_K):
            d = dest_ref[0, 0, t * TOP_K + k]
            pltpu.make_async_copy(buf.at[slot, _tile_rows(t)], xs_ref.at[_tile_rows(d)],
                                  sem.at[slot]).start(priority=k % 2)
        return carry

    lax.fori_loop(0, tm, issue, 0, unroll=8)

    def wait_slot(s):
        for _ in range(TOP_K):
            pltpu.make_async_copy(buf.at[s], xs_ref.at[_tile_rows(0, tm)], sem.at[s]).wait()

    @pl.when(i > 0)
    def _():
        wait_slot(1 - slot)

    @pl.when(i == pl.num_programs(0) - 1)
    def _():
        wait_slot(slot)


def _dispatch(last_block, n_valid, dest3, h1, g, n_slots):
    T = h1.shape[0]
    tm = TM_TOK
    grid_spec = pltpu.PrefetchScalarGridSpec(
        num_scalar_prefetch=2,
        grid=(T // tm,),
        in_specs=[pl.BlockSpec((1, 1, TOP_K * tm), lambda i, lb, nv: (i, 0, 0), memory_space=pltpu.SMEM),
                  pl.BlockSpec((tm, D_MODEL), lambda i, lb, nv: (i, 0)),
                  pl.BlockSpec((1, D_MODEL), lambda i, lb, nv: (0, 0), pipeline_mode=pl.Buffered(1))],
        out_specs=pl.BlockSpec(memory_space=pl.ANY),
        scratch_shapes=[pltpu.VMEM((2, tm * ROW_SUBLANES, LANES), F32), pltpu.SemaphoreType.DMA((2,)),
                        pltpu.SemaphoreType.DMA(())],
    )
    return pl.pallas_call(
        _dispatch_kernel,
        grid_spec=grid_spec,
        out_shape=jax.ShapeDtypeStruct((n_slots * ROW_SUBLANES, LANES), F32),
        compiler_params=_cparams("arbitrary"),
        name="dispatch",
    )(last_block, n_valid, dest3, h1, g)


def _experts_kernel(be_ref, nv_ref, x_ref, wgu_ref, bgu_ref, wd_ref, bd_ref, y_ref):
    del be_ref
    tm = x_ref.shape[0] // ROW_SUBLANES

    @pl.when(pl.program_id(0) < nv_ref[0])
    def _():
        x = _from_row_tiles(x_ref, (), 0, tm).astype(BF16)
        gu = jnp.dot(x, wgu_ref[0], preferred_element_type=F32) + bgu_ref[0]
        glu = jnp.minimum(gu[:, :D_EXPERT], SWIGLU_LIMIT)
        lin = jnp.clip(gu[:, D_EXPERT:], -SWIGLU_LIMIT, SWIGLU_LIMIT)
        act = glu * jax.nn.sigmoid(SWIGLU_ALPHA * glu) * (lin + 1.0)
        _to_row_tiles(y_ref, (), jnp.dot(act.astype(BF16), wd_ref[0], preferred_element_type=F32) + bd_ref[0])

    @pl.when(pl.program_id(0) >= nv_ref[0])
    def _():
        y_ref[...] = jnp.zeros_like(y_ref)


def _experts(block_expert, n_valid, xs, wgu, bgu, wd, bd):
    tm = TM_EXP
    n_blocks = xs.shape[0] // (tm * ROW_SUBLANES)
    live = lambda b, be, nv: jnp.maximum(jnp.minimum(b, nv[0] - 1), 0)
    grid_spec = pltpu.PrefetchScalarGridSpec(
        num_scalar_prefetch=2,
        grid=(n_blocks,),
        in_specs=[pl.BlockSpec((tm * ROW_SUBLANES, LANES), lambda b, be, nv: (live(b, be, nv), 0)),
                  pl.BlockSpec((1, D_MODEL, 2 * D_EXPERT), lambda b, be, nv: (be[b], 0, 0)),
                  pl.BlockSpec((1, 1, 2 * D_EXPERT), lambda b, be, nv: (be[b], 0, 0)),
                  pl.BlockSpec((1, D_EXPERT, D_MODEL), lambda b, be, nv: (be[b], 0, 0)),
                  pl.BlockSpec((1, 1, D_MODEL), lambda b, be, nv: (be[b], 0, 0))],
        out_specs=pl.BlockSpec((tm * ROW_SUBLANES, LANES), lambda b, be, nv: (b, 0)),
    )
    return pl.pallas_call(
        _experts_kernel,
        grid_spec=grid_spec,
        out_shape=jax.ShapeDtypeStruct(xs.shape, F32),
        compiler_params=_cparams("arbitrary"),
        name="experts",
    )(block_expert, n_valid, xs, wgu, bgu, wd, bd)


def _combine_kernel(dest_ref, next_dest_ref, h_ref, gate_ref, p_ref, gp_ref, wpg_ref, wpp_ref, gf_ref, ys_ref,
                    o_ref, buf, sem):
    tm = h_ref.shape[0]
    i = pl.program_id(0)
    slot = lax.rem(i, 2)

    def gather(dref, s):
        def issue(t, carry):
            for k in range(TOP_K):
                d = dref[0, 0, t * TOP_K + k]
                pltpu.make_async_copy(ys_ref.at[_tile_rows(d)], buf.at[s, _tile_rows(k * tm + t)],
                                      sem.at[s]).start(priority=k % 2)
            return carry
        lax.fori_loop(0, tm, issue, 0, unroll=8)

    def wait_slot(s):
        for _ in range(TOP_K):
            pltpu.make_async_copy(ys_ref.at[_tile_rows(0, tm)], buf.at[s, _tile_rows(0, tm)], sem.at[s]).wait()

    @pl.when(i == 0)
    def _():
        gather(dest_ref, 0)

    proj = jnp.dot(p_ref[...].astype(BF16), wpp_ref[...], preferred_element_type=F32)
    wait_slot(slot)
    h = h_ref[...]
    for k in range(TOP_K):
        h = h + gate_ref[:, k:k + 1] * _from_row_tiles(buf, (slot,), k * tm, tm)

    for t in range(tm):
        for k in range(TOP_K):
            d = next_dest_ref[0, 0, t * TOP_K + k]
            pltpu.make_async_copy(ys_ref.at[_tile_rows(d)], buf.at[1 - slot, pl.ds((k * tm + t) * ROW_SUBLANES, ROW_SUBLANES)],
                                  sem.at[1 - slot]).start(priority=k % 2)

    ple_gate = jax.nn.sigmoid(jnp.dot(_rms(h, gp_ref[...]).astype(BF16), wpg_ref[...], preferred_element_type=F32))
    h = h + ple_gate * proj
    o_ref[...] = _rms(h, gf_ref[...])

    @pl.when(i == pl.num_programs(0) - 1)
    def _():
        wait_slot(1 - slot)


def _combine(dest3, h1, gate, p2, gp, wpg, wpp, gf, ys):
    T = h1.shape[0]
    tm = TM_TOK
    n_tiles = T // tm
    row = lambda w: pl.BlockSpec((tm, w), lambda i: (i, 0))
    dest_spec = lambda ahead: pl.BlockSpec((1, 1, TOP_K * tm), lambda i: (jnp.minimum(i + ahead, n_tiles - 1), 0, 0),
                                           memory_space=pltpu.SMEM)
    return pl.pallas_call(
        _combine_kernel,
        grid=(n_tiles,),
        in_specs=[dest_spec(0), dest_spec(1),
                  row(D_MODEL), row(TOP_K), row(PLE_DIM), _resident((1, D_MODEL)),
                  _resident((D_MODEL, D_MODEL)), _resident((PLE_DIM, D_MODEL)), _resident((1, D_MODEL)),
                  pl.BlockSpec(memory_space=pl.ANY)],
        out_specs=row(D_MODEL),
        out_shape=jax.ShapeDtypeStruct((T, D_MODEL), F32),
        scratch_shapes=[pltpu.VMEM((2, TOP_K * tm * ROW_SUBLANES, LANES), F32), pltpu.SemaphoreType.DMA((2,))],
        compiler_params=_cparams("arbitrary"),
        name="combine",
    )(dest3, dest3, h1, gate, p2, gp, wpg, wpp, gf, ys)


def _layer(h, p_i, g_mix, w_in, rel_bias, w_att_out, ln_v_g, ln_v_b, w_spatial, b_spatial, w_gmlp_out, w_out,
           g_moe, w_router, b_router, w_gate_up, b_gate_up, w_down, b_down, g_ple, w_ple_gate, w_ple_proj,
           g_final, B, S):
    T = B * S
    row = lambda v: v.reshape(1, -1).astype(F32)

    *att_in, uv, gl = _in_proj(h, row(g_mix), w_in.astype(BF16))

    outs, lses = [], []
    for g, (window, dilation) in enumerate(ATT_GROUPS):
        assert window // dilation == BLK and S % (dilation * BLK) == 0
        bias = _bias_table(rel_bias[:, g * HEADS_PER_GROUP:(g + 1) * HEADS_PER_GROUP], dilation)
        o, lse = _attention_group(att_in[g], bias, dilation, B, S)
        outs.append(o)
        lses.append(lse)

    causal = jnp.asarray(np.tril(np.ones((CHUNK, CHUNK), np.float32)))
    w_c = (w_spatial.astype(F32) * causal[None]).astype(BF16)
    wc2 = jnp.concatenate([w_c[0::2], w_c[1::2]], axis=2)
    bs = jnp.repeat(b_spatial.astype(F32).T, GMLP_GD, axis=1)
    h1 = _mix(h, outs, lses, uv, gl, w_att_out.astype(BF16), w_gmlp_out.astype(BF16), w_out.astype(BF16),
              wc2, bs, row(ln_v_g), row(ln_v_b))

    eidx, gate, rank, counts = _router(h1, row(g_moe), w_router.astype(F32), row(b_router))
    cnt = counts[0].astype(jnp.int32)
    blk_counts = (cnt + TM_EXP - 1) // TM_EXP
    blk_end = jnp.cumsum(blk_counts)
    pad_start = (blk_end - blk_counts) * TM_EXP
    n_blocks = T * TOP_K // TM_EXP + N_EXPERTS
    n_valid = blk_end[-1:].astype(jnp.int32)
    blk = jnp.minimum(jnp.arange(n_blocks, dtype=jnp.int32), n_valid[0] - 1)
    block_expert = jnp.minimum(jnp.sum((blk_end[None, :] <= blk[:, None]).astype(jnp.int32), axis=1), N_EXPERTS - 1)
    expert_ids = jnp.arange(N_EXPERTS, dtype=jnp.int32)
    dest = rank + jnp.sum(jnp.where(eidx[..., None] == expert_ids, pad_start, 0), axis=-1)
    dest3 = dest.reshape(T // TM_TOK, 1, TM_TOK * TOP_K)

    last_block = jnp.maximum(blk_end - 1, 0).astype(jnp.int32)
    xs = _dispatch(last_block, n_valid, dest3, h1, row(g_moe), n_blocks * TM_EXP)
    ys = _experts(block_expert, n_valid, xs, w_gate_up.astype(BF16), b_gate_up.reshape(N_EXPERTS, 1, -1).astype(F32),
                  w_down.astype(BF16), b_down.reshape(N_EXPERTS, 1, -1).astype(F32))
    return _combine(dest3, h1, gate, p_i, row(g_ple), w_ple_gate.astype(BF16), w_ple_proj.astype(BF16),
                    row(g_final), ys)


def kernel(x, p, g_mix, w_in, rel_bias, w_att_out, ln_v_g, ln_v_b, w_spatial, b_spatial, w_gmlp_out, w_out, g_moe, w_router, b_router, w_gate_up, b_gate_up, w_down, b_down, g_ple, w_ple_gate, w_ple_proj, g_final):
    B, S, D = x.shape
    depth = p.shape[0]
    assert depth == 1, "the final RMSNorm is fused into the (single) layer's last kernel"
    out = _layer(x.reshape(B * S, D), p[0].reshape(B * S, PLE_DIM), g_mix[0], w_in[0], rel_bias, w_att_out[0],
                 ln_v_g[0], ln_v_b[0], w_spatial[0], b_spatial[0], w_gmlp_out[0], w_out[0], g_moe[0], w_router[0],
                 b_router[0], w_gate_up[0], b_gate_up[0], w_down[0], b_down[0], g_ple[0], w_ple_gate[0],
                 w_ple_proj[0], g_final, B, S)
    return out.reshape(B, S, D)
```

```python
import functools

import jax
import jax.numpy as jnp
import numpy as np
from jax import lax
from jax.experimental import pallas as pl
from jax.experimental.pallas import tpu as pltpu

F32 = jnp.float32
BF16 = jnp.bfloat16

D_MODEL = 1024
HEAD_DIM = 64
ATT_GROUPS = ((128, 1), (512, 4), (2048, 16))
HEADS_PER_GROUP = 4
GROUP_W = HEADS_PER_GROUP * HEAD_DIM
N_DIL = len(ATT_GROUPS)
ATT_W = N_DIL * GROUP_W
BLK = 128
REL_BUCKETS = 32
REL_MAX_DIST = 2048
CHUNK = 128
GMLP_W = 768
GMLP_GD = 64
N_BRANCH = 2
IN_W = 3 * ATT_W + 2 * GMLP_W + N_BRANCH * D_MODEL
N_EXPERTS = 32
TOP_K = 4
D_EXPERT = D_MODEL
SWIGLU_LIMIT = 7.0
SWIGLU_ALPHA = 1.702
PLE_DIM = 256
EPS = 1e-6
MASKED = -1e30

QKV_G = 3 * GROUP_W

LANES = 128
ROW_SUBLANES = D_MODEL // LANES
assert ROW_SUBLANES == 8
MXU_N = 256
VMEM_LIMIT = 56 * 1024 * 1024

TM_PROJ = 512
TM_TOK = 256
TM_EXP = 512
assert TM_EXP % TM_TOK == 0


def _cparams(*sem):
    return pltpu.CompilerParams(dimension_semantics=sem, vmem_limit_bytes=VMEM_LIMIT)


def _resident(shape):
    nd = len(shape)
    return pl.BlockSpec(shape, lambda *_: (0,) * nd, pipeline_mode=pl.Buffered(1))


def _rms(x, g):
    return x * lax.rsqrt(jnp.mean(x * x, axis=-1, keepdims=True) + EPS) * g


def _inproj_kernel(x_ref, g_ref, w_ref, a1_ref, a2_ref, a3_ref, uv_ref, gl_ref, scr):
    tm = x_ref.shape[0]
    n = _rms(x_ref[...], g_ref[...]).astype(BF16)
    att_refs = (a1_ref, a2_ref, a3_ref)
    n_att, n_uv = 3 * ATT_W // MXU_N, 2 * GMLP_W // MXU_N
    for c in range(IN_W // MXU_N):
        z = jnp.dot(n, w_ref[:, c * MXU_N:(c + 1) * MXU_N], preferred_element_type=F32)
        if c < n_att:
            which, g = divmod(c, N_DIL)
            d = ATT_GROUPS[g][1]
            dst = att_refs[g]
            cols = slice(which * GROUP_W, (which + 1) * GROUP_W)
            if d == 1:
                dst[0, 0, :, cols] = z.astype(BF16)
                continue
            scr[0] = z[:, :LANES]
            scr[1] = z[:, LANES:]
            for r in range(d):
                zr = jnp.concatenate([scr[0, pl.ds(r, tm // d, stride=d), :],
                                      scr[1, pl.ds(r, tm // d, stride=d), :]], axis=1)
                dst[0, r, :, cols] = zr.astype(BF16)
        elif c < n_att + n_uv:
            uv_ref[:, (c - n_att) * MXU_N:(c - n_att + 1) * MXU_N] = z.astype(BF16)
        else:
            gl_ref[:, (c - n_att - n_uv) * MXU_N:(c - n_att - n_uv + 1) * MXU_N] = z.astype(BF16)


def _plane_spec(d, tm, tiles_per_seq, width):
    return pl.BlockSpec((1, d, tm // d, width), lambda i: (i // tiles_per_seq, 0, i % tiles_per_seq, 0))


def _in_proj(x2, g, w_bf, B, S):
    T = x2.shape[0]
    tm = TM_PROJ
    row = lambda w: pl.BlockSpec((tm, w), lambda i: (i, 0))
    dils = [d for _, d in ATT_GROUPS]
    return pl.pallas_call(
        _inproj_kernel,
        grid=(T // tm,),
        in_specs=[row(D_MODEL), _resident((1, D_MODEL)), _resident((D_MODEL, IN_W))],
        out_specs=[_plane_spec(d, tm, S // tm, QKV_G) for d in dils] + [row(2 * GMLP_W), row(N_BRANCH * D_MODEL)],
        out_shape=[jax.ShapeDtypeStruct((B, d, S // d, QKV_G), BF16) for d in dils]
                  + [jax.ShapeDtypeStruct((T, 2 * GMLP_W), BF16),
                     jax.ShapeDtypeStruct((T, N_BRANCH * D_MODEL), BF16)],
        scratch_shapes=[pltpu.VMEM((2, tm, LANES), F32)],
        compiler_params=_cparams("parallel"),
        name="in_proj",
    )(x2, g, w_bf)


def _t5_bucket(n):
    exact = REL_BUCKETS // 2
    nf = np.maximum(n, 1).astype(np.float32)
    large = exact + (np.log(nf / exact) / np.log(REL_MAX_DIST / exact) * (REL_BUCKETS - exact)).astype(np.int32)
    large = np.minimum(large, REL_BUCKETS - 1)
    return np.where(n < exact, n, large).astype(np.int32)


def _bias_table(rel_bias_g, dilation):
    n = 3 * BLK
    dist = 2 * BLK - 1 - np.arange(n)
    valid = (dist >= 0) & (dist <= BLK)
    bucket = _t5_bucket(np.clip(dist, 0, BLK) * dilation)
    c = jnp.where(jnp.asarray(valid)[None, :], rel_bias_g.astype(F32)[bucket].T, MASKED)
    shifted = jnp.tile(c, (1, BLK))[:, :BLK * (n - 1)].reshape(HEADS_PER_GROUP, BLK, n - 1)
    return shifted[:, :, BLK - 1:].reshape(HEADS_PER_GROUP * BLK, 2 * BLK)


def _attn_kernel(cur_ref, prev_ref, bias_ref, o_ref, lse_ref):
    rg, rb = cur_ref.shape[1], cur_ref.shape[2] // BLK
    starts_sequence = pl.program_id(2) == 0
    lane_head = lax.broadcasted_iota(jnp.int32, (1, GROUP_W), 1) // HEAD_DIM
    head_bf = [(lane_head == h).astype(BF16) for h in range(HEADS_PER_GROUP)]
    head_f = [(lane_head == h).astype(F32) for h in range(HEADS_PER_GROUP)]
    key_is_prev = lax.broadcasted_iota(jnp.int32, (1, 2 * BLK), 1) < BLK
    nt = (((1,), (1,)), ((), ()))
    scale = HEAD_DIM ** -0.5
    qc, kc_, vc_ = slice(0, GROUP_W), slice(GROUP_W, 2 * GROUP_W), slice(2 * GROUP_W, 3 * GROUP_W)
    for r, j in [(r, j) for r in range(rg) for j in range(rb)]:
        rows = slice(j * BLK, (j + 1) * BLK)
        prev = prev_ref if j == 0 else cur_ref
        prows = slice(0, BLK) if j == 0 else slice((j - 1) * BLK, j * BLK)
        q = cur_ref[0, r, rows, qc]
        k = jnp.concatenate([prev[0, r, prows, kc_], cur_ref[0, r, rows, kc_]], axis=0)
        v = jnp.concatenate([prev[0, r, prows, vc_], cur_ref[0, r, rows, vc_]], axis=0)
        q_bd = jnp.concatenate([q * head_bf[h] for h in range(HEADS_PER_GROUP)], axis=0)
        s = lax.dot_general(q_bd, k, nt, preferred_element_type=F32) * scale + bias_ref[...]
        if j == 0:
            s = jnp.where(jnp.logical_and(starts_sequence, key_is_prev), MASKED, s)
        m = jnp.max(s, axis=-1, keepdims=True)
        p = jnp.exp(s - m)
        den = jnp.sum(p, axis=-1, keepdims=True)
        o = jnp.dot(p.astype(BF16), v, preferred_element_type=F32) / den
        l = m + jnp.log(den)
        out = jnp.zeros((BLK, GROUP_W), F32)
        lse = jnp.zeros((BLK, GROUP_W), F32)
        for h in range(HEADS_PER_GROUP):
            hr = slice(h * BLK, (h + 1) * BLK)
            out = out + o[hr] * head_f[h]
            lse = lse + l[hr] * head_f[h]
        o_ref[0, r, rows, :] = out.astype(BF16)
        lse_ref[0, r, rows, :] = lse


ATT_SUBBLOCKS = 8


def _attention_group(a, bias, dilation, B, S):
    sd = S // dilation
    rb = min(ATT_SUBBLOCKS, sd // BLK)
    rg = min(ATT_SUBBLOCKS // rb, dilation)
    o, lse = pl.pallas_call(
        _attn_kernel,
        grid=(B, dilation // rg, sd // (rb * BLK)),
        in_specs=[pl.BlockSpec((1, rg, rb * BLK, QKV_G), lambda b, r, n: (b, r, n, 0)),
                  pl.BlockSpec((1, rg, BLK, QKV_G), lambda b, r, n: (b, r, jnp.maximum(n * rb - 1, 0), 0)),
                  _resident((HEADS_PER_GROUP * BLK, 2 * BLK))],
        out_specs=[pl.BlockSpec((1, rg, rb * BLK, GROUP_W), lambda b, r, n: (b, r, n, 0))] * 2,
        out_shape=[jax.ShapeDtypeStruct((B, dilation, sd, GROUP_W), BF16),
                   jax.ShapeDtypeStruct((B, dilation, sd, GROUP_W), F32)],
        compiler_params=_cparams("parallel", "parallel", "parallel"),
        name=f"attn_d{dilation}",
    )(a, a, bias)
    return o, lse


def _gelu(x):
    return x * (lax.erf(x * (2.0 ** -0.5)) + 1.0) * 0.5


def _token_major(src_ref, d, scr, slot, tm):
    if d == 1:
        return src_ref[0, 0].astype(F32)
    for r in range(d):
        piece = src_ref[0, r].astype(F32)
        scr[slot, pl.ds(r, tm // d, stride=d), :] = piece[:, :LANES]
        scr[slot + 1, pl.ds(r, tm // d, stride=d), :] = piece[:, LANES:]
    return jnp.concatenate([scr[slot], scr[slot + 1]], axis=1)


def _mix_kernel(x_ref, o1_ref, o2_ref, o3_ref, l1_ref, l2_ref, l3_ref, uv_ref, gl_ref,
                wa_ref, wg_ref, wo_ref, wc_ref, bs_ref, lng_ref, lnb_ref, h_ref, g_scr, t_scr):
    tm = x_ref.shape[0]
    dils = [d for _, d in ATT_GROUPS]
    o1, o2, o3 = [_token_major(ref, d, t_scr, 4 * i, tm) for i, (ref, d) in enumerate(zip((o1_ref, o2_ref, o3_ref), dils))]
    l1, l2, l3 = [_token_major(ref, d, t_scr, 4 * i + 2, tm) for i, (ref, d) in enumerate(zip((l1_ref, l2_ref, l3_ref), dils))]
    lm = jnp.maximum(jnp.maximum(l1, l2), l3)
    e1, e2, e3 = jnp.exp(l1 - lm), jnp.exp(l2 - lm), jnp.exp(l3 - lm)
    att = (e1 * o1 + e2 * o2 + e3 * o3) / (e1 + e2 + e3)
    y_att = jnp.dot(att.astype(BF16), wa_ref[...], preferred_element_type=F32)

    zu = _gelu(uv_ref[:, :GMLP_W].astype(F32))
    zv = _gelu(uv_ref[:, GMLP_W:].astype(F32))
    mu = jnp.mean(zv, axis=-1, keepdims=True)
    var = jnp.mean(jnp.square(zv - mu), axis=-1, keepdims=True)
    vn = (zv - mu) * lax.rsqrt(var + EPS) * lng_ref[...] + lnb_ref[...]
    low_half = lax.broadcasted_iota(jnp.int32, (CHUNK, 2 * GMLP_GD), 1) < GMLP_GD
    for c in range(tm // CHUNK):
        rows = slice(c * CHUNK, (c + 1) * CHUNK)
        for s in range(GMLP_W // (2 * GMLP_GD)):
            cols = slice(s * 2 * GMLP_GD, (s + 1) * 2 * GMLP_GD)
            v2 = vn[rows, cols]
            rhs = jnp.concatenate([jnp.where(low_half, v2, 0.0), jnp.where(low_half, 0.0, v2)], axis=0).astype(BF16)
            mixed = jnp.dot(wc_ref[s], rhs, preferred_element_type=F32) + bs_ref[:, cols]
            g_scr[rows, cols] = (zu[rows, cols] * mixed).astype(BF16)
    y_gm = jnp.dot(g_scr[...], wg_ref[...], preferred_element_type=F32)

    gate_a = jax.nn.sigmoid(gl_ref[:, :D_MODEL].astype(F32))
    gate_g = jax.nn.sigmoid(gl_ref[:, D_MODEL:].astype(F32))
    merged = (gate_a * y_att + gate_g * y_gm).astype(BF16)
    h_ref[...] = x_ref[...] + jnp.dot(merged, wo_ref[...], preferred_element_type=F32)


def _mix(x2, outs, lses, uv, gl, wa, wg, wo, wc2, bs, lng, lnb, S):
    T = x2.shape[0]
    tm = TM_PROJ
    row = lambda w: pl.BlockSpec((tm, w), lambda i: (i, 0))
    att = [_plane_spec(d, tm, S // tm, GROUP_W) for _, d in ATT_GROUPS]
    return pl.pallas_call(
        _mix_kernel,
        grid=(T // tm,),
        in_specs=[row(D_MODEL)] + att + att + [row(2 * GMLP_W), row(N_BRANCH * D_MODEL),
                  _resident(wa.shape), _resident(wg.shape), _resident(wo.shape), _resident(wc2.shape),
                  _resident(bs.shape), _resident(lng.shape), _resident(lnb.shape)],
        out_specs=row(D_MODEL),
        out_shape=jax.ShapeDtypeStruct((T, D_MODEL), F32),
        scratch_shapes=[pltpu.VMEM((tm, GMLP_W), BF16), pltpu.VMEM((4 * N_DIL, tm, LANES), F32)],
        compiler_params=_cparams("parallel"),
        name="mix",
    )(x2, *outs, *lses, uv, gl, wa, wg, wo, wc2, bs, lng, lnb)


def _router_kernel(h_ref, g_ref, wr_ref, br_ref, eidx_ref, gate_ref, rank_ref, cnt_ref, carry):
    tm = h_ref.shape[0]

    @pl.when(pl.program_id(0) == 0)
    def _():
        carry[...] = jnp.zeros_like(carry)

    hn = _rms(h_ref[...], g_ref[...])
    hi = hn.astype(BF16)
    lo = (hn - hi.astype(F32)).astype(BF16)
    by_hi = jnp.dot(hi, wr_ref[...], preferred_element_type=F32)
    by_lo = jnp.dot(lo, wr_ref[:, :N_EXPERTS], preferred_element_type=F32)
    logits = by_hi[:, :N_EXPERTS] + by_hi[:, N_EXPERTS:] + by_lo + br_ref[...]
    lane = lax.broadcasted_iota(jnp.int32, (tm, N_EXPERTS), 1)
    vals, hots = [], []
    l = logits
    for k in range(TOP_K):
        m = jnp.max(l, axis=-1, keepdims=True)
        idx = jnp.min(jnp.where(l == m, lane, N_EXPERTS), axis=-1, keepdims=True)
        hot = lane == idx
        eidx_ref[:, k:k + 1] = idx
        vals.append(m)
        hots.append(hot)
        l = jnp.where(hot, -jnp.inf, l)
    ex = [jnp.exp(v - vals[0]) for v in vals]
    tot = ex[0] + ex[1] + ex[2] + ex[3]
    for k in range(TOP_K):
        gate_ref[:, k:k + 1] = ex[k] / tot
    multi = jnp.zeros((tm, N_EXPERTS), F32)
    for hot in hots:
        multi = multi + hot.astype(F32)
    r = lax.broadcasted_iota(jnp.int32, (tm, tm), 0)
    c = lax.broadcasted_iota(jnp.int32, (tm, tm), 1)
    strict_lower = (c < r).astype(BF16)
    before = jnp.dot(strict_lower, multi.astype(BF16), preferred_element_type=F32) + carry[...]
    for k in range(TOP_K):
        rank_ref[:, k:k + 1] = jnp.sum(jnp.where(hots[k], before, 0.0), axis=-1, keepdims=True).astype(jnp.int32)
    carry[...] += jnp.sum(multi, axis=0, keepdims=True)
    cnt_ref[...] = carry[...]


def _router(h1, g, wr, br):
    T = h1.shape[0]
    tm = TM_PROJ
    col4 = pl.BlockSpec((tm, TOP_K), lambda i: (i, 0))
    return pl.pallas_call(
        _router_kernel,
        grid=(T // tm,),
        in_specs=[pl.BlockSpec((tm, D_MODEL), lambda i: (i, 0)), _resident((1, D_MODEL)),
                  _resident((D_MODEL, 2 * N_EXPERTS)), _resident((1, N_EXPERTS))],
        out_specs=[col4, col4, col4, pl.BlockSpec((1, N_EXPERTS), lambda i: (0, 0))],
        out_shape=[jax.ShapeDtypeStruct((T, TOP_K), jnp.int32), jax.ShapeDtypeStruct((T, TOP_K), F32),
                   jax.ShapeDtypeStruct((T, TOP_K), jnp.int32), jax.ShapeDtypeStruct((1, N_EXPERTS), F32)],
        scratch_shapes=[pltpu.VMEM((1, N_EXPERTS), F32)],
        compiler_params=_cparams("arbitrary"),
        name="router",
    )(h1, g, wr, br)


def _to_row_tiles(ref, lead, value):
    n = value.shape[0]
    for c in range(ROW_SUBLANES):
        ref[(*lead, pl.ds(c, n, stride=ROW_SUBLANES), slice(None))] = value[:, c * LANES:(c + 1) * LANES]


def _from_row_tiles(ref, lead, first, n):
    return jnp.concatenate(
        [ref[(*lead, pl.ds(first * ROW_SUBLANES + c, n, stride=ROW_SUBLANES), slice(None))] for c in range(ROW_SUBLANES)],
        axis=1)


def _tile_rows(idx, n=1):
    return pl.ds(pl.multiple_of(idx * ROW_SUBLANES, ROW_SUBLANES), n * ROW_SUBLANES)


def _dispatch_kernel(last_ref, nv_ref, dest_ref, h_ref, g_ref, xs_ref, buf, sem, zero_sem):
    tm = h_ref.shape[0]
    n_blocks = xs_ref.shape[0] // (TM_EXP * ROW_SUBLANES)
    i = pl.program_id(0)
    slot = lax.rem(i, 2)

    @pl.when(i == 0)
    def _():
        buf[1] = jnp.zeros(buf.shape[1:], F32)

        def zero_block(b):
            for part in range(TM_EXP // tm):
                pltpu.make_async_copy(buf.at[1], xs_ref.at[_tile_rows(b * TM_EXP + part * tm, tm)], zero_sem).start()

        def zero_done():
            for part in range(TM_EXP // tm):
                pltpu.make_async_copy(buf.at[1], xs_ref.at[_tile_rows(0, tm)], zero_sem).wait()

        for e in range(N_EXPERTS):
            zero_block(last_ref[e])
        lax.fori_loop(nv_ref[0], n_blocks, lambda b, c: (zero_block(b), c)[1], 0)
        for e in range(N_EXPERTS):
            zero_done()
        lax.fori_loop(nv_ref[0], n_blocks, lambda b, c: (zero_done(), c)[1], 0)

    _to_row_tiles(buf, (slot,), _rms(h_ref[...], g_ref[...]))

    def issue(t, carry):
        for k in range(TOP_K):
            d = dest_ref[0, 0, t * TOP_K + k]
            pltpu.make_async_copy(buf.at[slot, _tile_rows(t)], xs_ref.at[_tile_rows(d)],
                                  sem.at[slot]).start(priority=k % 2)
        return carry

    lax.fori_loop(0, tm, issue, 0, unroll=8)

    def wait_slot(s):
        for _ in range(TOP_K):
            pltpu.make_async_copy(buf.at[s], xs_ref.at[_tile_rows(0, tm)], sem.at[s]).wait()

    @pl.when(i > 0)
    def _():
        wait_slot(1 - slot)

    @pl.when(i == pl.num_programs(0) - 1)
    def _():
        wait_slot(slot)


def _dispatch(last_block, n_valid, dest3, h1, g, n_slots):
    T = h1.shape[0]
    tm = TM_TOK
    grid_spec = pltpu.PrefetchScalarGridSpec(
        num_scalar_prefetch=2,
        grid=(T // tm,),
        in_specs=[pl.BlockSpec((1, 1, TOP_K * tm), lambda i, lb, nv: (i, 0, 0), memory_space=pltpu.SMEM),
                  pl.BlockSpec((tm, D_MODEL), lambda i, lb, nv: (i, 0)),
                  pl.BlockSpec((1, D_MODEL), lambda i, lb, nv: (0, 0), pipeline_mode=pl.Buffered(1))],
        out_specs=pl.BlockSpec(memory_space=pl.ANY),
        scratch_shapes=[pltpu.VMEM((2, tm * ROW_SUBLANES, LANES), F32), pltpu.SemaphoreType.DMA((2,)),
                        pltpu.SemaphoreType.DMA(())],
    )
    return pl.pallas_call(
        _dispatch_kernel,
        grid_spec=grid_spec,
        out_shape=jax.ShapeDtypeStruct((n_slots * ROW_SUBLANES, LANES), F32),
        compiler_params=_cparams("arbitrary"),
        name="dispatch",
    )(last_block, n_valid, dest3, h1, g)


def _experts_kernel(be_ref, nv_ref, x_ref, wgu_ref, bgu_ref, wd_ref, bd_ref, y_ref):
    del be_ref
    tm = x_ref.shape[0] // ROW_SUBLANES

    @pl.when(pl.program_id(0) < nv_ref[0])
    def _():
        x = _from_row_tiles(x_ref, (), 0, tm).astype(BF16)
        gu = jnp.dot(x, wgu_ref[0], preferred_element_type=F32) + bgu_ref[0]
        glu = jnp.minimum(gu[:, :D_EXPERT], SWIGLU_LIMIT)
        lin = jnp.clip(gu[:, D_EXPERT:], -SWIGLU_LIMIT, SWIGLU_LIMIT)
        act = glu * jax.nn.sigmoid(SWIGLU_ALPHA * glu) * (lin + 1.0)
        _to_row_tiles(y_ref, (), jnp.dot(act.astype(BF16), wd_ref[0], preferred_element_type=F32) + bd_ref[0])

    @pl.when(pl.program_id(0) >= nv_ref[0])
    def _():
        y_ref[...] = jnp.zeros_like(y_ref)


def _experts(block_expert, n_valid, xs, wgu, bgu, wd, bd):
    tm = TM_EXP
    n_blocks = xs.shape[0] // (tm * ROW_SUBLANES)
    live = lambda b, be, nv: jnp.maximum(jnp.minimum(b, nv[0] - 1), 0)
    grid_spec = pltpu.PrefetchScalarGridSpec(
        num_scalar_prefetch=2,
        grid=(n_blocks,),
        in_specs=[pl.BlockSpec((tm * ROW_SUBLANES, LANES), lambda b, be, nv: (live(b, be, nv), 0)),
                  pl.BlockSpec((1, D_MODEL, 2 * D_EXPERT), lambda b, be, nv: (be[b], 0, 0)),
                  pl.BlockSpec((1, 1, 2 * D_EXPERT), lambda b, be, nv: (be[b], 0, 0)),
                  pl.BlockSpec((1, D_EXPERT, D_MODEL), lambda b, be, nv: (be[b], 0, 0)),
                  pl.BlockSpec((1, 1, D_MODEL), lambda b, be, nv: (be[b], 0, 0))],
        out_specs=pl.BlockSpec((tm * ROW_SUBLANES, LANES), lambda b, be, nv: (b, 0)),
    )
    return pl.pallas_call(
        _experts_kernel,
        grid_spec=grid_spec,
        out_shape=jax.ShapeDtypeStruct(xs.shape, F32),
        compiler_params=_cparams("arbitrary"),
        name="experts",
    )(block_expert, n_valid, xs, wgu, bgu, wd, bd)


GATHER_AHEAD = 2
GATHER_SLOTS = GATHER_AHEAD + 1


def _combine_kernel(dest0_ref, dest1_ref, ahead_dest_ref, h_ref, gate_ref, p_ref, gp_ref, wpg_ref, wpp_ref, gf_ref,
                    ys_ref, o_ref, buf, sem):
    tm = h_ref.shape[0]
    i = pl.program_id(0)
    slot = lax.rem(i, GATHER_SLOTS)
    ahead_slot = lax.rem(i + GATHER_AHEAD, GATHER_SLOTS)

    def row_copy(dref, t, k, s):
        d = dref[0, 0, t * TOP_K + k]
        return pltpu.make_async_copy(ys_ref.at[_tile_rows(d)], buf.at[s, _tile_rows(k * tm + t)], sem.at[s])

    def gather(dref, s):
        def issue(t, carry):
            for k in range(TOP_K):
                row_copy(dref, t, k, s).start(priority=k % 2)
            return carry
        lax.fori_loop(0, tm, issue, 0, unroll=8)

    def wait_slot(s):
        for _ in range(TOP_K):
            pltpu.make_async_copy(ys_ref.at[_tile_rows(0, tm)], buf.at[s, _tile_rows(0, tm)], sem.at[s]).wait()

    @pl.when(i == 0)
    def _():
        gather(dest0_ref, 0)
        gather(dest1_ref, 1)

    proj = jnp.dot(p_ref[...].astype(BF16), wpp_ref[...], preferred_element_type=F32)
    wait_slot(slot)
    h = h_ref[...]
    for k in range(TOP_K):
        h = h + gate_ref[:, k:k + 1] * _from_row_tiles(buf, (slot,), k * tm, tm)

    for t in range(tm):
        for k in range(TOP_K):
            row_copy(ahead_dest_ref, t, k, ahead_slot).start(priority=k % 2)

    ple_gate = jax.nn.sigmoid(jnp.dot(_rms(h, gp_ref[...]).astype(BF16), wpg_ref[...], preferred_element_type=F32))
    h = h + ple_gate * proj
    o_ref[...] = _rms(h, gf_ref[...])

    @pl.when(i == pl.num_programs(0) - 1)
    def _():
        for ahead in range(1, GATHER_SLOTS):
            wait_slot(lax.rem(i + ahead, GATHER_SLOTS))


def _combine(dest3, h1, gate, p2, gp, wpg, wpp, gf, ys):
    T = h1.shape[0]
    tm = TM_TOK
    n_tiles = T // tm
    assert n_tiles > GATHER_AHEAD == 2
    row = lambda w: pl.BlockSpec((tm, w), lambda i: (i, 0))
    dest_spec = lambda ahead: pl.BlockSpec((1, 1, TOP_K * tm), lambda i: (jnp.minimum(i + ahead, n_tiles - 1), 0, 0),
                                           memory_space=pltpu.SMEM)
    return pl.pallas_call(
        _combine_kernel,
        grid=(n_tiles,),
        in_specs=[dest_spec(0), dest_spec(1), dest_spec(GATHER_AHEAD),
                  row(D_MODEL), row(TOP_K), row(PLE_DIM), _resident((1, D_MODEL)),
                  _resident((D_MODEL, D_MODEL)), _resident((PLE_DIM, D_MODEL)), _resident((1, D_MODEL)),
                  pl.BlockSpec(memory_space=pl.ANY)],
        out_specs=row(D_MODEL),
        out_shape=jax.ShapeDtypeStruct((T, D_MODEL), F32),
        scratch_shapes=[pltpu.VMEM((GATHER_SLOTS, TOP_K * tm * ROW_SUBLANES, LANES), F32),
                        pltpu.SemaphoreType.DMA((GATHER_SLOTS,))],
        compiler_params=_cparams("arbitrary"),
        name="combine",
    )(dest3, dest3, dest3, h1, gate, p2, gp, wpg, wpp, gf, ys)


def _layer(h, p_i, g_mix, w_in, rel_bias, w_att_out, ln_v_g, ln_v_b, w_spatial, b_spatial, w_gmlp_out, w_out,
           g_moe, w_router, b_router, w_gate_up, b_gate_up, w_down, b_down, g_ple, w_ple_gate, w_ple_proj,
           g_final, B, S):
    T = B * S
    row = lambda v: v.reshape(1, -1).astype(F32)

    assert S % TM_PROJ == 0
    *att_in, uv, gl = _in_proj(h, row(g_mix), w_in.astype(BF16), B, S)

    outs, lses = [], []
    for g, (window, dilation) in enumerate(ATT_GROUPS):
        assert window // dilation == BLK and S % (dilation * BLK) == 0
        bias = _bias_table(rel_bias[:, g * HEADS_PER_GROUP:(g + 1) * HEADS_PER_GROUP], dilation)
        o, lse = _attention_group(att_in[g], bias, dilation, B, S)
        outs.append(o)
        lses.append(lse)

    causal = jnp.asarray(np.tril(np.ones((CHUNK, CHUNK), np.float32)))
    w_c = (w_spatial.astype(F32) * causal[None]).astype(BF16)
    wc2 = jnp.concatenate([w_c[0::2], w_c[1::2]], axis=2)
    bs = jnp.repeat(b_spatial.astype(F32).T, GMLP_GD, axis=1)
    h1 = _mix(h, outs, lses, uv, gl, w_att_out.astype(BF16), w_gmlp_out.astype(BF16), w_out.astype(BF16),
              wc2, bs, row(ln_v_g), row(ln_v_b), S)

    wr_hi = w_router.astype(BF16)
    wr_lo = (w_router.astype(F32) - wr_hi.astype(F32)).astype(BF16)
    eidx, gate, rank, counts = _router(h1, row(g_moe), jnp.concatenate([wr_hi, wr_lo], axis=1), row(b_router))
    cnt = counts[0].astype(jnp.int32)
    blk_counts = (cnt + TM_EXP - 1) // TM_EXP
    blk_end = jnp.cumsum(blk_counts)
    pad_start = (blk_end - blk_counts) * TM_EXP
    n_blocks = T * TOP_K // TM_EXP + N_EXPERTS
    n_valid = blk_end[-1:].astype(jnp.int32)
    blk = jnp.minimum(jnp.arange(n_blocks, dtype=jnp.int32), n_valid[0] - 1)
    block_expert = jnp.minimum(jnp.sum((blk_end[None, :] <= blk[:, None]).astype(jnp.int32), axis=1), N_EXPERTS - 1)
    expert_ids = jnp.arange(N_EXPERTS, dtype=jnp.int32)
    dest = rank + jnp.sum(jnp.where(eidx[..., None] == expert_ids, pad_start, 0), axis=-1)
    dest3 = dest.reshape(T // TM_TOK, 1, TM_TOK * TOP_K)

    last_block = jnp.maximum(blk_end - 1, 0).astype(jnp.int32)
    xs = _dispatch(last_block, n_valid, dest3, h1, row(g_moe), n_blocks * TM_EXP)
    ys = _experts(block_expert, n_valid, xs, w_gate_up.astype(BF16), b_gate_up.reshape(N_EXPERTS, 1, -1).astype(F32),
                  w_down.astype(BF16), b_down.reshape(N_EXPERTS, 1, -1).astype(F32))
    return _combine(dest3, h1, gate, p_i, row(g_ple), w_ple_gate.astype(BF16), w_ple_proj.astype(BF16),
                    row(g_final), ys)


def kernel(x, p, g_mix, w_in, rel_bias, w_att_out, ln_v_g, ln_v_b, w_spatial, b_spatial, w_gmlp_out, w_out, g_moe, w_router, b_router, w_gate_up, b_gate_up, w_down, b_down, g_ple, w_ple_gate, w_ple_proj, g_final):
    B, S, D = x.shape
    depth = p.shape[0]
    assert depth == 1, "the final RMSNorm is fused into the (single) layer's last kernel"
    out = _layer(x.reshape(B * S, D), p[0].reshape(B * S, PLE_DIM), g_mix[0], w_in[0], rel_bias, w_att_out[0],
                 ln_v_g[0], ln_v_b[0], w_spatial[0], b_spatial[0], w_gmlp_out[0], w_out[0], g_moe[0], w_router[0],
                 b_router[0], w_gate_up[0], b_gate_up[0], w_down[0], b_down[0], g_ple[0], w_ple_gate[0],
                 w_ple_proj[0], g_final, B, S)
    return out.reshape(B, S, D)
```

```python
import functools

import jax
import jax.numpy as jnp
import numpy as np
from jax import lax
from jax.experimental import pallas as pl
from jax.experimental.pallas import tpu as pltpu

F32 = jnp.float32
BF16 = jnp.bfloat16

D_MODEL = 1024
HEAD_DIM = 64
ATT_GROUPS = ((128, 1), (512, 4), (2048, 16))
HEADS_PER_GROUP = 4
GROUP_W = HEADS_PER_GROUP * HEAD_DIM
N_DIL = len(ATT_GROUPS)
ATT_W = N_DIL * GROUP_W
BLK = 128
REL_BUCKETS = 32
REL_MAX_DIST = 2048
CHUNK = 128
GMLP_W = 768
GMLP_GD = 64
N_BRANCH = 2
IN_W = 3 * ATT_W + 2 * GMLP_W + N_BRANCH * D_MODEL
N_EXPERTS = 32
TOP_K = 4
D_EXPERT = D_MODEL
SWIGLU_LIMIT = 7.0
SWIGLU_ALPHA = 1.702
PLE_DIM = 256
EPS = 1e-6
MASKED = -1e30

QKV_G = 3 * GROUP_W

LANES = 128
ROW_SUBLANES = D_MODEL // LANES
assert ROW_SUBLANES == 8
MXU_N = 256
VMEM_LIMIT = 56 * 1024 * 1024

TM_PROJ = 512
TM_TOK = 256
TM_EXP = 512
assert TM_EXP % TM_TOK == 0


def _cparams(*sem):
    return pltpu.CompilerParams(dimension_semantics=sem, vmem_limit_bytes=VMEM_LIMIT)


def _resident(shape):
    nd = len(shape)
    return pl.BlockSpec(shape, lambda *_: (0,) * nd, pipeline_mode=pl.Buffered(1))


def _rms(x, g):
    return x * lax.rsqrt(jnp.mean(x * x, axis=-1, keepdims=True) + EPS) * g


def _load_weight_bf16(w_hbm, w_bf, stage, sem):
    rows = stage.shape[1]
    n_chunks = w_hbm.shape[0] // rows
    assert n_chunks * rows == w_hbm.shape[0] and stage.shape[2] == w_hbm.shape[1]

    def chunk(c):
        return pltpu.make_async_copy(w_hbm.at[pl.ds(c * rows, rows)], stage.at[c % 2], sem.at[c % 2])

    chunk(0).start()
    for c in range(n_chunks):
        if c + 1 < n_chunks:
            chunk(c + 1).start()
        chunk(c).wait()
        w_bf[c * rows:(c + 1) * rows, :] = stage[c % 2].astype(BF16)


def _inproj_kernel(x_ref, g_ref, w_hbm, a1_ref, a2_ref, a3_ref, uv_ref, gl_ref, scr, w_ref, w_stage, w_sem):
    @pl.when(pl.program_id(0) == 0)
    def _():
        _load_weight_bf16(w_hbm, w_ref, w_stage, w_sem)

    tm = x_ref.shape[0]
    n = _rms(x_ref[...], g_ref[...]).astype(BF16)
    att_refs = (a1_ref, a2_ref, a3_ref)
    n_att, n_uv = 3 * ATT_W // MXU_N, 2 * GMLP_W // MXU_N
    for c in range(IN_W // MXU_N):
        z = jnp.dot(n, w_ref[:, c * MXU_N:(c + 1) * MXU_N], preferred_element_type=F32)
        if c < n_att:
            which, g = divmod(c, N_DIL)
            d = ATT_GROUPS[g][1]
            dst = att_refs[g]
            cols = slice(which * GROUP_W, (which + 1) * GROUP_W)
            if d == 1:
                dst[0, 0, :, cols] = z.astype(BF16)
                continue
            scr[0] = z[:, :LANES]
            scr[1] = z[:, LANES:]
            for r in range(d):
                zr = jnp.concatenate([scr[0, pl.ds(r, tm // d, stride=d), :],
                                      scr[1, pl.ds(r, tm // d, stride=d), :]], axis=1)
                dst[0, r, :, cols] = zr.astype(BF16)
        elif c < n_att + n_uv:
            uv_ref[:, (c - n_att) * MXU_N:(c - n_att + 1) * MXU_N] = z.astype(BF16)
        else:
            gl_ref[:, (c - n_att - n_uv) * MXU_N:(c - n_att - n_uv + 1) * MXU_N] = z.astype(BF16)


def _plane_spec(d, tm, tiles_per_seq, width):
    return pl.BlockSpec((1, d, tm // d, width), lambda i: (i // tiles_per_seq, 0, i % tiles_per_seq, 0))


W_STAGE_ROWS = 128


def _weight_stage(width):
    return [pltpu.VMEM((2, W_STAGE_ROWS, width), F32), pltpu.SemaphoreType.DMA((2,))]


def _in_proj(x2, g, w, B, S):
    T = x2.shape[0]
    tm = TM_PROJ
    row = lambda w: pl.BlockSpec((tm, w), lambda i: (i, 0))
    dils = [d for _, d in ATT_GROUPS]
    return pl.pallas_call(
        _inproj_kernel,
        grid=(T // tm,),
        in_specs=[row(D_MODEL), _resident((1, D_MODEL)), pl.BlockSpec(memory_space=pl.ANY)],
        out_specs=[_plane_spec(d, tm, S // tm, QKV_G) for d in dils] + [row(2 * GMLP_W), row(N_BRANCH * D_MODEL)],
        out_shape=[jax.ShapeDtypeStruct((B, d, S // d, QKV_G), BF16) for d in dils]
                  + [jax.ShapeDtypeStruct((T, 2 * GMLP_W), BF16),
                     jax.ShapeDtypeStruct((T, N_BRANCH * D_MODEL), BF16)],
        scratch_shapes=[pltpu.VMEM((2, tm, LANES), F32), pltpu.VMEM((D_MODEL, IN_W), BF16)] + _weight_stage(IN_W),
        compiler_params=_cparams("arbitrary"),
        name="in_proj",
    )(x2, g, w)


def _t5_bucket(n):
    exact = REL_BUCKETS // 2
    nf = np.maximum(n, 1).astype(np.float32)
    large = exact + (np.log(nf / exact) / np.log(REL_MAX_DIST / exact) * (REL_BUCKETS - exact)).astype(np.int32)
    large = np.minimum(large, REL_BUCKETS - 1)
    return np.where(n < exact, n, large).astype(np.int32)


def _bias_table(rel_bias_g, dilation):
    n = 3 * BLK
    dist = 2 * BLK - 1 - np.arange(n)
    valid = (dist >= 0) & (dist <= BLK)
    bucket = _t5_bucket(np.clip(dist, 0, BLK) * dilation)
    c = jnp.where(jnp.asarray(valid)[None, :], rel_bias_g.astype(F32)[bucket].T, MASKED)
    shifted = jnp.tile(c, (1, BLK))[:, :BLK * (n - 1)].reshape(HEADS_PER_GROUP, BLK, n - 1)
    return shifted[:, :, BLK - 1:].reshape(HEADS_PER_GROUP * BLK, 2 * BLK)


def _attn_kernel(cur_ref, prev_ref, bias_ref, o_ref, lse_ref):
    rg, rb = cur_ref.shape[1], cur_ref.shape[2] // BLK
    starts_sequence = pl.program_id(2) == 0
    lane_head = lax.broadcasted_iota(jnp.int32, (1, GROUP_W), 1) // HEAD_DIM
    head_bf = [(lane_head == h).astype(BF16) for h in range(HEADS_PER_GROUP)]
    head_f = [(lane_head == h).astype(F32) for h in range(HEADS_PER_GROUP)]
    key_is_prev = lax.broadcasted_iota(jnp.int32, (1, 2 * BLK), 1) < BLK
    nt = (((1,), (1,)), ((), ()))
    scale = HEAD_DIM ** -0.5
    qc, kc_, vc_ = slice(0, GROUP_W), slice(GROUP_W, 2 * GROUP_W), slice(2 * GROUP_W, 3 * GROUP_W)
    for r, j in [(r, j) for r in range(rg) for j in range(rb)]:
        rows = slice(j * BLK, (j + 1) * BLK)
        prev = prev_ref if j == 0 else cur_ref
        prows = slice(0, BLK) if j == 0 else slice((j - 1) * BLK, j * BLK)
        q = cur_ref[0, r, rows, qc]
        k = jnp.concatenate([prev[0, r, prows, kc_], cur_ref[0, r, rows, kc_]], axis=0)
        v = jnp.concatenate([prev[0, r, prows, vc_], cur_ref[0, r, rows, vc_]], axis=0)
        q_bd = jnp.concatenate([q * head_bf[h] for h in range(HEADS_PER_GROUP)], axis=0)
        s = lax.dot_general(q_bd, k, nt, preferred_element_type=F32) * scale + bias_ref[...]
        if j == 0:
            s = jnp.where(jnp.logical_and(starts_sequence, key_is_prev), MASKED, s)
        m = jnp.max(s, axis=-1, keepdims=True)
        p = jnp.exp(s - m)
        den = jnp.sum(p, axis=-1, keepdims=True)
        o = jnp.dot(p.astype(BF16), v, preferred_element_type=F32) / den
        l = m + jnp.log(den)
        out = jnp.zeros((BLK, GROUP_W), F32)
        lse = jnp.zeros((BLK, GROUP_W), F32)
        for h in range(HEADS_PER_GROUP):
            hr = slice(h * BLK, (h + 1) * BLK)
            out = out + o[hr] * head_f[h]
            lse = lse + l[hr] * head_f[h]
        o_ref[0, r, rows, :] = out.astype(BF16)
        lse_ref[0, r, rows, :] = lse


ATT_SUBBLOCKS = 8


def _attention_group(a, bias, dilation, B, S):
    sd = S // dilation
    rb = min(ATT_SUBBLOCKS, sd // BLK)
    rg = min(ATT_SUBBLOCKS // rb, dilation)
    o, lse = pl.pallas_call(
        _attn_kernel,
        grid=(B, dilation // rg, sd // (rb * BLK)),
        in_specs=[pl.BlockSpec((1, rg, rb * BLK, QKV_G), lambda b, r, n: (b, r, n, 0)),
                  pl.BlockSpec((1, rg, BLK, QKV_G), lambda b, r, n: (b, r, jnp.maximum(n * rb - 1, 0), 0)),
                  _resident((HEADS_PER_GROUP * BLK, 2 * BLK))],
        out_specs=[pl.BlockSpec((1, rg, rb * BLK, GROUP_W), lambda b, r, n: (b, r, n, 0))] * 2,
        out_shape=[jax.ShapeDtypeStruct((B, dilation, sd, GROUP_W), BF16),
                   jax.ShapeDtypeStruct((B, dilation, sd, GROUP_W), F32)],
        compiler_params=_cparams("parallel", "parallel", "parallel"),
        name=f"attn_d{dilation}",
    )(a, a, bias)
    return o, lse


def _gelu(x):
    return x * (lax.erf(x * (2.0 ** -0.5)) + 1.0) * 0.5


def _token_major(src_ref, d, scr, slot, tm):
    if d == 1:
        return src_ref[0, 0].astype(F32)
    for r in range(d):
        piece = src_ref[0, r].astype(F32)
        scr[slot, pl.ds(r, tm // d, stride=d), :] = piece[:, :LANES]
        scr[slot + 1, pl.ds(r, tm // d, stride=d), :] = piece[:, LANES:]
    return jnp.concatenate([scr[slot], scr[slot + 1]], axis=1)


def _mix_kernel(x_ref, o1_ref, o2_ref, o3_ref, l1_ref, l2_ref, l3_ref, uv_ref, gl_ref,
                wa_hbm, wg_hbm, wo_hbm, wc_ref, bs_ref, lng_ref, lnb_ref, h_ref, g_scr, t_scr,
                wa_ref, wg_ref, wo_ref, w_stage, w_sem):
    @pl.when(pl.program_id(0) == 0)
    def _():
        for w_hbm, w_bf in ((wa_hbm, wa_ref), (wg_hbm, wg_ref), (wo_hbm, wo_ref)):
            _load_weight_bf16(w_hbm, w_bf, w_stage, w_sem)

    tm = x_ref.shape[0]
    dils = [d for _, d in ATT_GROUPS]
    o1, o2, o3 = [_token_major(ref, d, t_scr, 4 * i, tm) for i, (ref, d) in enumerate(zip((o1_ref, o2_ref, o3_ref), dils))]
    l1, l2, l3 = [_token_major(ref, d, t_scr, 4 * i + 2, tm) for i, (ref, d) in enumerate(zip((l1_ref, l2_ref, l3_ref), dils))]
    lm = jnp.maximum(jnp.maximum(l1, l2), l3)
    e1, e2, e3 = jnp.exp(l1 - lm), jnp.exp(l2 - lm), jnp.exp(l3 - lm)
    att = (e1 * o1 + e2 * o2 + e3 * o3) / (e1 + e2 + e3)
    y_att = jnp.dot(att.astype(BF16), wa_ref[...], preferred_element_type=F32)

    zu = _gelu(uv_ref[:, :GMLP_W].astype(F32))
    zv = _gelu(uv_ref[:, GMLP_W:].astype(F32))
    mu = jnp.mean(zv, axis=-1, keepdims=True)
    var = jnp.mean(jnp.square(zv - mu), axis=-1, keepdims=True)
    vn = (zv - mu) * lax.rsqrt(var + EPS) * lng_ref[...] + lnb_ref[...]
    low_half = lax.broadcasted_iota(jnp.int32, (CHUNK, 2 * GMLP_GD), 1) < GMLP_GD
    for c in range(tm // CHUNK):
        rows = slice(c * CHUNK, (c + 1) * CHUNK)
        for s in range(GMLP_W // (2 * GMLP_GD)):
            cols = slice(s * 2 * GMLP_GD, (s + 1) * 2 * GMLP_GD)
            v2 = vn[rows, cols]
            rhs = jnp.concatenate([jnp.where(low_half, v2, 0.0), jnp.where(low_half, 0.0, v2)], axis=0).astype(BF16)
            mixed = jnp.dot(wc_ref[s], rhs, preferred_element_type=F32) + bs_ref[:, cols]
            g_scr[rows, cols] = (zu[rows, cols] * mixed).astype(BF16)
    y_gm = jnp.dot(g_scr[...], wg_ref[...], preferred_element_type=F32)

    gate_a = jax.nn.sigmoid(gl_ref[:, :D_MODEL].astype(F32))
    gate_g = jax.nn.sigmoid(gl_ref[:, D_MODEL:].astype(F32))
    merged = (gate_a * y_att + gate_g * y_gm).astype(BF16)
    h_ref[...] = x_ref[...] + jnp.dot(merged, wo_ref[...], preferred_element_type=F32)


def _mix(x2, outs, lses, uv, gl, wa, wg, wo, wc2, bs, lng, lnb, S):
    T = x2.shape[0]
    tm = TM_PROJ
    row = lambda w: pl.BlockSpec((tm, w), lambda i: (i, 0))
    att = [_plane_spec(d, tm, S // tm, GROUP_W) for _, d in ATT_GROUPS]
    return pl.pallas_call(
        _mix_kernel,
        grid=(T // tm,),
        in_specs=[row(D_MODEL)] + att + att + [row(2 * GMLP_W), row(N_BRANCH * D_MODEL)]
                 + [pl.BlockSpec(memory_space=pl.ANY)] * 3
                 + [_resident(wc2.shape), _resident(bs.shape), _resident(lng.shape), _resident(lnb.shape)],
        out_specs=row(D_MODEL),
        out_shape=jax.ShapeDtypeStruct((T, D_MODEL), F32),
        scratch_shapes=[pltpu.VMEM((tm, GMLP_W), BF16), pltpu.VMEM((4 * N_DIL, tm, LANES), F32),
                        pltpu.VMEM(wa.shape, BF16), pltpu.VMEM(wg.shape, BF16), pltpu.VMEM(wo.shape, BF16)]
                       + _weight_stage(D_MODEL),
        compiler_params=_cparams("arbitrary"),
        name="mix",
    )(x2, *outs, *lses, uv, gl, wa, wg, wo, wc2, bs, lng, lnb)


def _router_kernel(h_ref, g_ref, wr_ref, br_ref, eidx_ref, gate_ref, rank_ref, cnt_ref, carry):
    tm = h_ref.shape[0]

    @pl.when(pl.program_id(0) == 0)
    def _():
        carry[...] = jnp.zeros_like(carry)

    hn = _rms(h_ref[...], g_ref[...])
    hi = hn.astype(BF16)
    lo = (hn - hi.astype(F32)).astype(BF16)
    by_hi = jnp.dot(hi, wr_ref[...], preferred_element_type=F32)
    by_lo = jnp.dot(lo, wr_ref[:, :N_EXPERTS], preferred_element_type=F32)
    logits = by_hi[:, :N_EXPERTS] + by_hi[:, N_EXPERTS:] + by_lo + br_ref[...]
    lane = lax.broadcasted_iota(jnp.int32, (tm, N_EXPERTS), 1)
    vals, hots = [], []
    l = logits
    for k in range(TOP_K):
        m = jnp.max(l, axis=-1, keepdims=True)
        idx = jnp.min(jnp.where(l == m, lane, N_EXPERTS), axis=-1, keepdims=True)
        hot = lane == idx
        eidx_ref[:, k:k + 1] = idx
        vals.append(m)
        hots.append(hot)
        l = jnp.where(hot, -jnp.inf, l)
    ex = [jnp.exp(v - vals[0]) for v in vals]
    tot = ex[0] + ex[1] + ex[2] + ex[3]
    for k in range(TOP_K):
        gate_ref[:, k:k + 1] = ex[k] / tot
    multi = jnp.zeros((tm, N_EXPERTS), F32)
    for hot in hots:
        multi = multi + hot.astype(F32)
    r = lax.broadcasted_iota(jnp.int32, (tm, tm), 0)
    c = lax.broadcasted_iota(jnp.int32, (tm, tm), 1)
    strict_lower = (c < r).astype(BF16)
    before = jnp.dot(strict_lower, multi.astype(BF16), preferred_element_type=F32) + carry[...]
    for k in range(TOP_K):
        rank_ref[:, k:k + 1] = jnp.sum(jnp.where(hots[k], before, 0.0), axis=-1, keepdims=True).astype(jnp.int32)
    carry[...] += jnp.sum(multi, axis=0, keepdims=True)
    cnt_ref[...] = carry[...]


def _router(h1, g, wr, br):
    T = h1.shape[0]
    tm = TM_PROJ
    col4 = pl.BlockSpec((tm, TOP_K), lambda i: (i, 0))
    return pl.pallas_call(
        _router_kernel,
        grid=(T // tm,),
        in_specs=[pl.BlockSpec((tm, D_MODEL), lambda i: (i, 0)), _resident((1, D_MODEL)),
                  _resident((D_MODEL, 2 * N_EXPERTS)), _resident((1, N_EXPERTS))],
        out_specs=[col4, col4, col4, pl.BlockSpec((1, N_EXPERTS), lambda i: (0, 0))],
        out_shape=[jax.ShapeDtypeStruct((T, TOP_K), jnp.int32), jax.ShapeDtypeStruct((T, TOP_K), F32),
                   jax.ShapeDtypeStruct((T, TOP_K), jnp.int32), jax.ShapeDtypeStruct((1, N_EXPERTS), F32)],
        scratch_shapes=[pltpu.VMEM((1, N_EXPERTS), F32)],
        compiler_params=_cparams("arbitrary"),
        name="router",
    )(h1, g, wr, br)


def _to_row_tiles(ref, lead, value):
    n = value.shape[0]
    for c in range(ROW_SUBLANES):
        ref[(*lead, pl.ds(c, n, stride=ROW_SUBLANES), slice(None))] = value[:, c * LANES:(c + 1) * LANES]


def _from_row_tiles(ref, lead, first, n):
    return jnp.concatenate(
        [ref[(*lead, pl.ds(first * ROW_SUBLANES + c, n, stride=ROW_SUBLANES), slice(None))] for c in range(ROW_SUBLANES)],
        axis=1)


def _tile_rows(idx, n=1):
    return pl.ds(pl.multiple_of(idx * ROW_SUBLANES, ROW_SUBLANES), n * ROW_SUBLANES)


def _dispatch_kernel(last_ref, nv_ref, dest_ref, h_ref, g_ref, wgu_ref, wd_ref, xs_ref, wgu_bf_ref, wd_bf_ref,
                     buf, sem, zero_sem):
    tm = h_ref.shape[0]
    n_blocks = xs_ref.shape[0] // (TM_EXP * ROW_SUBLANES)
    i = pl.program_id(0)
    slot = lax.rem(i, 2)

    wgu_bf_ref[...] = wgu_ref[...].astype(BF16)
    wd_bf_ref[...] = wd_ref[...].astype(BF16)

    @pl.when(i == 0)
    def _():
        buf[1] = jnp.zeros(buf.shape[1:], F32)

        def zero_block(b):
            for part in range(TM_EXP // tm):
                pltpu.make_async_copy(buf.at[1], xs_ref.at[_tile_rows(b * TM_EXP + part * tm, tm)], zero_sem).start()

        def zero_done():
            for part in range(TM_EXP // tm):
                pltpu.make_async_copy(buf.at[1], xs_ref.at[_tile_rows(0, tm)], zero_sem).wait()

        for e in range(N_EXPERTS):
            zero_block(last_ref[e])
        lax.fori_loop(nv_ref[0], n_blocks, lambda b, c: (zero_block(b), c)[1], 0)
        for e in range(N_EXPERTS):
            zero_done()
        lax.fori_loop(nv_ref[0], n_blocks, lambda b, c: (zero_done(), c)[1], 0)

    _to_row_tiles(buf, (slot,), _rms(h_ref[...], g_ref[...]))

    def issue(t, carry):
        for k in range(TOP_K):
            d = dest_ref[0, 0, t * TOP_K + k]
            pltpu.make_async_copy(buf.at[slot, _tile_rows(t)], xs_ref.at[_tile_rows(d)],
                                  sem.at[slot]).start(priority=k % 2)
        return carry

    lax.fori_loop(0, tm, issue, 0, unroll=8)

    def wait_slot(s):
        for _ in range(TOP_K):
            pltpu.make_async_copy(buf.at[s], xs_ref.at[_tile_rows(0, tm)], sem.at[s]).wait()

    @pl.when(i > 0)
    def _():
        wait_slot(1 - slot)

    @pl.when(i == pl.num_programs(0) - 1)
    def _():
        wait_slot(slot)


def _dispatch(last_block, n_valid, dest3, h1, g, n_slots, wgu, wd):
    T = h1.shape[0]
    tm = TM_TOK
    n_tiles = T // tm
    per_expert = n_tiles // N_EXPERTS
    assert per_expert * N_EXPERTS == n_tiles and D_MODEL % per_expert == 0
    w_rows = D_MODEL // per_expert
    w_spec = lambda width: pl.BlockSpec((1, w_rows, width), lambda i, lb, nv: (i // per_expert, i % per_expert, 0))
    grid_spec = pltpu.PrefetchScalarGridSpec(
        num_scalar_prefetch=2,
        grid=(n_tiles,),
        in_specs=[pl.BlockSpec((1, 1, TOP_K * tm), lambda i, lb, nv: (i, 0, 0), memory_space=pltpu.SMEM),
                  pl.BlockSpec((tm, D_MODEL), lambda i, lb, nv: (i, 0)),
                  pl.BlockSpec((1, D_MODEL), lambda i, lb, nv: (0, 0), pipeline_mode=pl.Buffered(1)),
                  w_spec(2 * D_EXPERT), w_spec(D_MODEL)],
        out_specs=[pl.BlockSpec(memory_space=pl.ANY), w_spec(2 * D_EXPERT), w_spec(D_MODEL)],
        scratch_shapes=[pltpu.VMEM((2, tm * ROW_SUBLANES, LANES), F32), pltpu.SemaphoreType.DMA((2,)),
                        pltpu.SemaphoreType.DMA(())],
    )
    return pl.pallas_call(
        _dispatch_kernel,
        grid_spec=grid_spec,
        out_shape=[jax.ShapeDtypeStruct((n_slots * ROW_SUBLANES, LANES), F32),
                   jax.ShapeDtypeStruct(wgu.shape, BF16), jax.ShapeDtypeStruct(wd.shape, BF16)],
        compiler_params=_cparams("arbitrary"),
        name="dispatch",
    )(last_block, n_valid, dest3, h1, g, wgu, wd)


def _experts_kernel(be_ref, nv_ref, x_ref, wgu_ref, bgu_ref, wd_ref, bd_ref, y_ref):
    del be_ref
    tm = x_ref.shape[0] // ROW_SUBLANES

    @pl.when(pl.program_id(0) < nv_ref[0])
    def _():
        x = _from_row_tiles(x_ref, (), 0, tm).astype(BF16)
        gu = jnp.dot(x, wgu_ref[0], preferred_element_type=F32) + bgu_ref[0]
        glu = jnp.minimum(gu[:, :D_EXPERT], SWIGLU_LIMIT)
        lin = jnp.clip(gu[:, D_EXPERT:], -SWIGLU_LIMIT, SWIGLU_LIMIT)
        act = glu * jax.nn.sigmoid(SWIGLU_ALPHA * glu) * (lin + 1.0)
        _to_row_tiles(y_ref, (), jnp.dot(act.astype(BF16), wd_ref[0], preferred_element_type=F32) + bd_ref[0])

    @pl.when(pl.program_id(0) >= nv_ref[0])
    def _():
        y_ref[...] = jnp.zeros_like(y_ref)


def _experts(block_expert, n_valid, xs, wgu, bgu, wd, bd):
    tm = TM_EXP
    n_blocks = xs.shape[0] // (tm * ROW_SUBLANES)
    live = lambda b, be, nv: jnp.maximum(jnp.minimum(b, nv[0] - 1), 0)
    grid_spec = pltpu.PrefetchScalarGridSpec(
        num_scalar_prefetch=2,
        grid=(n_blocks,),
        in_specs=[pl.BlockSpec((tm * ROW_SUBLANES, LANES), lambda b, be, nv: (live(b, be, nv), 0)),
                  pl.BlockSpec((1, D_MODEL, 2 * D_EXPERT), lambda b, be, nv: (be[b], 0, 0)),
                  pl.BlockSpec((1, 1, 2 * D_EXPERT), lambda b, be, nv: (be[b], 0, 0)),
                  pl.BlockSpec((1, D_EXPERT, D_MODEL), lambda b, be, nv: (be[b], 0, 0)),
                  pl.BlockSpec((1, 1, D_MODEL), lambda b, be, nv: (be[b], 0, 0))],
        out_specs=pl.BlockSpec((tm * ROW_SUBLANES, LANES), lambda b, be, nv: (b, 0)),
    )
    return pl.pallas_call(
        _experts_kernel,
        grid_spec=grid_spec,
        out_shape=jax.ShapeDtypeStruct(xs.shape, F32),
        compiler_params=_cparams("arbitrary"),
        name="experts",
    )(block_expert, n_valid, xs, wgu, bgu, wd, bd)


GATHER_AHEAD = 2
GATHER_SLOTS = GATHER_AHEAD + 1


def _combine_kernel(dest0_ref, dest1_ref, ahead_dest_ref, h_ref, gate_ref, p_ref, gp_ref, gf_ref, wpg_hbm, wpp_hbm,
                    ys_ref, o_ref, buf, sem, wpg_ref, wpp_ref, w_stage, w_sem):
    tm = h_ref.shape[0]
    i = pl.program_id(0)

    @pl.when(i == 0)
    def _():
        _load_weight_bf16(wpg_hbm, wpg_ref, w_stage, w_sem)
        _load_weight_bf16(wpp_hbm, wpp_ref, w_stage, w_sem)
    slot = lax.rem(i, GATHER_SLOTS)
    ahead_slot = lax.rem(i + GATHER_AHEAD, GATHER_SLOTS)

    def row_copy(dref, t, k, s):
        d = dref[0, 0, t * TOP_K + k]
        return pltpu.make_async_copy(ys_ref.at[_tile_rows(d)], buf.at[s, _tile_rows(k * tm + t)], sem.at[s])

    def gather(dref, s):
        def issue(t, carry):
            for k in range(TOP_K):
                row_copy(dref, t, k, s).start(priority=k % 2)
            return carry
        lax.fori_loop(0, tm, issue, 0, unroll=8)

    def wait_slot(s):
        for _ in range(TOP_K):
            pltpu.make_async_copy(ys_ref.at[_tile_rows(0, tm)], buf.at[s, _tile_rows(0, tm)], sem.at[s]).wait()

    @pl.when(i == 0)
    def _():
        gather(dest0_ref, 0)
        gather(dest1_ref, 1)

    proj = jnp.dot(p_ref[...].astype(BF16), wpp_ref[...], preferred_element_type=F32)
    wait_slot(slot)
    h = h_ref[...]
    for k in range(TOP_K):
        h = h + gate_ref[:, k:k + 1] * _from_row_tiles(buf, (slot,), k * tm, tm)

    for t in range(tm):
        for k in range(TOP_K):
            row_copy(ahead_dest_ref, t, k, ahead_slot).start(priority=k % 2)

    ple_gate = jax.nn.sigmoid(jnp.dot(_rms(h, gp_ref[...]).astype(BF16), wpg_ref[...], preferred_element_type=F32))
    h = h + ple_gate * proj
    o_ref[...] = _rms(h, gf_ref[...])

    @pl.when(i == pl.num_programs(0) - 1)
    def _():
        for ahead in range(1, GATHER_SLOTS):
            wait_slot(lax.rem(i + ahead, GATHER_SLOTS))


def _combine(dest3, h1, gate, p2, gp, wpg, wpp, gf, ys):
    T = h1.shape[0]
    tm = TM_TOK
    n_tiles = T // tm
    assert n_tiles > GATHER_AHEAD == 2
    row = lambda w: pl.BlockSpec((tm, w), lambda i: (i, 0))
    dest_spec = lambda ahead: pl.BlockSpec((1, 1, TOP_K * tm), lambda i: (jnp.minimum(i + ahead, n_tiles - 1), 0, 0),
                                           memory_space=pltpu.SMEM)
    return pl.pallas_call(
        _combine_kernel,
        grid=(n_tiles,),
        in_specs=[dest_spec(0), dest_spec(1), dest_spec(GATHER_AHEAD),
                  row(D_MODEL), row(TOP_K), row(PLE_DIM), _resident((1, D_MODEL)), _resident((1, D_MODEL))]
                 + [pl.BlockSpec(memory_space=pl.ANY)] * 3,
        out_specs=row(D_MODEL),
        out_shape=jax.ShapeDtypeStruct((T, D_MODEL), F32),
        scratch_shapes=[pltpu.VMEM((GATHER_SLOTS, TOP_K * tm * ROW_SUBLANES, LANES), F32),
                        pltpu.SemaphoreType.DMA((GATHER_SLOTS,)),
                        pltpu.VMEM(wpg.shape, BF16), pltpu.VMEM(wpp.shape, BF16)] + _weight_stage(D_MODEL),
        compiler_params=_cparams("arbitrary"),
        name="combine",
    )(dest3, dest3, dest3, h1, gate, p2, gp, gf, wpg, wpp, ys)


def _layer(h, p_i, g_mix, w_in, rel_bias, w_att_out, ln_v_g, ln_v_b, w_spatial, b_spatial, w_gmlp_out, w_out,
           g_moe, w_router, b_router, w_gate_up, b_gate_up, w_down, b_down, g_ple, w_ple_gate, w_ple_proj,
           g_final, B, S):
    T = B * S
    row = lambda v: v.reshape(1, -1).astype(F32)

    assert S % TM_PROJ == 0
    *att_in, uv, gl = _in_proj(h, row(g_mix), w_in.astype(F32), B, S)

    outs, lses = [], []
    for g, (window, dilation) in enumerate(ATT_GROUPS):
        assert window // dilation == BLK and S % (dilation * BLK) == 0
        bias = _bias_table(rel_bias[:, g * HEADS_PER_GROUP:(g + 1) * HEADS_PER_GROUP], dilation)
        o, lse = _attention_group(att_in[g], bias, dilation, B, S)
        outs.append(o)
        lses.append(lse)

    causal = jnp.asarray(np.tril(np.ones((CHUNK, CHUNK), np.float32)))
    w_c = (w_spatial.astype(F32) * causal[None]).astype(BF16)
    wc2 = jnp.concatenate([w_c[0::2], w_c[1::2]], axis=2)
    bs = jnp.repeat(b_spatial.astype(F32).T, GMLP_GD, axis=1)
    h1 = _mix(h, outs, lses, uv, gl, w_att_out.astype(F32), w_gmlp_out.astype(F32), w_out.astype(F32),
              wc2, bs, row(ln_v_g), row(ln_v_b), S)

    wr_hi = w_router.astype(BF16)
    wr_lo = (w_router.astype(F32) - wr_hi.astype(F32)).astype(BF16)
    eidx, gate, rank, counts = _router(h1, row(g_moe), jnp.concatenate([wr_hi, wr_lo], axis=1), row(b_router))
    cnt = counts[0].astype(jnp.int32)
    blk_counts = (cnt + TM_EXP - 1) // TM_EXP
    blk_end = jnp.cumsum(blk_counts)
    pad_start = (blk_end - blk_counts) * TM_EXP
    n_blocks = T * TOP_K // TM_EXP + N_EXPERTS
    n_valid = blk_end[-1:].astype(jnp.int32)
    blk = jnp.minimum(jnp.arange(n_blocks, dtype=jnp.int32), n_valid[0] - 1)
    block_expert = jnp.minimum(jnp.sum((blk_end[None, :] <= blk[:, None]).astype(jnp.int32), axis=1), N_EXPERTS - 1)
    expert_ids = jnp.arange(N_EXPERTS, dtype=jnp.int32)
    dest = rank + jnp.sum(jnp.where(eidx[..., None] == expert_ids, pad_start, 0), axis=-1)
    dest3 = dest.reshape(T // TM_TOK, 1, TM_TOK * TOP_K)

    last_block = jnp.maximum(blk_end - 1, 0).astype(jnp.int32)
    xs, wgu_bf, wd_bf = _dispatch(last_block, n_valid, dest3, h1, row(g_moe), n_blocks * TM_EXP,
                                  w_gate_up.astype(F32), w_down.astype(F32))
    ys = _experts(block_expert, n_valid, xs, wgu_bf, b_gate_up.reshape(N_EXPERTS, 1, -1).astype(F32),
                  wd_bf, b_down.reshape(N_EXPERTS, 1, -1).astype(F32))
    return _combine(dest3, h1, gate, p_i, row(g_ple), w_ple_gate.astype(F32), w_ple_proj.astype(F32),
                    row(g_final), ys)


def kernel(x, p, g_mix, w_in, rel_bias, w_att_out, ln_v_g, ln_v_b, w_spatial, b_spatial, w_gmlp_out, w_out, g_moe, w_router, b_router, w_gate_up, b_gate_up, w_down, b_down, g_ple, w_ple_gate, w_ple_proj, g_final):
    B, S, D = x.shape
    depth = p.shape[0]
    assert depth == 1, "the final RMSNorm is fused into the (single) layer's last kernel"
    out = _layer(x.reshape(B * S, D), p[0].reshape(B * S, PLE_DIM), g_mix[0], w_in[0], rel_bias, w_att_out[0],
                 ln_v_g[0], ln_v_b[0], w_spatial[0], b_spatial[0], w_gmlp_out[0], w_out[0], g_moe[0], w_router[0],
                 b_router[0], w_gate_up[0], b_gate_up[0], w_down[0], b_down[0], g_ple[0], w_ple_gate[0],
                 w_ple_proj[0], g_final, B, S)
    return out.reshape(B, S, D)
```

```python
import functools

import jax
import jax.numpy as jnp
import numpy as np
from jax import lax
from jax.experimental import pallas as pl
from jax.experimental.pallas import tpu as pltpu

F32 = jnp.float32
BF16 = jnp.bfloat16

D_MODEL = 1024
HEAD_DIM = 64
ATT_GROUPS = ((128, 1), (512, 4), (2048, 16))
HEADS_PER_GROUP = 4
GROUP_W = HEADS_PER_GROUP * HEAD_DIM
N_DIL = len(ATT_GROUPS)
ATT_W = N_DIL * GROUP_W
BLK = 128
REL_BUCKETS = 32
REL_MAX_DIST = 2048
CHUNK = 128
GMLP_W = 768
GMLP_GD = 64
N_BRANCH = 2
IN_W = 3 * ATT_W + 2 * GMLP_W + N_BRANCH * D_MODEL
N_EXPERTS = 32
TOP_K = 4
D_EXPERT = D_MODEL
SWIGLU_LIMIT = 7.0
SWIGLU_ALPHA = 1.702
PLE_DIM = 256
EPS = 1e-6
MASKED = -1e30

QKV_G = 3 * GROUP_W

LANES = 128
ROW_SUBLANES = D_MODEL // LANES
assert ROW_SUBLANES == 8
MXU_N = 256
VMEM_LIMIT = 56 * 1024 * 1024

TM_PROJ = 512
TM_TOK = 256
TM_EXP = 512
assert TM_EXP % TM_TOK == 0


def _cparams(*sem):
    return pltpu.CompilerParams(dimension_semantics=sem, vmem_limit_bytes=VMEM_LIMIT)


def _resident(shape):
    nd = len(shape)
    return pl.BlockSpec(shape, lambda *_: (0,) * nd, pipeline_mode=pl.Buffered(1))


def _rms(x, g):
    return x * lax.rsqrt(jnp.mean(x * x, axis=-1, keepdims=True) + EPS) * g


def _load_weight_bf16(w_hbm, w_bf, stage, sem):
    rows = stage.shape[1]
    n_chunks = w_hbm.shape[0] // rows
    assert n_chunks * rows == w_hbm.shape[0] and stage.shape[2] == w_hbm.shape[1]

    def chunk(c):
        return pltpu.make_async_copy(w_hbm.at[pl.ds(c * rows, rows)], stage.at[c % 2], sem.at[c % 2])

    chunk(0).start()
    for c in range(n_chunks):
        if c + 1 < n_chunks:
            chunk(c + 1).start()
        chunk(c).wait()
        w_bf[c * rows:(c + 1) * rows, :] = stage[c % 2].astype(BF16)


def _expert_slice_spec(n_steps, width):
    per_expert = n_steps // N_EXPERTS
    assert per_expert * N_EXPERTS == n_steps and D_MODEL % per_expert == 0
    return pl.BlockSpec((1, D_MODEL // per_expert, width), lambda i: (i // per_expert, i % per_expert, 0))


def _inproj_kernel(x_ref, g_ref, w_hbm, we_ref, a1_ref, a2_ref, a3_ref, uv_ref, gl_ref, we_bf_ref,
                   scr, w_ref, w_stage, w_sem):
    @pl.when(pl.program_id(0) == 0)
    def _():
        _load_weight_bf16(w_hbm, w_ref, w_stage, w_sem)

    we_bf_ref[...] = we_ref[...].astype(BF16)

    tm = x_ref.shape[0]
    n = _rms(x_ref[...], g_ref[...]).astype(BF16)
    att_refs = (a1_ref, a2_ref, a3_ref)
    n_att, n_uv = 3 * ATT_W // MXU_N, 2 * GMLP_W // MXU_N
    for c in range(IN_W // MXU_N):
        z = jnp.dot(n, w_ref[:, c * MXU_N:(c + 1) * MXU_N], preferred_element_type=F32)
        if c < n_att:
            which, g = divmod(c, N_DIL)
            d = ATT_GROUPS[g][1]
            dst = att_refs[g]
            cols = slice(which * GROUP_W, (which + 1) * GROUP_W)
            if d == 1:
                dst[0, 0, :, cols] = z.astype(BF16)
                continue
            scr[0] = z[:, :LANES]
            scr[1] = z[:, LANES:]
            for r in range(d):
                zr = jnp.concatenate([scr[0, pl.ds(r, tm // d, stride=d), :],
                                      scr[1, pl.ds(r, tm // d, stride=d), :]], axis=1)
                dst[0, r, :, cols] = zr.astype(BF16)
        elif c < n_att + n_uv:
            uv_ref[:, (c - n_att) * MXU_N:(c - n_att + 1) * MXU_N] = z.astype(BF16)
        else:
            gl_ref[:, (c - n_att - n_uv) * MXU_N:(c - n_att - n_uv + 1) * MXU_N] = z.astype(BF16)


def _plane_spec(d, tm, tiles_per_seq, width):
    return pl.BlockSpec((1, d, tm // d, width), lambda i: (i // tiles_per_seq, 0, i % tiles_per_seq, 0))


W_STAGE_ROWS = 128


def _weight_stage(width):
    return [pltpu.VMEM((2, W_STAGE_ROWS, width), F32), pltpu.SemaphoreType.DMA((2,))]


def _in_proj(x2, g, w, w_expert, B, S):
    T = x2.shape[0]
    tm = TM_PROJ
    row = lambda w: pl.BlockSpec((tm, w), lambda i: (i, 0))
    dils = [d for _, d in ATT_GROUPS]
    we_spec = _expert_slice_spec(T // tm, w_expert.shape[2])
    return pl.pallas_call(
        _inproj_kernel,
        grid=(T // tm,),
        in_specs=[row(D_MODEL), _resident((1, D_MODEL)), pl.BlockSpec(memory_space=pl.ANY), we_spec],
        out_specs=[_plane_spec(d, tm, S // tm, QKV_G) for d in dils] + [row(2 * GMLP_W), row(N_BRANCH * D_MODEL), we_spec],
        out_shape=[jax.ShapeDtypeStruct((B, d, S // d, QKV_G), BF16) for d in dils]
                  + [jax.ShapeDtypeStruct((T, 2 * GMLP_W), BF16),
                     jax.ShapeDtypeStruct((T, N_BRANCH * D_MODEL), BF16),
                     jax.ShapeDtypeStruct(w_expert.shape, BF16)],
        scratch_shapes=[pltpu.VMEM((2, tm, LANES), F32), pltpu.VMEM((D_MODEL, IN_W), BF16)] + _weight_stage(IN_W),
        compiler_params=_cparams("arbitrary"),
        name="in_proj",
    )(x2, g, w, w_expert)


def _t5_bucket(n):
    exact = REL_BUCKETS // 2
    nf = np.maximum(n, 1).astype(np.float32)
    large = exact + (np.log(nf / exact) / np.log(REL_MAX_DIST / exact) * (REL_BUCKETS - exact)).astype(np.int32)
    large = np.minimum(large, REL_BUCKETS - 1)
    return np.where(n < exact, n, large).astype(np.int32)


def _bias_table(rel_bias_g, dilation):
    n = 3 * BLK
    dist = 2 * BLK - 1 - np.arange(n)
    valid = (dist >= 0) & (dist <= BLK)
    bucket = _t5_bucket(np.clip(dist, 0, BLK) * dilation)
    c = jnp.where(jnp.asarray(valid)[None, :], rel_bias_g.astype(F32)[bucket].T, MASKED)
    shifted = jnp.tile(c, (1, BLK))[:, :BLK * (n - 1)].reshape(HEADS_PER_GROUP, BLK, n - 1)
    return shifted[:, :, BLK - 1:].reshape(HEADS_PER_GROUP * BLK, 2 * BLK)


def _attn_kernel(cur_ref, prev_ref, bias_ref, o_ref, lse_ref):
    rg, rb = cur_ref.shape[1], cur_ref.shape[2] // BLK
    starts_sequence = pl.program_id(2) == 0
    lane_head = lax.broadcasted_iota(jnp.int32, (1, GROUP_W), 1) // HEAD_DIM
    head_bf = [(lane_head == h).astype(BF16) for h in range(HEADS_PER_GROUP)]
    head_f = [(lane_head == h).astype(F32) for h in range(HEADS_PER_GROUP)]
    key_is_prev = lax.broadcasted_iota(jnp.int32, (1, 2 * BLK), 1) < BLK
    nt = (((1,), (1,)), ((), ()))
    scale = HEAD_DIM ** -0.5
    qc, kc_, vc_ = slice(0, GROUP_W), slice(GROUP_W, 2 * GROUP_W), slice(2 * GROUP_W, 3 * GROUP_W)
    for r, j in [(r, j) for r in range(rg) for j in range(rb)]:
        rows = slice(j * BLK, (j + 1) * BLK)
        prev = prev_ref if j == 0 else cur_ref
        prows = slice(0, BLK) if j == 0 else slice((j - 1) * BLK, j * BLK)
        q = cur_ref[0, r, rows, qc]
        k = jnp.concatenate([prev[0, r, prows, kc_], cur_ref[0, r, rows, kc_]], axis=0)
        v = jnp.concatenate([prev[0, r, prows, vc_], cur_ref[0, r, rows, vc_]], axis=0)
        q_bd = jnp.concatenate([q * head_bf[h] for h in range(HEADS_PER_GROUP)], axis=0)
        s = lax.dot_general(q_bd, k, nt, preferred_element_type=F32) * scale + bias_ref[...]
        if j == 0:
            s = jnp.where(jnp.logical_and(starts_sequence, key_is_prev), MASKED, s)
        m = jnp.max(s, axis=-1, keepdims=True)
        p = jnp.exp(s - m)
        den = jnp.sum(p, axis=-1, keepdims=True)
        o = jnp.dot(p.astype(BF16), v, preferred_element_type=F32) / den
        l = m + jnp.log(den)
        out = jnp.zeros((BLK, GROUP_W), F32)
        lse = jnp.zeros((BLK, GROUP_W), F32)
        for h in range(HEADS_PER_GROUP):
            hr = slice(h * BLK, (h + 1) * BLK)
            out = out + o[hr] * head_f[h]
            lse = lse + l[hr] * head_f[h]
        o_ref[0, r, rows, :] = out.astype(BF16)
        lse_ref[0, r, rows, :] = lse


ATT_SUBBLOCKS = 8


def _attention_group(a, bias, dilation, B, S):
    sd = S // dilation
    rb = min(ATT_SUBBLOCKS, sd // BLK)
    rg = min(ATT_SUBBLOCKS // rb, dilation)
    o, lse = pl.pallas_call(
        _attn_kernel,
        grid=(B, dilation // rg, sd // (rb * BLK)),
        in_specs=[pl.BlockSpec((1, rg, rb * BLK, QKV_G), lambda b, r, n: (b, r, n, 0)),
                  pl.BlockSpec((1, rg, BLK, QKV_G), lambda b, r, n: (b, r, jnp.maximum(n * rb - 1, 0), 0)),
                  _resident((HEADS_PER_GROUP * BLK, 2 * BLK))],
        out_specs=[pl.BlockSpec((1, rg, rb * BLK, GROUP_W), lambda b, r, n: (b, r, n, 0))] * 2,
        out_shape=[jax.ShapeDtypeStruct((B, dilation, sd, GROUP_W), BF16),
                   jax.ShapeDtypeStruct((B, dilation, sd, GROUP_W), F32)],
        compiler_params=_cparams("parallel", "parallel", "parallel"),
        name=f"attn_d{dilation}",
    )(a, a, bias)
    return o, lse


def _gelu(x):
    return x * (lax.erf(x * (2.0 ** -0.5)) + 1.0) * 0.5


def _token_major(src_ref, d, scr, slot, tm):
    if d == 1:
        return src_ref[0, 0].astype(F32)
    for r in range(d):
        piece = src_ref[0, r].astype(F32)
        scr[slot, pl.ds(r, tm // d, stride=d), :] = piece[:, :LANES]
        scr[slot + 1, pl.ds(r, tm // d, stride=d), :] = piece[:, LANES:]
    return jnp.concatenate([scr[slot], scr[slot + 1]], axis=1)


def _mix_kernel(x_ref, o1_ref, o2_ref, o3_ref, l1_ref, l2_ref, l3_ref, uv_ref, gl_ref,
                wa_hbm, wg_hbm, wo_hbm, wc_ref, bs_ref, lng_ref, lnb_ref, we_ref, h_ref, we_bf_ref, g_scr, t_scr,
                wa_ref, wg_ref, wo_ref, w_stage, w_sem):
    @pl.when(pl.program_id(0) == 0)
    def _():
        for w_hbm, w_bf in ((wa_hbm, wa_ref), (wg_hbm, wg_ref), (wo_hbm, wo_ref)):
            _load_weight_bf16(w_hbm, w_bf, w_stage, w_sem)

    we_bf_ref[...] = we_ref[...].astype(BF16)

    tm = x_ref.shape[0]
    dils = [d for _, d in ATT_GROUPS]
    o1, o2, o3 = [_token_major(ref, d, t_scr, 4 * i, tm) for i, (ref, d) in enumerate(zip((o1_ref, o2_ref, o3_ref), dils))]
    l1, l2, l3 = [_token_major(ref, d, t_scr, 4 * i + 2, tm) for i, (ref, d) in enumerate(zip((l1_ref, l2_ref, l3_ref), dils))]
    lm = jnp.maximum(jnp.maximum(l1, l2), l3)
    e1, e2, e3 = jnp.exp(l1 - lm), jnp.exp(l2 - lm), jnp.exp(l3 - lm)
    att = (e1 * o1 + e2 * o2 + e3 * o3) / (e1 + e2 + e3)
    y_att = jnp.dot(att.astype(BF16), wa_ref[...], preferred_element_type=F32)

    zu = _gelu(uv_ref[:, :GMLP_W].astype(F32))
    zv = _gelu(uv_ref[:, GMLP_W:].astype(F32))
    mu = jnp.mean(zv, axis=-1, keepdims=True)
    var = jnp.mean(jnp.square(zv - mu), axis=-1, keepdims=True)
    vn = (zv - mu) * lax.rsqrt(var + EPS) * lng_ref[...] + lnb_ref[...]
    low_half = lax.broadcasted_iota(jnp.int32, (CHUNK, 2 * GMLP_GD), 1) < GMLP_GD
    for c in range(tm // CHUNK):
        rows = slice(c * CHUNK, (c + 1) * CHUNK)
        for s in range(GMLP_W // (2 * GMLP_GD)):
            cols = slice(s * 2 * GMLP_GD, (s + 1) * 2 * GMLP_GD)
            v2 = vn[rows, cols]
            rhs = jnp.concatenate([jnp.where(low_half, v2, 0.0), jnp.where(low_half, 0.0, v2)], axis=0).astype(BF16)
            mixed = jnp.dot(wc_ref[s], rhs, preferred_element_type=F32) + bs_ref[:, cols]
            g_scr[rows, cols] = (zu[rows, cols] * mixed).astype(BF16)
    y_gm = jnp.dot(g_scr[...], wg_ref[...], preferred_element_type=F32)

    gate_a = jax.nn.sigmoid(gl_ref[:, :D_MODEL].astype(F32))
    gate_g = jax.nn.sigmoid(gl_ref[:, D_MODEL:].astype(F32))
    merged = (gate_a * y_att + gate_g * y_gm).astype(BF16)
    h_ref[...] = x_ref[...] + jnp.dot(merged, wo_ref[...], preferred_element_type=F32)


def _mix(x2, outs, lses, uv, gl, wa, wg, wo, wc2, bs, lng, lnb, w_expert, S):
    T = x2.shape[0]
    tm = TM_PROJ
    row = lambda w: pl.BlockSpec((tm, w), lambda i: (i, 0))
    att = [_plane_spec(d, tm, S // tm, GROUP_W) for _, d in ATT_GROUPS]
    we_spec = _expert_slice_spec(T // tm, w_expert.shape[2])
    return pl.pallas_call(
        _mix_kernel,
        grid=(T // tm,),
        in_specs=[row(D_MODEL)] + att + att + [row(2 * GMLP_W), row(N_BRANCH * D_MODEL)]
                 + [pl.BlockSpec(memory_space=pl.ANY)] * 3
                 + [_resident(wc2.shape), _resident(bs.shape), _resident(lng.shape), _resident(lnb.shape), we_spec],
        out_specs=[row(D_MODEL), we_spec],
        out_shape=[jax.ShapeDtypeStruct((T, D_MODEL), F32), jax.ShapeDtypeStruct(w_expert.shape, BF16)],
        scratch_shapes=[pltpu.VMEM((tm, GMLP_W), BF16), pltpu.VMEM((4 * N_DIL, tm, LANES), F32),
                        pltpu.VMEM(wa.shape, BF16), pltpu.VMEM(wg.shape, BF16), pltpu.VMEM(wo.shape, BF16)]
                       + _weight_stage(D_MODEL),
        compiler_params=_cparams("arbitrary"),
        name="mix",
    )(x2, *outs, *lses, uv, gl, wa, wg, wo, wc2, bs, lng, lnb, w_expert)


def _router_kernel(h_ref, g_ref, wr_ref, br_ref, eidx_ref, gate_ref, rank_ref, cnt_ref, carry):
    tm = h_ref.shape[0]

    @pl.when(pl.program_id(0) == 0)
    def _():
        carry[...] = jnp.zeros_like(carry)

    hn = _rms(h_ref[...], g_ref[...])
    hi = hn.astype(BF16)
    lo = (hn - hi.astype(F32)).astype(BF16)
    by_hi = jnp.dot(hi, wr_ref[...], preferred_element_type=F32)
    by_lo = jnp.dot(lo, wr_ref[:, :N_EXPERTS], preferred_element_type=F32)
    logits = by_hi[:, :N_EXPERTS] + by_hi[:, N_EXPERTS:] + by_lo + br_ref[...]
    lane = lax.broadcasted_iota(jnp.int32, (tm, N_EXPERTS), 1)
    vals, hots = [], []
    l = logits
    for k in range(TOP_K):
        m = jnp.max(l, axis=-1, keepdims=True)
        idx = jnp.min(jnp.where(l == m, lane, N_EXPERTS), axis=-1, keepdims=True)
        hot = lane == idx
        eidx_ref[:, k:k + 1] = idx
        vals.append(m)
        hots.append(hot)
        l = jnp.where(hot, -jnp.inf, l)
    ex = [jnp.exp(v - vals[0]) for v in vals]
    tot = ex[0] + ex[1] + ex[2] + ex[3]
    for k in range(TOP_K):
        gate_ref[:, k:k + 1] = ex[k] / tot
    multi = jnp.zeros((tm, N_EXPERTS), F32)
    for hot in hots:
        multi = multi + hot.astype(F32)
    r = lax.broadcasted_iota(jnp.int32, (tm, tm), 0)
    c = lax.broadcasted_iota(jnp.int32, (tm, tm), 1)
    strict_lower = (c < r).astype(BF16)
    before = jnp.dot(strict_lower, multi.astype(BF16), preferred_element_type=F32) + carry[...]
    for k in range(TOP_K):
        rank_ref[:, k:k + 1] = jnp.sum(jnp.where(hots[k], before, 0.0), axis=-1, keepdims=True).astype(jnp.int32)
    carry[...] += jnp.sum(multi, axis=0, keepdims=True)
    cnt_ref[...] = carry[...]


def _router(h1, g, wr, br):
    T = h1.shape[0]
    tm = TM_PROJ
    col4 = pl.BlockSpec((tm, TOP_K), lambda i: (i, 0))
    return pl.pallas_call(
        _router_kernel,
        grid=(T // tm,),
        in_specs=[pl.BlockSpec((tm, D_MODEL), lambda i: (i, 0)), _resident((1, D_MODEL)),
                  _resident((D_MODEL, 2 * N_EXPERTS)), _resident((1, N_EXPERTS))],
        out_specs=[col4, col4, col4, pl.BlockSpec((1, N_EXPERTS), lambda i: (0, 0))],
        out_shape=[jax.ShapeDtypeStruct((T, TOP_K), jnp.int32), jax.ShapeDtypeStruct((T, TOP_K), F32),
                   jax.ShapeDtypeStruct((T, TOP_K), jnp.int32), jax.ShapeDtypeStruct((1, N_EXPERTS), F32)],
        scratch_shapes=[pltpu.VMEM((1, N_EXPERTS), F32)],
        compiler_params=_cparams("arbitrary"),
        name="router",
    )(h1, g, wr, br)


def _to_row_tiles(ref, lead, value):
    n = value.shape[0]
    for c in range(ROW_SUBLANES):
        ref[(*lead, pl.ds(c, n, stride=ROW_SUBLANES), slice(None))] = value[:, c * LANES:(c + 1) * LANES]


def _from_row_tiles(ref, lead, first, n):
    return jnp.concatenate(
        [ref[(*lead, pl.ds(first * ROW_SUBLANES + c, n, stride=ROW_SUBLANES), slice(None))] for c in range(ROW_SUBLANES)],
        axis=1)


def _tile_rows(idx, n=1):
    return pl.ds(pl.multiple_of(idx * ROW_SUBLANES, ROW_SUBLANES), n * ROW_SUBLANES)


def _dispatch_kernel(last_ref, nv_ref, dest_ref, h_ref, g_ref, xs_ref, buf, sem, zero_sem):
    tm = h_ref.shape[0]
    n_blocks = xs_ref.shape[0] // (TM_EXP * ROW_SUBLANES)
    i = pl.program_id(0)
    slot = lax.rem(i, 2)

    @pl.when(i == 0)
    def _():
        buf[1] = jnp.zeros(buf.shape[1:], F32)

        def zero_block(b):
            for part in range(TM_EXP // tm):
                pltpu.make_async_copy(buf.at[1], xs_ref.at[_tile_rows(b * TM_EXP + part * tm, tm)], zero_sem).start()

        def zero_done():
            for part in range(TM_EXP // tm):
                pltpu.make_async_copy(buf.at[1], xs_ref.at[_tile_rows(0, tm)], zero_sem).wait()

        for e in range(N_EXPERTS):
            zero_block(last_ref[e])
        lax.fori_loop(nv_ref[0], n_blocks, lambda b, c: (zero_block(b), c)[1], 0)
        for e in range(N_EXPERTS):
            zero_done()
        lax.fori_loop(nv_ref[0], n_blocks, lambda b, c: (zero_done(), c)[1], 0)

    _to_row_tiles(buf, (slot,), _rms(h_ref[...], g_ref[...]))

    def issue(t, carry):
        for k in range(TOP_K):
            d = dest_ref[0, 0, t * TOP_K + k]
            pltpu.make_async_copy(buf.at[slot, _tile_rows(t)], xs_ref.at[_tile_rows(d)],
                                  sem.at[slot]).start(priority=k % 2)
        return carry

    lax.fori_loop(0, tm, issue, 0, unroll=8)

    def wait_slot(s):
        for _ in range(TOP_K):
            pltpu.make_async_copy(buf.at[s], xs_ref.at[_tile_rows(0, tm)], sem.at[s]).wait()

    @pl.when(i > 0)
    def _():
        wait_slot(1 - slot)

    @pl.when(i == pl.num_programs(0) - 1)
    def _():
        wait_slot(slot)


def _dispatch(last_block, n_valid, dest3, h1, g, n_slots):
    T = h1.shape[0]
    tm = TM_TOK
    grid_spec = pltpu.PrefetchScalarGridSpec(
        num_scalar_prefetch=2,
        grid=(T // tm,),
        in_specs=[pl.BlockSpec((1, 1, TOP_K * tm), lambda i, lb, nv: (i, 0, 0), memory_space=pltpu.SMEM),
                  pl.BlockSpec((tm, D_MODEL), lambda i, lb, nv: (i, 0)),
                  pl.BlockSpec((1, D_MODEL), lambda i, lb, nv: (0, 0), pipeline_mode=pl.Buffered(1))],
        out_specs=pl.BlockSpec(memory_space=pl.ANY),
        scratch_shapes=[pltpu.VMEM((2, tm * ROW_SUBLANES, LANES), F32), pltpu.SemaphoreType.DMA((2,)),
                        pltpu.SemaphoreType.DMA(())],
    )
    return pl.pallas_call(
        _dispatch_kernel,
        grid_spec=grid_spec,
        out_shape=jax.ShapeDtypeStruct((n_slots * ROW_SUBLANES, LANES), F32),
        compiler_params=_cparams("arbitrary"),
        name="dispatch",
    )(last_block, n_valid, dest3, h1, g)


def _experts_kernel(be_ref, nv_ref, x_ref, wgu_ref, bgu_ref, wd_ref, bd_ref, y_ref):
    del be_ref
    tm = x_ref.shape[0] // ROW_SUBLANES

    @pl.when(pl.program_id(0) < nv_ref[0])
    def _():
        x = _from_row_tiles(x_ref, (), 0, tm).astype(BF16)
        gu = jnp.dot(x, wgu_ref[0], preferred_element_type=F32) + bgu_ref[0]
        glu = jnp.minimum(gu[:, :D_EXPERT], SWIGLU_LIMIT)
        lin = jnp.clip(gu[:, D_EXPERT:], -SWIGLU_LIMIT, SWIGLU_LIMIT)
        act = glu * jax.nn.sigmoid(SWIGLU_ALPHA * glu) * (lin + 1.0)
        _to_row_tiles(y_ref, (), jnp.dot(act.astype(BF16), wd_ref[0], preferred_element_type=F32) + bd_ref[0])

    @pl.when(pl.program_id(0) >= nv_ref[0])
    def _():
        y_ref[...] = jnp.zeros_like(y_ref)


def _experts(block_expert, n_valid, xs, wgu, bgu, wd, bd):
    tm = TM_EXP
    n_blocks = xs.shape[0] // (tm * ROW_SUBLANES)
    live = lambda b, be, nv: jnp.maximum(jnp.minimum(b, nv[0] - 1), 0)
    grid_spec = pltpu.PrefetchScalarGridSpec(
        num_scalar_prefetch=2,
        grid=(n_blocks,),
        in_specs=[pl.BlockSpec((tm * ROW_SUBLANES, LANES), lambda b, be, nv: (live(b, be, nv), 0)),
                  pl.BlockSpec((1, D_MODEL, 2 * D_EXPERT), lambda b, be, nv: (be[b], 0, 0)),
                  pl.BlockSpec((1, 1, 2 * D_EXPERT), lambda b, be, nv: (be[b], 0, 0)),
                  pl.BlockSpec((1, D_EXPERT, D_MODEL), lambda b, be, nv: (be[b], 0, 0)),
                  pl.BlockSpec((1, 1, D_MODEL), lambda b, be, nv: (be[b], 0, 0))],
        out_specs=pl.BlockSpec((tm * ROW_SUBLANES, LANES), lambda b, be, nv: (b, 0)),
    )
    return pl.pallas_call(
        _experts_kernel,
        grid_spec=grid_spec,
        out_shape=jax.ShapeDtypeStruct(xs.shape, F32),
        compiler_params=_cparams("arbitrary"),
        name="experts",
    )(block_expert, n_valid, xs, wgu, bgu, wd, bd)


GATHER_AHEAD = 2
GATHER_SLOTS = GATHER_AHEAD + 1


def _combine_kernel(dest0_ref, dest1_ref, ahead_dest_ref, h_ref, gate_ref, p_ref, gp_ref, gf_ref, wpg_hbm, wpp_hbm,
                    ys_ref, o_ref, buf, sem, wpg_ref, wpp_ref, w_stage, w_sem):
    tm = h_ref.shape[0]
    i = pl.program_id(0)

    @pl.when(i == 0)
    def _():
        _load_weight_bf16(wpg_hbm, wpg_ref, w_stage, w_sem)
        _load_weight_bf16(wpp_hbm, wpp_ref, w_stage, w_sem)
    slot = lax.rem(i, GATHER_SLOTS)
    ahead_slot = lax.rem(i + GATHER_AHEAD, GATHER_SLOTS)

    def row_copy(dref, t, k, s):
        d = dref[0, 0, t * TOP_K + k]
        return pltpu.make_async_copy(ys_ref.at[_tile_rows(d)], buf.at[s, _tile_rows(k * tm + t)], sem.at[s])

    def gather(dref, s):
        def issue(t, carry):
            for k in range(TOP_K):
                row_copy(dref, t, k, s).start(priority=k % 2)
            return carry
        lax.fori_loop(0, tm, issue, 0, unroll=8)

    def wait_slot(s):
        for _ in range(TOP_K):
            pltpu.make_async_copy(ys_ref.at[_tile_rows(0, tm)], buf.at[s, _tile_rows(0, tm)], sem.at[s]).wait()

    @pl.when(i == 0)
    def _():
        gather(dest0_ref, 0)
        gather(dest1_ref, 1)

    proj = jnp.dot(p_ref[...].astype(BF16), wpp_ref[...], preferred_element_type=F32)
    wait_slot(slot)
    h = h_ref[...]
    for k in range(TOP_K):
        h = h + gate_ref[:, k:k + 1] * _from_row_tiles(buf, (slot,), k * tm, tm)

    for t in range(tm):
        for k in range(TOP_K):
            row_copy(ahead_dest_ref, t, k, ahead_slot).start(priority=k % 2)

    ple_gate = jax.nn.sigmoid(jnp.dot(_rms(h, gp_ref[...]).astype(BF16), wpg_ref[...], preferred_element_type=F32))
    h = h + ple_gate * proj
    o_ref[...] = _rms(h, gf_ref[...])

    @pl.when(i == pl.num_programs(0) - 1)
    def _():
        for ahead in range(1, GATHER_SLOTS):
            wait_slot(lax.rem(i + ahead, GATHER_SLOTS))


def _combine(dest3, h1, gate, p2, gp, wpg, wpp, gf, ys):
    T = h1.shape[0]
    tm = TM_TOK
    n_tiles = T // tm
    assert n_tiles > GATHER_AHEAD == 2
    row = lambda w: pl.BlockSpec((tm, w), lambda i: (i, 0))
    dest_spec = lambda ahead: pl.BlockSpec((1, 1, TOP_K * tm), lambda i: (jnp.minimum(i + ahead, n_tiles - 1), 0, 0),
                                           memory_space=pltpu.SMEM)
    return pl.pallas_call(
        _combine_kernel,
        grid=(n_tiles,),
        in_specs=[dest_spec(0), dest_spec(1), dest_spec(GATHER_AHEAD),
                  row(D_MODEL), row(TOP_K), row(PLE_DIM), _resident((1, D_MODEL)), _resident((1, D_MODEL))]
                 + [pl.BlockSpec(memory_space=pl.ANY)] * 3,
        out_specs=row(D_MODEL),
        out_shape=jax.ShapeDtypeStruct((T, D_MODEL), F32),
        scratch_shapes=[pltpu.VMEM((GATHER_SLOTS, TOP_K * tm * ROW_SUBLANES, LANES), F32),
                        pltpu.SemaphoreType.DMA((GATHER_SLOTS,)),
                        pltpu.VMEM(wpg.shape, BF16), pltpu.VMEM(wpp.shape, BF16)] + _weight_stage(D_MODEL),
        compiler_params=_cparams("arbitrary"),
        name="combine",
    )(dest3, dest3, dest3, h1, gate, p2, gp, gf, wpg, wpp, ys)


def _layer(h, p_i, g_mix, w_in, rel_bias, w_att_out, ln_v_g, ln_v_b, w_spatial, b_spatial, w_gmlp_out, w_out,
           g_moe, w_router, b_router, w_gate_up, b_gate_up, w_down, b_down, g_ple, w_ple_gate, w_ple_proj,
           g_final, B, S):
    T = B * S
    row = lambda v: v.reshape(1, -1).astype(F32)

    assert S % TM_PROJ == 0
    *att_in, uv, gl, wgu_bf = _in_proj(h, row(g_mix), w_in.astype(F32), w_gate_up.astype(F32), B, S)

    outs, lses = [], []
    for g, (window, dilation) in enumerate(ATT_GROUPS):
        assert window // dilation == BLK and S % (dilation * BLK) == 0
        bias = _bias_table(rel_bias[:, g * HEADS_PER_GROUP:(g + 1) * HEADS_PER_GROUP], dilation)
        o, lse = _attention_group(att_in[g], bias, dilation, B, S)
        outs.append(o)
        lses.append(lse)

    causal = jnp.asarray(np.tril(np.ones((CHUNK, CHUNK), np.float32)))
    w_c = (w_spatial.astype(F32) * causal[None]).astype(BF16)
    wc2 = jnp.concatenate([w_c[0::2], w_c[1::2]], axis=2)
    bs = jnp.repeat(b_spatial.astype(F32).T, GMLP_GD, axis=1)
    h1, wd_bf = _mix(h, outs, lses, uv, gl, w_att_out.astype(F32), w_gmlp_out.astype(F32), w_out.astype(F32),
                     wc2, bs, row(ln_v_g), row(ln_v_b), w_down.astype(F32), S)

    wr_hi = w_router.astype(BF16)
    wr_lo = (w_router.astype(F32) - wr_hi.astype(F32)).astype(BF16)
    eidx, gate, rank, counts = _router(h1, row(g_moe), jnp.concatenate([wr_hi, wr_lo], axis=1), row(b_router))
    cnt = counts[0].astype(jnp.int32)
    blk_counts = (cnt + TM_EXP - 1) // TM_EXP
    blk_end = jnp.cumsum(blk_counts)
    pad_start = (blk_end - blk_counts) * TM_EXP
    n_blocks = T * TOP_K // TM_EXP + N_EXPERTS
    n_valid = blk_end[-1:].astype(jnp.int32)
    blk = jnp.minimum(jnp.arange(n_blocks, dtype=jnp.int32), n_valid[0] - 1)
    block_expert = jnp.minimum(jnp.sum((blk_end[None, :] <= blk[:, None]).astype(jnp.int32), axis=1), N_EXPERTS - 1)
    expert_ids = jnp.arange(N_EXPERTS, dtype=jnp.int32)
    dest = rank + jnp.sum(jnp.where(eidx[..., None] == expert_ids, pad_start, 0), axis=-1)
    dest3 = dest.reshape(T // TM_TOK, 1, TM_TOK * TOP_K)

    last_block = jnp.maximum(blk_end - 1, 0).astype(jnp.int32)
    xs = _dispatch(last_block, n_valid, dest3, h1, row(g_moe), n_blocks * TM_EXP)
    ys = _experts(block_expert, n_valid, xs, wgu_bf, b_gate_up.reshape(N_EXPERTS, 1, -1).astype(F32),
                  wd_bf, b_down.reshape(N_EXPERTS, 1, -1).astype(F32))
    return _combine(dest3, h1, gate, p_i, row(g_ple), w_ple_gate.astype(F32), w_ple_proj.astype(F32),
                    row(g_final), ys)


def kernel(x, p, g_mix, w_in, rel_bias, w_att_out, ln_v_g, ln_v_b, w_spatial, b_spatial, w_gmlp_out, w_out, g_moe, w_router, b_router, w_gate_up, b_gate_up, w_down, b_down, g_ple, w_ple_gate, w_ple_proj, g_final):
    B, S, D = x.shape
    depth = p.shape[0]
    assert depth == 1, "the final RMSNorm is fused into the (single) layer's last kernel"
    out = _layer(x.reshape(B * S, D), p[0].reshape(B * S, PLE_DIM), g_mix[0], w_in[0], rel_bias, w_att_out[0],
                 ln_v_g[0], ln_v_b[0], w_spatial[0], b_spatial[0], w_gmlp_out[0], w_out[0], g_moe[0], w_router[0],
                 b_router[0], w_gate_up[0], b_gate_up[0], w_down[0], b_down[0], g_ple[0], w_ple_gate[0],
                 w_ple_proj[0], g_final, B, S)
    return out.reshape(B, S, D)
```

```python
import functools

import jax
import jax.numpy as jnp
import numpy as np
from jax import lax
from jax.experimental import pallas as pl
from jax.experimental.pallas import tpu as pltpu

F32 = jnp.float32
BF16 = jnp.bfloat16

D_MODEL = 1024
HEAD_DIM = 64
ATT_GROUPS = ((128, 1), (512, 4), (2048, 16))
HEADS_PER_GROUP = 4
GROUP_W = HEADS_PER_GROUP * HEAD_DIM
N_DIL = len(ATT_GROUPS)
ATT_W = N_DIL * GROUP_W
BLK = 128
REL_BUCKETS = 32
REL_MAX_DIST = 2048
CHUNK = 128
GMLP_W = 768
GMLP_GD = 64
N_BRANCH = 2
IN_W = 3 * ATT_W + 2 * GMLP_W + N_BRANCH * D_MODEL
N_EXPERTS = 32
TOP_K = 4
D_EXPERT = D_MODEL
SWIGLU_LIMIT = 7.0
SWIGLU_ALPHA = 1.702
PLE_DIM = 256
EPS = 1e-6
MASKED = -1e30
LOG2E = float(np.log2(np.e))
LN2 = float(np.log(2.0))

QKV_G = 3 * GROUP_W

LANES = 128
ROW_SUBLANES = D_MODEL // LANES
assert ROW_SUBLANES == 8
MXU_N = 256
VMEM_LIMIT = 56 * 1024 * 1024

TM_PROJ = 512
TM_TOK = 256
TM_EXP = 512
assert TM_EXP % TM_TOK == 0


def _cparams(*sem):
    return pltpu.CompilerParams(dimension_semantics=sem, vmem_limit_bytes=VMEM_LIMIT)


def _resident(shape):
    nd = len(shape)
    return pl.BlockSpec(shape, lambda *_: (0,) * nd, pipeline_mode=pl.Buffered(1))


def _rms(x, g):
    return x * lax.rsqrt(jnp.mean(x * x, axis=-1, keepdims=True) + EPS) * g


def _load_weight_bf16(w_hbm, w_bf, stage, sem):
    rows = stage.shape[1]
    n_chunks = w_hbm.shape[0] // rows
    assert n_chunks * rows == w_hbm.shape[0] and stage.shape[2] == w_hbm.shape[1]

    def chunk(c):
        return pltpu.make_async_copy(w_hbm.at[pl.ds(c * rows, rows)], stage.at[c % 2], sem.at[c % 2])

    chunk(0).start()
    for c in range(n_chunks):
        if c + 1 < n_chunks:
            chunk(c + 1).start()
        chunk(c).wait()
        w_bf[c * rows:(c + 1) * rows, :] = stage[c % 2].astype(BF16)


def _expert_slice_spec(n_steps, width):
    per_expert = n_steps // N_EXPERTS
    assert per_expert * N_EXPERTS == n_steps and D_MODEL % per_expert == 0
    return pl.BlockSpec((1, D_MODEL // per_expert, width), lambda i: (i // per_expert, i % per_expert, 0))


def _inproj_kernel(x_ref, g_ref, w_hbm, we_ref, a1_ref, a2_ref, a3_ref, uv_ref, gl_ref, we_bf_ref,
                   scr, w_ref, w_stage, w_sem):
    @pl.when(pl.program_id(0) == 0)
    def _():
        _load_weight_bf16(w_hbm, w_ref, w_stage, w_sem)

    we_bf_ref[...] = we_ref[...].astype(BF16)

    tm = x_ref.shape[0]
    n = _rms(x_ref[...], g_ref[...]).astype(BF16)
    att_refs = (a1_ref, a2_ref, a3_ref)
    n_att, n_uv = 3 * ATT_W // MXU_N, 2 * GMLP_W // MXU_N
    for c in range(IN_W // MXU_N):
        z = jnp.dot(n, w_ref[:, c * MXU_N:(c + 1) * MXU_N], preferred_element_type=F32)
        if c < n_att:
            which, g = divmod(c, N_DIL)
            d = ATT_GROUPS[g][1]
            dst = att_refs[g]
            cols = slice(which * GROUP_W, (which + 1) * GROUP_W)
            if d == 1:
                dst[0, 0, :, cols] = z.astype(BF16)
                continue
            scr[0] = z[:, :LANES]
            scr[1] = z[:, LANES:]
            for r in range(d):
                zr = jnp.concatenate([scr[0, pl.ds(r, tm // d, stride=d), :],
                                      scr[1, pl.ds(r, tm // d, stride=d), :]], axis=1)
                dst[0, r, :, cols] = zr.astype(BF16)
        elif c < n_att + n_uv:
            uv_ref[:, (c - n_att) * MXU_N:(c - n_att + 1) * MXU_N] = z.astype(BF16)
        else:
            gl_ref[:, (c - n_att - n_uv) * MXU_N:(c - n_att - n_uv + 1) * MXU_N] = z.astype(BF16)


def _plane_spec(d, tm, tiles_per_seq, width):
    return pl.BlockSpec((1, d, tm // d, width), lambda i: (i // tiles_per_seq, 0, i % tiles_per_seq, 0))


W_STAGE_ROWS = 128


def _weight_stage(width):
    return [pltpu.VMEM((2, W_STAGE_ROWS, width), F32), pltpu.SemaphoreType.DMA((2,))]


def _in_proj(x2, g, w, w_expert, B, S):
    T = x2.shape[0]
    tm = TM_PROJ
    row = lambda w: pl.BlockSpec((tm, w), lambda i: (i, 0))
    dils = [d for _, d in ATT_GROUPS]
    we_spec = _expert_slice_spec(T // tm, w_expert.shape[2])
    return pl.pallas_call(
        _inproj_kernel,
        grid=(T // tm,),
        in_specs=[row(D_MODEL), _resident((1, D_MODEL)), pl.BlockSpec(memory_space=pl.ANY), we_spec],
        out_specs=[_plane_spec(d, tm, S // tm, QKV_G) for d in dils] + [row(2 * GMLP_W), row(N_BRANCH * D_MODEL), we_spec],
        out_shape=[jax.ShapeDtypeStruct((B, d, S // d, QKV_G), BF16) for d in dils]
                  + [jax.ShapeDtypeStruct((T, 2 * GMLP_W), BF16),
                     jax.ShapeDtypeStruct((T, N_BRANCH * D_MODEL), BF16),
                     jax.ShapeDtypeStruct(w_expert.shape, BF16)],
        scratch_shapes=[pltpu.VMEM((2, tm, LANES), F32), pltpu.VMEM((D_MODEL, IN_W), BF16)] + _weight_stage(IN_W),
        compiler_params=_cparams("arbitrary"),
        name="in_proj",
    )(x2, g, w, w_expert)


def _t5_bucket(n):
    exact = REL_BUCKETS // 2
    nf = np.maximum(n, 1).astype(np.float32)
    large = exact + (np.log(nf / exact) / np.log(REL_MAX_DIST / exact) * (REL_BUCKETS - exact)).astype(np.int32)
    large = np.minimum(large, REL_BUCKETS - 1)
    return np.where(n < exact, n, large).astype(np.int32)


def _bias_table(rel_bias_g, dilation):
    n = 3 * BLK
    dist = 2 * BLK - 1 - np.arange(n)
    valid = (dist >= 0) & (dist <= BLK)
    bucket = _t5_bucket(np.clip(dist, 0, BLK) * dilation)
    c = jnp.where(jnp.asarray(valid)[None, :], rel_bias_g.astype(F32)[bucket].T * LOG2E, MASKED)
    shifted = jnp.tile(c, (1, BLK))[:, :BLK * (n - 1)].reshape(HEADS_PER_GROUP, BLK, n - 1)
    return shifted[:, :, BLK - 1:].reshape(HEADS_PER_GROUP * BLK, 2 * BLK)


def _attn_kernel(cur_ref, prev_ref, bias_ref, o_ref, lse_ref):
    rg, rb = cur_ref.shape[1], cur_ref.shape[2] // BLK
    starts_sequence = pl.program_id(2) == 0
    lane_head = lax.broadcasted_iota(jnp.int32, (1, GROUP_W), 1) // HEAD_DIM
    scale = HEAD_DIM ** -0.5
    head_bf = [jnp.where(lane_head == h, scale, 0.0).astype(BF16) for h in range(HEADS_PER_GROUP)]
    key_is_prev = lax.broadcasted_iota(jnp.int32, (1, 2 * BLK), 1) < BLK
    nt = (((1,), (1,)), ((), ()))
    qc, kc_, vc_ = slice(0, GROUP_W), slice(GROUP_W, 2 * GROUP_W), slice(2 * GROUP_W, 3 * GROUP_W)

    def by_head(x):
        sel = x[(HEADS_PER_GROUP - 1) * BLK:]
        for h in range(HEADS_PER_GROUP - 2, -1, -1):
            sel = jnp.where(lane_head == h, x[h * BLK:(h + 1) * BLK], sel)
        return sel

    for r, j in [(r, j) for r in range(rg) for j in range(rb)]:
        rows = slice(j * BLK, (j + 1) * BLK)
        prev = prev_ref if j == 0 else cur_ref
        prows = slice(0, BLK) if j == 0 else slice((j - 1) * BLK, j * BLK)
        q = cur_ref[0, r, rows, qc]
        k = jnp.concatenate([prev[0, r, prows, kc_], cur_ref[0, r, rows, kc_]], axis=0)
        v = jnp.concatenate([prev[0, r, prows, vc_], cur_ref[0, r, rows, vc_]], axis=0)
        q_bd = jnp.concatenate([q * head_bf[h] for h in range(HEADS_PER_GROUP)], axis=0)
        s = lax.dot_general(q_bd, k, nt, preferred_element_type=F32) * LOG2E + bias_ref[...]
        if j == 0:
            s = jnp.where(jnp.logical_and(starts_sequence, key_is_prev), MASKED, s)
        m = jnp.max(s, axis=-1, keepdims=True)
        p = jnp.exp2(s - m)
        den = jnp.sum(p, axis=-1, keepdims=True)
        o = jnp.dot(p.astype(BF16), v, preferred_element_type=F32)
        o_ref[0, r, rows, :] = (by_head(o) * by_head(1.0 / den)).astype(BF16)
        lse_ref[0, r, rows, :] = jnp.broadcast_to(by_head(m * LN2 + jnp.log(den)), (BLK, GROUP_W))


ATT_SUBBLOCKS = 8


def _attention_group(a, bias, dilation, B, S):
    sd = S // dilation
    rb = min(ATT_SUBBLOCKS, sd // BLK)
    rg = min(ATT_SUBBLOCKS // rb, dilation)
    o, lse = pl.pallas_call(
        _attn_kernel,
        grid=(B, dilation // rg, sd // (rb * BLK)),
        in_specs=[pl.BlockSpec((1, rg, rb * BLK, QKV_G), lambda b, r, n: (b, r, n, 0)),
                  pl.BlockSpec((1, rg, BLK, QKV_G), lambda b, r, n: (b, r, jnp.maximum(n * rb - 1, 0), 0)),
                  _resident((HEADS_PER_GROUP * BLK, 2 * BLK))],
        out_specs=[pl.BlockSpec((1, rg, rb * BLK, GROUP_W), lambda b, r, n: (b, r, n, 0))] * 2,
        out_shape=[jax.ShapeDtypeStruct((B, dilation, sd, GROUP_W), BF16),
                   jax.ShapeDtypeStruct((B, dilation, sd, GROUP_W), F32)],
        compiler_params=_cparams("parallel", "parallel", "parallel"),
        name=f"attn_d{dilation}",
    )(a, a, bias)
    return o, lse


def _gelu(x):
    return x * (lax.erf(x * (2.0 ** -0.5)) + 1.0) * 0.5


def _token_major(src_ref, d, scr, slot, tm):
    if d == 1:
        return src_ref[0, 0].astype(F32)
    for r in range(d):
        piece = src_ref[0, r].astype(F32)
        scr[slot, pl.ds(r, tm // d, stride=d), :] = piece[:, :LANES]
        scr[slot + 1, pl.ds(r, tm // d, stride=d), :] = piece[:, LANES:]
    return jnp.concatenate([scr[slot], scr[slot + 1]], axis=1)


def _mix_kernel(x_ref, o1_ref, o2_ref, o3_ref, l1_ref, l2_ref, l3_ref, uv_ref, gl_ref,
                wa_hbm, wg_hbm, wo_hbm, wc_ref, bs_ref, lng_ref, lnb_ref, we_ref, h_ref, we_bf_ref, g_scr, t_scr,
                wa_ref, wg_ref, wo_ref, w_stage, w_sem):
    @pl.when(pl.program_id(0) == 0)
    def _():
        for w_hbm, w_bf in ((wa_hbm, wa_ref), (wg_hbm, wg_ref), (wo_hbm, wo_ref)):
            _load_weight_bf16(w_hbm, w_bf, w_stage, w_sem)

    we_bf_ref[...] = we_ref[...].astype(BF16)

    tm = x_ref.shape[0]
    dils = [d for _, d in ATT_GROUPS]
    o1, o2, o3 = [_token_major(ref, d, t_scr, 4 * i, tm) for i, (ref, d) in enumerate(zip((o1_ref, o2_ref, o3_ref), dils))]
    l1, l2, l3 = [_token_major(ref, d, t_scr, 4 * i + 2, tm) for i, (ref, d) in enumerate(zip((l1_ref, l2_ref, l3_ref), dils))]
    lm = jnp.maximum(jnp.maximum(l1, l2), l3)
    e1, e2, e3 = jnp.exp(l1 - lm), jnp.exp(l2 - lm), jnp.exp(l3 - lm)
    att = (e1 * o1 + e2 * o2 + e3 * o3) / (e1 + e2 + e3)
    y_att = jnp.dot(att.astype(BF16), wa_ref[...], preferred_element_type=F32)

    zu = _gelu(uv_ref[:, :GMLP_W].astype(F32))
    zv = _gelu(uv_ref[:, GMLP_W:].astype(F32))
    mu = jnp.mean(zv, axis=-1, keepdims=True)
    var = jnp.mean(jnp.square(zv - mu), axis=-1, keepdims=True)
    vn = (zv - mu) * lax.rsqrt(var + EPS) * lng_ref[...] + lnb_ref[...]
    low_half = lax.broadcasted_iota(jnp.int32, (CHUNK, 2 * GMLP_GD), 1) < GMLP_GD
    for c in range(tm // CHUNK):
        rows = slice(c * CHUNK, (c + 1) * CHUNK)
        for s in range(GMLP_W // (2 * GMLP_GD)):
            cols = slice(s * 2 * GMLP_GD, (s + 1) * 2 * GMLP_GD)
            v2 = vn[rows, cols]
            rhs = jnp.concatenate([jnp.where(low_half, v2, 0.0), jnp.where(low_half, 0.0, v2)], axis=0).astype(BF16)
            mixed = jnp.dot(wc_ref[s], rhs, preferred_element_type=F32) + bs_ref[:, cols]
            g_scr[rows, cols] = (zu[rows, cols] * mixed).astype(BF16)
    y_gm = jnp.dot(g_scr[...], wg_ref[...], preferred_element_type=F32)

    gate_a = jax.nn.sigmoid(gl_ref[:, :D_MODEL].astype(F32))
    gate_g = jax.nn.sigmoid(gl_ref[:, D_MODEL:].astype(F32))
    merged = (gate_a * y_att + gate_g * y_gm).astype(BF16)
    h_ref[...] = x_ref[...] + jnp.dot(merged, wo_ref[...], preferred_element_type=F32)


def _mix(x2, outs, lses, uv, gl, wa, wg, wo, wc2, bs, lng, lnb, w_expert, S):
    T = x2.shape[0]
    tm = TM_PROJ
    row = lambda w: pl.BlockSpec((tm, w), lambda i: (i, 0))
    att = [_plane_spec(d, tm, S // tm, GROUP_W) for _, d in ATT_GROUPS]
    we_spec = _expert_slice_spec(T // tm, w_expert.shape[2])
    return pl.pallas_call(
        _mix_kernel,
        grid=(T // tm,),
        in_specs=[row(D_MODEL)] + att + att + [row(2 * GMLP_W), row(N_BRANCH * D_MODEL)]
                 + [pl.BlockSpec(memory_space=pl.ANY)] * 3
                 + [_resident(wc2.shape), _resident(bs.shape), _resident(lng.shape), _resident(lnb.shape), we_spec],
        out_specs=[row(D_MODEL), we_spec],
        out_shape=[jax.ShapeDtypeStruct((T, D_MODEL), F32), jax.ShapeDtypeStruct(w_expert.shape, BF16)],
        scratch_shapes=[pltpu.VMEM((tm, GMLP_W), BF16), pltpu.VMEM((4 * N_DIL, tm, LANES), F32),
                        pltpu.VMEM(wa.shape, BF16), pltpu.VMEM(wg.shape, BF16), pltpu.VMEM(wo.shape, BF16)]
                       + _weight_stage(D_MODEL),
        compiler_params=_cparams("arbitrary"),
        name="mix",
    )(x2, *outs, *lses, uv, gl, wa, wg, wo, wc2, bs, lng, lnb, w_expert)


def _router_kernel(h_ref, g_ref, wr_ref, br_ref, upper_ref, eidx_ref, gate_ref, rank_ref, cnt_ref, carry):
    tm = h_ref.shape[0]

    @pl.when(pl.program_id(0) == 0)
    def _():
        carry[...] = jnp.zeros_like(carry)

    hn = _rms(h_ref[...], g_ref[...])
    hi = hn.astype(BF16)
    lo = (hn - hi.astype(F32)).astype(BF16)
    nt = (((1,), (1,)), ((), ()))
    by_hi = lax.dot_general(wr_ref[...], hi, nt, preferred_element_type=F32)
    by_lo = lax.dot_general(wr_ref[:N_EXPERTS, :], lo, nt, preferred_element_type=F32)
    logits = by_hi[:N_EXPERTS] + by_hi[N_EXPERTS:] + by_lo + br_ref[...]
    expert = lax.broadcasted_iota(jnp.int32, (N_EXPERTS, tm), 0)
    vals, hots = [], []
    l = logits
    for k in range(TOP_K):
        m = jnp.max(l, axis=0, keepdims=True)
        idx = jnp.min(jnp.where(l == m, expert, N_EXPERTS), axis=0, keepdims=True)
        hot = expert == idx
        eidx_ref[0, k:k + 1, :] = idx
        vals.append(m)
        hots.append(hot)
        l = jnp.where(hot, -jnp.inf, l)
    ex = [jnp.exp(v - vals[0]) for v in vals]
    tot = ex[0] + ex[1] + ex[2] + ex[3]
    for k in range(TOP_K):
        gate_ref[0, k:k + 1, :] = ex[k] / tot
    multi = jnp.zeros((N_EXPERTS, tm), F32)
    for hot in hots:
        multi = multi + hot.astype(F32)
    before = jnp.dot(multi.astype(BF16), upper_ref[...], preferred_element_type=F32) + carry[...]
    for k in range(TOP_K):
        rank_ref[0, k:k + 1, :] = jnp.sum(jnp.where(hots[k], before, 0.0), axis=0, keepdims=True).astype(jnp.int32)
    carry[...] += jnp.sum(multi, axis=1, keepdims=True)
    cnt_ref[...] = carry[...]


def _router(h1, g, wr_t, br_col):
    T = h1.shape[0]
    tm = TM_PROJ
    upper = jnp.asarray(np.triu(np.ones((tm, tm), np.float32), k=1), BF16)
    k_rows = pl.BlockSpec((1, TOP_K, tm), lambda i: (i, 0, 0))
    k_shape = lambda dt: jax.ShapeDtypeStruct((T // tm, TOP_K, tm), dt)
    return pl.pallas_call(
        _router_kernel,
        grid=(T // tm,),
        in_specs=[pl.BlockSpec((tm, D_MODEL), lambda i: (i, 0)), _resident((1, D_MODEL)),
                  _resident((2 * N_EXPERTS, D_MODEL)), _resident((N_EXPERTS, 1)), _resident((tm, tm))],
        out_specs=[k_rows, k_rows, k_rows, pl.BlockSpec((N_EXPERTS, 1), lambda i: (0, 0))],
        out_shape=[k_shape(jnp.int32), k_shape(F32), k_shape(jnp.int32),
                   jax.ShapeDtypeStruct((N_EXPERTS, 1), F32)],
        scratch_shapes=[pltpu.VMEM((N_EXPERTS, 1), F32)],
        compiler_params=_cparams("arbitrary"),
        name="router",
    )(h1, g, wr_t, br_col, upper)


def _to_row_tiles(ref, lead, value):
    n = value.shape[0]
    for c in range(ROW_SUBLANES):
        ref[(*lead, pl.ds(c, n, stride=ROW_SUBLANES), slice(None))] = value[:, c * LANES:(c + 1) * LANES]


def _from_row_tiles(ref, lead, first, n):
    return jnp.concatenate(
        [ref[(*lead, pl.ds(first * ROW_SUBLANES + c, n, stride=ROW_SUBLANES), slice(None))] for c in range(ROW_SUBLANES)],
        axis=1)


def _tile_rows(idx, n=1):
    return pl.ds(pl.multiple_of(idx * ROW_SUBLANES, ROW_SUBLANES), n * ROW_SUBLANES)


def _dispatch_kernel(last_ref, nv_ref, dest_ref, h_ref, g_ref, xs_ref, buf, sem, zero_sem):
    tm = h_ref.shape[0]
    n_blocks = xs_ref.shape[0] // (TM_EXP * ROW_SUBLANES)
    i = pl.program_id(0)
    slot = lax.rem(i, 2)

    @pl.when(i == 0)
    def _():
        buf[1] = jnp.zeros(buf.shape[1:], F32)

        def zero_block(b):
            for part in range(TM_EXP // tm):
                pltpu.make_async_copy(buf.at[1], xs_ref.at[_tile_rows(b * TM_EXP + part * tm, tm)], zero_sem).start()

        def zero_done():
            for part in range(TM_EXP // tm):
                pltpu.make_async_copy(buf.at[1], xs_ref.at[_tile_rows(0, tm)], zero_sem).wait()

        for e in range(N_EXPERTS):
            zero_block(last_ref[e])
        lax.fori_loop(nv_ref[0], n_blocks, lambda b, c: (zero_block(b), c)[1], 0)
        for e in range(N_EXPERTS):
            zero_done()
        lax.fori_loop(nv_ref[0], n_blocks, lambda b, c: (zero_done(), c)[1], 0)

    _to_row_tiles(buf, (slot,), _rms(h_ref[...], g_ref[...]))

    def issue(t, carry):
        for k in range(TOP_K):
            d = dest_ref[0, 0, t * TOP_K + k]
            pltpu.make_async_copy(buf.at[slot, _tile_rows(t)], xs_ref.at[_tile_rows(d)],
                                  sem.at[slot]).start(priority=k % 2)
        return carry

    lax.fori_loop(0, tm, issue, 0, unroll=8)

    def wait_slot(s):
        for _ in range(TOP_K):
            pltpu.make_async_copy(buf.at[s], xs_ref.at[_tile_rows(0, tm)], sem.at[s]).wait()

    @pl.when(i > 0)
    def _():
        wait_slot(1 - slot)

    @pl.when(i == pl.num_programs(0) - 1)
    def _():
        wait_slot(slot)


def _dispatch(last_block, n_valid, dest3, h1, g, n_slots):
    T = h1.shape[0]
    tm = TM_TOK
    grid_spec = pltpu.PrefetchScalarGridSpec(
        num_scalar_prefetch=2,
        grid=(T // tm,),
        in_specs=[pl.BlockSpec((1, 1, TOP_K * tm), lambda i, lb, nv: (i, 0, 0), memory_space=pltpu.SMEM),
                  pl.BlockSpec((tm, D_MODEL), lambda i, lb, nv: (i, 0)),
                  pl.BlockSpec((1, D_MODEL), lambda i, lb, nv: (0, 0), pipeline_mode=pl.Buffered(1))],
        out_specs=pl.BlockSpec(memory_space=pl.ANY),
        scratch_shapes=[pltpu.VMEM((2, tm * ROW_SUBLANES, LANES), F32), pltpu.SemaphoreType.DMA((2,)),
                        pltpu.SemaphoreType.DMA(())],
    )
    return pl.pallas_call(
        _dispatch_kernel,
        grid_spec=grid_spec,
        out_shape=jax.ShapeDtypeStruct((n_slots * ROW_SUBLANES, LANES), F32),
        compiler_params=_cparams("arbitrary"),
        name="dispatch",
    )(last_block, n_valid, dest3, h1, g)


def _experts_kernel(be_ref, nv_ref, x_ref, wgu_ref, bgu_ref, wd_ref, bd_ref, y_ref):
    del be_ref
    tm = x_ref.shape[0] // ROW_SUBLANES

    @pl.when(pl.program_id(0) < nv_ref[0])
    def _():
        x = _from_row_tiles(x_ref, (), 0, tm).astype(BF16)
        gu = jnp.dot(x, wgu_ref[0], preferred_element_type=F32) + bgu_ref[0]
        glu = jnp.minimum(gu[:, :D_EXPERT], SWIGLU_LIMIT)
        lin = jnp.clip(gu[:, D_EXPERT:], -SWIGLU_LIMIT, SWIGLU_LIMIT)
        act = glu * jax.nn.sigmoid(SWIGLU_ALPHA * glu) * (lin + 1.0)
        _to_row_tiles(y_ref, (), jnp.dot(act.astype(BF16), wd_ref[0], preferred_element_type=F32) + bd_ref[0])

    @pl.when(pl.program_id(0) >= nv_ref[0])
    def _():
        y_ref[...] = jnp.zeros_like(y_ref)


def _experts(block_expert, n_valid, xs, wgu, bgu, wd, bd):
    tm = TM_EXP
    n_blocks = xs.shape[0] // (tm * ROW_SUBLANES)
    live = lambda b, be, nv: jnp.maximum(jnp.minimum(b, nv[0] - 1), 0)
    grid_spec = pltpu.PrefetchScalarGridSpec(
        num_scalar_prefetch=2,
        grid=(n_blocks,),
        in_specs=[pl.BlockSpec((tm * ROW_SUBLANES, LANES), lambda b, be, nv: (live(b, be, nv), 0)),
                  pl.BlockSpec((1, D_MODEL, 2 * D_EXPERT), lambda b, be, nv: (be[b], 0, 0)),
                  pl.BlockSpec((1, 1, 2 * D_EXPERT), lambda b, be, nv: (be[b], 0, 0)),
                  pl.BlockSpec((1, D_EXPERT, D_MODEL), lambda b, be, nv: (be[b], 0, 0)),
                  pl.BlockSpec((1, 1, D_MODEL), lambda b, be, nv: (be[b], 0, 0))],
        out_specs=pl.BlockSpec((tm * ROW_SUBLANES, LANES), lambda b, be, nv: (b, 0)),
    )
    return pl.pallas_call(
        _experts_kernel,
        grid_spec=grid_spec,
        out_shape=jax.ShapeDtypeStruct(xs.shape, F32),
        compiler_params=_cparams("arbitrary"),
        name="experts",
    )(block_expert, n_valid, xs, wgu, bgu, wd, bd)


GATHER_AHEAD = 2
GATHER_SLOTS = GATHER_AHEAD + 1


def _combine_kernel(dest0_ref, dest1_ref, ahead_dest_ref, h_ref, gate_ref, p_ref, gp_ref, gf_ref, wpg_hbm, wpp_hbm,
                    ys_ref, o_ref, buf, sem, wpg_ref, wpp_ref, w_stage, w_sem):
    tm = h_ref.shape[0]
    i = pl.program_id(0)

    @pl.when(i == 0)
    def _():
        _load_weight_bf16(wpg_hbm, wpg_ref, w_stage, w_sem)
        _load_weight_bf16(wpp_hbm, wpp_ref, w_stage, w_sem)
    slot = lax.rem(i, GATHER_SLOTS)
    ahead_slot = lax.rem(i + GATHER_AHEAD, GATHER_SLOTS)

    def row_copy(dref, t, k, s):
        d = dref[0, 0, t * TOP_K + k]
        return pltpu.make_async_copy(ys_ref.at[_tile_rows(d)], buf.at[s, _tile_rows(k * tm + t)], sem.at[s])

    def gather(dref, s):
        def issue(t, carry):
            for k in range(TOP_K):
                row_copy(dref, t, k, s).start(priority=k % 2)
            return carry
        lax.fori_loop(0, tm, issue, 0, unroll=8)

    def wait_slot(s):
        for _ in range(TOP_K):
            pltpu.make_async_copy(ys_ref.at[_tile_rows(0, tm)], buf.at[s, _tile_rows(0, tm)], sem.at[s]).wait()

    @pl.when(i == 0)
    def _():
        gather(dest0_ref, 0)
        gather(dest1_ref, 1)

    proj = jnp.dot(p_ref[...].astype(BF16), wpp_ref[...], preferred_element_type=F32)
    wait_slot(slot)
    h = h_ref[...]
    for k in range(TOP_K):
        h = h + gate_ref[:, k:k + 1] * _from_row_tiles(buf, (slot,), k * tm, tm)

    for t in range(tm):
        for k in range(TOP_K):
            row_copy(ahead_dest_ref, t, k, ahead_slot).start(priority=k % 2)

    ple_gate = jax.nn.sigmoid(jnp.dot(_rms(h, gp_ref[...]).astype(BF16), wpg_ref[...], preferred_element_type=F32))
    h = h + ple_gate * proj
    o_ref[...] = _rms(h, gf_ref[...])

    @pl.when(i == pl.num_programs(0) - 1)
    def _():
        for ahead in range(1, GATHER_SLOTS):
            wait_slot(lax.rem(i + ahead, GATHER_SLOTS))


def _combine(dest3, h1, gate, p2, gp, wpg, wpp, gf, ys):
    T = h1.shape[0]
    tm = TM_TOK
    n_tiles = T // tm
    assert n_tiles > GATHER_AHEAD == 2
    row = lambda w: pl.BlockSpec((tm, w), lambda i: (i, 0))
    dest_spec = lambda ahead: pl.BlockSpec((1, 1, TOP_K * tm), lambda i: (jnp.minimum(i + ahead, n_tiles - 1), 0, 0),
                                           memory_space=pltpu.SMEM)
    return pl.pallas_call(
        _combine_kernel,
        grid=(n_tiles,),
        in_specs=[dest_spec(0), dest_spec(1), dest_spec(GATHER_AHEAD),
                  row(D_MODEL), row(TOP_K), row(PLE_DIM), _resident((1, D_MODEL)), _resident((1, D_MODEL))]
                 + [pl.BlockSpec(memory_space=pl.ANY)] * 3,
        out_specs=row(D_MODEL),
        out_shape=jax.ShapeDtypeStruct((T, D_MODEL), F32),
        scratch_shapes=[pltpu.VMEM((GATHER_SLOTS, TOP_K * tm * ROW_SUBLANES, LANES), F32),
                        pltpu.SemaphoreType.DMA((GATHER_SLOTS,)),
                        pltpu.VMEM(wpg.shape, BF16), pltpu.VMEM(wpp.shape, BF16)] + _weight_stage(D_MODEL),
        compiler_params=_cparams("arbitrary"),
        name="combine",
    )(dest3, dest3, dest3, h1, gate, p2, gp, gf, wpg, wpp, ys)


def _layer(h, p_i, g_mix, w_in, rel_bias, w_att_out, ln_v_g, ln_v_b, w_spatial, b_spatial, w_gmlp_out, w_out,
           g_moe, w_router, b_router, w_gate_up, b_gate_up, w_down, b_down, g_ple, w_ple_gate, w_ple_proj,
           g_final, B, S):
    T = B * S
    row = lambda v: v.reshape(1, -1).astype(F32)

    assert S % TM_PROJ == 0
    *att_in, uv, gl, wgu_bf = _in_proj(h, row(g_mix), w_in.astype(F32), w_gate_up.astype(F32), B, S)

    outs, lses = [], []
    for g, (window, dilation) in enumerate(ATT_GROUPS):
        assert window // dilation == BLK and S % (dilation * BLK) == 0
        bias = _bias_table(rel_bias[:, g * HEADS_PER_GROUP:(g + 1) * HEADS_PER_GROUP], dilation)
        o, lse = _attention_group(att_in[g], bias, dilation, B, S)
        outs.append(o)
        lses.append(lse)

    causal = jnp.asarray(np.tril(np.ones((CHUNK, CHUNK), np.float32)))
    w_c = (w_spatial.astype(F32) * causal[None]).astype(BF16)
    wc2 = jnp.concatenate([w_c[0::2], w_c[1::2]], axis=2)
    bs = jnp.repeat(b_spatial.astype(F32).T, GMLP_GD, axis=1)
    h1, wd_bf = _mix(h, outs, lses, uv, gl, w_att_out.astype(F32), w_gmlp_out.astype(F32), w_out.astype(F32),
                     wc2, bs, row(ln_v_g), row(ln_v_b), w_down.astype(F32), S)

    wr_hi = w_router.astype(BF16)
    wr_lo = (w_router.astype(F32) - wr_hi.astype(F32)).astype(BF16)
    eidx, gate, rank, counts = _router(h1, row(g_moe), jnp.concatenate([wr_hi, wr_lo], axis=1).T,
                                       b_router.reshape(-1, 1).astype(F32))
    cnt = counts[:, 0].astype(jnp.int32)
    blk_counts = (cnt + TM_EXP - 1) // TM_EXP
    blk_end = jnp.cumsum(blk_counts)
    pad_start = (blk_end - blk_counts) * TM_EXP
    n_blocks = T * TOP_K // TM_EXP + N_EXPERTS
    n_valid = blk_end[-1:].astype(jnp.int32)
    blk = jnp.minimum(jnp.arange(n_blocks, dtype=jnp.int32), n_valid[0] - 1)
    block_expert = jnp.minimum(jnp.sum((blk_end[None, :] <= blk[:, None]).astype(jnp.int32), axis=1), N_EXPERTS - 1)
    expert_ids = jnp.arange(N_EXPERTS, dtype=jnp.int32)
    dest = rank + jnp.sum(jnp.where(eidx[..., None] == expert_ids, pad_start, 0), axis=-1)
    token_major = lambda a: jnp.transpose(a, (0, 2, 1)).reshape(T, TOP_K)
    dest3 = token_major(dest).reshape(T // TM_TOK, 1, TM_TOK * TOP_K)
    gate = token_major(gate)

    last_block = jnp.maximum(blk_end - 1, 0).astype(jnp.int32)
    xs = _dispatch(last_block, n_valid, dest3, h1, row(g_moe), n_blocks * TM_EXP)
    ys = _experts(block_expert, n_valid, xs, wgu_bf, b_gate_up.reshape(N_EXPERTS, 1, -1).astype(F32),
                  wd_bf, b_down.reshape(N_EXPERTS, 1, -1).astype(F32))
    return _combine(dest3, h1, gate, p_i, row(g_ple), w_ple_gate.astype(F32), w_ple_proj.astype(F32),
                    row(g_final), ys)


def kernel(x, p, g_mix, w_in, rel_bias, w_att_out, ln_v_g, ln_v_b, w_spatial, b_spatial, w_gmlp_out, w_out, g_moe, w_router, b_router, w_gate_up, b_gate_up, w_down, b_down, g_ple, w_ple_gate, w_ple_proj, g_final):
    B, S, D = x.shape
    depth = p.shape[0]
    assert depth == 1, "the final RMSNorm is fused into the (single) layer's last kernel"
    out = _layer(x.reshape(B * S, D), p[0].reshape(B * S, PLE_DIM), g_mix[0], w_in[0], rel_bias, w_att_out[0],
                 ln_v_g[0], ln_v_b[0], w_spatial[0], b_spatial[0], w_gmlp_out[0], w_out[0], g_moe[0], w_router[0],
                 b_router[0], w_gate_up[0], b_gate_up[0], w_down[0], b_down[0], g_ple[0], w_ple_gate[0],
                 w_ple_proj[0], g_final, B, S)
    return out.reshape(B, S, D)
```

```python
import functools

import jax
import jax.numpy as jnp
import numpy as np
from jax import lax
from jax.experimental import pallas as pl
from jax.experimental.pallas import tpu as pltpu

F32 = jnp.float32
BF16 = jnp.bfloat16

D_MODEL = 1024
HEAD_DIM = 64
ATT_GROUPS = ((128, 1), (512, 4), (2048, 16))
HEADS_PER_GROUP = 4
GROUP_W = HEADS_PER_GROUP * HEAD_DIM
N_DIL = len(ATT_GROUPS)
ATT_W = N_DIL * GROUP_W
BLK = 128
REL_BUCKETS = 32
REL_MAX_DIST = 2048
CHUNK = 128
GMLP_W = 768
GMLP_GD = 64
N_BRANCH = 2
IN_W = 3 * ATT_W + 2 * GMLP_W + N_BRANCH * D_MODEL
N_EXPERTS = 32
TOP_K = 4
D_EXPERT = D_MODEL
SWIGLU_LIMIT = 7.0
SWIGLU_ALPHA = 1.702
PLE_DIM = 256
EPS = 1e-6
MASKED = -1e30
LOG2E = float(np.log2(np.e))
LN2 = float(np.log(2.0))

QKV_G = 3 * GROUP_W

LANES = 128
ROW_SUBLANES = D_MODEL // LANES
assert ROW_SUBLANES == 8
MXU_N = 256
VMEM_LIMIT = 56 * 1024 * 1024

TM_PROJ = 512
TM_TOK = 256
TM_EXP = 512
assert TM_EXP % TM_TOK == 0


def _cparams(*sem):
    return pltpu.CompilerParams(dimension_semantics=sem, vmem_limit_bytes=VMEM_LIMIT)


def _resident(shape):
    nd = len(shape)
    return pl.BlockSpec(shape, lambda *_: (0,) * nd, pipeline_mode=pl.Buffered(1))


def _rms(x, g):
    return x * lax.rsqrt(jnp.mean(x * x, axis=-1, keepdims=True) + EPS) * g


def _load_weight_bf16(w_hbm, w_bf, stage, sem):
    rows = stage.shape[1]
    n_chunks = w_hbm.shape[0] // rows
    assert n_chunks * rows == w_hbm.shape[0] and stage.shape[2] == w_hbm.shape[1]

    def chunk(c):
        return pltpu.make_async_copy(w_hbm.at[pl.ds(c * rows, rows)], stage.at[c % 2], sem.at[c % 2])

    chunk(0).start()
    for c in range(n_chunks):
        if c + 1 < n_chunks:
            chunk(c + 1).start()
        chunk(c).wait()
        w_bf[c * rows:(c + 1) * rows, :] = stage[c % 2].astype(BF16)


def _expert_slice_spec(n_steps, width):
    per_expert = n_steps // N_EXPERTS
    assert per_expert * N_EXPERTS == n_steps and D_MODEL % per_expert == 0
    return pl.BlockSpec((1, D_MODEL // per_expert, width), lambda i: (i // per_expert, i % per_expert, 0))


def _inproj_kernel(x_ref, g_ref, w_hbm, we_ref, a1_ref, a2_ref, a3_ref, uv_ref, gl_ref, we_bf_ref,
                   scr, w_ref, w_stage, w_sem):
    @pl.when(pl.program_id(0) == 0)
    def _():
        _load_weight_bf16(w_hbm, w_ref, w_stage, w_sem)

    we_bf_ref[...] = we_ref[...].astype(BF16)

    tm = x_ref.shape[0]
    n = _rms(x_ref[...], g_ref[...]).astype(BF16)
    att_refs = (a1_ref, a2_ref, a3_ref)
    n_att, n_uv = 3 * ATT_W // MXU_N, 2 * GMLP_W // MXU_N
    for c in range(IN_W // MXU_N):
        z = jnp.dot(n, w_ref[:, c * MXU_N:(c + 1) * MXU_N], preferred_element_type=F32)
        if c < n_att:
            which, g = divmod(c, N_DIL)
            d = ATT_GROUPS[g][1]
            dst = att_refs[g]
            cols = slice(which * GROUP_W, (which + 1) * GROUP_W)
            if d == 1:
                dst[0, 0, :, cols] = z.astype(BF16)
                continue
            scr[0] = z[:, :LANES]
            scr[1] = z[:, LANES:]
            for r in range(d):
                zr = jnp.concatenate([scr[0, pl.ds(r, tm // d, stride=d), :],
                                      scr[1, pl.ds(r, tm // d, stride=d), :]], axis=1)
                dst[0, r, :, cols] = zr.astype(BF16)
        elif c < n_att + n_uv:
            uv_ref[:, (c - n_att) * MXU_N:(c - n_att + 1) * MXU_N] = z.astype(BF16)
        else:
            gl_ref[:, (c - n_att - n_uv) * MXU_N:(c - n_att - n_uv + 1) * MXU_N] = z.astype(BF16)


def _plane_spec(d, tm, tiles_per_seq, width):
    return pl.BlockSpec((1, d, tm // d, width), lambda i: (i // tiles_per_seq, 0, i % tiles_per_seq, 0))


W_STAGE_ROWS = 128


def _weight_stage(width):
    return [pltpu.VMEM((2, W_STAGE_ROWS, width), F32), pltpu.SemaphoreType.DMA((2,))]


def _in_proj(x2, g, w, w_expert, B, S):
    T = x2.shape[0]
    tm = TM_PROJ
    row = lambda w: pl.BlockSpec((tm, w), lambda i: (i, 0))
    dils = [d for _, d in ATT_GROUPS]
    we_spec = _expert_slice_spec(T // tm, w_expert.shape[2])
    return pl.pallas_call(
        _inproj_kernel,
        grid=(T // tm,),
        in_specs=[row(D_MODEL), _resident((1, D_MODEL)), pl.BlockSpec(memory_space=pl.ANY), we_spec],
        out_specs=[_plane_spec(d, tm, S // tm, QKV_G) for d in dils] + [row(2 * GMLP_W), row(N_BRANCH * D_MODEL), we_spec],
        out_shape=[jax.ShapeDtypeStruct((B, d, S // d, QKV_G), BF16) for d in dils]
                  + [jax.ShapeDtypeStruct((T, 2 * GMLP_W), BF16),
                     jax.ShapeDtypeStruct((T, N_BRANCH * D_MODEL), BF16),
                     jax.ShapeDtypeStruct(w_expert.shape, BF16)],
        scratch_shapes=[pltpu.VMEM((2, tm, LANES), F32), pltpu.VMEM((D_MODEL, IN_W), BF16)] + _weight_stage(IN_W),
        compiler_params=_cparams("arbitrary"),
        name="in_proj",
    )(x2, g, w, w_expert)


def _t5_bucket(n):
    exact = REL_BUCKETS // 2
    nf = np.maximum(n, 1).astype(np.float32)
    large = exact + (np.log(nf / exact) / np.log(REL_MAX_DIST / exact) * (REL_BUCKETS - exact)).astype(np.int32)
    large = np.minimum(large, REL_BUCKETS - 1)
    return np.where(n < exact, n, large).astype(np.int32)


def _bias_table(rel_bias_g, dilation):
    n = 3 * BLK
    dist = 2 * BLK - 1 - np.arange(n)
    valid = (dist >= 0) & (dist <= BLK)
    bucket = _t5_bucket(np.clip(dist, 0, BLK) * dilation)
    c = jnp.where(jnp.asarray(valid)[None, :], rel_bias_g.astype(F32)[bucket].T * LOG2E, MASKED)
    shifted = jnp.tile(c, (1, BLK))[:, :BLK * (n - 1)].reshape(HEADS_PER_GROUP, BLK, n - 1)
    return shifted[:, :, BLK - 1:].reshape(HEADS_PER_GROUP * BLK, 2 * BLK)


def _attn_kernel(cur_ref, prev_ref, bias_ref, o_ref, lse_ref):
    rg, rb = cur_ref.shape[1], cur_ref.shape[2] // BLK
    starts_sequence = pl.program_id(2) == 0
    lane_head = lax.broadcasted_iota(jnp.int32, (1, GROUP_W), 1) // HEAD_DIM
    scale = HEAD_DIM ** -0.5
    head_bf = [jnp.where(lane_head == h, scale, 0.0).astype(BF16) for h in range(HEADS_PER_GROUP)]
    key_is_prev = lax.broadcasted_iota(jnp.int32, (1, 2 * BLK), 1) < BLK
    nt = (((1,), (1,)), ((), ()))
    qc, kc_, vc_ = slice(0, GROUP_W), slice(GROUP_W, 2 * GROUP_W), slice(2 * GROUP_W, 3 * GROUP_W)

    def by_head(x):
        sel = x[(HEADS_PER_GROUP - 1) * BLK:]
        for h in range(HEADS_PER_GROUP - 2, -1, -1):
            sel = jnp.where(lane_head == h, x[h * BLK:(h + 1) * BLK], sel)
        return sel

    for r, j in [(r, j) for r in range(rg) for j in range(rb)]:
        rows = slice(j * BLK, (j + 1) * BLK)
        prev = prev_ref if j == 0 else cur_ref
        prows = slice(0, BLK) if j == 0 else slice((j - 1) * BLK, j * BLK)
        q = cur_ref[0, r, rows, qc]
        k = jnp.concatenate([prev[0, r, prows, kc_], cur_ref[0, r, rows, kc_]], axis=0)
        v = jnp.concatenate([prev[0, r, prows, vc_], cur_ref[0, r, rows, vc_]], axis=0)
        q_bd = jnp.concatenate([q * head_bf[h] for h in range(HEADS_PER_GROUP)], axis=0)
        s = lax.dot_general(q_bd, k, nt, preferred_element_type=F32) * LOG2E + bias_ref[...]
        if j == 0:
            s = jnp.where(jnp.logical_and(starts_sequence, key_is_prev), MASKED, s)
        m = jnp.max(s, axis=-1, keepdims=True)
        p = jnp.exp2(s - m)
        den = jnp.sum(p, axis=-1, keepdims=True)
        o = jnp.dot(p.astype(BF16), v, preferred_element_type=F32)
        o_ref[0, r, rows, :] = (by_head(o) * by_head(1.0 / den)).astype(BF16)
        lse_ref[0, r, rows, :] = jnp.broadcast_to(by_head(m * LN2 + jnp.log(den)), (BLK, GROUP_W))


ATT_SUBBLOCKS = 8


def _attention_group(a, bias, dilation, B, S):
    sd = S // dilation
    rb = min(ATT_SUBBLOCKS, sd // BLK)
    rg = min(ATT_SUBBLOCKS // rb, dilation)
    o, lse = pl.pallas_call(
        _attn_kernel,
        grid=(B, dilation // rg, sd // (rb * BLK)),
        in_specs=[pl.BlockSpec((1, rg, rb * BLK, QKV_G), lambda b, r, n: (b, r, n, 0)),
                  pl.BlockSpec((1, rg, BLK, QKV_G), lambda b, r, n: (b, r, jnp.maximum(n * rb - 1, 0), 0)),
                  _resident((HEADS_PER_GROUP * BLK, 2 * BLK))],
        out_specs=[pl.BlockSpec((1, rg, rb * BLK, GROUP_W), lambda b, r, n: (b, r, n, 0))] * 2,
        out_shape=[jax.ShapeDtypeStruct((B, dilation, sd, GROUP_W), BF16),
                   jax.ShapeDtypeStruct((B, dilation, sd, GROUP_W), F32)],
        compiler_params=_cparams("parallel", "parallel", "parallel"),
        name=f"attn_d{dilation}",
    )(a, a, bias)
    return o, lse


def _gelu(x):
    return x * (lax.erf(x * (2.0 ** -0.5)) + 1.0) * 0.5


def _token_major(src_ref, d, scr, slot, tm):
    if d == 1:
        return src_ref[0, 0].astype(F32)
    for r in range(d):
        piece = src_ref[0, r].astype(F32)
        scr[slot, pl.ds(r, tm // d, stride=d), :] = piece[:, :LANES]
        scr[slot + 1, pl.ds(r, tm // d, stride=d), :] = piece[:, LANES:]
    return jnp.concatenate([scr[slot], scr[slot + 1]], axis=1)


def _mix_kernel(x_ref, o1_ref, o2_ref, o3_ref, l1_ref, l2_ref, l3_ref, uv_ref, gl_ref,
                wa_hbm, wg_hbm, wo_hbm, wc_ref, bs_ref, lng_ref, lnb_ref, we_ref, h_ref, we_bf_ref, g_scr, t_scr,
                wa_ref, wg_ref, wo_ref, w_stage, w_sem):
    @pl.when(pl.program_id(0) == 0)
    def _():
        for w_hbm, w_bf in ((wa_hbm, wa_ref), (wg_hbm, wg_ref), (wo_hbm, wo_ref)):
            _load_weight_bf16(w_hbm, w_bf, w_stage, w_sem)

    we_bf_ref[...] = we_ref[...].astype(BF16)

    tm = x_ref.shape[0]
    dils = [d for _, d in ATT_GROUPS]
    o1, o2, o3 = [_token_major(ref, d, t_scr, 4 * i, tm) for i, (ref, d) in enumerate(zip((o1_ref, o2_ref, o3_ref), dils))]
    l1, l2, l3 = [_token_major(ref, d, t_scr, 4 * i + 2, tm) for i, (ref, d) in enumerate(zip((l1_ref, l2_ref, l3_ref), dils))]
    lm = jnp.maximum(jnp.maximum(l1, l2), l3)
    e1, e2, e3 = jnp.exp(l1 - lm), jnp.exp(l2 - lm), jnp.exp(l3 - lm)
    att = (e1 * o1 + e2 * o2 + e3 * o3) / (e1 + e2 + e3)
    y_att = jnp.dot(att.astype(BF16), wa_ref[...], preferred_element_type=F32)

    zu = _gelu(uv_ref[:, :GMLP_W].astype(F32))
    zv = _gelu(uv_ref[:, GMLP_W:].astype(F32))
    mu = jnp.mean(zv, axis=-1, keepdims=True)
    var = jnp.mean(jnp.square(zv - mu), axis=-1, keepdims=True)
    vn = (zv - mu) * lax.rsqrt(var + EPS) * lng_ref[...] + lnb_ref[...]
    low_half = lax.broadcasted_iota(jnp.int32, (CHUNK, 2 * GMLP_GD), 1) < GMLP_GD
    for c in range(tm // CHUNK):
        rows = slice(c * CHUNK, (c + 1) * CHUNK)
        for s in range(GMLP_W // (2 * GMLP_GD)):
            cols = slice(s * 2 * GMLP_GD, (s + 1) * 2 * GMLP_GD)
            v2 = vn[rows, cols]
            rhs = jnp.concatenate([jnp.where(low_half, v2, 0.0), jnp.where(low_half, 0.0, v2)], axis=0).astype(BF16)
            mixed = jnp.dot(wc_ref[s], rhs, preferred_element_type=F32) + bs_ref[:, cols]
            g_scr[rows, cols] = (zu[rows, cols] * mixed).astype(BF16)
    y_gm = jnp.dot(g_scr[...], wg_ref[...], preferred_element_type=F32)

    gate_a = jax.nn.sigmoid(gl_ref[:, :D_MODEL].astype(F32))
    gate_g = jax.nn.sigmoid(gl_ref[:, D_MODEL:].astype(F32))
    merged = (gate_a * y_att + gate_g * y_gm).astype(BF16)
    h_ref[...] = x_ref[...] + jnp.dot(merged, wo_ref[...], preferred_element_type=F32)


def _mix(x2, outs, lses, uv, gl, wa, wg, wo, wc2, bs, lng, lnb, w_expert, S):
    T = x2.shape[0]
    tm = TM_PROJ
    row = lambda w: pl.BlockSpec((tm, w), lambda i: (i, 0))
    att = [_plane_spec(d, tm, S // tm, GROUP_W) for _, d in ATT_GROUPS]
    we_spec = _expert_slice_spec(T // tm, w_expert.shape[2])
    return pl.pallas_call(
        _mix_kernel,
        grid=(T // tm,),
        in_specs=[row(D_MODEL)] + att + att + [row(2 * GMLP_W), row(N_BRANCH * D_MODEL)]
                 + [pl.BlockSpec(memory_space=pl.ANY)] * 3
                 + [_resident(wc2.shape), _resident(bs.shape), _resident(lng.shape), _resident(lnb.shape), we_spec],
        out_specs=[row(D_MODEL), we_spec],
        out_shape=[jax.ShapeDtypeStruct((T, D_MODEL), F32), jax.ShapeDtypeStruct(w_expert.shape, BF16)],
        scratch_shapes=[pltpu.VMEM((tm, GMLP_W), BF16), pltpu.VMEM((4 * N_DIL, tm, LANES), F32),
                        pltpu.VMEM(wa.shape, BF16), pltpu.VMEM(wg.shape, BF16), pltpu.VMEM(wo.shape, BF16)]
                       + _weight_stage(D_MODEL),
        compiler_params=_cparams("arbitrary"),
        name="mix",
    )(x2, *outs, *lses, uv, gl, wa, wg, wo, wc2, bs, lng, lnb, w_expert)


def _router_kernel(h_ref, g_ref, wr_ref, br_ref, upper_ref, eidx_ref, gate_ref, rank_ref, cnt_ref, carry):
    tm = h_ref.shape[0]

    @pl.when(pl.program_id(0) == 0)
    def _():
        carry[...] = jnp.zeros_like(carry)

    hn = _rms(h_ref[...], g_ref[...])
    hi = hn.astype(BF16)
    lo = (hn - hi.astype(F32)).astype(BF16)
    nt = (((1,), (1,)), ((), ()))
    by_hi = lax.dot_general(wr_ref[...], hi, nt, preferred_element_type=F32)
    by_lo = lax.dot_general(wr_ref[:N_EXPERTS, :], lo, nt, preferred_element_type=F32)
    logits = by_hi[:N_EXPERTS] + by_hi[N_EXPERTS:] + by_lo + br_ref[...]
    expert = lax.broadcasted_iota(jnp.int32, (N_EXPERTS, tm), 0)
    vals, hots = [], []
    l = logits
    for k in range(TOP_K):
        m = jnp.max(l, axis=0, keepdims=True)
        idx = jnp.min(jnp.where(l == m, expert, N_EXPERTS), axis=0, keepdims=True)
        hot = expert == idx
        eidx_ref[0, k:k + 1, :] = idx
        vals.append(m)
        hots.append(hot)
        l = jnp.where(hot, -jnp.inf, l)
    ex = [jnp.exp(v - vals[0]) for v in vals]
    tot = ex[0] + ex[1] + ex[2] + ex[3]
    for k in range(TOP_K):
        gate_ref[0, k:k + 1, :] = ex[k] / tot
    multi = jnp.zeros((N_EXPERTS, tm), F32)
    for hot in hots:
        multi = multi + hot.astype(F32)
    before = jnp.dot(multi.astype(BF16), upper_ref[...], preferred_element_type=F32) + carry[...]
    for k in range(TOP_K):
        rank_ref[0, k:k + 1, :] = jnp.sum(jnp.where(hots[k], before, 0.0), axis=0, keepdims=True).astype(jnp.int32)
    carry[...] += jnp.sum(multi, axis=1, keepdims=True)
    cnt_ref[...] = carry[...]


def _router(h1, g, wr_t, br_col):
    T = h1.shape[0]
    tm = TM_PROJ
    upper = jnp.asarray(np.triu(np.ones((tm, tm), np.float32), k=1), BF16)
    k_rows = pl.BlockSpec((1, TOP_K, tm), lambda i: (i, 0, 0))
    k_shape = lambda dt: jax.ShapeDtypeStruct((T // tm, TOP_K, tm), dt)
    return pl.pallas_call(
        _router_kernel,
        grid=(T // tm,),
        in_specs=[pl.BlockSpec((tm, D_MODEL), lambda i: (i, 0)), _resident((1, D_MODEL)),
                  _resident((2 * N_EXPERTS, D_MODEL)), _resident((N_EXPERTS, 1)), _resident((tm, tm))],
        out_specs=[k_rows, k_rows, k_rows, pl.BlockSpec((N_EXPERTS, 1), lambda i: (0, 0))],
        out_shape=[k_shape(jnp.int32), k_shape(F32), k_shape(jnp.int32),
                   jax.ShapeDtypeStruct((N_EXPERTS, 1), F32)],
        scratch_shapes=[pltpu.VMEM((N_EXPERTS, 1), F32)],
        compiler_params=_cparams("arbitrary"),
        name="router",
    )(h1, g, wr_t, br_col, upper)


def _to_row_tiles(ref, lead, value):
    n = value.shape[0]
    for c in range(ROW_SUBLANES):
        ref[(*lead, pl.ds(c, n, stride=ROW_SUBLANES), slice(None))] = value[:, c * LANES:(c + 1) * LANES]


def _from_row_tiles(ref, lead, first, n):
    return jnp.concatenate(
        [ref[(*lead, pl.ds(first * ROW_SUBLANES + c, n, stride=ROW_SUBLANES), slice(None))] for c in range(ROW_SUBLANES)],
        axis=1)


def _tile_rows(idx, n=1):
    return pl.ds(pl.multiple_of(idx * ROW_SUBLANES, ROW_SUBLANES), n * ROW_SUBLANES)


def _dispatch_kernel(last_ref, nv_ref, dest_ref, h_ref, g_ref, xs_ref, buf, sem, zero_sem):
    tm = h_ref.shape[0]
    n_blocks = xs_ref.shape[0] // (TM_EXP * ROW_SUBLANES)
    i = pl.program_id(0)
    slot = lax.rem(i, 2)

    @pl.when(i == 0)
    def _():
        buf[1] = jnp.zeros(buf.shape[1:], F32)

        def zero_block(b):
            for part in range(TM_EXP // tm):
                pltpu.make_async_copy(buf.at[1], xs_ref.at[_tile_rows(b * TM_EXP + part * tm, tm)], zero_sem).start()

        def zero_done():
            for part in range(TM_EXP // tm):
                pltpu.make_async_copy(buf.at[1], xs_ref.at[_tile_rows(0, tm)], zero_sem).wait()

        for e in range(N_EXPERTS):
            zero_block(last_ref[e])
        lax.fori_loop(nv_ref[0], n_blocks, lambda b, c: (zero_block(b), c)[1], 0)
        for e in range(N_EXPERTS):
            zero_done()
        lax.fori_loop(nv_ref[0], n_blocks, lambda b, c: (zero_done(), c)[1], 0)

    _to_row_tiles(buf, (slot,), _rms(h_ref[...], g_ref[...]))

    def issue(t, carry):
        for k in range(TOP_K):
            d = dest_ref[0, 0, t * TOP_K + k]
            pltpu.make_async_copy(buf.at[slot, _tile_rows(t)], xs_ref.at[_tile_rows(d)],
                                  sem.at[slot]).start(priority=k % 2)
        return carry

    lax.fori_loop(0, tm, issue, 0, unroll=8)

    def wait_slot(s):
        for _ in range(TOP_K):
            pltpu.make_async_copy(buf.at[s], xs_ref.at[_tile_rows(0, tm)], sem.at[s]).wait()

    @pl.when(i > 0)
    def _():
        wait_slot(1 - slot)

    @pl.when(i == pl.num_programs(0) - 1)
    def _():
        wait_slot(slot)


def _dispatch(last_block, n_valid, dest3, h1, g, n_slots):
    T = h1.shape[0]
    tm = TM_TOK
    grid_spec = pltpu.PrefetchScalarGridSpec(
        num_scalar_prefetch=2,
        grid=(T // tm,),
        in_specs=[pl.BlockSpec((1, 1, TOP_K * tm), lambda i, lb, nv: (i, 0, 0), memory_space=pltpu.SMEM),
                  pl.BlockSpec((tm, D_MODEL), lambda i, lb, nv: (i, 0)),
                  pl.BlockSpec((1, D_MODEL), lambda i, lb, nv: (0, 0), pipeline_mode=pl.Buffered(1))],
        out_specs=pl.BlockSpec(memory_space=pl.ANY),
        scratch_shapes=[pltpu.VMEM((2, tm * ROW_SUBLANES, LANES), F32), pltpu.SemaphoreType.DMA((2,)),
                        pltpu.SemaphoreType.DMA(())],
    )
    return pl.pallas_call(
        _dispatch_kernel,
        grid_spec=grid_spec,
        out_shape=jax.ShapeDtypeStruct((n_slots * ROW_SUBLANES, LANES), F32),
        compiler_params=_cparams("arbitrary"),
        name="dispatch",
    )(last_block, n_valid, dest3, h1, g)


def _experts_kernel(be_ref, nv_ref, x_ref, wgu_ref, bgu_ref, wd_ref, bd_ref, y_ref):
    del be_ref
    tm = x_ref.shape[0] // ROW_SUBLANES

    @pl.when(pl.program_id(0) < nv_ref[0])
    def _():
        x = _from_row_tiles(x_ref, (), 0, tm).astype(BF16)
        gu = jnp.dot(x, wgu_ref[0], preferred_element_type=F32) + bgu_ref[0]
        glu = jnp.minimum(gu[:, :D_EXPERT], SWIGLU_LIMIT)
        lin = jnp.clip(gu[:, D_EXPERT:], -SWIGLU_LIMIT, SWIGLU_LIMIT)
        act = glu * jax.nn.sigmoid(SWIGLU_ALPHA * glu) * (lin + 1.0)
        _to_row_tiles(y_ref, (), jnp.dot(act.astype(BF16), wd_ref[0], preferred_element_type=F32) + bd_ref[0])

    @pl.when(pl.program_id(0) >= nv_ref[0])
    def _():
        y_ref[...] = jnp.zeros_like(y_ref)


def _experts(block_expert, n_valid, xs, wgu, bgu, wd, bd):
    tm = TM_EXP
    n_blocks = xs.shape[0] // (tm * ROW_SUBLANES)
    live = lambda b, be, nv: jnp.maximum(jnp.minimum(b, nv[0] - 1), 0)
    grid_spec = pltpu.PrefetchScalarGridSpec(
        num_scalar_prefetch=2,
        grid=(n_blocks,),
        in_specs=[pl.BlockSpec((tm * ROW_SUBLANES, LANES), lambda b, be, nv: (live(b, be, nv), 0)),
                  pl.BlockSpec((1, D_MODEL, 2 * D_EXPERT), lambda b, be, nv: (be[b], 0, 0)),
                  pl.BlockSpec((1, 1, 2 * D_EXPERT), lambda b, be, nv: (be[b], 0, 0)),
                  pl.BlockSpec((1, D_EXPERT, D_MODEL), lambda b, be, nv: (be[b], 0, 0)),
                  pl.BlockSpec((1, 1, D_MODEL), lambda b, be, nv: (be[b], 0, 0))],
        out_specs=pl.BlockSpec((tm * ROW_SUBLANES, LANES), lambda b, be, nv: (b, 0)),
    )
    return pl.pallas_call(
        _experts_kernel,
        grid_spec=grid_spec,
        out_shape=jax.ShapeDtypeStruct(xs.shape, F32),
        compiler_params=_cparams("arbitrary"),
        name="experts",
    )(block_expert, n_valid, xs, wgu, bgu, wd, bd)


def _combine_kernel(dest0_ref, next_dest_ref, h_ref, gate_ref, p_ref, gp_ref, gf_ref, wpg_hbm, wpp_hbm,
                    ys_ref, o_ref, buf0, buf1, sem, wpg_ref, wpp_ref, w_stage, w_sem, *, n_tiles):
    tm = h_ref.shape[0]
    i = pl.program_id(0)
    bufs = (buf0, buf1)

    def row_copy(dref, t, k, s):
        d = dref[0, 0, t * TOP_K + k]
        return pltpu.make_async_copy(ys_ref.at[_tile_rows(d)], bufs[s].at[_tile_rows(k * tm + t)], sem.at[s])

    def wait_slot(s):
        for _ in range(TOP_K):
            pltpu.make_async_copy(ys_ref.at[_tile_rows(0, tm)], bufs[s].at[_tile_rows(0, tm)], sem.at[s]).wait()

    @pl.when(i == 0)
    def _():
        _load_weight_bf16(wpg_hbm, wpg_ref, w_stage, w_sem)
        _load_weight_bf16(wpp_hbm, wpp_ref, w_stage, w_sem)

        def issue(t, carry):
            for k in range(TOP_K):
                row_copy(dest0_ref, t, k, 0).start(priority=k % 2)
            return carry
        lax.fori_loop(0, tm, issue, 0, unroll=8)

    def step(s):
        wait_slot(s)
        for t in range(tm):
            for k in range(TOP_K):
                row_copy(next_dest_ref, t, k, 1 - s).start(priority=k % 2)
        proj = jnp.dot(p_ref[...].astype(BF16), wpp_ref[...], preferred_element_type=F32)
        h = h_ref[...]
        for k in range(TOP_K):
            h = h + gate_ref[:, k:k + 1] * _from_row_tiles(bufs[s], (), k * tm, tm)
        ple_gate = jax.nn.sigmoid(jnp.dot(_rms(h, gp_ref[...]).astype(BF16), wpg_ref[...], preferred_element_type=F32))
        h = h + ple_gate * proj
        o_ref[...] = _rms(h, gf_ref[...])

    for s in range(2):
        pl.when(lax.rem(i, 2) == s)(functools.partial(step, s))

    @pl.when(i == n_tiles - 1)
    def _():
        wait_slot(n_tiles % 2)


def _combine(dest3, h1, gate, p2, gp, wpg, wpp, gf, ys):
    T = h1.shape[0]
    tm = TM_TOK
    n_tiles = T // tm
    row = lambda w: pl.BlockSpec((tm, w), lambda i: (i, 0))
    dest_spec = lambda ahead: pl.BlockSpec((1, 1, TOP_K * tm), lambda i: (jnp.minimum(i + ahead, n_tiles - 1), 0, 0),
                                           memory_space=pltpu.SMEM)
    gather_buf = pltpu.VMEM((TOP_K * tm * ROW_SUBLANES, LANES), F32)
    return pl.pallas_call(
        functools.partial(_combine_kernel, n_tiles=n_tiles),
        grid=(n_tiles,),
        in_specs=[dest_spec(0), dest_spec(1),
                  row(D_MODEL), row(TOP_K), row(PLE_DIM), _resident((1, D_MODEL)), _resident((1, D_MODEL))]
                 + [pl.BlockSpec(memory_space=pl.ANY)] * 3,
        out_specs=row(D_MODEL),
        out_shape=jax.ShapeDtypeStruct((T, D_MODEL), F32),
        scratch_shapes=[gather_buf, gather_buf, pltpu.SemaphoreType.DMA((2,)),
                        pltpu.VMEM(wpg.shape, BF16), pltpu.VMEM(wpp.shape, BF16)] + _weight_stage(D_MODEL),
        compiler_params=_cparams("arbitrary"),
        name="combine",
    )(dest3, dest3, h1, gate, p2, gp, gf, wpg, wpp, ys)


def _layer(h, p_i, g_mix, w_in, rel_bias, w_att_out, ln_v_g, ln_v_b, w_spatial, b_spatial, w_gmlp_out, w_out,
           g_moe, w_router, b_router, w_gate_up, b_gate_up, w_down, b_down, g_ple, w_ple_gate, w_ple_proj,
           g_final, B, S):
    T = B * S
    row = lambda v: v.reshape(1, -1).astype(F32)

    assert S % TM_PROJ == 0
    *att_in, uv, gl, wgu_bf = _in_proj(h, row(g_mix), w_in.astype(F32), w_gate_up.astype(F32), B, S)

    outs, lses = [], []
    for g, (window, dilation) in enumerate(ATT_GROUPS):
        assert window // dilation == BLK and S % (dilation * BLK) == 0
        bias = _bias_table(rel_bias[:, g * HEADS_PER_GROUP:(g + 1) * HEADS_PER_GROUP], dilation)
        o, lse = _attention_group(att_in[g], bias, dilation, B, S)
        outs.append(o)
        lses.append(lse)

    causal = jnp.asarray(np.tril(np.ones((CHUNK, CHUNK), np.float32)))
    w_c = (w_spatial.astype(F32) * causal[None]).astype(BF16)
    wc2 = jnp.concatenate([w_c[0::2], w_c[1::2]], axis=2)
    bs = jnp.repeat(b_spatial.astype(F32).T, GMLP_GD, axis=1)
    h1, wd_bf = _mix(h, outs, lses, uv, gl, w_att_out.astype(F32), w_gmlp_out.astype(F32), w_out.astype(F32),
                     wc2, bs, row(ln_v_g), row(ln_v_b), w_down.astype(F32), S)

    wr_hi = w_router.astype(BF16)
    wr_lo = (w_router.astype(F32) - wr_hi.astype(F32)).astype(BF16)
    eidx, gate, rank, counts = _router(h1, row(g_moe), jnp.concatenate([wr_hi, wr_lo], axis=1).T,
                                       b_router.reshape(-1, 1).astype(F32))
    cnt = counts[:, 0].astype(jnp.int32)
    blk_counts = (cnt + TM_EXP - 1) // TM_EXP
    blk_end = jnp.cumsum(blk_counts)
    pad_start = (blk_end - blk_counts) * TM_EXP
    n_blocks = T * TOP_K // TM_EXP + N_EXPERTS
    n_valid = blk_end[-1:].astype(jnp.int32)
    blk = jnp.minimum(jnp.arange(n_blocks, dtype=jnp.int32), n_valid[0] - 1)
    block_expert = jnp.minimum(jnp.sum((blk_end[None, :] <= blk[:, None]).astype(jnp.int32), axis=1), N_EXPERTS - 1)
    expert_ids = jnp.arange(N_EXPERTS, dtype=jnp.int32)
    dest = rank + jnp.sum(jnp.where(eidx[..., None] == expert_ids, pad_start, 0), axis=-1)
    token_major = lambda a: jnp.transpose(a, (0, 2, 1)).reshape(T, TOP_K)
    dest3 = token_major(dest).reshape(T // TM_TOK, 1, TM_TOK * TOP_K)
    gate = token_major(gate)

    last_block = jnp.maximum(blk_end - 1, 0).astype(jnp.int32)
    xs = _dispatch(last_block, n_valid, dest3, h1, row(g_moe), n_blocks * TM_EXP)
    ys = _experts(block_expert, n_valid, xs, wgu_bf, b_gate_up.reshape(N_EXPERTS, 1, -1).astype(F32),
                  wd_bf, b_down.reshape(N_EXPERTS, 1, -1).astype(F32))
    return _combine(dest3, h1, gate, p_i, row(g_ple), w_ple_gate.astype(F32), w_ple_proj.astype(F32),
                    row(g_final), ys)


def kernel(x, p, g_mix, w_in, rel_bias, w_att_out, ln_v_g, ln_v_b, w_spatial, b_spatial, w_gmlp_out, w_out, g_moe, w_router, b_router, w_gate_up, b_gate_up, w_down, b_down, g_ple, w_ple_gate, w_ple_proj, g_final):
    B, S, D = x.shape
    depth = p.shape[0]
    assert depth == 1, "the final RMSNorm is fused into the (single) layer's last kernel"
    out = _layer(x.reshape(B * S, D), p[0].reshape(B * S, PLE_DIM), g_mix[0], w_in[0], rel_bias, w_att_out[0],
                 ln_v_g[0], ln_v_b[0], w_spatial[0], b_spatial[0], w_gmlp_out[0], w_out[0], g_moe[0], w_router[0],
                 b_router[0], w_gate_up[0], b_gate_up[0], w_down[0], b_down[0], g_ple[0], w_ple_gate[0],
                 w_ple_proj[0], g_final, B, S)
    return out.reshape(B, S, D)
```

```python
import functools

import jax
import jax.numpy as jnp
import numpy as np
from jax import lax
from jax.experimental import pallas as pl
from jax.experimental.pallas import tpu as pltpu

F32 = jnp.float32
BF16 = jnp.bfloat16

D_MODEL = 1024
HEAD_DIM = 64
ATT_GROUPS = ((128, 1), (512, 4), (2048, 16))
HEADS_PER_GROUP = 4
GROUP_W = HEADS_PER_GROUP * HEAD_DIM
N_DIL = len(ATT_GROUPS)
ATT_W = N_DIL * GROUP_W
BLK = 128
REL_BUCKETS = 32
REL_MAX_DIST = 2048
CHUNK = 128
GMLP_W = 768
GMLP_GD = 64
N_BRANCH = 2
IN_W = 3 * ATT_W + 2 * GMLP_W + N_BRANCH * D_MODEL
N_EXPERTS = 32
TOP_K = 4
D_EXPERT = D_MODEL
SWIGLU_LIMIT = 7.0
SWIGLU_ALPHA = 1.702
PLE_DIM = 256
EPS = 1e-6
MASKED = -1e30
LOG2E = float(np.log2(np.e))
LN2 = float(np.log(2.0))

QKV_G = 3 * GROUP_W

LANES = 128
ROW_SUBLANES = D_MODEL // LANES
assert ROW_SUBLANES == 8
MXU_N = 256
VMEM_LIMIT = 56 * 1024 * 1024

TM_PROJ = 512
TM_TOK = 256
TM_EXP = 512
assert TM_EXP % TM_TOK == 0


def _cparams(*sem):
    return pltpu.CompilerParams(dimension_semantics=sem, vmem_limit_bytes=VMEM_LIMIT)


def _resident(shape):
    nd = len(shape)
    return pl.BlockSpec(shape, lambda *_: (0,) * nd, pipeline_mode=pl.Buffered(1))


def _rms(x, g):
    return x * lax.rsqrt(jnp.mean(x * x, axis=-1, keepdims=True) + EPS) * g


def _load_weight_bf16(w_hbm, w_bf, stage, sem):
    rows = stage.shape[1]
    n_chunks = w_hbm.shape[0] // rows
    assert n_chunks * rows == w_hbm.shape[0] and stage.shape[2] == w_hbm.shape[1]

    def chunk(c):
        return pltpu.make_async_copy(w_hbm.at[pl.ds(c * rows, rows)], stage.at[c % 2], sem.at[c % 2])

    chunk(0).start()
    for c in range(n_chunks):
        if c + 1 < n_chunks:
            chunk(c + 1).start()
        chunk(c).wait()
        w_bf[c * rows:(c + 1) * rows, :] = stage[c % 2].astype(BF16)


def _expert_slice_spec(n_steps, width):
    per_expert = n_steps // N_EXPERTS
    assert per_expert * N_EXPERTS == n_steps and D_MODEL % per_expert == 0
    return pl.BlockSpec((1, D_MODEL // per_expert, width), lambda i: (i // per_expert, i % per_expert, 0))


def _inproj_kernel(x_ref, g_ref, w_hbm, we_ref, a1_ref, a2_ref, a3_ref, uv_ref, gl_ref, we_bf_ref,
                   scr, w_ref, w_stage, w_sem):
    @pl.when(pl.program_id(0) == 0)
    def _():
        _load_weight_bf16(w_hbm, w_ref, w_stage, w_sem)

    we_bf_ref[...] = we_ref[...].astype(BF16)

    tm = x_ref.shape[0]
    n = _rms(x_ref[...], g_ref[...]).astype(BF16)
    att_refs = (a1_ref, a2_ref, a3_ref)
    n_att, n_uv = 3 * ATT_W // MXU_N, 2 * GMLP_W // MXU_N
    for c in range(IN_W // MXU_N):
        z = jnp.dot(n, w_ref[:, c * MXU_N:(c + 1) * MXU_N], preferred_element_type=F32)
        if c < n_att:
            which, g = divmod(c, N_DIL)
            d = ATT_GROUPS[g][1]
            dst = att_refs[g]
            cols = slice(which * GROUP_W, (which + 1) * GROUP_W)
            if d == 1:
                dst[0, 0, :, cols] = z.astype(BF16)
                continue
            scr[0] = z[:, :LANES]
            scr[1] = z[:, LANES:]
            for r in range(d):
                zr = jnp.concatenate([scr[0, pl.ds(r, tm // d, stride=d), :],
                                      scr[1, pl.ds(r, tm // d, stride=d), :]], axis=1)
                dst[0, r, :, cols] = zr.astype(BF16)
        elif c < n_att + n_uv:
            uv_ref[:, (c - n_att) * MXU_N:(c - n_att + 1) * MXU_N] = z.astype(BF16)
        else:
            gl_ref[:, (c - n_att - n_uv) * MXU_N:(c - n_att - n_uv + 1) * MXU_N] = z.astype(BF16)


def _plane_spec(d, tm, tiles_per_seq, width):
    return pl.BlockSpec((1, d, tm // d, width), lambda i: (i // tiles_per_seq, 0, i % tiles_per_seq, 0))


W_STAGE_ROWS = 128


def _weight_stage(width):
    return [pltpu.VMEM((2, W_STAGE_ROWS, width), F32), pltpu.SemaphoreType.DMA((2,))]


def _in_proj(x2, g, w, w_expert, B, S):
    T = x2.shape[0]
    tm = TM_PROJ
    row = lambda w: pl.BlockSpec((tm, w), lambda i: (i, 0))
    dils = [d for _, d in ATT_GROUPS]
    we_spec = _expert_slice_spec(T // tm, w_expert.shape[2])
    return pl.pallas_call(
        _inproj_kernel,
        grid=(T // tm,),
        in_specs=[row(D_MODEL), _resident((1, D_MODEL)), pl.BlockSpec(memory_space=pl.ANY), we_spec],
        out_specs=[_plane_spec(d, tm, S // tm, QKV_G) for d in dils] + [row(2 * GMLP_W), row(N_BRANCH * D_MODEL), we_spec],
        out_shape=[jax.ShapeDtypeStruct((B, d, S // d, QKV_G), BF16) for d in dils]
                  + [jax.ShapeDtypeStruct((T, 2 * GMLP_W), BF16),
                     jax.ShapeDtypeStruct((T, N_BRANCH * D_MODEL), BF16),
                     jax.ShapeDtypeStruct(w_expert.shape, BF16)],
        scratch_shapes=[pltpu.VMEM((2, tm, LANES), F32), pltpu.VMEM((D_MODEL, IN_W), BF16)] + _weight_stage(IN_W),
        compiler_params=_cparams("arbitrary"),
        name="in_proj",
    )(x2, g, w, w_expert)


def _t5_bucket(n):
    exact = REL_BUCKETS // 2
    nf = np.maximum(n, 1).astype(np.float32)
    large = exact + (np.log(nf / exact) / np.log(REL_MAX_DIST / exact) * (REL_BUCKETS - exact)).astype(np.int32)
    large = np.minimum(large, REL_BUCKETS - 1)
    return np.where(n < exact, n, large).astype(np.int32)


def _bias_table(rel_bias_g, dilation):
    n = 3 * BLK
    dist = 2 * BLK - 1 - np.arange(n)
    valid = (dist >= 0) & (dist <= BLK)
    bucket = _t5_bucket(np.clip(dist, 0, BLK) * dilation)
    c = jnp.where(jnp.asarray(valid)[None, :], rel_bias_g.astype(F32)[bucket].T * LOG2E, MASKED)
    shifted = jnp.tile(c, (1, BLK))[:, :BLK * (n - 1)].reshape(HEADS_PER_GROUP, BLK, n - 1)
    return shifted[:, :, BLK - 1:].reshape(HEADS_PER_GROUP * BLK, 2 * BLK)


def _attn_kernel(cur_ref, prev_ref, bias_ref, o_ref, lse_ref):
    rg, rb = cur_ref.shape[1], cur_ref.shape[2] // BLK
    starts_sequence = pl.program_id(2) == 0
    lane_head = lax.broadcasted_iota(jnp.int32, (1, GROUP_W), 1) // HEAD_DIM
    scale = HEAD_DIM ** -0.5
    head_bf = [jnp.where(lane_head == h, scale, 0.0).astype(BF16) for h in range(HEADS_PER_GROUP)]
    key_is_prev = lax.broadcasted_iota(jnp.int32, (1, 2 * BLK), 1) < BLK
    nt = (((1,), (1,)), ((), ()))
    qc, kc_, vc_ = slice(0, GROUP_W), slice(GROUP_W, 2 * GROUP_W), slice(2 * GROUP_W, 3 * GROUP_W)

    def by_head(x):
        sel = x[(HEADS_PER_GROUP - 1) * BLK:]
        for h in range(HEADS_PER_GROUP - 2, -1, -1):
            sel = jnp.where(lane_head == h, x[h * BLK:(h + 1) * BLK], sel)
        return sel

    for r, j in [(r, j) for r in range(rg) for j in range(rb)]:
        rows = slice(j * BLK, (j + 1) * BLK)
        prev = prev_ref if j == 0 else cur_ref
        prows = slice(0, BLK) if j == 0 else slice((j - 1) * BLK, j * BLK)
        q = cur_ref[0, r, rows, qc]
        k = jnp.concatenate([prev[0, r, prows, kc_], cur_ref[0, r, rows, kc_]], axis=0)
        v = jnp.concatenate([prev[0, r, prows, vc_], cur_ref[0, r, rows, vc_]], axis=0)
        q_bd = jnp.concatenate([q * head_bf[h] for h in range(HEADS_PER_GROUP)], axis=0)
        s = lax.dot_general(q_bd, k, nt, preferred_element_type=F32) * LOG2E + bias_ref[...]
        if j == 0:
            s = jnp.where(jnp.logical_and(starts_sequence, key_is_prev), MASKED, s)
        m = jnp.max(s, axis=-1, keepdims=True)
        p = jnp.exp2(s - m)
        den = jnp.sum(p, axis=-1, keepdims=True)
        o = jnp.dot(p.astype(BF16), v, preferred_element_type=F32)
        o_ref[0, r, rows, :] = (by_head(o) * by_head(1.0 / den)).astype(BF16)
        lse_ref[0, r, rows, :] = jnp.broadcast_to(by_head(m * LN2 + jnp.log(den)), (BLK, GROUP_W))


ATT_SUBBLOCKS = 8


def _attention_group(a, bias, dilation, B, S):
    sd = S // dilation
    rb = min(ATT_SUBBLOCKS, sd // BLK)
    rg = min(ATT_SUBBLOCKS // rb, dilation)
    o, lse = pl.pallas_call(
        _attn_kernel,
        grid=(B, dilation // rg, sd // (rb * BLK)),
        in_specs=[pl.BlockSpec((1, rg, rb * BLK, QKV_G), lambda b, r, n: (b, r, n, 0)),
                  pl.BlockSpec((1, rg, BLK, QKV_G), lambda b, r, n: (b, r, jnp.maximum(n * rb - 1, 0), 0)),
                  _resident((HEADS_PER_GROUP * BLK, 2 * BLK))],
        out_specs=[pl.BlockSpec((1, rg, rb * BLK, GROUP_W), lambda b, r, n: (b, r, n, 0))] * 2,
        out_shape=[jax.ShapeDtypeStruct((B, dilation, sd, GROUP_W), BF16),
                   jax.ShapeDtypeStruct((B, dilation, sd, GROUP_W), F32)],
        compiler_params=_cparams("parallel", "parallel", "parallel"),
        name=f"attn_d{dilation}",
    )(a, a, bias)
    return o, lse


def _gelu(x):
    return x * (lax.erf(x * (2.0 ** -0.5)) + 1.0) * 0.5


def _token_major(src_ref, d, scr, slot, tm):
    if d == 1:
        return src_ref[0, 0].astype(F32)
    for r in range(d):
        piece = src_ref[0, r].astype(F32)
        scr[slot, pl.ds(r, tm // d, stride=d), :] = piece[:, :LANES]
        scr[slot + 1, pl.ds(r, tm // d, stride=d), :] = piece[:, LANES:]
    return jnp.concatenate([scr[slot], scr[slot + 1]], axis=1)


def _mix_kernel(x_ref, o1_ref, o2_ref, o3_ref, l1_ref, l2_ref, l3_ref, uv_ref, gl_ref,
                wa_hbm, wg_hbm, wo_hbm, wc_ref, bs_ref, lng_ref, lnb_ref, we_ref, h_ref, we_bf_ref, g_scr, t_scr,
                wa_ref, wg_ref, wo_ref, w_stage, w_sem):
    @pl.when(pl.program_id(0) == 0)
    def _():
        for w_hbm, w_bf in ((wa_hbm, wa_ref), (wg_hbm, wg_ref), (wo_hbm, wo_ref)):
            _load_weight_bf16(w_hbm, w_bf, w_stage, w_sem)

    we_bf_ref[...] = we_ref[...].astype(BF16)

    tm = x_ref.shape[0]
    dils = [d for _, d in ATT_GROUPS]
    o1, o2, o3 = [_token_major(ref, d, t_scr, 4 * i, tm) for i, (ref, d) in enumerate(zip((o1_ref, o2_ref, o3_ref), dils))]
    l1, l2, l3 = [_token_major(ref, d, t_scr, 4 * i + 2, tm) for i, (ref, d) in enumerate(zip((l1_ref, l2_ref, l3_ref), dils))]
    lm = jnp.maximum(jnp.maximum(l1, l2), l3)
    e1, e2, e3 = jnp.exp(l1 - lm), jnp.exp(l2 - lm), jnp.exp(l3 - lm)
    att = (e1 * o1 + e2 * o2 + e3 * o3) / (e1 + e2 + e3)
    y_att = jnp.dot(att.astype(BF16), wa_ref[...], preferred_element_type=F32)

    zu = _gelu(uv_ref[:, :GMLP_W].astype(F32))
    zv = _gelu(uv_ref[:, GMLP_W:].astype(F32))
    mu = jnp.mean(zv, axis=-1, keepdims=True)
    var = jnp.mean(jnp.square(zv - mu), axis=-1, keepdims=True)
    vn = (zv - mu) * lax.rsqrt(var + EPS) * lng_ref[...] + lnb_ref[...]
    low_half = lax.broadcasted_iota(jnp.int32, (CHUNK, 2 * GMLP_GD), 1) < GMLP_GD
    for c in range(tm // CHUNK):
        rows = slice(c * CHUNK, (c + 1) * CHUNK)
        for s in range(GMLP_W // (2 * GMLP_GD)):
            cols = slice(s * 2 * GMLP_GD, (s + 1) * 2 * GMLP_GD)
            v2 = vn[rows, cols]
            rhs = jnp.concatenate([jnp.where(low_half, v2, 0.0), jnp.where(low_half, 0.0, v2)], axis=0).astype(BF16)
            mixed = jnp.dot(wc_ref[s], rhs, preferred_element_type=F32) + bs_ref[:, cols]
            g_scr[rows, cols] = (zu[rows, cols] * mixed).astype(BF16)
    y_gm = jnp.dot(g_scr[...], wg_ref[...], preferred_element_type=F32)

    gate_a = jax.nn.sigmoid(gl_ref[:, :D_MODEL].astype(F32))
    gate_g = jax.nn.sigmoid(gl_ref[:, D_MODEL:].astype(F32))
    merged = (gate_a * y_att + gate_g * y_gm).astype(BF16)
    h_ref[...] = x_ref[...] + jnp.dot(merged, wo_ref[...], preferred_element_type=F32)


def _mix(x2, outs, lses, uv, gl, wa, wg, wo, wc2, bs, lng, lnb, w_expert, S):
    T = x2.shape[0]
    tm = TM_PROJ
    row = lambda w: pl.BlockSpec((tm, w), lambda i: (i, 0))
    att = [_plane_spec(d, tm, S // tm, GROUP_W) for _, d in ATT_GROUPS]
    we_spec = _expert_slice_spec(T // tm, w_expert.shape[2])
    return pl.pallas_call(
        _mix_kernel,
        grid=(T // tm,),
        in_specs=[row(D_MODEL)] + att + att + [row(2 * GMLP_W), row(N_BRANCH * D_MODEL)]
                 + [pl.BlockSpec(memory_space=pl.ANY)] * 3
                 + [_resident(wc2.shape), _resident(bs.shape), _resident(lng.shape), _resident(lnb.shape), we_spec],
        out_specs=[row(D_MODEL), we_spec],
        out_shape=[jax.ShapeDtypeStruct((T, D_MODEL), F32), jax.ShapeDtypeStruct(w_expert.shape, BF16)],
        scratch_shapes=[pltpu.VMEM((tm, GMLP_W), BF16), pltpu.VMEM((4 * N_DIL, tm, LANES), F32),
                        pltpu.VMEM(wa.shape, BF16), pltpu.VMEM(wg.shape, BF16), pltpu.VMEM(wo.shape, BF16)]
                       + _weight_stage(D_MODEL),
        compiler_params=_cparams("arbitrary"),
        name="mix",
    )(x2, *outs, *lses, uv, gl, wa, wg, wo, wc2, bs, lng, lnb, w_expert)


def _router_kernel(h_ref, g_ref, wr_ref, br_ref, upper_ref, eidx_ref, gate_ref, rank_ref, cnt_ref, carry):
    tm = h_ref.shape[0]

    @pl.when(pl.program_id(0) == 0)
    def _():
        carry[...] = jnp.zeros_like(carry)

    hn = _rms(h_ref[...], g_ref[...])
    hi = hn.astype(BF16)
    lo = (hn - hi.astype(F32)).astype(BF16)
    nt = (((1,), (1,)), ((), ()))
    by_hi = lax.dot_general(wr_ref[...], hi, nt, preferred_element_type=F32)
    by_lo = lax.dot_general(wr_ref[:N_EXPERTS, :], lo, nt, preferred_element_type=F32)
    logits = by_hi[:N_EXPERTS] + by_hi[N_EXPERTS:] + by_lo + br_ref[...]
    expert = lax.broadcasted_iota(jnp.int32, (N_EXPERTS, tm), 0)
    vals, hots = [], []
    l = logits
    for k in range(TOP_K):
        m = jnp.max(l, axis=0, keepdims=True)
        idx = jnp.min(jnp.where(l == m, expert, N_EXPERTS), axis=0, keepdims=True)
        hot = expert == idx
        eidx_ref[0, k:k + 1, :] = idx
        vals.append(m)
        hots.append(hot)
        l = jnp.where(hot, -jnp.inf, l)
    ex = [jnp.exp(v - vals[0]) for v in vals]
    tot = ex[0] + ex[1] + ex[2] + ex[3]
    for k in range(TOP_K):
        gate_ref[0, k:k + 1, :] = ex[k] / tot
    multi = jnp.zeros((N_EXPERTS, tm), F32)
    for hot in hots:
        multi = multi + hot.astype(F32)
    before = jnp.dot(multi.astype(BF16), upper_ref[...], preferred_element_type=F32) + carry[...]
    for k in range(TOP_K):
        rank_ref[0, k:k + 1, :] = jnp.sum(jnp.where(hots[k], before, 0.0), axis=0, keepdims=True).astype(jnp.int32)
    carry[...] += jnp.sum(multi, axis=1, keepdims=True)
    cnt_ref[...] = carry[...]


def _router(h1, g, wr_t, br_col):
    T = h1.shape[0]
    tm = TM_PROJ
    upper = jnp.asarray(np.triu(np.ones((tm, tm), np.float32), k=1), BF16)
    k_rows = pl.BlockSpec((1, TOP_K, tm), lambda i: (i, 0, 0))
    k_shape = lambda dt: jax.ShapeDtypeStruct((T // tm, TOP_K, tm), dt)
    return pl.pallas_call(
        _router_kernel,
        grid=(T // tm,),
        in_specs=[pl.BlockSpec((tm, D_MODEL), lambda i: (i, 0)), _resident((1, D_MODEL)),
                  _resident((2 * N_EXPERTS, D_MODEL)), _resident((N_EXPERTS, 1)), _resident((tm, tm))],
        out_specs=[k_rows, k_rows, k_rows, pl.BlockSpec((N_EXPERTS, 1), lambda i: (0, 0))],
        out_shape=[k_shape(jnp.int32), k_shape(F32), k_shape(jnp.int32),
                   jax.ShapeDtypeStruct((N_EXPERTS, 1), F32)],
        scratch_shapes=[pltpu.VMEM((N_EXPERTS, 1), F32)],
        compiler_params=_cparams("arbitrary"),
        name="router",
    )(h1, g, wr_t, br_col, upper)


def _to_row_tiles(ref, lead, value):
    n = value.shape[0]
    for c in range(ROW_SUBLANES):
        ref[(*lead, pl.ds(c, n, stride=ROW_SUBLANES), slice(None))] = value[:, c * LANES:(c + 1) * LANES]


def _from_row_tiles(ref, lead, first, n):
    return jnp.concatenate(
        [ref[(*lead, pl.ds(first * ROW_SUBLANES + c, n, stride=ROW_SUBLANES), slice(None))] for c in range(ROW_SUBLANES)],
        axis=1)


def _tile_rows(idx, n=1):
    return pl.ds(pl.multiple_of(idx * ROW_SUBLANES, ROW_SUBLANES), n * ROW_SUBLANES)


def _dispatch_kernel(last_ref, nv_ref, dest_ref, h_ref, g_ref, xs_ref, buf, sem, zero_sem):
    tm = h_ref.shape[0]
    n_blocks = xs_ref.shape[0] // (TM_EXP * ROW_SUBLANES)
    i = pl.program_id(0)
    slot = lax.rem(i, 2)

    @pl.when(i == 0)
    def _():
        buf[1] = jnp.zeros(buf.shape[1:], F32)

        def zero_block(b):
            for part in range(TM_EXP // tm):
                pltpu.make_async_copy(buf.at[1], xs_ref.at[_tile_rows(b * TM_EXP + part * tm, tm)], zero_sem).start()

        def zero_done():
            for part in range(TM_EXP // tm):
                pltpu.make_async_copy(buf.at[1], xs_ref.at[_tile_rows(0, tm)], zero_sem).wait()

        for e in range(N_EXPERTS):
            zero_block(last_ref[e])
        lax.fori_loop(nv_ref[0], n_blocks, lambda b, c: (zero_block(b), c)[1], 0)
        for e in range(N_EXPERTS):
            zero_done()
        lax.fori_loop(nv_ref[0], n_blocks, lambda b, c: (zero_done(), c)[1], 0)

    _to_row_tiles(buf, (slot,), _rms(h_ref[...], g_ref[...]))

    def issue(t, carry):
        for k in range(TOP_K):
            d = dest_ref[0, 0, t * TOP_K + k]
            pltpu.make_async_copy(buf.at[slot, _tile_rows(t)], xs_ref.at[_tile_rows(d)],
                                  sem.at[slot]).start(priority=k % 2)
        return carry

    lax.fori_loop(0, tm, issue, 0, unroll=8)

    def wait_slot(s):
        for _ in range(TOP_K):
            pltpu.make_async_copy(buf.at[s], xs_ref.at[_tile_rows(0, tm)], sem.at[s]).wait()

    @pl.when(i > 0)
    def _():
        wait_slot(1 - slot)

    @pl.when(i == pl.num_programs(0) - 1)
    def _():
        wait_slot(slot)


def _dispatch(last_block, n_valid, dest3, h1, g, n_slots):
    T = h1.shape[0]
    tm = TM_TOK
    grid_spec = pltpu.PrefetchScalarGridSpec(
        num_scalar_prefetch=2,
        grid=(T // tm,),
        in_specs=[pl.BlockSpec((1, 1, TOP_K * tm), lambda i, lb, nv: (i, 0, 0), memory_space=pltpu.SMEM),
                  pl.BlockSpec((tm, D_MODEL), lambda i, lb, nv: (i, 0)),
                  pl.BlockSpec((1, D_MODEL), lambda i, lb, nv: (0, 0), pipeline_mode=pl.Buffered(1))],
        out_specs=pl.BlockSpec(memory_space=pl.ANY),
        scratch_shapes=[pltpu.VMEM((2, tm * ROW_SUBLANES, LANES), F32), pltpu.SemaphoreType.DMA((2,)),
                        pltpu.SemaphoreType.DMA(())],
    )
    return pl.pallas_call(
        _dispatch_kernel,
        grid_spec=grid_spec,
        out_shape=jax.ShapeDtypeStruct((n_slots * ROW_SUBLANES, LANES), F32),
        compiler_params=_cparams("arbitrary"),
        name="dispatch",
    )(last_block, n_valid, dest3, h1, g)


def _experts_kernel(be_ref, nv_ref, x_ref, wgu_ref, bgu_ref, wd_ref, bd_ref, y_ref):
    del be_ref
    tm = x_ref.shape[0] // ROW_SUBLANES

    @pl.when(pl.program_id(0) < nv_ref[0])
    def _():
        x = _from_row_tiles(x_ref, (), 0, tm).astype(BF16)
        gu = jnp.dot(x, wgu_ref[0], preferred_element_type=F32) + bgu_ref[0]
        glu = jnp.minimum(gu[:, :D_EXPERT], SWIGLU_LIMIT)
        lin = jnp.clip(gu[:, D_EXPERT:], -SWIGLU_LIMIT, SWIGLU_LIMIT)
        act = glu * jax.nn.sigmoid(SWIGLU_ALPHA * glu) * (lin + 1.0)
        _to_row_tiles(y_ref, (), jnp.dot(act.astype(BF16), wd_ref[0], preferred_element_type=F32) + bd_ref[0])

    @pl.when(pl.program_id(0) >= nv_ref[0])
    def _():
        y_ref[...] = jnp.zeros_like(y_ref)


def _experts(block_expert, n_valid, xs, wgu, bgu, wd, bd):
    tm = TM_EXP
    n_blocks = xs.shape[0] // (tm * ROW_SUBLANES)
    live = lambda b, be, nv: jnp.maximum(jnp.minimum(b, nv[0] - 1), 0)
    grid_spec = pltpu.PrefetchScalarGridSpec(
        num_scalar_prefetch=2,
        grid=(n_blocks,),
        in_specs=[pl.BlockSpec((tm * ROW_SUBLANES, LANES), lambda b, be, nv: (live(b, be, nv), 0)),
                  pl.BlockSpec((1, D_MODEL, 2 * D_EXPERT), lambda b, be, nv: (be[b], 0, 0)),
                  pl.BlockSpec((1, 1, 2 * D_EXPERT), lambda b, be, nv: (be[b], 0, 0)),
                  pl.BlockSpec((1, D_EXPERT, D_MODEL), lambda b, be, nv: (be[b], 0, 0)),
                  pl.BlockSpec((1, 1, D_MODEL), lambda b, be, nv: (be[b], 0, 0))],
        out_specs=pl.BlockSpec((tm * ROW_SUBLANES, LANES), lambda b, be, nv: (b, 0)),
    )
    return pl.pallas_call(
        _experts_kernel,
        grid_spec=grid_spec,
        out_shape=jax.ShapeDtypeStruct(xs.shape, F32),
        compiler_params=_cparams("arbitrary"),
        name="experts",
    )(block_expert, n_valid, xs, wgu, bgu, wd, bd)


GATHER_SLOTS = 3


def _combine_kernel(*refs, n_tiles):
    ahead = GATHER_SLOTS - 1
    prime_refs, refs = refs[:ahead], refs[ahead:]
    (ahead_dest_ref, h_ref, gate_ref, p_ref, gp_ref, gf_ref, wpg_hbm, wpp_hbm, ys_ref, o_ref), refs = refs[:10], refs[10:]
    bufs, (sem, wpg_ref, wpp_ref, w_stage, w_sem) = refs[:GATHER_SLOTS], refs[GATHER_SLOTS:]
    tm = h_ref.shape[0]
    i = pl.program_id(0)

    def row_copy(dref, t, k, s):
        d = dref[0, 0, t * TOP_K + k]
        return pltpu.make_async_copy(ys_ref.at[_tile_rows(d)], bufs[s].at[_tile_rows(k * tm + t)], sem.at[s])

    def wait_slot(s):
        for _ in range(TOP_K):
            pltpu.make_async_copy(ys_ref.at[_tile_rows(0, tm)], bufs[s].at[_tile_rows(0, tm)], sem.at[s]).wait()

    @pl.when(i == 0)
    def _():
        _load_weight_bf16(wpg_hbm, wpg_ref, w_stage, w_sem)
        _load_weight_bf16(wpp_hbm, wpp_ref, w_stage, w_sem)

        for s, dref in enumerate(prime_refs):
            def issue(t, carry, s=s, dref=dref):
                for k in range(TOP_K):
                    row_copy(dref, t, k, s).start(priority=k % 2)
                return carry
            lax.fori_loop(0, tm, issue, 0, unroll=8)

    def step(s):
        wait_slot(s)
        for t in range(tm):
            for k in range(TOP_K):
                row_copy(ahead_dest_ref, t, k, (s + ahead) % GATHER_SLOTS).start(priority=k % 2)
        proj = jnp.dot(p_ref[...].astype(BF16), wpp_ref[...], preferred_element_type=F32)
        h = h_ref[...]
        for k in range(TOP_K):
            h = h + gate_ref[:, k:k + 1] * _from_row_tiles(bufs[s], (), k * tm, tm)
        ple_gate = jax.nn.sigmoid(jnp.dot(_rms(h, gp_ref[...]).astype(BF16), wpg_ref[...], preferred_element_type=F32))
        h = h + ple_gate * proj
        o_ref[...] = _rms(h, gf_ref[...])

    for s in range(GATHER_SLOTS):
        pl.when(lax.rem(i, GATHER_SLOTS) == s)(functools.partial(step, s))

    @pl.when(i == n_tiles - 1)
    def _():
        for extra in range(ahead):
            wait_slot((n_tiles + extra) % GATHER_SLOTS)


def _combine(dest3, h1, gate, p2, gp, wpg, wpp, gf, ys):
    T = h1.shape[0]
    tm = TM_TOK
    n_tiles = T // tm
    row = lambda w: pl.BlockSpec((tm, w), lambda i: (i, 0))
    dest_spec = lambda ahead: pl.BlockSpec((1, 1, TOP_K * tm), lambda i: (jnp.minimum(i + ahead, n_tiles - 1), 0, 0),
                                           memory_space=pltpu.SMEM)
    gather_buf = pltpu.VMEM((TOP_K * tm * ROW_SUBLANES, LANES), F32)
    ahead = GATHER_SLOTS - 1
    assert n_tiles > ahead
    return pl.pallas_call(
        functools.partial(_combine_kernel, n_tiles=n_tiles),
        grid=(n_tiles,),
        in_specs=[dest_spec(a) for a in range(ahead)] + [dest_spec(ahead),
                  row(D_MODEL), row(TOP_K), row(PLE_DIM), _resident((1, D_MODEL)), _resident((1, D_MODEL))]
                 + [pl.BlockSpec(memory_space=pl.ANY)] * 3,
        out_specs=row(D_MODEL),
        out_shape=jax.ShapeDtypeStruct((T, D_MODEL), F32),
        scratch_shapes=[gather_buf] * GATHER_SLOTS + [pltpu.SemaphoreType.DMA((GATHER_SLOTS,)),
                        pltpu.VMEM(wpg.shape, BF16), pltpu.VMEM(wpp.shape, BF16)] + _weight_stage(D_MODEL),
        compiler_params=_cparams("arbitrary"),
        name="combine",
    )(*([dest3] * (ahead + 1)), h1, gate, p2, gp, gf, wpg, wpp, ys)


def _layer(h, p_i, g_mix, w_in, rel_bias, w_att_out, ln_v_g, ln_v_b, w_spatial, b_spatial, w_gmlp_out, w_out,
           g_moe, w_router, b_router, w_gate_up, b_gate_up, w_down, b_down, g_ple, w_ple_gate, w_ple_proj,
           g_final, B, S):
    T = B * S
    row = lambda v: v.reshape(1, -1).astype(F32)

    assert S % TM_PROJ == 0
    *att_in, uv, gl, wgu_bf = _in_proj(h, row(g_mix), w_in.astype(F32), w_gate_up.astype(F32), B, S)

    outs, lses = [], []
    for g, (window, dilation) in enumerate(ATT_GROUPS):
        assert window // dilation == BLK and S % (dilation * BLK) == 0
        bias = _bias_table(rel_bias[:, g * HEADS_PER_GROUP:(g + 1) * HEADS_PER_GROUP], dilation)
        o, lse = _attention_group(att_in[g], bias, dilation, B, S)
        outs.append(o)
        lses.append(lse)

    causal = jnp.asarray(np.tril(np.ones((CHUNK, CHUNK), np.float32)))
    w_c = (w_spatial.astype(F32) * causal[None]).astype(BF16)
    wc2 = jnp.concatenate([w_c[0::2], w_c[1::2]], axis=2)
    bs = jnp.repeat(b_spatial.astype(F32).T, GMLP_GD, axis=1)
    h1, wd_bf = _mix(h, outs, lses, uv, gl, w_att_out.astype(F32), w_gmlp_out.astype(F32), w_out.astype(F32),
                     wc2, bs, row(ln_v_g), row(ln_v_b), w_down.astype(F32), S)

    wr_hi = w_router.astype(BF16)
    wr_lo = (w_router.astype(F32) - wr_hi.astype(F32)).astype(BF16)
    eidx, gate, rank, counts = _router(h1, row(g_moe), jnp.concatenate([wr_hi, wr_lo], axis=1).T,
                                       b_router.reshape(-1, 1).astype(F32))
    cnt = counts[:, 0].astype(jnp.int32)
    blk_counts = (cnt + TM_EXP - 1) // TM_EXP
    blk_end = jnp.cumsum(blk_counts)
    pad_start = (blk_end - blk_counts) * TM_EXP
    n_blocks = T * TOP_K // TM_EXP + N_EXPERTS
    n_valid = blk_end[-1:].astype(jnp.int32)
    blk = jnp.minimum(jnp.arange(n_blocks, dtype=jnp.int32), n_valid[0] - 1)
    block_expert = jnp.minimum(jnp.sum((blk_end[None, :] <= blk[:, None]).astype(jnp.int32), axis=1), N_EXPERTS - 1)
    expert_ids = jnp.arange(N_EXPERTS, dtype=jnp.int32)
    dest = rank + jnp.sum(jnp.where(eidx[..., None] == expert_ids, pad_start, 0), axis=-1)
    token_major = lambda a: jnp.transpose(a, (0, 2, 1)).reshape(T, TOP_K)
    dest3 = token_major(dest).reshape(T // TM_TOK, 1, TM_TOK * TOP_K)
    gate = token_major(gate)

    last_block = jnp.maximum(blk_end - 1, 0).astype(jnp.int32)
    xs = _dispatch(last_block, n_valid, dest3, h1, row(g_moe), n_blocks * TM_EXP)
    ys = _experts(block_expert, n_valid, xs, wgu_bf, b_gate_up.reshape(N_EXPERTS, 1, -1).astype(F32),
                  wd_bf, b_down.reshape(N_EXPERTS, 1, -1).astype(F32))
    return _combine(dest3, h1, gate, p_i, row(g_ple), w_ple_gate.astype(F32), w_ple_proj.astype(F32),
                    row(g_final), ys)


def kernel(x, p, g_mix, w_in, rel_bias, w_att_out, ln_v_g, ln_v_b, w_spatial, b_spatial, w_gmlp_out, w_out, g_moe, w_router, b_router, w_gate_up, b_gate_up, w_down, b_down, g_ple, w_ple_gate, w_ple_proj, g_final):
    B, S, D = x.shape
    depth = p.shape[0]
    assert depth == 1, "the final RMSNorm is fused into the (single) layer's last kernel"
    out = _layer(x.reshape(B * S, D), p[0].reshape(B * S, PLE_DIM), g_mix[0], w_in[0], rel_bias, w_att_out[0],
                 ln_v_g[0], ln_v_b[0], w_spatial[0], b_spatial[0], w_gmlp_out[0], w_out[0], g_moe[0], w_router[0],
                 b_router[0], w_gate_up[0], b_gate_up[0], w_down[0], b_down[0], g_ple[0], w_ple_gate[0],
                 w_ple_proj[0], g_final, B, S)
    return out.reshape(B, S, D)
```

```python
import functools

import jax
import jax.numpy as jnp
import numpy as np
from jax import lax
from jax.experimental import pallas as pl
from jax.experimental.pallas import tpu as pltpu

F32 = jnp.float32
BF16 = jnp.bfloat16

D_MODEL = 1024
HEAD_DIM = 64
ATT_GROUPS = ((128, 1), (512, 4), (2048, 16))
HEADS_PER_GROUP = 4
GROUP_W = HEADS_PER_GROUP * HEAD_DIM
N_DIL = len(ATT_GROUPS)
ATT_W = N_DIL * GROUP_W
BLK = 128
REL_BUCKETS = 32
REL_MAX_DIST = 2048
CHUNK = 128
GMLP_W = 768
GMLP_GD = 64
N_BRANCH = 2
IN_W = 3 * ATT_W + 2 * GMLP_W + N_BRANCH * D_MODEL
N_EXPERTS = 32
TOP_K = 4
D_EXPERT = D_MODEL
SWIGLU_LIMIT = 7.0
SWIGLU_ALPHA = 1.702
PLE_DIM = 256
EPS = 1e-6
MASKED = -1e30
LOG2E = float(np.log2(np.e))
LN2 = float(np.log(2.0))

QKV_G = 3 * GROUP_W

LANES = 128
ROW_SUBLANES = D_MODEL // LANES
assert ROW_SUBLANES == 8
MXU_N = 256
VMEM_LIMIT = 56 * 1024 * 1024

TM_PROJ = 512
TM_TOK = 256
TM_EXP = 512
assert TM_EXP % TM_TOK == 0


def _cparams(*sem):
    return pltpu.CompilerParams(dimension_semantics=sem, vmem_limit_bytes=VMEM_LIMIT)


def _resident(shape):
    nd = len(shape)
    return pl.BlockSpec(shape, lambda *_: (0,) * nd, pipeline_mode=pl.Buffered(1))


def _rms(x, g):
    return x * lax.rsqrt(jnp.mean(x * x, axis=-1, keepdims=True) + EPS) * g


def _load_weight_bf16(w_hbm, w_bf, stage, sem):
    rows = stage.shape[1]
    n_chunks = w_hbm.shape[0] // rows
    assert n_chunks * rows == w_hbm.shape[0] and stage.shape[2] == w_hbm.shape[1]

    def chunk(c):
        return pltpu.make_async_copy(w_hbm.at[pl.ds(c * rows, rows)], stage.at[c % 2], sem.at[c % 2])

    chunk(0).start()
    for c in range(n_chunks):
        if c + 1 < n_chunks:
            chunk(c + 1).start()
        chunk(c).wait()
        w_bf[c * rows:(c + 1) * rows, :] = stage[c % 2].astype(BF16)


def _expert_slice_spec(n_steps, width):
    per_expert = n_steps // N_EXPERTS
    assert per_expert * N_EXPERTS == n_steps and D_MODEL % per_expert == 0
    return pl.BlockSpec((1, D_MODEL // per_expert, width), lambda i: (i // per_expert, i % per_expert, 0))


def _inproj_kernel(x_ref, g_ref, w_hbm, we_ref, a1_ref, a2_ref, a3_ref, uv_ref, gl_ref, we_bf_ref,
                   scr, w_ref, w_stage, w_sem):
    @pl.when(pl.program_id(0) == 0)
    def _():
        _load_weight_bf16(w_hbm, w_ref, w_stage, w_sem)

    we_bf_ref[...] = we_ref[...].astype(BF16)

    tm = x_ref.shape[0]
    n = _rms(x_ref[...], g_ref[...]).astype(BF16)
    att_refs = (a1_ref, a2_ref, a3_ref)
    n_att, n_uv = 3 * ATT_W // MXU_N, 2 * GMLP_W // MXU_N
    for c in range(IN_W // MXU_N):
        z = jnp.dot(n, w_ref[:, c * MXU_N:(c + 1) * MXU_N], preferred_element_type=F32)
        if c < n_att:
            which, g = divmod(c, N_DIL)
            d = ATT_GROUPS[g][1]
            dst = att_refs[g]
            cols = slice(which * GROUP_W, (which + 1) * GROUP_W)
            if d == 1:
                dst[0, 0, :, cols] = z.astype(BF16)
                continue
            scr[0] = z[:, :LANES]
            scr[1] = z[:, LANES:]
            for r in range(d):
                zr = jnp.concatenate([scr[0, pl.ds(r, tm // d, stride=d), :],
                                      scr[1, pl.ds(r, tm // d, stride=d), :]], axis=1)
                dst[0, r, :, cols] = zr.astype(BF16)
        elif c < n_att + n_uv:
            uv_ref[:, (c - n_att) * MXU_N:(c - n_att + 1) * MXU_N] = z.astype(BF16)
        else:
            gl_ref[:, (c - n_att - n_uv) * MXU_N:(c - n_att - n_uv + 1) * MXU_N] = z.astype(BF16)


def _plane_spec(d, tm, tiles_per_seq, width):
    return pl.BlockSpec((1, d, tm // d, width), lambda i: (i // tiles_per_seq, 0, i % tiles_per_seq, 0))


W_STAGE_ROWS = 128


def _weight_stage(width):
    return [pltpu.VMEM((2, W_STAGE_ROWS, width), F32), pltpu.SemaphoreType.DMA((2,))]


def _in_proj(x2, g, w, w_expert, B, S):
    T = x2.shape[0]
    tm = TM_PROJ
    row = lambda w: pl.BlockSpec((tm, w), lambda i: (i, 0))
    dils = [d for _, d in ATT_GROUPS]
    we_spec = _expert_slice_spec(T // tm, w_expert.shape[2])
    return pl.pallas_call(
        _inproj_kernel,
        grid=(T // tm,),
        in_specs=[row(D_MODEL), _resident((1, D_MODEL)), pl.BlockSpec(memory_space=pl.ANY), we_spec],
        out_specs=[_plane_spec(d, tm, S // tm, QKV_G) for d in dils] + [row(2 * GMLP_W), row(N_BRANCH * D_MODEL), we_spec],
        out_shape=[jax.ShapeDtypeStruct((B, d, S // d, QKV_G), BF16) for d in dils]
                  + [jax.ShapeDtypeStruct((T, 2 * GMLP_W), BF16),
                     jax.ShapeDtypeStruct((T, N_BRANCH * D_MODEL), BF16),
                     jax.ShapeDtypeStruct(w_expert.shape, BF16)],
        scratch_shapes=[pltpu.VMEM((2, tm, LANES), F32), pltpu.VMEM((D_MODEL, IN_W), BF16)] + _weight_stage(IN_W),
        compiler_params=_cparams("arbitrary"),
        name="in_proj",
    )(x2, g, w, w_expert)


def _t5_bucket(n):
    exact = REL_BUCKETS // 2
    nf = np.maximum(n, 1).astype(np.float32)
    large = exact + (np.log(nf / exact) / np.log(REL_MAX_DIST / exact) * (REL_BUCKETS - exact)).astype(np.int32)
    large = np.minimum(large, REL_BUCKETS - 1)
    return np.where(n < exact, n, large).astype(np.int32)


def _bias_table(rel_bias_g, dilation):
    n = 3 * BLK
    dist = 2 * BLK - 1 - np.arange(n)
    valid = (dist >= 0) & (dist <= BLK)
    bucket = _t5_bucket(np.clip(dist, 0, BLK) * dilation)
    c = jnp.where(jnp.asarray(valid)[None, :], rel_bias_g.astype(F32)[bucket].T * LOG2E, MASKED)
    shifted = jnp.tile(c, (1, BLK))[:, :BLK * (n - 1)].reshape(HEADS_PER_GROUP, BLK, n - 1)
    return shifted[:, :, BLK - 1:].reshape(HEADS_PER_GROUP * BLK, 2 * BLK)


def _attn_kernel(cur_ref, prev_ref, bias_ref, o_ref, lse_ref):
    rg, rb = cur_ref.shape[1], cur_ref.shape[2] // BLK
    starts_sequence = pl.program_id(2) == 0
    lane_head = lax.broadcasted_iota(jnp.int32, (1, GROUP_W), 1) // HEAD_DIM
    scale = HEAD_DIM ** -0.5
    head_bf = [jnp.where(lane_head == h, scale, 0.0).astype(BF16) for h in range(HEADS_PER_GROUP)]
    key_is_prev = lax.broadcasted_iota(jnp.int32, (1, 2 * BLK), 1) < BLK
    nt = (((1,), (1,)), ((), ()))
    qc, kc_, vc_ = slice(0, GROUP_W), slice(GROUP_W, 2 * GROUP_W), slice(2 * GROUP_W, 3 * GROUP_W)

    def by_head(x):
        sel = x[(HEADS_PER_GROUP - 1) * BLK:]
        for h in range(HEADS_PER_GROUP - 2, -1, -1):
            sel = jnp.where(lane_head == h, x[h * BLK:(h + 1) * BLK], sel)
        return sel

    for r, j in [(r, j) for r in range(rg) for j in range(rb)]:
        rows = slice(j * BLK, (j + 1) * BLK)
        prev = prev_ref if j == 0 else cur_ref
        prows = slice(0, BLK) if j == 0 else slice((j - 1) * BLK, j * BLK)
        q = cur_ref[0, r, rows, qc]
        k = jnp.concatenate([prev[0, r, prows, kc_], cur_ref[0, r, rows, kc_]], axis=0)
        v = jnp.concatenate([prev[0, r, prows, vc_], cur_ref[0, r, rows, vc_]], axis=0)
        q_bd = jnp.concatenate([q * head_bf[h] for h in range(HEADS_PER_GROUP)], axis=0)
        s = lax.dot_general(q_bd, k, nt, preferred_element_type=F32) * LOG2E + bias_ref[...]
        if j == 0:
            s = jnp.where(jnp.logical_and(starts_sequence, key_is_prev), MASKED, s)
        m = jnp.max(s, axis=-1, keepdims=True)
        p = jnp.exp2(s - m)
        den = jnp.sum(p, axis=-1, keepdims=True)
        o = jnp.dot(p.astype(BF16), v, preferred_element_type=F32)
        o_ref[0, r, rows, :] = (by_head(o) * by_head(1.0 / den)).astype(BF16)
        lse_ref[0, r, rows, :] = jnp.broadcast_to(by_head(m * LN2 + jnp.log(den)), (BLK, GROUP_W))


ATT_SUBBLOCKS = 8


def _attention_group(a, bias, dilation, B, S):
    sd = S // dilation
    rb = min(ATT_SUBBLOCKS, sd // BLK)
    rg = min(ATT_SUBBLOCKS // rb, dilation)
    o, lse = pl.pallas_call(
        _attn_kernel,
        grid=(B, dilation // rg, sd // (rb * BLK)),
        in_specs=[pl.BlockSpec((1, rg, rb * BLK, QKV_G), lambda b, r, n: (b, r, n, 0)),
                  pl.BlockSpec((1, rg, BLK, QKV_G), lambda b, r, n: (b, r, jnp.maximum(n * rb - 1, 0), 0)),
                  _resident((HEADS_PER_GROUP * BLK, 2 * BLK))],
        out_specs=[pl.BlockSpec((1, rg, rb * BLK, GROUP_W), lambda b, r, n: (b, r, n, 0))] * 2,
        out_shape=[jax.ShapeDtypeStruct((B, dilation, sd, GROUP_W), BF16),
                   jax.ShapeDtypeStruct((B, dilation, sd, GROUP_W), F32)],
        compiler_params=_cparams("parallel", "parallel", "parallel"),
        name=f"attn_d{dilation}",
    )(a, a, bias)
    return o, lse


def _gelu(x):
    return x * (lax.erf(x * (2.0 ** -0.5)) + 1.0) * 0.5


def _token_major(src_ref, d, scr, slot, tm):
    if d == 1:
        return src_ref[0, 0].astype(F32)
    for r in range(d):
        piece = src_ref[0, r].astype(F32)
        scr[slot, pl.ds(r, tm // d, stride=d), :] = piece[:, :LANES]
        scr[slot + 1, pl.ds(r, tm // d, stride=d), :] = piece[:, LANES:]
    return jnp.concatenate([scr[slot], scr[slot + 1]], axis=1)


def _mix_kernel(x_ref, o1_ref, o2_ref, o3_ref, l1_ref, l2_ref, l3_ref, uv_ref, gl_ref,
                wa_hbm, wg_hbm, wo_hbm, wc_ref, bs_ref, lng_ref, lnb_ref, we_ref, h_ref, we_bf_ref, g_scr, t_scr,
                wa_ref, wg_ref, wo_ref, w_stage, w_sem):
    @pl.when(pl.program_id(0) == 0)
    def _():
        for w_hbm, w_bf in ((wa_hbm, wa_ref), (wg_hbm, wg_ref), (wo_hbm, wo_ref)):
            _load_weight_bf16(w_hbm, w_bf, w_stage, w_sem)

    we_bf_ref[...] = we_ref[...].astype(BF16)

    tm = x_ref.shape[0]
    dils = [d for _, d in ATT_GROUPS]
    o1, o2, o3 = [_token_major(ref, d, t_scr, 4 * i, tm) for i, (ref, d) in enumerate(zip((o1_ref, o2_ref, o3_ref), dils))]
    l1, l2, l3 = [_token_major(ref, d, t_scr, 4 * i + 2, tm) for i, (ref, d) in enumerate(zip((l1_ref, l2_ref, l3_ref), dils))]
    lm = jnp.maximum(jnp.maximum(l1, l2), l3)
    e1, e2, e3 = jnp.exp(l1 - lm), jnp.exp(l2 - lm), jnp.exp(l3 - lm)
    att = (e1 * o1 + e2 * o2 + e3 * o3) / (e1 + e2 + e3)
    y_att = jnp.dot(att.astype(BF16), wa_ref[...], preferred_element_type=F32)

    zu = _gelu(uv_ref[:, :GMLP_W].astype(F32))
    zv = _gelu(uv_ref[:, GMLP_W:].astype(F32))
    mu = jnp.mean(zv, axis=-1, keepdims=True)
    var = jnp.mean(jnp.square(zv - mu), axis=-1, keepdims=True)
    vn = (zv - mu) * lax.rsqrt(var + EPS) * lng_ref[...] + lnb_ref[...]
    low_half = lax.broadcasted_iota(jnp.int32, (CHUNK, 2 * GMLP_GD), 1) < GMLP_GD
    for c in range(tm // CHUNK):
        rows = slice(c * CHUNK, (c + 1) * CHUNK)
        for s in range(GMLP_W // (2 * GMLP_GD)):
            cols = slice(s * 2 * GMLP_GD, (s + 1) * 2 * GMLP_GD)
            v2 = vn[rows, cols]
            rhs = jnp.concatenate([jnp.where(low_half, v2, 0.0), jnp.where(low_half, 0.0, v2)], axis=0).astype(BF16)
            mixed = jnp.dot(wc_ref[s], rhs, preferred_element_type=F32) + bs_ref[:, cols]
            g_scr[rows, cols] = (zu[rows, cols] * mixed).astype(BF16)
    y_gm = jnp.dot(g_scr[...], wg_ref[...], preferred_element_type=F32)

    gate_a = jax.nn.sigmoid(gl_ref[:, :D_MODEL].astype(F32))
    gate_g = jax.nn.sigmoid(gl_ref[:, D_MODEL:].astype(F32))
    merged = (gate_a * y_att + gate_g * y_gm).astype(BF16)
    h_ref[...] = x_ref[...] + jnp.dot(merged, wo_ref[...], preferred_element_type=F32)


def _mix(x2, outs, lses, uv, gl, wa, wg, wo, wc2, bs, lng, lnb, w_expert, S):
    T = x2.shape[0]
    tm = TM_PROJ
    row = lambda w: pl.BlockSpec((tm, w), lambda i: (i, 0))
    att = [_plane_spec(d, tm, S // tm, GROUP_W) for _, d in ATT_GROUPS]
    we_spec = _expert_slice_spec(T // tm, w_expert.shape[2])
    return pl.pallas_call(
        _mix_kernel,
        grid=(T // tm,),
        in_specs=[row(D_MODEL)] + att + att + [row(2 * GMLP_W), row(N_BRANCH * D_MODEL)]
                 + [pl.BlockSpec(memory_space=pl.ANY)] * 3
                 + [_resident(wc2.shape), _resident(bs.shape), _resident(lng.shape), _resident(lnb.shape), we_spec],
        out_specs=[row(D_MODEL), we_spec],
        out_shape=[jax.ShapeDtypeStruct((T, D_MODEL), F32), jax.ShapeDtypeStruct(w_expert.shape, BF16)],
        scratch_shapes=[pltpu.VMEM((tm, GMLP_W), BF16), pltpu.VMEM((4 * N_DIL, tm, LANES), F32),
                        pltpu.VMEM(wa.shape, BF16), pltpu.VMEM(wg.shape, BF16), pltpu.VMEM(wo.shape, BF16)]
                       + _weight_stage(D_MODEL),
        compiler_params=_cparams("arbitrary"),
        name="mix",
    )(x2, *outs, *lses, uv, gl, wa, wg, wo, wc2, bs, lng, lnb, w_expert)


def _router_kernel(h_ref, g_ref, wr_ref, br_ref, upper_ref, eidx_ref, gate_ref, rank_ref, cnt_ref, carry):
    tm = h_ref.shape[0]

    @pl.when(pl.program_id(0) == 0)
    def _():
        carry[...] = jnp.zeros_like(carry)

    hn = _rms(h_ref[...], g_ref[...])
    hi = hn.astype(BF16)
    lo = (hn - hi.astype(F32)).astype(BF16)
    nt = (((1,), (1,)), ((), ()))
    by_hi = lax.dot_general(wr_ref[...], hi, nt, preferred_element_type=F32)
    by_lo = lax.dot_general(wr_ref[:N_EXPERTS, :], lo, nt, preferred_element_type=F32)
    logits = by_hi[:N_EXPERTS] + by_hi[N_EXPERTS:] + by_lo + br_ref[...]
    expert = lax.broadcasted_iota(jnp.int32, (N_EXPERTS, tm), 0)
    vals, hots = [], []
    l = logits
    for k in range(TOP_K):
        m = jnp.max(l, axis=0, keepdims=True)
        idx = jnp.min(jnp.where(l == m, expert, N_EXPERTS), axis=0, keepdims=True)
        hot = expert == idx
        eidx_ref[0, k:k + 1, :] = idx
        vals.append(m)
        hots.append(hot)
        l = jnp.where(hot, -jnp.inf, l)
    ex = [jnp.exp(v - vals[0]) for v in vals]
    tot = ex[0] + ex[1] + ex[2] + ex[3]
    for k in range(TOP_K):
        gate_ref[0, k:k + 1, :] = ex[k] / tot
    multi = jnp.zeros((N_EXPERTS, tm), F32)
    for hot in hots:
        multi = multi + hot.astype(F32)
    before = jnp.dot(multi.astype(BF16), upper_ref[...], preferred_element_type=F32) + carry[...]
    for k in range(TOP_K):
        rank_ref[0, k:k + 1, :] = jnp.sum(jnp.where(hots[k], before, 0.0), axis=0, keepdims=True).astype(jnp.int32)
    carry[...] += jnp.sum(multi, axis=1, keepdims=True)
    cnt_ref[...] = carry[...]


def _router(h1, g, wr_t, br_col):
    T = h1.shape[0]
    tm = TM_PROJ
    upper = jnp.asarray(np.triu(np.ones((tm, tm), np.float32), k=1), BF16)
    k_rows = pl.BlockSpec((1, TOP_K, tm), lambda i: (i, 0, 0))
    k_shape = lambda dt: jax.ShapeDtypeStruct((T // tm, TOP_K, tm), dt)
    return pl.pallas_call(
        _router_kernel,
        grid=(T // tm,),
        in_specs=[pl.BlockSpec((tm, D_MODEL), lambda i: (i, 0)), _resident((1, D_MODEL)),
                  _resident((2 * N_EXPERTS, D_MODEL)), _resident((N_EXPERTS, 1)), _resident((tm, tm))],
        out_specs=[k_rows, k_rows, k_rows, pl.BlockSpec((N_EXPERTS, 1), lambda i: (0, 0))],
        out_shape=[k_shape(jnp.int32), k_shape(F32), k_shape(jnp.int32),
                   jax.ShapeDtypeStruct((N_EXPERTS, 1), F32)],
        scratch_shapes=[pltpu.VMEM((N_EXPERTS, 1), F32)],
        compiler_params=_cparams("arbitrary"),
        name="router",
    )(h1, g, wr_t, br_col, upper)


def _to_row_tiles(ref, lead, value):
    n = value.shape[0]
    for c in range(ROW_SUBLANES):
        ref[(*lead, pl.ds(c, n, stride=ROW_SUBLANES), slice(None))] = value[:, c * LANES:(c + 1) * LANES]


def _from_row_tiles(ref, lead, first, n):
    return jnp.concatenate(
        [ref[(*lead, pl.ds(first * ROW_SUBLANES + c, n, stride=ROW_SUBLANES), slice(None))] for c in range(ROW_SUBLANES)],
        axis=1)


def _tile_rows(idx, n=1):
    return pl.ds(pl.multiple_of(idx * ROW_SUBLANES, ROW_SUBLANES), n * ROW_SUBLANES)


def _dispatch_kernel(last_ref, nv_ref, dest_ref, h_ref, g_ref, xs_ref, buf, sem, zero_sem):
    tm = h_ref.shape[0]
    n_blocks = xs_ref.shape[0] // (TM_EXP * ROW_SUBLANES)
    i = pl.program_id(0)
    slot = lax.rem(i, 2)

    @pl.when(i == 0)
    def _():
        buf[1] = jnp.zeros(buf.shape[1:], F32)

        def zero_block(b):
            for part in range(TM_EXP // tm):
                pltpu.make_async_copy(buf.at[1], xs_ref.at[_tile_rows(b * TM_EXP + part * tm, tm)], zero_sem).start()

        def zero_done():
            for part in range(TM_EXP // tm):
                pltpu.make_async_copy(buf.at[1], xs_ref.at[_tile_rows(0, tm)], zero_sem).wait()

        for e in range(N_EXPERTS):
            zero_block(last_ref[e])
        lax.fori_loop(nv_ref[0], n_blocks, lambda b, c: (zero_block(b), c)[1], 0)
        for e in range(N_EXPERTS):
            zero_done()
        lax.fori_loop(nv_ref[0], n_blocks, lambda b, c: (zero_done(), c)[1], 0)

    _to_row_tiles(buf, (slot,), _rms(h_ref[...], g_ref[...]))

    def issue(t, carry):
        for k in range(TOP_K):
            d = dest_ref[0, 0, t * TOP_K + k]
            pltpu.make_async_copy(buf.at[slot, _tile_rows(t)], xs_ref.at[_tile_rows(d)],
                                  sem.at[slot]).start(priority=k % 2)
        return carry

    lax.fori_loop(0, tm, issue, 0, unroll=8)

    def wait_slot(s):
        for _ in range(TOP_K):
            pltpu.make_async_copy(buf.at[s], xs_ref.at[_tile_rows(0, tm)], sem.at[s]).wait()

    @pl.when(i > 0)
    def _():
        wait_slot(1 - slot)

    @pl.when(i == pl.num_programs(0) - 1)
    def _():
        wait_slot(slot)


def _dispatch(last_block, n_valid, dest3, h1, g, n_slots):
    T = h1.shape[0]
    tm = TM_TOK
    grid_spec = pltpu.PrefetchScalarGridSpec(
        num_scalar_prefetch=2,
        grid=(T // tm,),
        in_specs=[pl.BlockSpec((1, 1, TOP_K * tm), lambda i, lb, nv: (i, 0, 0), memory_space=pltpu.SMEM),
                  pl.BlockSpec((tm, D_MODEL), lambda i, lb, nv: (i, 0)),
                  pl.BlockSpec((1, D_MODEL), lambda i, lb, nv: (0, 0), pipeline_mode=pl.Buffered(1))],
        out_specs=pl.BlockSpec(memory_space=pl.ANY),
        scratch_shapes=[pltpu.VMEM((2, tm * ROW_SUBLANES, LANES), F32), pltpu.SemaphoreType.DMA((2,)),
                        pltpu.SemaphoreType.DMA(())],
    )
    return pl.pallas_call(
        _dispatch_kernel,
        grid_spec=grid_spec,
        out_shape=jax.ShapeDtypeStruct((n_slots * ROW_SUBLANES, LANES), F32),
        compiler_params=_cparams("arbitrary"),
        name="dispatch",
    )(last_block, n_valid, dest3, h1, g)


def _experts_kernel(be_ref, nv_ref, xs_hbm, wgu_ref, bgu_ref, wd_ref, bd_ref, ys_hbm, xbuf, ybuf, xsem, ysem):
    del be_ref
    tm = xbuf.shape[1]
    b = pl.program_id(0)
    n_valid = nv_ref[0]
    slot = lax.rem(b, 2)

    def x_copies(blk, s):
        return [pltpu.make_async_copy(xs_hbm.at[pl.ds(blk * tm, tm), c, :],
                                      xbuf.at[s, :, pl.ds(c * LANES, LANES)], xsem.at[s]) for c in range(ROW_SUBLANES)]

    def y_copies(blk, s):
        return [pltpu.make_async_copy(ybuf.at[s, :, pl.ds(c * LANES, LANES)],
                                      ys_hbm.at[pl.ds(blk * tm, tm), c, :], ysem.at[s]) for c in range(ROW_SUBLANES)]

    @pl.when(jnp.logical_and(b == 0, n_valid > 0))
    def _():
        for cp in x_copies(0, 0):
            cp.start()

    @pl.when(b + 1 < n_valid)
    def _():
        for cp in x_copies(b + 1, 1 - slot):
            cp.start()

    @pl.when(b < n_valid)
    def _():
        for cp in x_copies(b, slot):
            cp.wait()
        x = xbuf[slot].astype(BF16)
        gu = jnp.dot(x, wgu_ref[0], preferred_element_type=F32) + bgu_ref[0]
        glu = jnp.minimum(gu[:, :D_EXPERT], SWIGLU_LIMIT)
        lin = jnp.clip(gu[:, D_EXPERT:], -SWIGLU_LIMIT, SWIGLU_LIMIT)
        act = glu * jax.nn.sigmoid(SWIGLU_ALPHA * glu) * (lin + 1.0)
        ybuf[slot] = jnp.dot(act.astype(BF16), wd_ref[0], preferred_element_type=F32) + bd_ref[0]

    @pl.when(b >= n_valid)
    def _():
        ybuf[slot] = jnp.zeros(ybuf.shape[1:], F32)

    for cp in y_copies(b, slot):
        cp.start()

    @pl.when(b > 0)
    def _():
        for cp in y_copies(b - 1, 1 - slot):
            cp.wait()

    @pl.when(b == pl.num_programs(0) - 1)
    def _():
        for cp in y_copies(b, slot):
            cp.wait()


def _experts(block_expert, n_valid, xs, wgu, bgu, wd, bd):
    tm = TM_EXP
    n_slots = xs.shape[0] // ROW_SUBLANES
    n_blocks = n_slots // tm
    any_space = pl.BlockSpec(memory_space=pl.ANY)
    grid_spec = pltpu.PrefetchScalarGridSpec(
        num_scalar_prefetch=2,
        grid=(n_blocks,),
        in_specs=[any_space,
                  pl.BlockSpec((1, D_MODEL, 2 * D_EXPERT), lambda b, be, nv: (be[b], 0, 0)),
                  pl.BlockSpec((1, 1, 2 * D_EXPERT), lambda b, be, nv: (be[b], 0, 0)),
                  pl.BlockSpec((1, D_EXPERT, D_MODEL), lambda b, be, nv: (be[b], 0, 0)),
                  pl.BlockSpec((1, 1, D_MODEL), lambda b, be, nv: (be[b], 0, 0))],
        out_specs=any_space,
        scratch_shapes=[pltpu.VMEM((2, tm, D_MODEL), F32), pltpu.VMEM((2, tm, D_MODEL), F32),
                        pltpu.SemaphoreType.DMA((2,)), pltpu.SemaphoreType.DMA((2,))],
    )
    ys = pl.pallas_call(
        _experts_kernel,
        grid_spec=grid_spec,
        out_shape=jax.ShapeDtypeStruct((n_slots, ROW_SUBLANES, LANES), F32),
        compiler_params=_cparams("arbitrary"),
        name="experts",
    )(block_expert, n_valid, xs.reshape(n_slots, ROW_SUBLANES, LANES), wgu, bgu, wd, bd)
    return ys.reshape(xs.shape)


GATHER_SLOTS = 3


def _combine_kernel(*refs, n_tiles):
    ahead = GATHER_SLOTS - 1
    prime_refs, refs = refs[:ahead], refs[ahead:]
    (ahead_dest_ref, h_ref, gate_ref, p_ref, gp_ref, gf_ref, wpg_hbm, wpp_hbm, ys_ref, o_ref), refs = refs[:10], refs[10:]
    bufs, (sem, wpg_ref, wpp_ref, w_stage, w_sem) = refs[:GATHER_SLOTS], refs[GATHER_SLOTS:]
    tm = h_ref.shape[0]
    i = pl.program_id(0)

    def row_copy(dref, t, k, s):
        d = dref[0, 0, t * TOP_K + k]
        return pltpu.make_async_copy(ys_ref.at[_tile_rows(d)], bufs[s].at[_tile_rows(k * tm + t)], sem.at[s])

    def wait_slot(s):
        for _ in range(TOP_K):
            pltpu.make_async_copy(ys_ref.at[_tile_rows(0, tm)], bufs[s].at[_tile_rows(0, tm)], sem.at[s]).wait()

    @pl.when(i == 0)
    def _():
        _load_weight_bf16(wpg_hbm, wpg_ref, w_stage, w_sem)
        _load_weight_bf16(wpp_hbm, wpp_ref, w_stage, w_sem)

        for s, dref in enumerate(prime_refs):
            def issue(t, carry, s=s, dref=dref):
                for k in range(TOP_K):
                    row_copy(dref, t, k, s).start(priority=k % 2)
                return carry
            lax.fori_loop(0, tm, issue, 0, unroll=8)

    def step(s):
        wait_slot(s)
        for t in range(tm):
            for k in range(TOP_K):
                row_copy(ahead_dest_ref, t, k, (s + ahead) % GATHER_SLOTS).start(priority=k % 2)
        proj = jnp.dot(p_ref[...].astype(BF16), wpp_ref[...], preferred_element_type=F32)
        h = h_ref[...]
        for k in range(TOP_K):
            h = h + gate_ref[:, k:k + 1] * _from_row_tiles(bufs[s], (), k * tm, tm)
        ple_gate = jax.nn.sigmoid(jnp.dot(_rms(h, gp_ref[...]).astype(BF16), wpg_ref[...], preferred_element_type=F32))
        h = h + ple_gate * proj
        o_ref[...] = _rms(h, gf_ref[...])

    for s in range(GATHER_SLOTS):
        pl.when(lax.rem(i, GATHER_SLOTS) == s)(functools.partial(step, s))

    @pl.when(i == n_tiles - 1)
    def _():
        for extra in range(ahead):
            wait_slot((n_tiles + extra) % GATHER_SLOTS)


def _combine(dest3, h1, gate, p2, gp, wpg, wpp, gf, ys):
    T = h1.shape[0]
    tm = TM_TOK
    n_tiles = T // tm
    row = lambda w: pl.BlockSpec((tm, w), lambda i: (i, 0))
    dest_spec = lambda ahead: pl.BlockSpec((1, 1, TOP_K * tm), lambda i: (jnp.minimum(i + ahead, n_tiles - 1), 0, 0),
                                           memory_space=pltpu.SMEM)
    gather_buf = pltpu.VMEM((TOP_K * tm * ROW_SUBLANES, LANES), F32)
    ahead = GATHER_SLOTS - 1
    assert n_tiles > ahead
    return pl.pallas_call(
        functools.partial(_combine_kernel, n_tiles=n_tiles),
        grid=(n_tiles,),
        in_specs=[dest_spec(a) for a in range(ahead)] + [dest_spec(ahead),
                  row(D_MODEL), row(TOP_K), row(PLE_DIM), _resident((1, D_MODEL)), _resident((1, D_MODEL))]
                 + [pl.BlockSpec(memory_space=pl.ANY)] * 3,
        out_specs=row(D_MODEL),
        out_shape=jax.ShapeDtypeStruct((T, D_MODEL), F32),
        scratch_shapes=[gather_buf] * GATHER_SLOTS + [pltpu.SemaphoreType.DMA((GATHER_SLOTS,)),
                        pltpu.VMEM(wpg.shape, BF16), pltpu.VMEM(wpp.shape, BF16)] + _weight_stage(D_MODEL),
        compiler_params=_cparams("arbitrary"),
        name="combine",
    )(*([dest3] * (ahead + 1)), h1, gate, p2, gp, gf, wpg, wpp, ys)


def _layer(h, p_i, g_mix, w_in, rel_bias, w_att_out, ln_v_g, ln_v_b, w_spatial, b_spatial, w_gmlp_out, w_out,
           g_moe, w_router, b_router, w_gate_up, b_gate_up, w_down, b_down, g_ple, w_ple_gate, w_ple_proj,
           g_final, B, S):
    T = B * S
    row = lambda v: v.reshape(1, -1).astype(F32)

    assert S % TM_PROJ == 0
    *att_in, uv, gl, wgu_bf = _in_proj(h, row(g_mix), w_in.astype(F32), w_gate_up.astype(F32), B, S)

    outs, lses = [], []
    for g, (window, dilation) in enumerate(ATT_GROUPS):
        assert window // dilation == BLK and S % (dilation * BLK) == 0
        bias = _bias_table(rel_bias[:, g * HEADS_PER_GROUP:(g + 1) * HEADS_PER_GROUP], dilation)
        o, lse = _attention_group(att_in[g], bias, dilation, B, S)
        outs.append(o)
        lses.append(lse)

    causal = jnp.asarray(np.tril(np.ones((CHUNK, CHUNK), np.float32)))
    w_c = (w_spatial.astype(F32) * causal[None]).astype(BF16)
    wc2 = jnp.concatenate([w_c[0::2], w_c[1::2]], axis=2)
    bs = jnp.repeat(b_spatial.astype(F32).T, GMLP_GD, axis=1)
    h1, wd_bf = _mix(h, outs, lses, uv, gl, w_att_out.astype(F32), w_gmlp_out.astype(F32), w_out.astype(F32),
                     wc2, bs, row(ln_v_g), row(ln_v_b), w_down.astype(F32), S)

    wr_hi = w_router.astype(BF16)
    wr_lo = (w_router.astype(F32) - wr_hi.astype(F32)).astype(BF16)
    eidx, gate, rank, counts = _router(h1, row(g_moe), jnp.concatenate([wr_hi, wr_lo], axis=1).T,
                                       b_router.reshape(-1, 1).astype(F32))
    cnt = counts[:, 0].astype(jnp.int32)
    blk_counts = (cnt + TM_EXP - 1) // TM_EXP
    blk_end = jnp.cumsum(blk_counts)
    pad_start = (blk_end - blk_counts) * TM_EXP
    n_blocks = T * TOP_K // TM_EXP + N_EXPERTS
    n_valid = blk_end[-1:].astype(jnp.int32)
    blk = jnp.minimum(jnp.arange(n_blocks, dtype=jnp.int32), n_valid[0] - 1)
    block_expert = jnp.minimum(jnp.sum((blk_end[None, :] <= blk[:, None]).astype(jnp.int32), axis=1), N_EXPERTS - 1)
    expert_ids = jnp.arange(N_EXPERTS, dtype=jnp.int32)
    dest = rank + jnp.sum(jnp.where(eidx[..., None] == expert_ids, pad_start, 0), axis=-1)
    token_major = lambda a: jnp.transpose(a, (0, 2, 1)).reshape(T, TOP_K)
    dest3 = token_major(dest).reshape(T // TM_TOK, 1, TM_TOK * TOP_K)
    gate = token_major(gate)

    last_block = jnp.maximum(blk_end - 1, 0).astype(jnp.int32)
    xs = _dispatch(last_block, n_valid, dest3, h1, row(g_moe), n_blocks * TM_EXP)
    ys = _experts(block_expert, n_valid, xs, wgu_bf, b_gate_up.reshape(N_EXPERTS, 1, -1).astype(F32),
                  wd_bf, b_down.reshape(N_EXPERTS, 1, -1).astype(F32))
    return _combine(dest3, h1, gate, p_i, row(g_ple), w_ple_gate.astype(F32), w_ple_proj.astype(F32),
                    row(g_final), ys)


def kernel(x, p, g_mix, w_in, rel_bias, w_att_out, ln_v_g, ln_v_b, w_spatial, b_spatial, w_gmlp_out, w_out, g_moe, w_router, b_router, w_gate_up, b_gate_up, w_down, b_down, g_ple, w_ple_gate, w_ple_proj, g_final):
    B, S, D = x.shape
    depth = p.shape[0]
    assert depth == 1, "the final RMSNorm is fused into the (single) layer's last kernel"
    out = _layer(x.reshape(B * S, D), p[0].reshape(B * S, PLE_DIM), g_mix[0], w_in[0], rel_bias, w_att_out[0],
                 ln_v_g[0], ln_v_b[0], w_spatial[0], b_spatial[0], w_gmlp_out[0], w_out[0], g_moe[0], w_router[0],
                 b_router[0], w_gate_up[0], b_gate_up[0], w_down[0], b_down[0], g_ple[0], w_ple_gate[0],
                 w_ple_proj[0], g_final, B, S)
    return out.reshape(B, S, D)
```

```python
import functools

import jax
import jax.numpy as jnp
import numpy as np
from jax import lax
from jax.experimental import pallas as pl
from jax.experimental.pallas import tpu as pltpu

F32 = jnp.float32
BF16 = jnp.bfloat16

D_MODEL = 1024
HEAD_DIM = 64
ATT_GROUPS = ((128, 1), (512, 4), (2048, 16))
HEADS_PER_GROUP = 4
GROUP_W = HEADS_PER_GROUP * HEAD_DIM
N_DIL = len(ATT_GROUPS)
ATT_W = N_DIL * GROUP_W
BLK = 128
REL_BUCKETS = 32
REL_MAX_DIST = 2048
CHUNK = 128
GMLP_W = 768
GMLP_GD = 64
N_BRANCH = 2
IN_W = 3 * ATT_W + 2 * GMLP_W + N_BRANCH * D_MODEL
N_EXPERTS = 32
TOP_K = 4
D_EXPERT = D_MODEL
SWIGLU_LIMIT = 7.0
SWIGLU_ALPHA = 1.702
PLE_DIM = 256
EPS = 1e-6
MASKED = -1e30
LOG2E = float(np.log2(np.e))
LN2 = float(np.log(2.0))

QKV_G = 3 * GROUP_W

LANES = 128
ROW_SUBLANES = D_MODEL // LANES
assert ROW_SUBLANES == 8
MXU_N = 256
VMEM_LIMIT = 56 * 1024 * 1024

TM_PROJ = 512
TM_TOK = 256
TM_EXP = 512
assert TM_EXP % TM_TOK == 0


def _cparams(*sem):
    return pltpu.CompilerParams(dimension_semantics=sem, vmem_limit_bytes=VMEM_LIMIT)


def _resident(shape):
    nd = len(shape)
    return pl.BlockSpec(shape, lambda *_: (0,) * nd, pipeline_mode=pl.Buffered(1))


def _rms(x, g):
    return x * lax.rsqrt(jnp.mean(x * x, axis=-1, keepdims=True) + EPS) * g


def _load_weight_bf16(w_hbm, w_bf, stage, sem):
    rows = stage.shape[1]
    n_chunks = w_hbm.shape[0] // rows
    assert n_chunks * rows == w_hbm.shape[0] and stage.shape[2] == w_hbm.shape[1]

    def chunk(c):
        return pltpu.make_async_copy(w_hbm.at[pl.ds(c * rows, rows)], stage.at[c % 2], sem.at[c % 2])

    chunk(0).start()
    for c in range(n_chunks):
        if c + 1 < n_chunks:
            chunk(c + 1).start()
        chunk(c).wait()
        w_bf[c * rows:(c + 1) * rows, :] = stage[c % 2].astype(BF16)


def _expert_slice_spec(n_steps, width):
    per_expert = n_steps // N_EXPERTS
    assert per_expert * N_EXPERTS == n_steps and D_MODEL % per_expert == 0
    return pl.BlockSpec((1, D_MODEL // per_expert, width), lambda i: (i // per_expert, i % per_expert, 0))


def _inproj_kernel(x_ref, g_ref, w_hbm, we_ref, a1_ref, a2_ref, a3_ref, uv_ref, gl_ref, we_bf_ref,
                   scr, w_ref, w_stage, w_sem):
    @pl.when(pl.program_id(0) == 0)
    def _():
        _load_weight_bf16(w_hbm, w_ref, w_stage, w_sem)

    we_bf_ref[...] = we_ref[...].astype(BF16)

    tm = x_ref.shape[0]
    n = _rms(x_ref[...], g_ref[...]).astype(BF16)
    att_refs = (a1_ref, a2_ref, a3_ref)
    n_att, n_uv = 3 * ATT_W // MXU_N, 2 * GMLP_W // MXU_N
    for c in range(IN_W // MXU_N):
        z = jnp.dot(n, w_ref[:, c * MXU_N:(c + 1) * MXU_N], preferred_element_type=F32)
        if c < n_att:
            which, g = divmod(c, N_DIL)
            d = ATT_GROUPS[g][1]
            dst = att_refs[g]
            cols = slice(which * GROUP_W, (which + 1) * GROUP_W)
            if d == 1:
                dst[0, 0, :, cols] = z.astype(BF16)
                continue
            scr[0] = z[:, :LANES]
            scr[1] = z[:, LANES:]
            for r in range(d):
                zr = jnp.concatenate([scr[0, pl.ds(r, tm // d, stride=d), :],
                                      scr[1, pl.ds(r, tm // d, stride=d), :]], axis=1)
                dst[0, r, :, cols] = zr.astype(BF16)
        elif c < n_att + n_uv:
            uv_ref[:, (c - n_att) * MXU_N:(c - n_att + 1) * MXU_N] = z.astype(BF16)
        else:
            gl_ref[:, (c - n_att - n_uv) * MXU_N:(c - n_att - n_uv + 1) * MXU_N] = z.astype(BF16)


def _plane_spec(d, tm, tiles_per_seq, width):
    return pl.BlockSpec((1, d, tm // d, width), lambda i: (i // tiles_per_seq, 0, i % tiles_per_seq, 0))


W_STAGE_ROWS = 128


def _weight_stage(width):
    return [pltpu.VMEM((2, W_STAGE_ROWS, width), F32), pltpu.SemaphoreType.DMA((2,))]


def _in_proj(x2, g, w, w_expert, B, S):
    T = x2.shape[0]
    tm = TM_PROJ
    row = lambda w: pl.BlockSpec((tm, w), lambda i: (i, 0))
    dils = [d for _, d in ATT_GROUPS]
    we_spec = _expert_slice_spec(T // tm, w_expert.shape[2])
    return pl.pallas_call(
        _inproj_kernel,
        grid=(T // tm,),
        in_specs=[row(D_MODEL), _resident((1, D_MODEL)), pl.BlockSpec(memory_space=pl.ANY), we_spec],
        out_specs=[_plane_spec(d, tm, S // tm, QKV_G) for d in dils] + [row(2 * GMLP_W), row(N_BRANCH * D_MODEL), we_spec],
        out_shape=[jax.ShapeDtypeStruct((B, d, S // d, QKV_G), BF16) for d in dils]
                  + [jax.ShapeDtypeStruct((T, 2 * GMLP_W), BF16),
                     jax.ShapeDtypeStruct((T, N_BRANCH * D_MODEL), BF16),
                     jax.ShapeDtypeStruct(w_expert.shape, BF16)],
        scratch_shapes=[pltpu.VMEM((2, tm, LANES), F32), pltpu.VMEM((D_MODEL, IN_W), BF16)] + _weight_stage(IN_W),
        compiler_params=_cparams("arbitrary"),
        name="in_proj",
    )(x2, g, w, w_expert)


def _t5_bucket(n):
    exact = REL_BUCKETS // 2
    nf = np.maximum(n, 1).astype(np.float32)
    large = exact + (np.log(nf / exact) / np.log(REL_MAX_DIST / exact) * (REL_BUCKETS - exact)).astype(np.int32)
    large = np.minimum(large, REL_BUCKETS - 1)
    return np.where(n < exact, n, large).astype(np.int32)


def _bias_table(rel_bias_g, dilation):
    n = 3 * BLK
    dist = 2 * BLK - 1 - np.arange(n)
    valid = (dist >= 0) & (dist <= BLK)
    bucket = _t5_bucket(np.clip(dist, 0, BLK) * dilation)
    c = jnp.where(jnp.asarray(valid)[None, :], rel_bias_g.astype(F32)[bucket].T * LOG2E, MASKED)
    shifted = jnp.tile(c, (1, BLK))[:, :BLK * (n - 1)].reshape(HEADS_PER_GROUP, BLK, n - 1)
    return shifted[:, :, BLK - 1:].reshape(HEADS_PER_GROUP * BLK, 2 * BLK)


def _attn_kernel(cur_ref, prev_ref, bias_ref, o_ref, lse_ref):
    rg, rb = cur_ref.shape[1], cur_ref.shape[2] // BLK
    starts_sequence = pl.program_id(2) == 0
    lane_head = lax.broadcasted_iota(jnp.int32, (1, GROUP_W), 1) // HEAD_DIM
    scale = HEAD_DIM ** -0.5
    head_bf = [jnp.where(lane_head == h, scale, 0.0).astype(BF16) for h in range(HEADS_PER_GROUP)]
    key_is_prev = lax.broadcasted_iota(jnp.int32, (1, 2 * BLK), 1) < BLK
    nt = (((1,), (1,)), ((), ()))
    qc, kc_, vc_ = slice(0, GROUP_W), slice(GROUP_W, 2 * GROUP_W), slice(2 * GROUP_W, 3 * GROUP_W)

    def by_head(x):
        sel = x[(HEADS_PER_GROUP - 1) * BLK:]
        for h in range(HEADS_PER_GROUP - 2, -1, -1):
            sel = jnp.where(lane_head == h, x[h * BLK:(h + 1) * BLK], sel)
        return sel

    for r, j in [(r, j) for r in range(rg) for j in range(rb)]:
        rows = slice(j * BLK, (j + 1) * BLK)
        prev = prev_ref if j == 0 else cur_ref
        prows = slice(0, BLK) if j == 0 else slice((j - 1) * BLK, j * BLK)
        q = cur_ref[0, r, rows, qc]
        k = jnp.concatenate([prev[0, r, prows, kc_], cur_ref[0, r, rows, kc_]], axis=0)
        v = jnp.concatenate([prev[0, r, prows, vc_], cur_ref[0, r, rows, vc_]], axis=0)
        q_bd = jnp.concatenate([q * head_bf[h] for h in range(HEADS_PER_GROUP)], axis=0)
        s = lax.dot_general(q_bd, k, nt, preferred_element_type=F32) * LOG2E + bias_ref[...]
        if j == 0:
            s = jnp.where(jnp.logical_and(starts_sequence, key_is_prev), MASKED, s)
        m = jnp.max(s, axis=-1, keepdims=True)
        p = jnp.exp2(s - m)
        den = jnp.sum(p, axis=-1, keepdims=True)
        o = jnp.dot(p.astype(BF16), v, preferred_element_type=F32)
        den_h = jnp.broadcast_to(by_head(den), (BLK, GROUP_W))
        o_ref[0, r, rows, :] = (by_head(o) / den_h).astype(BF16)
        lse_ref[0, r, rows, :] = by_head(m) * LN2 + jnp.log(den_h)


ATT_SUBBLOCKS = 8


def _attention_group(a, bias, dilation, B, S):
    sd = S // dilation
    rb = min(ATT_SUBBLOCKS, sd // BLK)
    rg = min(ATT_SUBBLOCKS // rb, dilation)
    o, lse = pl.pallas_call(
        _attn_kernel,
        grid=(B, dilation // rg, sd // (rb * BLK)),
        in_specs=[pl.BlockSpec((1, rg, rb * BLK, QKV_G), lambda b, r, n: (b, r, n, 0)),
                  pl.BlockSpec((1, rg, BLK, QKV_G), lambda b, r, n: (b, r, jnp.maximum(n * rb - 1, 0), 0)),
                  _resident((HEADS_PER_GROUP * BLK, 2 * BLK))],
        out_specs=[pl.BlockSpec((1, rg, rb * BLK, GROUP_W), lambda b, r, n: (b, r, n, 0))] * 2,
        out_shape=[jax.ShapeDtypeStruct((B, dilation, sd, GROUP_W), BF16),
                   jax.ShapeDtypeStruct((B, dilation, sd, GROUP_W), F32)],
        compiler_params=_cparams("parallel", "parallel", "parallel"),
        name=f"attn_d{dilation}",
    )(a, a, bias)
    return o, lse


def _sigmoid(x):
    return 0.5 * jnp.tanh(0.5 * x) + 0.5


def _gelu(x):
    return x * (lax.erf(x * (2.0 ** -0.5)) + 1.0) * 0.5


def _token_major(src_ref, d, scr, slot, tm):
    if d == 1:
        return src_ref[0, 0].astype(F32)
    for r in range(d):
        piece = src_ref[0, r].astype(F32)
        scr[slot, pl.ds(r, tm // d, stride=d), :] = piece[:, :LANES]
        scr[slot + 1, pl.ds(r, tm // d, stride=d), :] = piece[:, LANES:]
    return jnp.concatenate([scr[slot], scr[slot + 1]], axis=1)


def _mix_kernel(x_ref, o1_ref, o2_ref, o3_ref, l1_ref, l2_ref, l3_ref, uv_ref, gl_ref,
                wa_hbm, wg_hbm, wo_hbm, wc_ref, bs_ref, lng_ref, lnb_ref, we_ref, h_ref, we_bf_ref, g_scr, t_scr,
                wa_ref, wg_ref, wo_ref, w_stage, w_sem):
    @pl.when(pl.program_id(0) == 0)
    def _():
        for w_hbm, w_bf in ((wa_hbm, wa_ref), (wg_hbm, wg_ref), (wo_hbm, wo_ref)):
            _load_weight_bf16(w_hbm, w_bf, w_stage, w_sem)

    we_bf_ref[...] = we_ref[...].astype(BF16)

    tm = x_ref.shape[0]
    dils = [d for _, d in ATT_GROUPS]
    o1, o2, o3 = [_token_major(ref, d, t_scr, 4 * i, tm) for i, (ref, d) in enumerate(zip((o1_ref, o2_ref, o3_ref), dils))]
    l1, l2, l3 = [_token_major(ref, d, t_scr, 4 * i + 2, tm) for i, (ref, d) in enumerate(zip((l1_ref, l2_ref, l3_ref), dils))]
    lm = jnp.maximum(jnp.maximum(l1, l2), l3)
    e1, e2, e3 = jnp.exp(l1 - lm), jnp.exp(l2 - lm), jnp.exp(l3 - lm)
    att = (e1 * o1 + e2 * o2 + e3 * o3) / (e1 + e2 + e3)
    y_att = jnp.dot(att.astype(BF16), wa_ref[...], preferred_element_type=F32)

    zu = _gelu(uv_ref[:, :GMLP_W].astype(F32))
    zv = _gelu(uv_ref[:, GMLP_W:].astype(F32))
    mu = jnp.mean(zv, axis=-1, keepdims=True)
    var = jnp.mean(jnp.square(zv - mu), axis=-1, keepdims=True)
    vn = (zv - mu) * lax.rsqrt(var + EPS) * lng_ref[...] + lnb_ref[...]
    low_half = lax.broadcasted_iota(jnp.int32, (CHUNK, 2 * GMLP_GD), 1) < GMLP_GD
    for c in range(tm // CHUNK):
        rows = slice(c * CHUNK, (c + 1) * CHUNK)
        for s in range(GMLP_W // (2 * GMLP_GD)):
            cols = slice(s * 2 * GMLP_GD, (s + 1) * 2 * GMLP_GD)
            v2 = vn[rows, cols]
            rhs = jnp.concatenate([jnp.where(low_half, v2, 0.0), jnp.where(low_half, 0.0, v2)], axis=0).astype(BF16)
            mixed = jnp.dot(wc_ref[s], rhs, preferred_element_type=F32) + bs_ref[:, cols]
            g_scr[rows, cols] = (zu[rows, cols] * mixed).astype(BF16)
    y_gm = jnp.dot(g_scr[...], wg_ref[...], preferred_element_type=F32)

    gate_a = _sigmoid(gl_ref[:, :D_MODEL].astype(F32))
    gate_g = _sigmoid(gl_ref[:, D_MODEL:].astype(F32))
    merged = (gate_a * y_att + gate_g * y_gm).astype(BF16)
    h_ref[...] = x_ref[...] + jnp.dot(merged, wo_ref[...], preferred_element_type=F32)


def _mix(x2, outs, lses, uv, gl, wa, wg, wo, wc2, bs, lng, lnb, w_expert, S):
    T = x2.shape[0]
    tm = TM_PROJ
    row = lambda w: pl.BlockSpec((tm, w), lambda i: (i, 0))
    att = [_plane_spec(d, tm, S // tm, GROUP_W) for _, d in ATT_GROUPS]
    we_spec = _expert_slice_spec(T // tm, w_expert.shape[2])
    return pl.pallas_call(
        _mix_kernel,
        grid=(T // tm,),
        in_specs=[row(D_MODEL)] + att + att + [row(2 * GMLP_W), row(N_BRANCH * D_MODEL)]
                 + [pl.BlockSpec(memory_space=pl.ANY)] * 3
                 + [_resident(wc2.shape), _resident(bs.shape), _resident(lng.shape), _resident(lnb.shape), we_spec],
        out_specs=[row(D_MODEL), we_spec],
        out_shape=[jax.ShapeDtypeStruct((T, D_MODEL), F32), jax.ShapeDtypeStruct(w_expert.shape, BF16)],
        scratch_shapes=[pltpu.VMEM((tm, GMLP_W), BF16), pltpu.VMEM((4 * N_DIL, tm, LANES), F32),
                        pltpu.VMEM(wa.shape, BF16), pltpu.VMEM(wg.shape, BF16), pltpu.VMEM(wo.shape, BF16)]
                       + _weight_stage(D_MODEL),
        compiler_params=_cparams("arbitrary"),
        name="mix",
    )(x2, *outs, *lses, uv, gl, wa, wg, wo, wc2, bs, lng, lnb, w_expert)


def _router_kernel(h_ref, g_ref, wr_ref, br_ref, upper_ref, eidx_ref, gate_ref, rank_ref, cnt_ref, carry):
    tm = h_ref.shape[0]

    @pl.when(pl.program_id(0) == 0)
    def _():
        carry[...] = jnp.zeros_like(carry)

    hn = _rms(h_ref[...], g_ref[...])
    hi = hn.astype(BF16)
    lo = (hn - hi.astype(F32)).astype(BF16)
    nt = (((1,), (1,)), ((), ()))
    by_hi = lax.dot_general(wr_ref[...], hi, nt, preferred_element_type=F32)
    by_lo = lax.dot_general(wr_ref[:N_EXPERTS, :], lo, nt, preferred_element_type=F32)
    logits = by_hi[:N_EXPERTS] + by_hi[N_EXPERTS:] + by_lo + br_ref[...]
    expert = lax.broadcasted_iota(jnp.int32, (N_EXPERTS, tm), 0)
    vals, hots = [], []
    l = logits
    for k in range(TOP_K):
        m = jnp.max(l, axis=0, keepdims=True)
        idx = jnp.min(jnp.where(l == m, expert, N_EXPERTS), axis=0, keepdims=True)
        hot = expert == idx
        eidx_ref[0, k:k + 1, :] = idx
        vals.append(m)
        hots.append(hot)
        l = jnp.where(hot, -jnp.inf, l)
    ex = [jnp.exp(v - vals[0]) for v in vals]
    tot = ex[0] + ex[1] + ex[2] + ex[3]
    for k in range(TOP_K):
        gate_ref[0, k:k + 1, :] = ex[k] / tot
    multi = jnp.zeros((N_EXPERTS, tm), F32)
    for hot in hots:
        multi = multi + hot.astype(F32)
    before = jnp.dot(multi.astype(BF16), upper_ref[...], preferred_element_type=F32) + carry[...]
    for k in range(TOP_K):
        rank_ref[0, k:k + 1, :] = jnp.sum(jnp.where(hots[k], before, 0.0), axis=0, keepdims=True).astype(jnp.int32)
    carry[...] += jnp.sum(multi, axis=1, keepdims=True)
    cnt_ref[...] = carry[...]


def _router(h1, g, wr_t, br_col):
    T = h1.shape[0]
    tm = TM_PROJ
    upper = jnp.asarray(np.triu(np.ones((tm, tm), np.float32), k=1), BF16)
    k_rows = pl.BlockSpec((1, TOP_K, tm), lambda i: (i, 0, 0))
    k_shape = lambda dt: jax.ShapeDtypeStruct((T // tm, TOP_K, tm), dt)
    return pl.pallas_call(
        _router_kernel,
        grid=(T // tm,),
        in_specs=[pl.BlockSpec((tm, D_MODEL), lambda i: (i, 0)), _resident((1, D_MODEL)),
                  _resident((2 * N_EXPERTS, D_MODEL)), _resident((N_EXPERTS, 1)), _resident((tm, tm))],
        out_specs=[k_rows, k_rows, k_rows, pl.BlockSpec((N_EXPERTS, 1), lambda i: (0, 0))],
        out_shape=[k_shape(jnp.int32), k_shape(F32), k_shape(jnp.int32),
                   jax.ShapeDtypeStruct((N_EXPERTS, 1), F32)],
        scratch_shapes=[pltpu.VMEM((N_EXPERTS, 1), F32)],
        compiler_params=_cparams("arbitrary"),
        name="router",
    )(h1, g, wr_t, br_col, upper)


def _to_row_tiles(ref, lead, value):
    n = value.shape[0]
    for c in range(ROW_SUBLANES):
        ref[(*lead, pl.ds(c, n, stride=ROW_SUBLANES), slice(None))] = value[:, c * LANES:(c + 1) * LANES]


def _from_row_tiles(ref, lead, first, n):
    return jnp.concatenate(
        [ref[(*lead, pl.ds(first * ROW_SUBLANES + c, n, stride=ROW_SUBLANES), slice(None))] for c in range(ROW_SUBLANES)],
        axis=1)


def _tile_rows(idx, n=1):
    return pl.ds(pl.multiple_of(idx * ROW_SUBLANES, ROW_SUBLANES), n * ROW_SUBLANES)


def _dispatch_kernel(last_ref, nv_ref, dest_ref, h_ref, g_ref, xs_ref, buf, sem, zero_sem):
    tm = h_ref.shape[0]
    n_blocks = xs_ref.shape[0] // (TM_EXP * ROW_SUBLANES)
    i = pl.program_id(0)
    slot = lax.rem(i, 2)

    @pl.when(i == 0)
    def _():
        buf[1] = jnp.zeros(buf.shape[1:], F32)

        def zero_block(b):
            for part in range(TM_EXP // tm):
                pltpu.make_async_copy(buf.at[1], xs_ref.at[_tile_rows(b * TM_EXP + part * tm, tm)], zero_sem).start()

        def zero_done():
            for part in range(TM_EXP // tm):
                pltpu.make_async_copy(buf.at[1], xs_ref.at[_tile_rows(0, tm)], zero_sem).wait()

        for e in range(N_EXPERTS):
            zero_block(last_ref[e])
        lax.fori_loop(nv_ref[0], n_blocks, lambda b, c: (zero_block(b), c)[1], 0)
        for e in range(N_EXPERTS):
            zero_done()
        lax.fori_loop(nv_ref[0], n_blocks, lambda b, c: (zero_done(), c)[1], 0)

    _to_row_tiles(buf, (slot,), _rms(h_ref[...], g_ref[...]))

    def issue(t, carry):
        for k in range(TOP_K):
            d = dest_ref[0, 0, k * tm + t]
            pltpu.make_async_copy(buf.at[slot, _tile_rows(t)], xs_ref.at[_tile_rows(d)],
                                  sem.at[slot]).start(priority=k % 2)
        return carry

    lax.fori_loop(0, tm, issue, 0, unroll=8)

    def wait_slot(s):
        for _ in range(TOP_K):
            pltpu.make_async_copy(buf.at[s], xs_ref.at[_tile_rows(0, tm)], sem.at[s]).wait()

    @pl.when(i > 0)
    def _():
        wait_slot(1 - slot)

    @pl.when(i == pl.num_programs(0) - 1)
    def _():
        wait_slot(slot)


def _dispatch(last_block, n_valid, dest3, h1, g, n_slots):
    T = h1.shape[0]
    tm = TM_TOK
    grid_spec = pltpu.PrefetchScalarGridSpec(
        num_scalar_prefetch=2,
        grid=(T // tm,),
        in_specs=[pl.BlockSpec((1, 1, TOP_K * tm), lambda i, lb, nv: (i, 0, 0), memory_space=pltpu.SMEM),
                  pl.BlockSpec((tm, D_MODEL), lambda i, lb, nv: (i, 0)),
                  pl.BlockSpec((1, D_MODEL), lambda i, lb, nv: (0, 0), pipeline_mode=pl.Buffered(1))],
        out_specs=pl.BlockSpec(memory_space=pl.ANY),
        scratch_shapes=[pltpu.VMEM((2, tm * ROW_SUBLANES, LANES), F32), pltpu.SemaphoreType.DMA((2,)),
                        pltpu.SemaphoreType.DMA(())],
    )
    return pl.pallas_call(
        _dispatch_kernel,
        grid_spec=grid_spec,
        out_shape=jax.ShapeDtypeStruct((n_slots * ROW_SUBLANES, LANES), F32),
        compiler_params=_cparams("arbitrary"),
        name="dispatch",
    )(last_block, n_valid, dest3, h1, g)


def _experts_kernel(be_ref, nv_ref, xs_hbm, wgu_ref, bgu_ref, wd_ref, bd_ref, ys_hbm, xbuf, ybuf, xsem, ysem):
    del be_ref
    tm = xbuf.shape[1]
    b = pl.program_id(0)
    n_valid = nv_ref[0]
    slot = lax.rem(b, 2)

    def x_copies(blk, s):
        return [pltpu.make_async_copy(xs_hbm.at[pl.ds(blk * tm, tm), c, :],
                                      xbuf.at[s, :, pl.ds(c * LANES, LANES)], xsem.at[s]) for c in range(ROW_SUBLANES)]

    def y_copies(blk, s):
        return [pltpu.make_async_copy(ybuf.at[s, :, pl.ds(c * LANES, LANES)],
                                      ys_hbm.at[pl.ds(blk * tm, tm), c, :], ysem.at[s]) for c in range(ROW_SUBLANES)]

    @pl.when(jnp.logical_and(b == 0, n_valid > 0))
    def _():
        for cp in x_copies(0, 0):
            cp.start()

    @pl.when(b + 1 < n_valid)
    def _():
        for cp in x_copies(b + 1, 1 - slot):
            cp.start()

    @pl.when(b < n_valid)
    def _():
        for cp in x_copies(b, slot):
            cp.wait()
        x = xbuf[slot].astype(BF16)
        gu = jnp.dot(x, wgu_ref[0], preferred_element_type=F32) + bgu_ref[0]
        glu = jnp.minimum(gu[:, :D_EXPERT], SWIGLU_LIMIT)
        lin = jnp.clip(gu[:, D_EXPERT:], -SWIGLU_LIMIT, SWIGLU_LIMIT)
        act = glu * jax.nn.sigmoid(SWIGLU_ALPHA * glu) * (lin + 1.0)
        ybuf[slot] = jnp.dot(act.astype(BF16), wd_ref[0], preferred_element_type=F32) + bd_ref[0]

    @pl.when(b >= n_valid)
    def _():
        ybuf[slot] = jnp.zeros(ybuf.shape[1:], F32)

    for cp in y_copies(b, slot):
        cp.start()

    @pl.when(b > 0)
    def _():
        for cp in y_copies(b - 1, 1 - slot):
            cp.wait()

    @pl.when(b == pl.num_programs(0) - 1)
    def _():
        for cp in y_copies(b, slot):
            cp.wait()


def _experts(block_expert, n_valid, xs, wgu, bgu, wd, bd):
    tm = TM_EXP
    n_slots = xs.shape[0] // ROW_SUBLANES
    n_blocks = n_slots // tm
    any_space = pl.BlockSpec(memory_space=pl.ANY)
    grid_spec = pltpu.PrefetchScalarGridSpec(
        num_scalar_prefetch=2,
        grid=(n_blocks,),
        in_specs=[any_space,
                  pl.BlockSpec((1, D_MODEL, 2 * D_EXPERT), lambda b, be, nv: (be[b], 0, 0)),
                  pl.BlockSpec((1, 1, 2 * D_EXPERT), lambda b, be, nv: (be[b], 0, 0)),
                  pl.BlockSpec((1, D_EXPERT, D_MODEL), lambda b, be, nv: (be[b], 0, 0)),
                  pl.BlockSpec((1, 1, D_MODEL), lambda b, be, nv: (be[b], 0, 0))],
        out_specs=any_space,
        scratch_shapes=[pltpu.VMEM((2, tm, D_MODEL), F32), pltpu.VMEM((2, tm, D_MODEL), F32),
                        pltpu.SemaphoreType.DMA((2,)), pltpu.SemaphoreType.DMA((2,))],
    )
    ys = pl.pallas_call(
        _experts_kernel,
        grid_spec=grid_spec,
        out_shape=jax.ShapeDtypeStruct((n_slots, ROW_SUBLANES, LANES), F32),
        compiler_params=_cparams("arbitrary"),
        name="experts",
    )(block_expert, n_valid, xs.reshape(n_slots, ROW_SUBLANES, LANES), wgu, bgu, wd, bd)
    return ys.reshape(xs.shape)


GATHER_SLOTS = 3


def _combine_kernel(*refs, n_tiles):
    ahead = GATHER_SLOTS - 1
    prime_refs, refs = refs[:ahead], refs[ahead:]
    (ahead_dest_ref, h_ref, gate_ref, p_ref, gp_ref, gf_ref, wpg_hbm, wpp_hbm, ys_ref, o_ref), refs = refs[:10], refs[10:]
    bufs, (sem, wpg_ref, wpp_ref, w_stage, w_sem) = refs[:GATHER_SLOTS], refs[GATHER_SLOTS:]
    tm = h_ref.shape[0]
    i = pl.program_id(0)

    def row_copy(dref, t, k, s):
        d = dref[0, 0, k * tm + t]
        return pltpu.make_async_copy(ys_ref.at[_tile_rows(d)], bufs[s].at[_tile_rows(k * tm + t)], sem.at[s])

    def wait_slot(s):
        for _ in range(TOP_K):
            pltpu.make_async_copy(ys_ref.at[_tile_rows(0, tm)], bufs[s].at[_tile_rows(0, tm)], sem.at[s]).wait()

    @pl.when(i == 0)
    def _():
        _load_weight_bf16(wpg_hbm, wpg_ref, w_stage, w_sem)
        _load_weight_bf16(wpp_hbm, wpp_ref, w_stage, w_sem)

        for s, dref in enumerate(prime_refs):
            def issue(t, carry, s=s, dref=dref):
                for k in range(TOP_K):
                    row_copy(dref, t, k, s).start(priority=k % 2)
                return carry
            lax.fori_loop(0, tm, issue, 0, unroll=8)

    def step(s):
        wait_slot(s)
        for t in range(tm):
            for k in range(TOP_K):
                row_copy(ahead_dest_ref, t, k, (s + ahead) % GATHER_SLOTS).start(priority=k % 2)
        proj = jnp.dot(p_ref[...].astype(BF16), wpp_ref[...], preferred_element_type=F32)
        h = h_ref[...]
        for k in range(TOP_K):
            h = h + gate_ref[:, k:k + 1] * _from_row_tiles(bufs[s], (), k * tm, tm)
        ple_gate = jax.nn.sigmoid(jnp.dot(_rms(h, gp_ref[...]).astype(BF16), wpg_ref[...], preferred_element_type=F32))
        h = h + ple_gate * proj
        o_ref[...] = _rms(h, gf_ref[...])

    for s in range(GATHER_SLOTS):
        pl.when(lax.rem(i, GATHER_SLOTS) == s)(functools.partial(step, s))

    @pl.when(i == n_tiles - 1)
    def _():
        for extra in range(ahead):
            wait_slot((n_tiles + extra) % GATHER_SLOTS)


def _combine(dest3, h1, gate, p2, gp, wpg, wpp, gf, ys):
    T = h1.shape[0]
    tm = TM_TOK
    n_tiles = T // tm
    row = lambda w: pl.BlockSpec((tm, w), lambda i: (i, 0))
    dest_spec = lambda ahead: pl.BlockSpec((1, 1, TOP_K * tm), lambda i: (jnp.minimum(i + ahead, n_tiles - 1), 0, 0),
                                           memory_space=pltpu.SMEM)
    gather_buf = pltpu.VMEM((TOP_K * tm * ROW_SUBLANES, LANES), F32)
    ahead = GATHER_SLOTS - 1
    assert n_tiles > ahead
    return pl.pallas_call(
        functools.partial(_combine_kernel, n_tiles=n_tiles),
        grid=(n_tiles,),
        in_specs=[dest_spec(a) for a in range(ahead)] + [dest_spec(ahead),
                  row(D_MODEL), row(TOP_K), row(PLE_DIM), _resident((1, D_MODEL)), _resident((1, D_MODEL))]
                 + [pl.BlockSpec(memory_space=pl.ANY)] * 3,
        out_specs=row(D_MODEL),
        out_shape=jax.ShapeDtypeStruct((T, D_MODEL), F32),
        scratch_shapes=[gather_buf] * GATHER_SLOTS + [pltpu.SemaphoreType.DMA((GATHER_SLOTS,)),
                        pltpu.VMEM(wpg.shape, BF16), pltpu.VMEM(wpp.shape, BF16)] + _weight_stage(D_MODEL),
        compiler_params=_cparams("arbitrary"),
        name="combine",
    )(*([dest3] * (ahead + 1)), h1, gate, p2, gp, gf, wpg, wpp, ys)


def _layer(h, p_i, g_mix, w_in, rel_bias, w_att_out, ln_v_g, ln_v_b, w_spatial, b_spatial, w_gmlp_out, w_out,
           g_moe, w_router, b_router, w_gate_up, b_gate_up, w_down, b_down, g_ple, w_ple_gate, w_ple_proj,
           g_final, B, S):
    T = B * S
    row = lambda v: v.reshape(1, -1).astype(F32)

    assert S % TM_PROJ == 0
    *att_in, uv, gl, wgu_bf = _in_proj(h, row(g_mix), w_in.astype(F32), w_gate_up.astype(F32), B, S)

    outs, lses = [], []
    for g, (window, dilation) in enumerate(ATT_GROUPS):
        assert window // dilation == BLK and S % (dilation * BLK) == 0
        bias = _bias_table(rel_bias[:, g * HEADS_PER_GROUP:(g + 1) * HEADS_PER_GROUP], dilation)
        o, lse = _attention_group(att_in[g], bias, dilation, B, S)
        outs.append(o)
        lses.append(lse)

    causal = jnp.asarray(np.tril(np.ones((CHUNK, CHUNK), np.float32)))
    w_c = (w_spatial.astype(F32) * causal[None]).astype(BF16)
    wc2 = jnp.concatenate([w_c[0::2], w_c[1::2]], axis=2)
    bs = jnp.repeat(b_spatial.astype(F32).T, GMLP_GD, axis=1)
    h1, wd_bf = _mix(h, outs, lses, uv, gl, w_att_out.astype(F32), w_gmlp_out.astype(F32), w_out.astype(F32),
                     wc2, bs, row(ln_v_g), row(ln_v_b), w_down.astype(F32), S)

    wr_hi = w_router.astype(BF16)
    wr_lo = (w_router.astype(F32) - wr_hi.astype(F32)).astype(BF16)
    eidx, gate, rank, counts = _router(h1, row(g_moe), jnp.concatenate([wr_hi, wr_lo], axis=1).T,
                                       b_router.reshape(-1, 1).astype(F32))
    cnt = counts[:, 0].astype(jnp.int32)
    blk_counts = (cnt + TM_EXP - 1) // TM_EXP
    blk_end = jnp.cumsum(blk_counts)
    pad_start = (blk_end - blk_counts) * TM_EXP
    n_blocks = T * TOP_K // TM_EXP + N_EXPERTS
    n_valid = blk_end[-1:].astype(jnp.int32)
    blk = jnp.minimum(jnp.arange(n_blocks, dtype=jnp.int32), n_valid[0] - 1)
    block_expert = jnp.minimum(jnp.sum((blk_end[None, :] <= blk[:, None]).astype(jnp.int32), axis=1), N_EXPERTS - 1)
    expert_ids = jnp.arange(N_EXPERTS, dtype=jnp.int32)
    dest = rank + jnp.sum(jnp.where(eidx[..., None] == expert_ids, pad_start, 0), axis=-1)
    split = TM_PROJ // TM_TOK
    dest3 = jnp.transpose(dest.reshape(T // TM_PROJ, TOP_K, split, TM_TOK), (0, 2, 1, 3)).reshape(
        T // TM_TOK, 1, TOP_K * TM_TOK)
    gate = jnp.transpose(gate, (0, 2, 1)).reshape(T, TOP_K)

    last_block = jnp.maximum(blk_end - 1, 0).astype(jnp.int32)
    xs = _dispatch(last_block, n_valid, dest3, h1, row(g_moe), n_blocks * TM_EXP)
    ys = _experts(block_expert, n_valid, xs, wgu_bf, b_gate_up.reshape(N_EXPERTS, 1, -1).astype(F32),
                  wd_bf, b_down.reshape(N_EXPERTS, 1, -1).astype(F32))
    return _combine(dest3, h1, gate, p_i, row(g_ple), w_ple_gate.astype(F32), w_ple_proj.astype(F32),
                    row(g_final), ys)


def kernel(x, p, g_mix, w_in, rel_bias, w_att_out, ln_v_g, ln_v_b, w_spatial, b_spatial, w_gmlp_out, w_out, g_moe, w_router, b_router, w_gate_up, b_gate_up, w_down, b_down, g_ple, w_ple_gate, w_ple_proj, g_final):
    B, S, D = x.shape
    depth = p.shape[0]
    assert depth == 1, "the final RMSNorm is fused into the (single) layer's last kernel"
    out = _layer(x.reshape(B * S, D), p[0].reshape(B * S, PLE_DIM), g_mix[0], w_in[0], rel_bias, w_att_out[0],
                 ln_v_g[0], ln_v_b[0], w_spatial[0], b_spatial[0], w_gmlp_out[0], w_out[0], g_moe[0], w_router[0],
                 b_router[0], w_gate_up[0], b_gate_up[0], w_down[0], b_down[0], g_ple[0], w_ple_gate[0],
                 w_ple_proj[0], g_final, B, S)
    return out.reshape(B, S, D)
```

```python
import functools

import jax
import jax.numpy as jnp
import numpy as np
from jax import lax
from jax.experimental import pallas as pl
from jax.experimental.pallas import tpu as pltpu

F32 = jnp.float32
BF16 = jnp.bfloat16

D_MODEL = 1024
HEAD_DIM = 64
ATT_GROUPS = ((128, 1), (512, 4), (2048, 16))
HEADS_PER_GROUP = 4
GROUP_W = HEADS_PER_GROUP * HEAD_DIM
N_DIL = len(ATT_GROUPS)
ATT_W = N_DIL * GROUP_W
BLK = 128
REL_BUCKETS = 32
REL_MAX_DIST = 2048
CHUNK = 128
GMLP_W = 768
GMLP_GD = 64
N_BRANCH = 2
IN_W = 3 * ATT_W + 2 * GMLP_W + N_BRANCH * D_MODEL
N_EXPERTS = 32
TOP_K = 4
D_EXPERT = D_MODEL
SWIGLU_LIMIT = 7.0
SWIGLU_ALPHA = 1.702
PLE_DIM = 256
EPS = 1e-6
MASKED = -1e30
LOG2E = float(np.log2(np.e))
LN2 = float(np.log(2.0))

QKV_G = 3 * GROUP_W

LANES = 128
ROW_SUBLANES = D_MODEL // LANES
assert ROW_SUBLANES == 8
MXU_N = 256
VMEM_LIMIT = 56 * 1024 * 1024

TM_PROJ = 512
TM_TOK = 256
TM_EXP = 512
assert TM_EXP % TM_TOK == 0


def _cparams(*sem):
    return pltpu.CompilerParams(dimension_semantics=sem, vmem_limit_bytes=VMEM_LIMIT)


def _resident(shape):
    nd = len(shape)
    return pl.BlockSpec(shape, lambda *_: (0,) * nd, pipeline_mode=pl.Buffered(1))


def _rms(x, g):
    return x * lax.rsqrt(jnp.mean(x * x, axis=-1, keepdims=True) + EPS) * g


def _load_weight_bf16(w_hbm, w_bf, stage, sem):
    rows = stage.shape[1]
    n_chunks = w_hbm.shape[0] // rows
    assert n_chunks * rows == w_hbm.shape[0] and stage.shape[2] == w_hbm.shape[1]

    def chunk(c):
        return pltpu.make_async_copy(w_hbm.at[pl.ds(c * rows, rows)], stage.at[c % 2], sem.at[c % 2])

    chunk(0).start()
    for c in range(n_chunks):
        if c + 1 < n_chunks:
            chunk(c + 1).start()
        chunk(c).wait()
        w_bf[c * rows:(c + 1) * rows, :] = stage[c % 2].astype(BF16)


def _expert_slice_spec(n_steps, width):
    per_expert = n_steps // N_EXPERTS
    assert per_expert * N_EXPERTS == n_steps and D_MODEL % per_expert == 0
    return pl.BlockSpec((1, D_MODEL // per_expert, width), lambda i: (i // per_expert, i % per_expert, 0))


def _inproj_kernel(x_ref, g_ref, w_hbm, we_ref, a1_ref, a2_ref, a3_ref, uv_ref, gl_ref, we_bf_ref,
                   scr, w_ref, w_stage, w_sem):
    @pl.when(pl.program_id(0) == 0)
    def _():
        _load_weight_bf16(w_hbm, w_ref, w_stage, w_sem)

    we_bf_ref[...] = we_ref[...].astype(BF16)

    tm = x_ref.shape[0]
    n = _rms(x_ref[...], g_ref[...]).astype(BF16)
    att_refs = (a1_ref, a2_ref, a3_ref)
    n_att, n_uv = 3 * ATT_W // MXU_N, 2 * GMLP_W // MXU_N
    for c in range(IN_W // MXU_N):
        z = jnp.dot(n, w_ref[:, c * MXU_N:(c + 1) * MXU_N], preferred_element_type=F32)
        if c < n_att:
            which, g = divmod(c, N_DIL)
            d = ATT_GROUPS[g][1]
            dst = att_refs[g]
            cols = slice(which * GROUP_W, (which + 1) * GROUP_W)
            if d == 1:
                dst[0, 0, :, cols] = z.astype(BF16)
                continue
            scr[0] = z[:, :LANES]
            scr[1] = z[:, LANES:]
            for r in range(d):
                zr = jnp.concatenate([scr[0, pl.ds(r, tm // d, stride=d), :],
                                      scr[1, pl.ds(r, tm // d, stride=d), :]], axis=1)
                dst[0, r, :, cols] = zr.astype(BF16)
        elif c < n_att + n_uv:
            uv_ref[:, (c - n_att) * MXU_N:(c - n_att + 1) * MXU_N] = z.astype(BF16)
        else:
            gl_ref[:, (c - n_att - n_uv) * MXU_N:(c - n_att - n_uv + 1) * MXU_N] = z.astype(BF16)


def _plane_spec(d, tm, tiles_per_seq, width):
    return pl.BlockSpec((1, d, tm // d, width), lambda i: (i // tiles_per_seq, 0, i % tiles_per_seq, 0))


W_STAGE_ROWS = 128


def _weight_stage(width):
    return [pltpu.VMEM((2, W_STAGE_ROWS, width), F32), pltpu.SemaphoreType.DMA((2,))]


def _in_proj(x2, g, w, w_expert, B, S):
    T = x2.shape[0]
    tm = TM_PROJ
    row = lambda w: pl.BlockSpec((tm, w), lambda i: (i, 0))
    dils = [d for _, d in ATT_GROUPS]
    we_spec = _expert_slice_spec(T // tm, w_expert.shape[2])
    return pl.pallas_call(
        _inproj_kernel,
        grid=(T // tm,),
        in_specs=[row(D_MODEL), _resident((1, D_MODEL)), pl.BlockSpec(memory_space=pl.ANY), we_spec],
        out_specs=[_plane_spec(d, tm, S // tm, QKV_G) for d in dils] + [row(2 * GMLP_W), row(N_BRANCH * D_MODEL), we_spec],
        out_shape=[jax.ShapeDtypeStruct((B, d, S // d, QKV_G), BF16) for d in dils]
                  + [jax.ShapeDtypeStruct((T, 2 * GMLP_W), BF16),
                     jax.ShapeDtypeStruct((T, N_BRANCH * D_MODEL), BF16),
                     jax.ShapeDtypeStruct(w_expert.shape, BF16)],
        scratch_shapes=[pltpu.VMEM((2, tm, LANES), F32), pltpu.VMEM((D_MODEL, IN_W), BF16)] + _weight_stage(IN_W),
        compiler_params=_cparams("arbitrary"),
        name="in_proj",
    )(x2, g, w, w_expert)


def _t5_bucket(n):
    exact = REL_BUCKETS // 2
    nf = np.maximum(n, 1).astype(np.float32)
    large = exact + (np.log(nf / exact) / np.log(REL_MAX_DIST / exact) * (REL_BUCKETS - exact)).astype(np.int32)
    large = np.minimum(large, REL_BUCKETS - 1)
    return np.where(n < exact, n, large).astype(np.int32)


def _bias_table(rel_bias_g, dilation):
    n = 3 * BLK
    dist = 2 * BLK - 1 - np.arange(n)
    valid = (dist >= 0) & (dist <= BLK)
    bucket = _t5_bucket(np.clip(dist, 0, BLK) * dilation)
    c = jnp.where(jnp.asarray(valid)[None, :], rel_bias_g.astype(F32)[bucket].T * LOG2E, MASKED)
    shifted = jnp.tile(c, (1, BLK))[:, :BLK * (n - 1)].reshape(HEADS_PER_GROUP, BLK, n - 1)
    return shifted[:, :, BLK - 1:].reshape(HEADS_PER_GROUP * BLK, 2 * BLK)


def _attn_kernel(cur_ref, prev_ref, bias_ref, o_ref, lse_ref):
    rg, rb = cur_ref.shape[1], cur_ref.shape[2] // BLK
    starts_sequence = pl.program_id(2) == 0
    lane_head = lax.broadcasted_iota(jnp.int32, (1, GROUP_W), 1) // HEAD_DIM
    scale = HEAD_DIM ** -0.5
    head_bf = [jnp.where(lane_head == h, scale, 0.0).astype(BF16) for h in range(HEADS_PER_GROUP)]
    key_is_prev = lax.broadcasted_iota(jnp.int32, (1, 2 * BLK), 1) < BLK
    nt = (((1,), (1,)), ((), ()))
    qc, kc_, vc_ = slice(0, GROUP_W), slice(GROUP_W, 2 * GROUP_W), slice(2 * GROUP_W, 3 * GROUP_W)

    def by_head(x):
        sel = x[(HEADS_PER_GROUP - 1) * BLK:]
        for h in range(HEADS_PER_GROUP - 2, -1, -1):
            sel = jnp.where(lane_head == h, x[h * BLK:(h + 1) * BLK], sel)
        return sel

    for r, j in [(r, j) for r in range(rg) for j in range(rb)]:
        rows = slice(j * BLK, (j + 1) * BLK)
        prev = prev_ref if j == 0 else cur_ref
        prows = slice(0, BLK) if j == 0 else slice((j - 1) * BLK, j * BLK)
        q = cur_ref[0, r, rows, qc]
        k = jnp.concatenate([prev[0, r, prows, kc_], cur_ref[0, r, rows, kc_]], axis=0)
        v = jnp.concatenate([prev[0, r, prows, vc_], cur_ref[0, r, rows, vc_]], axis=0)
        q_bd = jnp.concatenate([q * head_bf[h] for h in range(HEADS_PER_GROUP)], axis=0)
        s = lax.dot_general(q_bd, k, nt, preferred_element_type=F32) * LOG2E + bias_ref[...]
        if j == 0:
            s = jnp.where(jnp.logical_and(starts_sequence, key_is_prev), MASKED, s)
        m = jnp.max(s, axis=-1, keepdims=True)
        p = jnp.exp2(s - m)
        den = jnp.sum(p, axis=-1, keepdims=True)
        o = jnp.dot(p.astype(BF16), v, preferred_element_type=F32)
        den_h = jnp.broadcast_to(by_head(den), (BLK, GROUP_W))
        o_ref[0, r, rows, :] = (by_head(o) / den_h).astype(BF16)
        lse_ref[0, r, rows, :] = by_head(m) * LN2 + jnp.log(den_h)


ATT_SUBBLOCKS = 8


def _attention_group(a, bias, dilation, B, S):
    sd = S // dilation
    rb = min(ATT_SUBBLOCKS, sd // BLK)
    rg = min(ATT_SUBBLOCKS // rb, dilation)
    o, lse = pl.pallas_call(
        _attn_kernel,
        grid=(B, dilation // rg, sd // (rb * BLK)),
        in_specs=[pl.BlockSpec((1, rg, rb * BLK, QKV_G), lambda b, r, n: (b, r, n, 0)),
                  pl.BlockSpec((1, rg, BLK, QKV_G), lambda b, r, n: (b, r, jnp.maximum(n * rb - 1, 0), 0)),
                  _resident((HEADS_PER_GROUP * BLK, 2 * BLK))],
        out_specs=[pl.BlockSpec((1, rg, rb * BLK, GROUP_W), lambda b, r, n: (b, r, n, 0))] * 2,
        out_shape=[jax.ShapeDtypeStruct((B, dilation, sd, GROUP_W), BF16),
                   jax.ShapeDtypeStruct((B, dilation, sd, GROUP_W), F32)],
        compiler_params=_cparams("parallel", "parallel", "parallel"),
        name=f"attn_d{dilation}",
    )(a, a, bias)
    return o, lse


def _sigmoid(x):
    return 0.5 * jnp.tanh(0.5 * x) + 0.5


def _gelu(x):
    return x * (lax.erf(x * (2.0 ** -0.5)) + 1.0) * 0.5


def _token_major(src_ref, d, scr, slot, tm):
    if d == 1:
        return src_ref[0, 0].astype(F32)
    for r in range(d):
        piece = src_ref[0, r].astype(F32)
        scr[slot, pl.ds(r, tm // d, stride=d), :] = piece[:, :LANES]
        scr[slot + 1, pl.ds(r, tm // d, stride=d), :] = piece[:, LANES:]
    return jnp.concatenate([scr[slot], scr[slot + 1]], axis=1)


def _mix_kernel(x_ref, o1_ref, o2_ref, o3_ref, l1_ref, l2_ref, l3_ref, uv_ref, gl_ref,
                wa_hbm, wg_hbm, wo_hbm, wc_ref, bs_ref, lng_ref, lnb_ref, we_ref, h_ref, we_bf_ref, g_scr, t_scr,
                wa_ref, wg_ref, wo_ref, w_stage, w_sem):
    @pl.when(pl.program_id(0) == 0)
    def _():
        for w_hbm, w_bf in ((wa_hbm, wa_ref), (wg_hbm, wg_ref), (wo_hbm, wo_ref)):
            _load_weight_bf16(w_hbm, w_bf, w_stage, w_sem)

    we_bf_ref[...] = we_ref[...].astype(BF16)

    tm = x_ref.shape[0]
    dils = [d for _, d in ATT_GROUPS]
    o1, o2, o3 = [_token_major(ref, d, t_scr, 4 * i, tm) for i, (ref, d) in enumerate(zip((o1_ref, o2_ref, o3_ref), dils))]
    l1, l2, l3 = [_token_major(ref, d, t_scr, 4 * i + 2, tm) for i, (ref, d) in enumerate(zip((l1_ref, l2_ref, l3_ref), dils))]
    lm = jnp.maximum(jnp.maximum(l1, l2), l3)
    e1, e2, e3 = jnp.exp(l1 - lm), jnp.exp(l2 - lm), jnp.exp(l3 - lm)
    att = (e1 * o1 + e2 * o2 + e3 * o3) / (e1 + e2 + e3)
    y_att = jnp.dot(att.astype(BF16), wa_ref[...], preferred_element_type=F32)

    zu = _gelu(uv_ref[:, :GMLP_W].astype(F32))
    zv = _gelu(uv_ref[:, GMLP_W:].astype(F32))
    mu = jnp.mean(zv, axis=-1, keepdims=True)
    var = jnp.mean(jnp.square(zv - mu), axis=-1, keepdims=True)
    vn = (zv - mu) * lax.rsqrt(var + EPS) * lng_ref[...] + lnb_ref[...]
    low_half = lax.broadcasted_iota(jnp.int32, (CHUNK, 2 * GMLP_GD), 1) < GMLP_GD
    for c in range(tm // CHUNK):
        rows = slice(c * CHUNK, (c + 1) * CHUNK)
        for s in range(GMLP_W // (2 * GMLP_GD)):
            cols = slice(s * 2 * GMLP_GD, (s + 1) * 2 * GMLP_GD)
            v2 = vn[rows, cols]
            rhs = jnp.concatenate([jnp.where(low_half, v2, 0.0), jnp.where(low_half, 0.0, v2)], axis=0).astype(BF16)
            mixed = jnp.dot(wc_ref[s], rhs, preferred_element_type=F32) + bs_ref[:, cols]
            g_scr[rows, cols] = (zu[rows, cols] * mixed).astype(BF16)
    y_gm = jnp.dot(g_scr[...], wg_ref[...], preferred_element_type=F32)

    gate_a = _sigmoid(gl_ref[:, :D_MODEL].astype(F32))
    gate_g = _sigmoid(gl_ref[:, D_MODEL:].astype(F32))
    merged = (gate_a * y_att + gate_g * y_gm).astype(BF16)
    h_ref[...] = x_ref[...] + jnp.dot(merged, wo_ref[...], preferred_element_type=F32)


def _mix(x2, outs, lses, uv, gl, wa, wg, wo, wc2, bs, lng, lnb, w_expert, S):
    T = x2.shape[0]
    tm = TM_PROJ
    row = lambda w: pl.BlockSpec((tm, w), lambda i: (i, 0))
    att = [_plane_spec(d, tm, S // tm, GROUP_W) for _, d in ATT_GROUPS]
    we_spec = _expert_slice_spec(T // tm, w_expert.shape[2])
    return pl.pallas_call(
        _mix_kernel,
        grid=(T // tm,),
        in_specs=[row(D_MODEL)] + att + att + [row(2 * GMLP_W), row(N_BRANCH * D_MODEL)]
                 + [pl.BlockSpec(memory_space=pl.ANY)] * 3
                 + [_resident(wc2.shape), _resident(bs.shape), _resident(lng.shape), _resident(lnb.shape), we_spec],
        out_specs=[row(D_MODEL), we_spec],
        out_shape=[jax.ShapeDtypeStruct((T, D_MODEL), F32), jax.ShapeDtypeStruct(w_expert.shape, BF16)],
        scratch_shapes=[pltpu.VMEM((tm, GMLP_W), BF16), pltpu.VMEM((4 * N_DIL, tm, LANES), F32),
                        pltpu.VMEM(wa.shape, BF16), pltpu.VMEM(wg.shape, BF16), pltpu.VMEM(wo.shape, BF16)]
                       + _weight_stage(D_MODEL),
        compiler_params=_cparams("arbitrary"),
        name="mix",
    )(x2, *outs, *lses, uv, gl, wa, wg, wo, wc2, bs, lng, lnb, w_expert)


def _router_kernel(h_ref, g_ref, wr_ref, br_ref, upper_ref, eidx_ref, gate_ref, rank_ref, cnt_ref, carry):
    tm = h_ref.shape[0]

    @pl.when(pl.program_id(0) == 0)
    def _():
        carry[...] = jnp.zeros_like(carry)

    hn = _rms(h_ref[...], g_ref[...])
    hi = hn.astype(BF16)
    lo = (hn - hi.astype(F32)).astype(BF16)
    nt = (((1,), (1,)), ((), ()))
    by_hi = lax.dot_general(wr_ref[...], hi, nt, preferred_element_type=F32)
    by_lo = lax.dot_general(wr_ref[:N_EXPERTS, :], lo, nt, preferred_element_type=F32)
    logits = by_hi[:N_EXPERTS] + by_hi[N_EXPERTS:] + by_lo + br_ref[...]
    expert = lax.broadcasted_iota(jnp.int32, (N_EXPERTS, tm), 0)
    vals, hots = [], []
    l = logits
    for k in range(TOP_K):
        m = jnp.max(l, axis=0, keepdims=True)
        idx = jnp.min(jnp.where(l == m, expert, N_EXPERTS), axis=0, keepdims=True)
        hot = expert == idx
        eidx_ref[0, k:k + 1, :] = idx
        vals.append(m)
        hots.append(hot)
        l = jnp.where(hot, -jnp.inf, l)
    ex = [jnp.exp(v - vals[0]) for v in vals]
    tot = ex[0] + ex[1] + ex[2] + ex[3]
    for k in range(TOP_K):
        gate_ref[0, k:k + 1, :] = ex[k] / tot
    multi = jnp.zeros((N_EXPERTS, tm), F32)
    for hot in hots:
        multi = multi + hot.astype(F32)
    before = jnp.dot(multi.astype(BF16), upper_ref[...], preferred_element_type=F32) + carry[...]
    for k in range(TOP_K):
        rank_ref[0, k:k + 1, :] = jnp.sum(jnp.where(hots[k], before, 0.0), axis=0, keepdims=True).astype(jnp.int32)
    carry[...] += jnp.sum(multi, axis=1, keepdims=True)
    cnt_ref[...] = carry[...]


def _router(h1, g, wr_t, br_col):
    T = h1.shape[0]
    tm = TM_PROJ
    upper = jnp.asarray(np.triu(np.ones((tm, tm), np.float32), k=1), BF16)
    k_rows = pl.BlockSpec((1, TOP_K, tm), lambda i: (i, 0, 0))
    k_shape = lambda dt: jax.ShapeDtypeStruct((T // tm, TOP_K, tm), dt)
    return pl.pallas_call(
        _router_kernel,
        grid=(T // tm,),
        in_specs=[pl.BlockSpec((tm, D_MODEL), lambda i: (i, 0)), _resident((1, D_MODEL)),
                  _resident((2 * N_EXPERTS, D_MODEL)), _resident((N_EXPERTS, 1)), _resident((tm, tm))],
        out_specs=[k_rows, k_rows, k_rows, pl.BlockSpec((N_EXPERTS, 1), lambda i: (0, 0))],
        out_shape=[k_shape(jnp.int32), k_shape(F32), k_shape(jnp.int32),
                   jax.ShapeDtypeStruct((N_EXPERTS, 1), F32)],
        scratch_shapes=[pltpu.VMEM((N_EXPERTS, 1), F32)],
        compiler_params=_cparams("arbitrary"),
        name="router",
    )(h1, g, wr_t, br_col, upper)


def _to_row_tiles(ref, lead, value):
    n = value.shape[0]
    for c in range(ROW_SUBLANES):
        ref[(*lead, pl.ds(c, n, stride=ROW_SUBLANES), slice(None))] = value[:, c * LANES:(c + 1) * LANES]


def _from_row_tiles(ref, lead, first, n):
    return jnp.concatenate(
        [ref[(*lead, pl.ds(first * ROW_SUBLANES + c, n, stride=ROW_SUBLANES), slice(None))] for c in range(ROW_SUBLANES)],
        axis=1)


def _tile_rows(idx, n=1):
    return pl.ds(pl.multiple_of(idx * ROW_SUBLANES, ROW_SUBLANES), n * ROW_SUBLANES)


def _dispatch_kernel(last_ref, nv_ref, dest_ref, h_ref, g_ref, xs_ref, stage_ref, buf, sem, zero_sem, stage_sem):
    tm = h_ref.shape[0]
    n_blocks = xs_ref.shape[0] // (TM_EXP * ROW_SUBLANES)
    i = pl.program_id(0)
    slot = lax.rem(i, 2)

    @pl.when(i == 0)
    def _():
        buf[1] = jnp.zeros(buf.shape[1:], F32)

        def zero_block(b):
            for part in range(TM_EXP // tm):
                pltpu.make_async_copy(buf.at[1], xs_ref.at[_tile_rows(b * TM_EXP + part * tm, tm)], zero_sem).start()

        def zero_done():
            for part in range(TM_EXP // tm):
                pltpu.make_async_copy(buf.at[1], xs_ref.at[_tile_rows(0, tm)], zero_sem).wait()

        for e in range(N_EXPERTS):
            zero_block(last_ref[e])
        lax.fori_loop(nv_ref[0], n_blocks, lambda b, c: (zero_block(b), c)[1], 0)
        for e in range(N_EXPERTS):
            zero_done()
        lax.fori_loop(nv_ref[0], n_blocks, lambda b, c: (zero_done(), c)[1], 0)

    _to_row_tiles(buf, (slot,), _rms(h_ref[...], g_ref[...]))

    stage_copy = pltpu.make_async_copy(buf.at[slot], stage_ref.at[_tile_rows(i * tm, tm)], stage_sem)
    stage_copy.start()

    def issue_from_vmem(t, carry):
        for k in range(TOP_K // 2):
            d = dest_ref[0, 0, k * tm + t]
            pltpu.make_async_copy(buf.at[slot, _tile_rows(t)], xs_ref.at[_tile_rows(d)],
                                  sem.at[slot]).start(priority=k % 2)
        return carry

    lax.fori_loop(0, tm, issue_from_vmem, 0, unroll=8)
    stage_copy.wait()

    def issue_from_hbm(t, carry):
        for k in range(TOP_K // 2, TOP_K):
            d = dest_ref[0, 0, k * tm + t]
            pltpu.make_async_copy(stage_ref.at[_tile_rows(i * tm + t)], xs_ref.at[_tile_rows(d)],
                                  sem.at[slot]).start(priority=k % 2)
        return carry

    lax.fori_loop(0, tm, issue_from_hbm, 0, unroll=8)

    def wait_slot(s):
        for _ in range(TOP_K):
            pltpu.make_async_copy(buf.at[s], xs_ref.at[_tile_rows(0, tm)], sem.at[s]).wait()

    @pl.when(i > 0)
    def _():
        wait_slot(1 - slot)

    @pl.when(i == pl.num_programs(0) - 1)
    def _():
        wait_slot(slot)


def _dispatch(last_block, n_valid, dest3, h1, g, n_slots):
    T = h1.shape[0]
    tm = TM_TOK
    grid_spec = pltpu.PrefetchScalarGridSpec(
        num_scalar_prefetch=2,
        grid=(T // tm,),
        in_specs=[pl.BlockSpec((1, 1, TOP_K * tm), lambda i, lb, nv: (i, 0, 0), memory_space=pltpu.SMEM),
                  pl.BlockSpec((tm, D_MODEL), lambda i, lb, nv: (i, 0)),
                  pl.BlockSpec((1, D_MODEL), lambda i, lb, nv: (0, 0), pipeline_mode=pl.Buffered(1))],
        out_specs=[pl.BlockSpec(memory_space=pl.ANY)] * 2,
        scratch_shapes=[pltpu.VMEM((2, tm * ROW_SUBLANES, LANES), F32), pltpu.SemaphoreType.DMA((2,)),
                        pltpu.SemaphoreType.DMA(()), pltpu.SemaphoreType.DMA(())],
    )
    xs, _ = pl.pallas_call(
        _dispatch_kernel,
        grid_spec=grid_spec,
        out_shape=[jax.ShapeDtypeStruct((n_slots * ROW_SUBLANES, LANES), F32),
                   jax.ShapeDtypeStruct((T * ROW_SUBLANES, LANES), F32)],
        compiler_params=_cparams("arbitrary"),
        name="dispatch",
    )(last_block, n_valid, dest3, h1, g)
    return xs


def _experts_kernel(be_ref, nv_ref, xs_hbm, wgu_ref, bgu_ref, wd_ref, bd_ref, ys_hbm, xbuf, ybuf, xsem, ysem):
    del be_ref
    tm = xbuf.shape[1]
    b = pl.program_id(0)
    n_valid = nv_ref[0]
    slot = lax.rem(b, 2)

    def x_copies(blk, s):
        return [pltpu.make_async_copy(xs_hbm.at[pl.ds(blk * tm, tm), c, :],
                                      xbuf.at[s, :, pl.ds(c * LANES, LANES)], xsem.at[s]) for c in range(ROW_SUBLANES)]

    def y_copies(blk, s):
        return [pltpu.make_async_copy(ybuf.at[s, :, pl.ds(c * LANES, LANES)],
                                      ys_hbm.at[pl.ds(blk * tm, tm), c, :], ysem.at[s]) for c in range(ROW_SUBLANES)]

    @pl.when(jnp.logical_and(b == 0, n_valid > 0))
    def _():
        for cp in x_copies(0, 0):
            cp.start()

    @pl.when(b + 1 < n_valid)
    def _():
        for cp in x_copies(b + 1, 1 - slot):
            cp.start()

    @pl.when(b < n_valid)
    def _():
        for cp in x_copies(b, slot):
            cp.wait()
        x = xbuf[slot].astype(BF16)
        gu = jnp.dot(x, wgu_ref[0], preferred_element_type=F32) + bgu_ref[0]
        glu = jnp.minimum(gu[:, :D_EXPERT], SWIGLU_LIMIT)
        lin = jnp.clip(gu[:, D_EXPERT:], -SWIGLU_LIMIT, SWIGLU_LIMIT)
        act = glu * jax.nn.sigmoid(SWIGLU_ALPHA * glu) * (lin + 1.0)
        ybuf[slot] = jnp.dot(act.astype(BF16), wd_ref[0], preferred_element_type=F32) + bd_ref[0]

    @pl.when(b >= n_valid)
    def _():
        ybuf[slot] = jnp.zeros(ybuf.shape[1:], F32)

    for cp in y_copies(b, slot):
        cp.start()

    @pl.when(b > 0)
    def _():
        for cp in y_copies(b - 1, 1 - slot):
            cp.wait()

    @pl.when(b == pl.num_programs(0) - 1)
    def _():
        for cp in y_copies(b, slot):
            cp.wait()


def _experts(block_expert, n_valid, xs, wgu, bgu, wd, bd):
    tm = TM_EXP
    n_slots = xs.shape[0] // ROW_SUBLANES
    n_blocks = n_slots // tm
    any_space = pl.BlockSpec(memory_space=pl.ANY)
    grid_spec = pltpu.PrefetchScalarGridSpec(
        num_scalar_prefetch=2,
        grid=(n_blocks,),
        in_specs=[any_space,
                  pl.BlockSpec((1, D_MODEL, 2 * D_EXPERT), lambda b, be, nv: (be[b], 0, 0)),
                  pl.BlockSpec((1, 1, 2 * D_EXPERT), lambda b, be, nv: (be[b], 0, 0)),
                  pl.BlockSpec((1, D_EXPERT, D_MODEL), lambda b, be, nv: (be[b], 0, 0)),
                  pl.BlockSpec((1, 1, D_MODEL), lambda b, be, nv: (be[b], 0, 0))],
        out_specs=any_space,
        scratch_shapes=[pltpu.VMEM((2, tm, D_MODEL), F32), pltpu.VMEM((2, tm, D_MODEL), F32),
                        pltpu.SemaphoreType.DMA((2,)), pltpu.SemaphoreType.DMA((2,))],
    )
    ys = pl.pallas_call(
        _experts_kernel,
        grid_spec=grid_spec,
        out_shape=jax.ShapeDtypeStruct((n_slots, ROW_SUBLANES, LANES), F32),
        compiler_params=_cparams("arbitrary"),
        name="experts",
    )(block_expert, n_valid, xs.reshape(n_slots, ROW_SUBLANES, LANES), wgu, bgu, wd, bd)
    return ys.reshape(xs.shape)


GATHER_SLOTS = 3


def _combine_kernel(*refs, n_tiles):
    ahead = GATHER_SLOTS - 1
    prime_refs, refs = refs[:ahead], refs[ahead:]
    (ahead_dest_ref, h_ref, gate_ref, p_ref, gp_ref, gf_ref, wpg_hbm, wpp_hbm, ys_ref, o_ref), refs = refs[:10], refs[10:]
    bufs, (sem, wpg_ref, wpp_ref, w_stage, w_sem) = refs[:GATHER_SLOTS], refs[GATHER_SLOTS:]
    tm = h_ref.shape[0]
    i = pl.program_id(0)

    def row_copy(dref, t, k, s):
        d = dref[0, 0, k * tm + t]
        return pltpu.make_async_copy(ys_ref.at[_tile_rows(d)], bufs[s].at[_tile_rows(k * tm + t)], sem.at[s])

    def wait_slot(s):
        for _ in range(TOP_K):
            pltpu.make_async_copy(ys_ref.at[_tile_rows(0, tm)], bufs[s].at[_tile_rows(0, tm)], sem.at[s]).wait()

    @pl.when(i == 0)
    def _():
        _load_weight_bf16(wpg_hbm, wpg_ref, w_stage, w_sem)
        _load_weight_bf16(wpp_hbm, wpp_ref, w_stage, w_sem)

        for s, dref in enumerate(prime_refs):
            def issue(t, carry, s=s, dref=dref):
                for k in range(TOP_K):
                    row_copy(dref, t, k, s).start(priority=k % 2)
                return carry
            lax.fori_loop(0, tm, issue, 0, unroll=8)

    def step(s):
        wait_slot(s)
        for t in range(tm):
            for k in range(TOP_K):
                row_copy(ahead_dest_ref, t, k, (s + ahead) % GATHER_SLOTS).start(priority=k % 2)
        proj = jnp.dot(p_ref[...].astype(BF16), wpp_ref[...], preferred_element_type=F32)
        h = h_ref[...]
        for k in range(TOP_K):
            h = h + gate_ref[:, k:k + 1] * _from_row_tiles(bufs[s], (), k * tm, tm)
        ple_gate = jax.nn.sigmoid(jnp.dot(_rms(h, gp_ref[...]).astype(BF16), wpg_ref[...], preferred_element_type=F32))
        h = h + ple_gate * proj
        o_ref[...] = _rms(h, gf_ref[...])

    for s in range(GATHER_SLOTS):
        pl.when(lax.rem(i, GATHER_SLOTS) == s)(functools.partial(step, s))

    @pl.when(i == n_tiles - 1)
    def _():
        for extra in range(ahead):
            wait_slot((n_tiles + extra) % GATHER_SLOTS)


def _combine(dest3, h1, gate, p2, gp, wpg, wpp, gf, ys):
    T = h1.shape[0]
    tm = TM_TOK
    n_tiles = T // tm
    row = lambda w: pl.BlockSpec((tm, w), lambda i: (i, 0))
    dest_spec = lambda ahead: pl.BlockSpec((1, 1, TOP_K * tm), lambda i: (jnp.minimum(i + ahead, n_tiles - 1), 0, 0),
                                           memory_space=pltpu.SMEM)
    gather_buf = pltpu.VMEM((TOP_K * tm * ROW_SUBLANES, LANES), F32)
    ahead = GATHER_SLOTS - 1
    assert n_tiles > ahead
    return pl.pallas_call(
        functools.partial(_combine_kernel, n_tiles=n_tiles),
        grid=(n_tiles,),
        in_specs=[dest_spec(a) for a in range(ahead)] + [dest_spec(ahead),
                  row(D_MODEL), row(TOP_K), row(PLE_DIM), _resident((1, D_MODEL)), _resident((1, D_MODEL))]
                 + [pl.BlockSpec(memory_space=pl.ANY)] * 3,
        out_specs=row(D_MODEL),
        out_shape=jax.ShapeDtypeStruct((T, D_MODEL), F32),
        scratch_shapes=[gather_buf] * GATHER_SLOTS + [pltpu.SemaphoreType.DMA((GATHER_SLOTS,)),
                        pltpu.VMEM(wpg.shape, BF16), pltpu.VMEM(wpp.shape, BF16)] + _weight_stage(D_MODEL),
        compiler_params=_cparams("arbitrary"),
        name="combine",
    )(*([dest3] * (ahead + 1)), h1, gate, p2, gp, gf, wpg, wpp, ys)


def _layer(h, p_i, g_mix, w_in, rel_bias, w_att_out, ln_v_g, ln_v_b, w_spatial, b_spatial, w_gmlp_out, w_out,
           g_moe, w_router, b_router, w_gate_up, b_gate_up, w_down, b_down, g_ple, w_ple_gate, w_ple_proj,
           g_final, B, S):
    T = B * S
    row = lambda v: v.reshape(1, -1).astype(F32)

    assert S % TM_PROJ == 0
    *att_in, uv, gl, wgu_bf = _in_proj(h, row(g_mix), w_in.astype(F32), w_gate_up.astype(F32), B, S)

    outs, lses = [], []
    for g, (window, dilation) in enumerate(ATT_GROUPS):
        assert window // dilation == BLK and S % (dilation * BLK) == 0
        bias = _bias_table(rel_bias[:, g * HEADS_PER_GROUP:(g + 1) * HEADS_PER_GROUP], dilation)
        o, lse = _attention_group(att_in[g], bias, dilation, B, S)
        outs.append(o)
        lses.append(lse)

    causal = jnp.asarray(np.tril(np.ones((CHUNK, CHUNK), np.float32)))
    w_c = (w_spatial.astype(F32) * causal[None]).astype(BF16)
    wc2 = jnp.concatenate([w_c[0::2], w_c[1::2]], axis=2)
    bs = jnp.repeat(b_spatial.astype(F32).T, GMLP_GD, axis=1)
    h1, wd_bf = _mix(h, outs, lses, uv, gl, w_att_out.astype(F32), w_gmlp_out.astype(F32), w_out.astype(F32),
                     wc2, bs, row(ln_v_g), row(ln_v_b), w_down.astype(F32), S)

    wr_hi = w_router.astype(BF16)
    wr_lo = (w_router.astype(F32) - wr_hi.astype(F32)).astype(BF16)
    eidx, gate, rank, counts = _router(h1, row(g_moe), jnp.concatenate([wr_hi, wr_lo], axis=1).T,
                                       b_router.reshape(-1, 1).astype(F32))
    cnt = counts[:, 0].astype(jnp.int32)
    blk_counts = (cnt + TM_EXP - 1) // TM_EXP
    blk_end = jnp.cumsum(blk_counts)
    pad_start = (blk_end - blk_counts) * TM_EXP
    n_blocks = T * TOP_K // TM_EXP + N_EXPERTS
    n_valid = blk_end[-1:].astype(jnp.int32)
    blk = jnp.minimum(jnp.arange(n_blocks, dtype=jnp.int32), n_valid[0] - 1)
    block_expert = jnp.minimum(jnp.sum((blk_end[None, :] <= blk[:, None]).astype(jnp.int32), axis=1), N_EXPERTS - 1)
    expert_ids = jnp.arange(N_EXPERTS, dtype=jnp.int32)
    dest = rank + jnp.sum(jnp.where(eidx[..., None] == expert_ids, pad_start, 0), axis=-1)
    split = TM_PROJ // TM_TOK
    dest3 = jnp.transpose(dest.reshape(T // TM_PROJ, TOP_K, split, TM_TOK), (0, 2, 1, 3)).reshape(
        T // TM_TOK, 1, TOP_K * TM_TOK)
    gate = jnp.transpose(gate, (0, 2, 1)).reshape(T, TOP_K)

    last_block = jnp.maximum(blk_end - 1, 0).astype(jnp.int32)
    xs = _dispatch(last_block, n_valid, dest3, h1, row(g_moe), n_blocks * TM_EXP)
    ys = _experts(block_expert, n_valid, xs, wgu_bf, b_gate_up.reshape(N_EXPERTS, 1, -1).astype(F32),
                  wd_bf, b_down.reshape(N_EXPERTS, 1, -1).astype(F32))
    return _combine(dest3, h1, gate, p_i, row(g_ple), w_ple_gate.astype(F32), w_ple_proj.astype(F32),
                    row(g_final), ys)


def kernel(x, p, g_mix, w_in, rel_bias, w_att_out, ln_v_g, ln_v_b, w_spatial, b_spatial, w_gmlp_out, w_out, g_moe, w_router, b_router, w_gate_up, b_gate_up, w_down, b_down, g_ple, w_ple_gate, w_ple_proj, g_final):
    B, S, D = x.shape
    depth = p.shape[0]
    assert depth == 1, "the final RMSNorm is fused into the (single) layer's last kernel"
    out = _layer(x.reshape(B * S, D), p[0].reshape(B * S, PLE_DIM), g_mix[0], w_in[0], rel_bias, w_att_out[0],
                 ln_v_g[0], ln_v_b[0], w_spatial[0], b_spatial[0], w_gmlp_out[0], w_out[0], g_moe[0], w_router[0],
                 b_router[0], w_gate_up[0], b_gate_up[0], w_down[0], b_down[0], g_ple[0], w_ple_gate[0],
                 w_ple_proj[0], g_final, B, S)
    return out.reshape(B, S, D)
```

```python
import functools

import jax
import jax.numpy as jnp
import numpy as np
from jax import lax
from jax.experimental import pallas as pl
from jax.experimental.pallas import tpu as pltpu

F32 = jnp.float32
BF16 = jnp.bfloat16

D_MODEL = 1024
HEAD_DIM = 64
ATT_GROUPS = ((128, 1), (512, 4), (2048, 16))
HEADS_PER_GROUP = 4
GROUP_W = HEADS_PER_GROUP * HEAD_DIM
N_DIL = len(ATT_GROUPS)
ATT_W = N_DIL * GROUP_W
BLK = 128
REL_BUCKETS = 32
REL_MAX_DIST = 2048
CHUNK = 128
GMLP_W = 768
GMLP_GD = 64
N_BRANCH = 2
IN_W = 3 * ATT_W + 2 * GMLP_W + N_BRANCH * D_MODEL
N_EXPERTS = 32
TOP_K = 4
D_EXPERT = D_MODEL
SWIGLU_LIMIT = 7.0
SWIGLU_ALPHA = 1.702
PLE_DIM = 256
EPS = 1e-6
MASKED = -1e30
LOG2E = float(np.log2(np.e))
LN2 = float(np.log(2.0))

QKV_G = 3 * GROUP_W

LANES = 128
ROW_SUBLANES = D_MODEL // LANES
assert ROW_SUBLANES == 8
MXU_N = 256
VMEM_LIMIT = 56 * 1024 * 1024

TM_PROJ = 512
TM_TOK = 256
TM_EXP = 512
assert TM_EXP % TM_TOK == 0


def _cparams(*sem):
    return pltpu.CompilerParams(dimension_semantics=sem, vmem_limit_bytes=VMEM_LIMIT)


def _resident(shape):
    nd = len(shape)
    return pl.BlockSpec(shape, lambda *_: (0,) * nd, pipeline_mode=pl.Buffered(1))


def _rms(x, g):
    return x * lax.rsqrt(jnp.mean(x * x, axis=-1, keepdims=True) + EPS) * g


def _load_weight_bf16(w_hbm, w_bf, stage, sem):
    rows = stage.shape[1]
    n_chunks = w_hbm.shape[0] // rows
    assert n_chunks * rows == w_hbm.shape[0] and stage.shape[2] == w_hbm.shape[1]

    def chunk(c):
        return pltpu.make_async_copy(w_hbm.at[pl.ds(c * rows, rows)], stage.at[c % 2], sem.at[c % 2])

    chunk(0).start()
    for c in range(n_chunks):
        if c + 1 < n_chunks:
            chunk(c + 1).start()
        chunk(c).wait()
        w_bf[c * rows:(c + 1) * rows, :] = stage[c % 2].astype(BF16)


def _expert_slice_spec(n_steps, width):
    per_expert = n_steps // N_EXPERTS
    assert per_expert * N_EXPERTS == n_steps and D_MODEL % per_expert == 0
    return pl.BlockSpec((1, D_MODEL // per_expert, width), lambda i: (i // per_expert, i % per_expert, 0))


def _inproj_kernel(x0_ref, xn_ref, g_ref, w_hbm, we_ref, a1_ref, a2_ref, a3_ref, uv_ref, gl_ref, we_bf_ref,
                   scr, w_ref, w_stage, w_sem, n_scr, n_tmp):
    i = pl.program_id(0)
    slot = lax.rem(i, 2)

    @pl.when(i == 0)
    def _():
        _load_weight_bf16(w_hbm, w_ref, w_stage, w_sem)
        n_scr[0] = _rms(x0_ref[...], g_ref[...]).astype(BF16)

    tm = xn_ref.shape[0]
    n = n_scr[slot]
    att_refs = (a1_ref, a2_ref, a3_ref)
    n_att, n_uv = 3 * ATT_W // MXU_N, 2 * GMLP_W // MXU_N
    n_pieces = 16
    side_rows, we_rows = tm // n_pieces, we_ref.shape[1] // n_pieces
    for c in range(IN_W // MXU_N):
        if c < n_pieces:
            we_bf_ref[0, c * we_rows:(c + 1) * we_rows, :] = we_ref[0, c * we_rows:(c + 1) * we_rows, :].astype(BF16)
            rows = slice(c * side_rows, (c + 1) * side_rows)
            n_tmp[rows, :] = _rms(xn_ref[rows, :], g_ref[...]).astype(BF16)
        z = jnp.dot(n, w_ref[:, c * MXU_N:(c + 1) * MXU_N], preferred_element_type=F32)
        if c < n_att:
            which, g = divmod(c, N_DIL)
            d = ATT_GROUPS[g][1]
            dst = att_refs[g]
            cols = slice(which * GROUP_W, (which + 1) * GROUP_W)
            if d == 1:
                dst[0, 0, :, cols] = z.astype(BF16)
                continue
            scr[0] = z[:, :LANES]
            scr[1] = z[:, LANES:]
            for r in range(d):
                zr = jnp.concatenate([scr[0, pl.ds(r, tm // d, stride=d), :],
                                      scr[1, pl.ds(r, tm // d, stride=d), :]], axis=1)
                dst[0, r, :, cols] = zr.astype(BF16)
        elif c < n_att + n_uv:
            uv_ref[:, (c - n_att) * MXU_N:(c - n_att + 1) * MXU_N] = z.astype(BF16)
        else:
            gl_ref[:, (c - n_att - n_uv) * MXU_N:(c - n_att - n_uv + 1) * MXU_N] = z.astype(BF16)
    n_scr[1 - slot] = n_tmp[...]


def _plane_spec(d, tm, tiles_per_seq, width):
    return pl.BlockSpec((1, d, tm // d, width), lambda i: (i // tiles_per_seq, 0, i % tiles_per_seq, 0))


W_STAGE_ROWS = 128


def _weight_stage(width):
    return [pltpu.VMEM((2, W_STAGE_ROWS, width), F32), pltpu.SemaphoreType.DMA((2,))]


def _in_proj(x2, g, w, w_expert, B, S):
    T = x2.shape[0]
    tm = TM_PROJ
    row = lambda w: pl.BlockSpec((tm, w), lambda i: (i, 0))
    dils = [d for _, d in ATT_GROUPS]
    we_spec = _expert_slice_spec(T // tm, w_expert.shape[2])
    n_steps = T // tm
    first_tile = pl.BlockSpec((tm, D_MODEL), lambda i: (0, 0), pipeline_mode=pl.Buffered(1))
    next_tile = pl.BlockSpec((tm, D_MODEL), lambda i: (jnp.minimum(i + 1, n_steps - 1), 0))
    return pl.pallas_call(
        _inproj_kernel,
        grid=(n_steps,),
        in_specs=[first_tile, next_tile, _resident((1, D_MODEL)), pl.BlockSpec(memory_space=pl.ANY), we_spec],
        out_specs=[_plane_spec(d, tm, S // tm, QKV_G) for d in dils] + [row(2 * GMLP_W), row(N_BRANCH * D_MODEL), we_spec],
        out_shape=[jax.ShapeDtypeStruct((B, d, S // d, QKV_G), BF16) for d in dils]
                  + [jax.ShapeDtypeStruct((T, 2 * GMLP_W), BF16),
                     jax.ShapeDtypeStruct((T, N_BRANCH * D_MODEL), BF16),
                     jax.ShapeDtypeStruct(w_expert.shape, BF16)],
        scratch_shapes=[pltpu.VMEM((2, tm, LANES), F32), pltpu.VMEM((D_MODEL, IN_W), BF16)] + _weight_stage(IN_W)
                       + [pltpu.VMEM((2, tm, D_MODEL), BF16), pltpu.VMEM((tm, D_MODEL), BF16)],
        compiler_params=_cparams("arbitrary"),
        name="in_proj",
    )(x2, x2, g, w, w_expert)


def _t5_bucket(n):
    exact = REL_BUCKETS // 2
    nf = np.maximum(n, 1).astype(np.float32)
    large = exact + (np.log(nf / exact) / np.log(REL_MAX_DIST / exact) * (REL_BUCKETS - exact)).astype(np.int32)
    large = np.minimum(large, REL_BUCKETS - 1)
    return np.where(n < exact, n, large).astype(np.int32)


def _bias_table(rel_bias_g, dilation):
    n = 3 * BLK
    dist = 2 * BLK - 1 - np.arange(n)
    valid = (dist >= 0) & (dist <= BLK)
    bucket = _t5_bucket(np.clip(dist, 0, BLK) * dilation)
    c = jnp.where(jnp.asarray(valid)[None, :], rel_bias_g.astype(F32)[bucket].T * LOG2E, MASKED)
    shifted = jnp.tile(c, (1, BLK))[:, :BLK * (n - 1)].reshape(HEADS_PER_GROUP, BLK, n - 1)
    return shifted[:, :, BLK - 1:].reshape(HEADS_PER_GROUP * BLK, 2 * BLK)


def _attn_kernel(cur_ref, prev_ref, bias_ref, o_ref, lse_ref):
    rg, rb = cur_ref.shape[1], cur_ref.shape[2] // BLK
    starts_sequence = pl.program_id(2) == 0
    lane_head = lax.broadcasted_iota(jnp.int32, (1, GROUP_W), 1) // HEAD_DIM
    scale = HEAD_DIM ** -0.5
    head_bf = [jnp.where(lane_head == h, scale, 0.0).astype(BF16) for h in range(HEADS_PER_GROUP)]
    key_is_prev = lax.broadcasted_iota(jnp.int32, (1, 2 * BLK), 1) < BLK
    nt = (((1,), (1,)), ((), ()))
    qc, kc_, vc_ = slice(0, GROUP_W), slice(GROUP_W, 2 * GROUP_W), slice(2 * GROUP_W, 3 * GROUP_W)

    def by_head(x):
        sel = x[(HEADS_PER_GROUP - 1) * BLK:]
        for h in range(HEADS_PER_GROUP - 2, -1, -1):
            sel = jnp.where(lane_head == h, x[h * BLK:(h + 1) * BLK], sel)
        return sel

    for r, j in [(r, j) for r in range(rg) for j in range(rb)]:
        rows = slice(j * BLK, (j + 1) * BLK)
        prev = prev_ref if j == 0 else cur_ref
        prows = slice(0, BLK) if j == 0 else slice((j - 1) * BLK, j * BLK)
        q = cur_ref[0, r, rows, qc]
        k = jnp.concatenate([prev[0, r, prows, kc_], cur_ref[0, r, rows, kc_]], axis=0)
        v = jnp.concatenate([prev[0, r, prows, vc_], cur_ref[0, r, rows, vc_]], axis=0)
        q_bd = jnp.concatenate([q * head_bf[h] for h in range(HEADS_PER_GROUP)], axis=0)
        s = lax.dot_general(q_bd, k, nt, preferred_element_type=F32) * LOG2E + bias_ref[...]
        if j == 0:
            s = jnp.where(jnp.logical_and(starts_sequence, key_is_prev), MASKED, s)
        m = jnp.max(s, axis=-1, keepdims=True)
        p = jnp.exp2(s - m)
        den = jnp.sum(p, axis=-1, keepdims=True)
        o = jnp.dot(p.astype(BF16), v, preferred_element_type=F32)
        den_h = jnp.broadcast_to(by_head(den), (BLK, GROUP_W))
        o_ref[0, r, rows, :] = (by_head(o) / den_h).astype(BF16)
        lse_ref[0, r, rows, :] = by_head(m) * LN2 + jnp.log(den_h)


ATT_SUBBLOCKS = 8


def _attention_group(a, bias, dilation, B, S):
    sd = S // dilation
    rb = min(ATT_SUBBLOCKS, sd // BLK)
    rg = min(ATT_SUBBLOCKS // rb, dilation)
    o, lse = pl.pallas_call(
        _attn_kernel,
        grid=(B, dilation // rg, sd // (rb * BLK)),
        in_specs=[pl.BlockSpec((1, rg, rb * BLK, QKV_G), lambda b, r, n: (b, r, n, 0)),
                  pl.BlockSpec((1, rg, BLK, QKV_G), lambda b, r, n: (b, r, jnp.maximum(n * rb - 1, 0), 0)),
                  _resident((HEADS_PER_GROUP * BLK, 2 * BLK))],
        out_specs=[pl.BlockSpec((1, rg, rb * BLK, GROUP_W), lambda b, r, n: (b, r, n, 0))] * 2,
        out_shape=[jax.ShapeDtypeStruct((B, dilation, sd, GROUP_W), BF16),
                   jax.ShapeDtypeStruct((B, dilation, sd, GROUP_W), F32)],
        compiler_params=_cparams("parallel", "parallel", "parallel"),
        name=f"attn_d{dilation}",
    )(a, a, bias)
    return o, lse


def _sigmoid(x):
    return 0.5 * jnp.tanh(0.5 * x) + 0.5


def _gelu(x):
    return x * (lax.erf(x * (2.0 ** -0.5)) + 1.0) * 0.5


def _token_major(src_ref, d, scr, slot, tm):
    if d == 1:
        return src_ref[0, 0].astype(F32)
    for r in range(d):
        piece = src_ref[0, r].astype(F32)
        scr[slot, pl.ds(r, tm // d, stride=d), :] = piece[:, :LANES]
        scr[slot + 1, pl.ds(r, tm // d, stride=d), :] = piece[:, LANES:]
    return jnp.concatenate([scr[slot], scr[slot + 1]], axis=1)


def _mix_kernel(x_ref, o1_ref, o2_ref, o3_ref, l1_ref, l2_ref, l3_ref, uv_ref, gl_ref,
                wa_hbm, wg_hbm, wo_hbm, wc_ref, bs_ref, lng_ref, lnb_ref, we_ref, h_ref, we_bf_ref, g_scr, t_scr,
                wa_ref, wg_ref, wo_ref, w_stage, w_sem):
    @pl.when(pl.program_id(0) == 0)
    def _():
        for w_hbm, w_bf in ((wa_hbm, wa_ref), (wg_hbm, wg_ref), (wo_hbm, wo_ref)):
            _load_weight_bf16(w_hbm, w_bf, w_stage, w_sem)

    we_bf_ref[...] = we_ref[...].astype(BF16)

    tm = x_ref.shape[0]
    dils = [d for _, d in ATT_GROUPS]
    o1, o2, o3 = [_token_major(ref, d, t_scr, 4 * i, tm) for i, (ref, d) in enumerate(zip((o1_ref, o2_ref, o3_ref), dils))]
    l1, l2, l3 = [_token_major(ref, d, t_scr, 4 * i + 2, tm) for i, (ref, d) in enumerate(zip((l1_ref, l2_ref, l3_ref), dils))]
    lm = jnp.maximum(jnp.maximum(l1, l2), l3)
    e1, e2, e3 = jnp.exp(l1 - lm), jnp.exp(l2 - lm), jnp.exp(l3 - lm)
    att = (e1 * o1 + e2 * o2 + e3 * o3) / (e1 + e2 + e3)
    y_att = jnp.dot(att.astype(BF16), wa_ref[...], preferred_element_type=F32)

    zu = _gelu(uv_ref[:, :GMLP_W].astype(F32))
    zv = _gelu(uv_ref[:, GMLP_W:].astype(F32))
    mu = jnp.mean(zv, axis=-1, keepdims=True)
    var = jnp.mean(jnp.square(zv - mu), axis=-1, keepdims=True)
    vn = (zv - mu) * lax.rsqrt(var + EPS) * lng_ref[...] + lnb_ref[...]
    low_half = lax.broadcasted_iota(jnp.int32, (CHUNK, 2 * GMLP_GD), 1) < GMLP_GD
    for c in range(tm // CHUNK):
        rows = slice(c * CHUNK, (c + 1) * CHUNK)
        for s in range(GMLP_W // (2 * GMLP_GD)):
            cols = slice(s * 2 * GMLP_GD, (s + 1) * 2 * GMLP_GD)
            v2 = vn[rows, cols]
            rhs = jnp.concatenate([jnp.where(low_half, v2, 0.0), jnp.where(low_half, 0.0, v2)], axis=0).astype(BF16)
            mixed = jnp.dot(wc_ref[s], rhs, preferred_element_type=F32) + bs_ref[:, cols]
            g_scr[rows, cols] = (zu[rows, cols] * mixed).astype(BF16)
    y_gm = jnp.dot(g_scr[...], wg_ref[...], preferred_element_type=F32)

    gate_a = _sigmoid(gl_ref[:, :D_MODEL].astype(F32))
    gate_g = _sigmoid(gl_ref[:, D_MODEL:].astype(F32))
    merged = (gate_a * y_att + gate_g * y_gm).astype(BF16)
    h_ref[...] = x_ref[...] + jnp.dot(merged, wo_ref[...], preferred_element_type=F32)


def _mix(x2, outs, lses, uv, gl, wa, wg, wo, wc2, bs, lng, lnb, w_expert, S):
    T = x2.shape[0]
    tm = TM_PROJ
    row = lambda w: pl.BlockSpec((tm, w), lambda i: (i, 0))
    att = [_plane_spec(d, tm, S // tm, GROUP_W) for _, d in ATT_GROUPS]
    we_spec = _expert_slice_spec(T // tm, w_expert.shape[2])
    return pl.pallas_call(
        _mix_kernel,
        grid=(T // tm,),
        in_specs=[row(D_MODEL)] + att + att + [row(2 * GMLP_W), row(N_BRANCH * D_MODEL)]
                 + [pl.BlockSpec(memory_space=pl.ANY)] * 3
                 + [_resident(wc2.shape), _resident(bs.shape), _resident(lng.shape), _resident(lnb.shape), we_spec],
        out_specs=[row(D_MODEL), we_spec],
        out_shape=[jax.ShapeDtypeStruct((T, D_MODEL), F32), jax.ShapeDtypeStruct(w_expert.shape, BF16)],
        scratch_shapes=[pltpu.VMEM((tm, GMLP_W), BF16), pltpu.VMEM((4 * N_DIL, tm, LANES), F32),
                        pltpu.VMEM(wa.shape, BF16), pltpu.VMEM(wg.shape, BF16), pltpu.VMEM(wo.shape, BF16)]
                       + _weight_stage(D_MODEL),
        compiler_params=_cparams("arbitrary"),
        name="mix",
    )(x2, *outs, *lses, uv, gl, wa, wg, wo, wc2, bs, lng, lnb, w_expert)


def _router_kernel(h_ref, g_ref, wr_ref, br_ref, upper_ref, eidx_ref, gate_ref, rank_ref, cnt_ref, carry):
    tm = h_ref.shape[0]

    @pl.when(pl.program_id(0) == 0)
    def _():
        carry[...] = jnp.zeros_like(carry)

    hn = _rms(h_ref[...], g_ref[...])
    hi = hn.astype(BF16)
    lo = (hn - hi.astype(F32)).astype(BF16)
    nt = (((1,), (1,)), ((), ()))
    by_hi = lax.dot_general(wr_ref[...], hi, nt, preferred_element_type=F32)
    by_lo = lax.dot_general(wr_ref[:N_EXPERTS, :], lo, nt, preferred_element_type=F32)
    logits = by_hi[:N_EXPERTS] + by_hi[N_EXPERTS:] + by_lo + br_ref[...]
    expert = lax.broadcasted_iota(jnp.int32, (N_EXPERTS, tm), 0)
    vals, hots = [], []
    l = logits
    for k in range(TOP_K):
        m = jnp.max(l, axis=0, keepdims=True)
        idx = jnp.min(jnp.where(l == m, expert, N_EXPERTS), axis=0, keepdims=True)
        hot = expert == idx
        eidx_ref[0, k:k + 1, :] = idx
        vals.append(m)
        hots.append(hot)
        l = jnp.where(hot, -jnp.inf, l)
    ex = [jnp.exp(v - vals[0]) for v in vals]
    tot = ex[0] + ex[1] + ex[2] + ex[3]
    for k in range(TOP_K):
        gate_ref[0, k:k + 1, :] = ex[k] / tot
    multi = jnp.zeros((N_EXPERTS, tm), F32)
    for hot in hots:
        multi = multi + hot.astype(F32)
    before = jnp.dot(multi.astype(BF16), upper_ref[...], preferred_element_type=F32) + carry[...]
    for k in range(TOP_K):
        rank_ref[0, k:k + 1, :] = jnp.sum(jnp.where(hots[k], before, 0.0), axis=0, keepdims=True).astype(jnp.int32)
    carry[...] += jnp.sum(multi, axis=1, keepdims=True)
    cnt_ref[...] = carry[...]


def _router(h1, g, wr_t, br_col):
    T = h1.shape[0]
    tm = TM_PROJ
    upper = jnp.asarray(np.triu(np.ones((tm, tm), np.float32), k=1), BF16)
    k_rows = pl.BlockSpec((1, TOP_K, tm), lambda i: (i, 0, 0))
    k_shape = lambda dt: jax.ShapeDtypeStruct((T // tm, TOP_K, tm), dt)
    return pl.pallas_call(
        _router_kernel,
        grid=(T // tm,),
        in_specs=[pl.BlockSpec((tm, D_MODEL), lambda i: (i, 0)), _resident((1, D_MODEL)),
                  _resident((2 * N_EXPERTS, D_MODEL)), _resident((N_EXPERTS, 1)), _resident((tm, tm))],
        out_specs=[k_rows, k_rows, k_rows, pl.BlockSpec((N_EXPERTS, 1), lambda i: (0, 0))],
        out_shape=[k_shape(jnp.int32), k_shape(F32), k_shape(jnp.int32),
                   jax.ShapeDtypeStruct((N_EXPERTS, 1), F32)],
        scratch_shapes=[pltpu.VMEM((N_EXPERTS, 1), F32)],
        compiler_params=_cparams("arbitrary"),
        name="router",
    )(h1, g, wr_t, br_col, upper)


def _to_row_tiles(ref, lead, value):
    n = value.shape[0]
    for c in range(ROW_SUBLANES):
        ref[(*lead, pl.ds(c, n, stride=ROW_SUBLANES), slice(None))] = value[:, c * LANES:(c + 1) * LANES]


def _from_row_tiles(ref, lead, first, n):
    return jnp.concatenate(
        [ref[(*lead, pl.ds(first * ROW_SUBLANES + c, n, stride=ROW_SUBLANES), slice(None))] for c in range(ROW_SUBLANES)],
        axis=1)


def _tile_rows(idx, n=1):
    return pl.ds(pl.multiple_of(idx * ROW_SUBLANES, ROW_SUBLANES), n * ROW_SUBLANES)


def _dispatch_kernel(last_ref, nv_ref, dest_ref, h_ref, g_ref, xs_ref, buf, sem, zero_sem):
    tm = h_ref.shape[0]
    n_blocks = xs_ref.shape[0] // (TM_EXP * ROW_SUBLANES)
    i = pl.program_id(0)
    slot = lax.rem(i, 2)

    @pl.when(i == 0)
    def _():
        buf[1] = jnp.zeros(buf.shape[1:], F32)

        def zero_block(b):
            for part in range(TM_EXP // tm):
                pltpu.make_async_copy(buf.at[1], xs_ref.at[_tile_rows(b * TM_EXP + part * tm, tm)], zero_sem).start()

        def zero_done():
            for part in range(TM_EXP // tm):
                pltpu.make_async_copy(buf.at[1], xs_ref.at[_tile_rows(0, tm)], zero_sem).wait()

        for e in range(N_EXPERTS):
            zero_block(last_ref[e])
        lax.fori_loop(nv_ref[0], n_blocks, lambda b, c: (zero_block(b), c)[1], 0)
        for e in range(N_EXPERTS):
            zero_done()
        lax.fori_loop(nv_ref[0], n_blocks, lambda b, c: (zero_done(), c)[1], 0)

    _to_row_tiles(buf, (slot,), _rms(h_ref[...], g_ref[...]))

    def issue(t, carry):
        for k in range(TOP_K):
            d = dest_ref[0, 0, k * tm + t]
            pltpu.make_async_copy(buf.at[slot, _tile_rows(t)], xs_ref.at[_tile_rows(d)],
                                  sem.at[slot]).start(priority=k % 2)
        return carry

    lax.fori_loop(0, tm, issue, 0, unroll=8)

    def wait_slot(s):
        for _ in range(TOP_K):
            pltpu.make_async_copy(buf.at[s], xs_ref.at[_tile_rows(0, tm)], sem.at[s]).wait()

    @pl.when(i > 0)
    def _():
        wait_slot(1 - slot)

    @pl.when(i == pl.num_programs(0) - 1)
    def _():
        wait_slot(slot)


def _dispatch(last_block, n_valid, dest3, h1, g, n_slots):
    T = h1.shape[0]
    tm = TM_TOK
    grid_spec = pltpu.PrefetchScalarGridSpec(
        num_scalar_prefetch=2,
        grid=(T // tm,),
        in_specs=[pl.BlockSpec((1, 1, TOP_K * tm), lambda i, lb, nv: (i, 0, 0), memory_space=pltpu.SMEM),
                  pl.BlockSpec((tm, D_MODEL), lambda i, lb, nv: (i, 0)),
                  pl.BlockSpec((1, D_MODEL), lambda i, lb, nv: (0, 0), pipeline_mode=pl.Buffered(1))],
        out_specs=pl.BlockSpec(memory_space=pl.ANY),
        scratch_shapes=[pltpu.VMEM((2, tm * ROW_SUBLANES, LANES), F32), pltpu.SemaphoreType.DMA((2,)),
                        pltpu.SemaphoreType.DMA(())],
    )
    return pl.pallas_call(
        _dispatch_kernel,
        grid_spec=grid_spec,
        out_shape=jax.ShapeDtypeStruct((n_slots * ROW_SUBLANES, LANES), F32),
        compiler_params=_cparams("arbitrary"),
        name="dispatch",
    )(last_block, n_valid, dest3, h1, g)


def _experts_kernel(be_ref, nv_ref, xs_hbm, wgu_ref, bgu_ref, wd_ref, bd_ref, ys_hbm, xbuf, ybuf, xsem, ysem):
    del be_ref
    tm = xbuf.shape[1]
    b = pl.program_id(0)
    n_valid = nv_ref[0]
    slot = lax.rem(b, 2)

    def x_copies(blk, s):
        return [pltpu.make_async_copy(xs_hbm.at[pl.ds(blk * tm, tm), c, :],
                                      xbuf.at[s, :, pl.ds(c * LANES, LANES)], xsem.at[s]) for c in range(ROW_SUBLANES)]

    def y_copies(blk, s):
        return [pltpu.make_async_copy(ybuf.at[s, :, pl.ds(c * LANES, LANES)],
                                      ys_hbm.at[pl.ds(blk * tm, tm), c, :], ysem.at[s]) for c in range(ROW_SUBLANES)]

    @pl.when(jnp.logical_and(b == 0, n_valid > 0))
    def _():
        for cp in x_copies(0, 0):
            cp.start()

    @pl.when(b + 1 < n_valid)
    def _():
        for cp in x_copies(b + 1, 1 - slot):
            cp.start()

    @pl.when(b < n_valid)
    def _():
        for cp in x_copies(b, slot):
            cp.wait()
        x = xbuf[slot].astype(BF16)
        gu = jnp.dot(x, wgu_ref[0], preferred_element_type=F32) + bgu_ref[0]
        glu = jnp.minimum(gu[:, :D_EXPERT], SWIGLU_LIMIT)
        lin = jnp.clip(gu[:, D_EXPERT:], -SWIGLU_LIMIT, SWIGLU_LIMIT)
        act = glu * jax.nn.sigmoid(SWIGLU_ALPHA * glu) * (lin + 1.0)
        ybuf[slot] = jnp.dot(act.astype(BF16), wd_ref[0], preferred_element_type=F32) + bd_ref[0]

    @pl.when(b >= n_valid)
    def _():
        ybuf[slot] = jnp.zeros(ybuf.shape[1:], F32)

    for cp in y_copies(b, slot):
        cp.start()

    @pl.when(b > 0)
    def _():
        for cp in y_copies(b - 1, 1 - slot):
            cp.wait()

    @pl.when(b == pl.num_programs(0) - 1)
    def _():
        for cp in y_copies(b, slot):
            cp.wait()


def _experts(block_expert, n_valid, xs, wgu, bgu, wd, bd):
    tm = TM_EXP
    n_slots = xs.shape[0] // ROW_SUBLANES
    n_blocks = n_slots // tm
    any_space = pl.BlockSpec(memory_space=pl.ANY)
    grid_spec = pltpu.PrefetchScalarGridSpec(
        num_scalar_prefetch=2,
        grid=(n_blocks,),
        in_specs=[any_space,
                  pl.BlockSpec((1, D_MODEL, 2 * D_EXPERT), lambda b, be, nv: (be[b], 0, 0)),
                  pl.BlockSpec((1, 1, 2 * D_EXPERT), lambda b, be, nv: (be[b], 0, 0)),
                  pl.BlockSpec((1, D_EXPERT, D_MODEL), lambda b, be, nv: (be[b], 0, 0)),
                  pl.BlockSpec((1, 1, D_MODEL), lambda b, be, nv: (be[b], 0, 0))],
        out_specs=any_space,
        scratch_shapes=[pltpu.VMEM((2, tm, D_MODEL), F32), pltpu.VMEM((2, tm, D_MODEL), F32),
                        pltpu.SemaphoreType.DMA((2,)), pltpu.SemaphoreType.DMA((2,))],
    )
    ys = pl.pallas_call(
        _experts_kernel,
        grid_spec=grid_spec,
        out_shape=jax.ShapeDtypeStruct((n_slots, ROW_SUBLANES, LANES), F32),
        compiler_params=_cparams("arbitrary"),
        name="experts",
    )(block_expert, n_valid, xs.reshape(n_slots, ROW_SUBLANES, LANES), wgu, bgu, wd, bd)
    return ys.reshape(xs.shape)


GATHER_SLOTS = 3


def _combine_kernel(*refs, n_tiles):
    ahead = GATHER_SLOTS - 1
    prime_refs, refs = refs[:ahead], refs[ahead:]
    (ahead_dest_ref, h_ref, gate_ref, p_ref, gp_ref, gf_ref, wpg_hbm, wpp_hbm, ys_ref, o_ref), refs = refs[:10], refs[10:]
    bufs, (sem, wpg_ref, wpp_ref, w_stage, w_sem) = refs[:GATHER_SLOTS], refs[GATHER_SLOTS:]
    tm = h_ref.shape[0]
    i = pl.program_id(0)

    def row_copy(dref, t, k, s):
        d = dref[0, 0, k * tm + t]
        return pltpu.make_async_copy(ys_ref.at[_tile_rows(d)], bufs[s].at[_tile_rows(k * tm + t)], sem.at[s])

    def wait_slot(s):
        for _ in range(TOP_K):
            pltpu.make_async_copy(ys_ref.at[_tile_rows(0, tm)], bufs[s].at[_tile_rows(0, tm)], sem.at[s]).wait()

    @pl.when(i == 0)
    def _():
        _load_weight_bf16(wpg_hbm, wpg_ref, w_stage, w_sem)
        _load_weight_bf16(wpp_hbm, wpp_ref, w_stage, w_sem)

        for s, dref in enumerate(prime_refs):
            def issue(t, carry, s=s, dref=dref):
                for k in range(TOP_K):
                    row_copy(dref, t, k, s).start(priority=k % 2)
                return carry
            lax.fori_loop(0, tm, issue, 0, unroll=8)

    def step(s):
        wait_slot(s)
        for t in range(tm):
            for k in range(TOP_K):
                row_copy(ahead_dest_ref, t, k, (s + ahead) % GATHER_SLOTS).start(priority=k % 2)
        proj = jnp.dot(p_ref[...].astype(BF16), wpp_ref[...], preferred_element_type=F32)
        h = h_ref[...]
        for k in range(TOP_K):
            h = h + gate_ref[:, k:k + 1] * _from_row_tiles(bufs[s], (), k * tm, tm)
        ple_gate = jax.nn.sigmoid(jnp.dot(_rms(h, gp_ref[...]).astype(BF16), wpg_ref[...], preferred_element_type=F32))
        h = h + ple_gate * proj
        o_ref[...] = _rms(h, gf_ref[...])

    for s in range(GATHER_SLOTS):
        pl.when(lax.rem(i, GATHER_SLOTS) == s)(functools.partial(step, s))

    @pl.when(i == n_tiles - 1)
    def _():
        for extra in range(ahead):
            wait_slot((n_tiles + extra) % GATHER_SLOTS)


def _combine(dest3, h1, gate, p2, gp, wpg, wpp, gf, ys):
    T = h1.shape[0]
    tm = TM_TOK
    n_tiles = T // tm
    row = lambda w: pl.BlockSpec((tm, w), lambda i: (i, 0))
    dest_spec = lambda ahead: pl.BlockSpec((1, 1, TOP_K * tm), lambda i: (jnp.minimum(i + ahead, n_tiles - 1), 0, 0),
                                           memory_space=pltpu.SMEM)
    gather_buf = pltpu.VMEM((TOP_K * tm * ROW_SUBLANES, LANES), F32)
    ahead = GATHER_SLOTS - 1
    assert n_tiles > ahead
    return pl.pallas_call(
        functools.partial(_combine_kernel, n_tiles=n_tiles),
        grid=(n_tiles,),
        in_specs=[dest_spec(a) for a in range(ahead)] + [dest_spec(ahead),
                  row(D_MODEL), row(TOP_K), row(PLE_DIM), _resident((1, D_MODEL)), _resident((1, D_MODEL))]
                 + [pl.BlockSpec(memory_space=pl.ANY)] * 3,
        out_specs=row(D_MODEL),
        out_shape=jax.ShapeDtypeStruct((T, D_MODEL), F32),
        scratch_shapes=[gather_buf] * GATHER_SLOTS + [pltpu.SemaphoreType.DMA((GATHER_SLOTS,)),
                        pltpu.VMEM(wpg.shape, BF16), pltpu.VMEM(wpp.shape, BF16)] + _weight_stage(D_MODEL),
        compiler_params=_cparams("arbitrary"),
        name="combine",
    )(*([dest3] * (ahead + 1)), h1, gate, p2, gp, gf, wpg, wpp, ys)


def _layer(h, p_i, g_mix, w_in, rel_bias, w_att_out, ln_v_g, ln_v_b, w_spatial, b_spatial, w_gmlp_out, w_out,
           g_moe, w_router, b_router, w_gate_up, b_gate_up, w_down, b_down, g_ple, w_ple_gate, w_ple_proj,
           g_final, B, S):
    T = B * S
    row = lambda v: v.reshape(1, -1).astype(F32)

    assert S % TM_PROJ == 0
    *att_in, uv, gl, wgu_bf = _in_proj(h, row(g_mix), w_in.astype(F32), w_gate_up.astype(F32), B, S)

    outs, lses = [], []
    for g, (window, dilation) in enumerate(ATT_GROUPS):
        assert window // dilation == BLK and S % (dilation * BLK) == 0
        bias = _bias_table(rel_bias[:, g * HEADS_PER_GROUP:(g + 1) * HEADS_PER_GROUP], dilation)
        o, lse = _attention_group(att_in[g], bias, dilation, B, S)
        outs.append(o)
        lses.append(lse)

    causal = jnp.asarray(np.tril(np.ones((CHUNK, CHUNK), np.float32)))
    w_c = (w_spatial.astype(F32) * causal[None]).astype(BF16)
    wc2 = jnp.concatenate([w_c[0::2], w_c[1::2]], axis=2)
    bs = jnp.repeat(b_spatial.astype(F32).T, GMLP_GD, axis=1)
    h1, wd_bf = _mix(h, outs, lses, uv, gl, w_att_out.astype(F32), w_gmlp_out.astype(F32), w_out.astype(F32),
                     wc2, bs, row(ln_v_g), row(ln_v_b), w_down.astype(F32), S)

    wr_hi = w_router.astype(BF16)
    wr_lo = (w_router.astype(F32) - wr_hi.astype(F32)).astype(BF16)
    eidx, gate, rank, counts = _router(h1, row(g_moe), jnp.concatenate([wr_hi, wr_lo], axis=1).T,
                                       b_router.reshape(-1, 1).astype(F32))
    cnt = counts[:, 0].astype(jnp.int32)
    blk_counts = (cnt + TM_EXP - 1) // TM_EXP
    blk_end = jnp.cumsum(blk_counts)
    pad_start = (blk_end - blk_counts) * TM_EXP
    n_blocks = T * TOP_K // TM_EXP + N_EXPERTS
    n_valid = blk_end[-1:].astype(jnp.int32)
    blk = jnp.minimum(jnp.arange(n_blocks, dtype=jnp.int32), n_valid[0] - 1)
    block_expert = jnp.minimum(jnp.sum((blk_end[None, :] <= blk[:, None]).astype(jnp.int32), axis=1), N_EXPERTS - 1)
    expert_ids = jnp.arange(N_EXPERTS, dtype=jnp.int32)
    dest = rank + jnp.sum(jnp.where(eidx[..., None] == expert_ids, pad_start, 0), axis=-1)
    split = TM_PROJ // TM_TOK
    dest3 = jnp.transpose(dest.reshape(T // TM_PROJ, TOP_K, split, TM_TOK), (0, 2, 1, 3)).reshape(
        T // TM_TOK, 1, TOP_K * TM_TOK)
    gate = jnp.transpose(gate, (0, 2, 1)).reshape(T, TOP_K)

    last_block = jnp.maximum(blk_end - 1, 0).astype(jnp.int32)
    xs = _dispatch(last_block, n_valid, dest3, h1, row(g_moe), n_blocks * TM_EXP)
    ys = _experts(block_expert, n_valid, xs, wgu_bf, b_gate_up.reshape(N_EXPERTS, 1, -1).astype(F32),
                  wd_bf, b_down.reshape(N_EXPERTS, 1, -1).astype(F32))
    return _combine(dest3, h1, gate, p_i, row(g_ple), w_ple_gate.astype(F32), w_ple_proj.astype(F32),
                    row(g_final), ys)


def kernel(x, p, g_mix, w_in, rel_bias, w_att_out, ln_v_g, ln_v_b, w_spatial, b_spatial, w_gmlp_out, w_out, g_moe, w_router, b_router, w_gate_up, b_gate_up, w_down, b_down, g_ple, w_ple_gate, w_ple_proj, g_final):
    B, S, D = x.shape
    depth = p.shape[0]
    assert depth == 1, "the final RMSNorm is fused into the (single) layer's last kernel"
    out = _layer(x.reshape(B * S, D), p[0].reshape(B * S, PLE_DIM), g_mix[0], w_in[0], rel_bias, w_att_out[0],
                 ln_v_g[0], ln_v_b[0], w_spatial[0], b_spatial[0], w_gmlp_out[0], w_out[0], g_moe[0], w_router[0],
                 b_router[0], w_gate_up[0], b_gate_up[0], w_down[0], b_down[0], g_ple[0], w_ple_gate[0],
                 w_ple_proj[0], g_final, B, S)
    return out.reshape(B, S, D)
```

```python
import functools

import jax
import jax.numpy as jnp
import numpy as np
from jax import lax
from jax.experimental import pallas as pl
from jax.experimental.pallas import tpu as pltpu

F32 = jnp.float32
BF16 = jnp.bfloat16

D_MODEL = 1024
HEAD_DIM = 64
ATT_GROUPS = ((128, 1), (512, 4), (2048, 16))
HEADS_PER_GROUP = 4
GROUP_W = HEADS_PER_GROUP * HEAD_DIM
N_DIL = len(ATT_GROUPS)
ATT_W = N_DIL * GROUP_W
BLK = 128
REL_BUCKETS = 32
REL_MAX_DIST = 2048
CHUNK = 128
GMLP_W = 768
GMLP_GD = 64
N_BRANCH = 2
IN_W = 3 * ATT_W + 2 * GMLP_W + N_BRANCH * D_MODEL
N_EXPERTS = 32
TOP_K = 4
D_EXPERT = D_MODEL
SWIGLU_LIMIT = 7.0
SWIGLU_ALPHA = 1.702
PLE_DIM = 256
EPS = 1e-6
MASKED = -1e30
LOG2E = float(np.log2(np.e))
LN2 = float(np.log(2.0))

QKV_G = 3 * GROUP_W

LANES = 128
ROW_SUBLANES = D_MODEL // LANES
assert ROW_SUBLANES == 8
MXU_N = 256
VMEM_LIMIT = 56 * 1024 * 1024

TM_PROJ = 512
TM_TOK = 256
TM_EXP = 512
assert TM_EXP % TM_TOK == 0


def _cparams(*sem):
    return pltpu.CompilerParams(dimension_semantics=sem, vmem_limit_bytes=VMEM_LIMIT)


def _resident(shape):
    nd = len(shape)
    return pl.BlockSpec(shape, lambda *_: (0,) * nd, pipeline_mode=pl.Buffered(1))


def _rms(x, g):
    return x * lax.rsqrt(jnp.mean(x * x, axis=-1, keepdims=True) + EPS) * g


def _load_weight_bf16(w_hbm, w_bf, stage, sem):
    rows = stage.shape[1]
    n_chunks = w_hbm.shape[0] // rows
    assert n_chunks * rows == w_hbm.shape[0] and stage.shape[2] == w_hbm.shape[1]

    def chunk(c):
        return pltpu.make_async_copy(w_hbm.at[pl.ds(c * rows, rows)], stage.at[c % 2], sem.at[c % 2])

    chunk(0).start()
    for c in range(n_chunks):
        if c + 1 < n_chunks:
            chunk(c + 1).start()
        chunk(c).wait()
        w_bf[c * rows:(c + 1) * rows, :] = stage[c % 2].astype(BF16)


def _expert_slice_spec(n_steps, width):
    per_expert = n_steps // N_EXPERTS
    assert per_expert * N_EXPERTS == n_steps and D_MODEL % per_expert == 0
    return pl.BlockSpec((1, D_MODEL // per_expert, width), lambda i: (i // per_expert, i % per_expert, 0))


def _inproj_kernel(x0_ref, xn_ref, g_ref, w_hbm, we_ref, a1_ref, a2_ref, a3_ref, uv_ref, gl_ref, we_bf_ref,
                   scr, w_ref, w_stage, w_sem, n_scr, n_tmp):
    i = pl.program_id(0)
    slot = lax.rem(i, 2)

    @pl.when(i == 0)
    def _():
        _load_weight_bf16(w_hbm, w_ref, w_stage, w_sem)
        n_scr[0] = _rms(x0_ref[...], g_ref[...]).astype(BF16)

    tm = xn_ref.shape[0]
    n = n_scr[slot]
    att_refs = (a1_ref, a2_ref, a3_ref)
    n_att, n_uv = 3 * ATT_W // MXU_N, 2 * GMLP_W // MXU_N
    n_pieces = 16
    side_rows, we_rows = tm // n_pieces, we_ref.shape[1] // n_pieces
    for c in range(IN_W // MXU_N):
        if c < n_pieces:
            we_bf_ref[0, c * we_rows:(c + 1) * we_rows, :] = we_ref[0, c * we_rows:(c + 1) * we_rows, :].astype(BF16)
            rows = slice(c * side_rows, (c + 1) * side_rows)
            n_tmp[rows, :] = _rms(xn_ref[rows, :], g_ref[...]).astype(BF16)
        z = jnp.dot(n, w_ref[:, c * MXU_N:(c + 1) * MXU_N], preferred_element_type=F32)
        if c < n_att:
            which, g = divmod(c, N_DIL)
            d = ATT_GROUPS[g][1]
            dst = att_refs[g]
            cols = slice(which * GROUP_W, (which + 1) * GROUP_W)
            if d == 1:
                dst[0, 0, :, cols] = z.astype(BF16)
                continue
            scr[0] = z[:, :LANES]
            scr[1] = z[:, LANES:]
            for r in range(d):
                zr = jnp.concatenate([scr[0, pl.ds(r, tm // d, stride=d), :],
                                      scr[1, pl.ds(r, tm // d, stride=d), :]], axis=1)
                dst[0, r, :, cols] = zr.astype(BF16)
        elif c < n_att + n_uv:
            uv_ref[:, (c - n_att) * MXU_N:(c - n_att + 1) * MXU_N] = z.astype(BF16)
        else:
            gl_ref[:, (c - n_att - n_uv) * MXU_N:(c - n_att - n_uv + 1) * MXU_N] = z.astype(BF16)
    n_scr[1 - slot] = n_tmp[...]


def _plane_spec(d, tm, tiles_per_seq, width):
    return pl.BlockSpec((1, d, tm // d, width), lambda i: (i // tiles_per_seq, 0, i % tiles_per_seq, 0))


W_STAGE_ROWS = 128


def _weight_stage(width):
    return [pltpu.VMEM((2, W_STAGE_ROWS, width), F32), pltpu.SemaphoreType.DMA((2,))]


def _in_proj(x2, g, w, w_expert, B, S):
    T = x2.shape[0]
    tm = TM_PROJ
    row = lambda w: pl.BlockSpec((tm, w), lambda i: (i, 0))
    dils = [d for _, d in ATT_GROUPS]
    we_spec = _expert_slice_spec(T // tm, w_expert.shape[2])
    n_steps = T // tm
    first_tile = pl.BlockSpec((tm, D_MODEL), lambda i: (0, 0), pipeline_mode=pl.Buffered(1))
    next_tile = pl.BlockSpec((tm, D_MODEL), lambda i: (jnp.minimum(i + 1, n_steps - 1), 0))
    return pl.pallas_call(
        _inproj_kernel,
        grid=(n_steps,),
        in_specs=[first_tile, next_tile, _resident((1, D_MODEL)), pl.BlockSpec(memory_space=pl.ANY), we_spec],
        out_specs=[_plane_spec(d, tm, S // tm, QKV_G) for d in dils] + [row(2 * GMLP_W), row(N_BRANCH * D_MODEL), we_spec],
        out_shape=[jax.ShapeDtypeStruct((B, d, S // d, QKV_G), BF16) for d in dils]
                  + [jax.ShapeDtypeStruct((T, 2 * GMLP_W), BF16),
                     jax.ShapeDtypeStruct((T, N_BRANCH * D_MODEL), BF16),
                     jax.ShapeDtypeStruct(w_expert.shape, BF16)],
        scratch_shapes=[pltpu.VMEM((2, tm, LANES), F32), pltpu.VMEM((D_MODEL, IN_W), BF16)] + _weight_stage(IN_W)
                       + [pltpu.VMEM((2, tm, D_MODEL), BF16), pltpu.VMEM((tm, D_MODEL), BF16)],
        compiler_params=_cparams("arbitrary"),
        name="in_proj",
    )(x2, x2, g, w, w_expert)


def _t5_bucket(n):
    exact = REL_BUCKETS // 2
    nf = np.maximum(n, 1).astype(np.float32)
    large = exact + (np.log(nf / exact) / np.log(REL_MAX_DIST / exact) * (REL_BUCKETS - exact)).astype(np.int32)
    large = np.minimum(large, REL_BUCKETS - 1)
    return np.where(n < exact, n, large).astype(np.int32)


def _bias_table(rel_bias_g, dilation):
    n = 3 * BLK
    dist = 2 * BLK - 1 - np.arange(n)
    valid = (dist >= 0) & (dist <= BLK)
    bucket = _t5_bucket(np.clip(dist, 0, BLK) * dilation)
    c = jnp.where(jnp.asarray(valid)[None, :], rel_bias_g.astype(F32)[bucket].T * LOG2E, MASKED)
    shifted = jnp.tile(c, (1, BLK))[:, :BLK * (n - 1)].reshape(HEADS_PER_GROUP, BLK, n - 1)
    return shifted[:, :, BLK - 1:].reshape(HEADS_PER_GROUP * BLK, 2 * BLK)


def _attn_kernel(cur_ref, prev_ref, bias_ref, o_ref, lse_ref):
    rg, rb = cur_ref.shape[1], cur_ref.shape[2] // BLK
    starts_sequence = pl.program_id(2) == 0
    lane_head = lax.broadcasted_iota(jnp.int32, (1, GROUP_W), 1) // HEAD_DIM
    scale = HEAD_DIM ** -0.5
    head_bf = [jnp.where(lane_head == h, scale, 0.0).astype(BF16) for h in range(HEADS_PER_GROUP)]
    key_is_prev = lax.broadcasted_iota(jnp.int32, (1, 2 * BLK), 1) < BLK
    nt = (((1,), (1,)), ((), ()))
    qc, kc_, vc_ = slice(0, GROUP_W), slice(GROUP_W, 2 * GROUP_W), slice(2 * GROUP_W, 3 * GROUP_W)

    def by_head(x):
        sel = x[(HEADS_PER_GROUP - 1) * BLK:]
        for h in range(HEADS_PER_GROUP - 2, -1, -1):
            sel = jnp.where(lane_head == h, x[h * BLK:(h + 1) * BLK], sel)
        return sel

    for r, j in [(r, j) for r in range(rg) for j in range(rb)]:
        rows = slice(j * BLK, (j + 1) * BLK)
        prev = prev_ref if j == 0 else cur_ref
        prows = slice(0, BLK) if j == 0 else slice((j - 1) * BLK, j * BLK)
        q = cur_ref[0, r, rows, qc]
        k = jnp.concatenate([prev[0, r, prows, kc_], cur_ref[0, r, rows, kc_]], axis=0)
        v = jnp.concatenate([prev[0, r, prows, vc_], cur_ref[0, r, rows, vc_]], axis=0)
        q_bd = jnp.concatenate([q * head_bf[h] for h in range(HEADS_PER_GROUP)], axis=0)
        s = lax.dot_general(q_bd, k, nt, preferred_element_type=F32) * LOG2E + bias_ref[...]
        if j == 0:
            s = jnp.where(jnp.logical_and(starts_sequence, key_is_prev), MASKED, s)
        m = jnp.max(s, axis=-1, keepdims=True)
        p = jnp.exp2(s - m)
        den = jnp.sum(p, axis=-1, keepdims=True)
        o = jnp.dot(p.astype(BF16), v, preferred_element_type=F32)
        den_h = jnp.broadcast_to(by_head(den), (BLK, GROUP_W))
        o_ref[0, r, rows, :] = (by_head(o) / den_h).astype(BF16)
        lse_ref[0, r, rows, :] = by_head(m) * LN2 + jnp.log(den_h)


ATT_SUBBLOCKS = 8


def _attention_group(a, bias, dilation, B, S):
    sd = S // dilation
    rb = min(ATT_SUBBLOCKS, sd // BLK)
    rg = min(ATT_SUBBLOCKS // rb, dilation)
    o, lse = pl.pallas_call(
        _attn_kernel,
        grid=(B, dilation // rg, sd // (rb * BLK)),
        in_specs=[pl.BlockSpec((1, rg, rb * BLK, QKV_G), lambda b, r, n: (b, r, n, 0)),
                  pl.BlockSpec((1, rg, BLK, QKV_G), lambda b, r, n: (b, r, jnp.maximum(n * rb - 1, 0), 0)),
                  _resident((HEADS_PER_GROUP * BLK, 2 * BLK))],
        out_specs=[pl.BlockSpec((1, rg, rb * BLK, GROUP_W), lambda b, r, n: (b, r, n, 0))] * 2,
        out_shape=[jax.ShapeDtypeStruct((B, dilation, sd, GROUP_W), BF16),
                   jax.ShapeDtypeStruct((B, dilation, sd, GROUP_W), F32)],
        compiler_params=_cparams("parallel", "parallel", "parallel"),
        name=f"attn_d{dilation}",
    )(a, a, bias)
    return o, lse


def _sigmoid(x):
    return 0.5 * jnp.tanh(0.5 * x) + 0.5


def _gelu(x):
    return x * (lax.erf(x * (2.0 ** -0.5)) + 1.0) * 0.5


def _token_major(src_ref, d, scr, slot, tm):
    if d == 1:
        return src_ref[0, 0].astype(F32)
    for r in range(d):
        piece = src_ref[0, r].astype(F32)
        scr[slot, pl.ds(r, tm // d, stride=d), :] = piece[:, :LANES]
        scr[slot + 1, pl.ds(r, tm // d, stride=d), :] = piece[:, LANES:]
    return jnp.concatenate([scr[slot], scr[slot + 1]], axis=1)


def _mix_kernel(x_ref, o1_ref, o2_ref, o3_ref, l1_ref, l2_ref, l3_ref, uv_ref, gl_ref,
                wa_hbm, wg_hbm, wo_hbm, wc_ref, bs_ref, lng_ref, lnb_ref, we_ref, h_ref, we_bf_ref, g_scr, t_scr,
                wa_ref, wg_ref, wo_ref, w_stage, w_sem):
    @pl.when(pl.program_id(0) == 0)
    def _():
        for w_hbm, w_bf in ((wa_hbm, wa_ref), (wg_hbm, wg_ref), (wo_hbm, wo_ref)):
            _load_weight_bf16(w_hbm, w_bf, w_stage, w_sem)

    we_bf_ref[...] = we_ref[...].astype(BF16)

    tm = x_ref.shape[0]
    dils = [d for _, d in ATT_GROUPS]
    o1, o2, o3 = [_token_major(ref, d, t_scr, 4 * i, tm) for i, (ref, d) in enumerate(zip((o1_ref, o2_ref, o3_ref), dils))]
    l1, l2, l3 = [_token_major(ref, d, t_scr, 4 * i + 2, tm) for i, (ref, d) in enumerate(zip((l1_ref, l2_ref, l3_ref), dils))]
    lm = jnp.maximum(jnp.maximum(l1, l2), l3)
    e1, e2, e3 = jnp.exp(l1 - lm), jnp.exp(l2 - lm), jnp.exp(l3 - lm)
    att = (e1 * o1 + e2 * o2 + e3 * o3) / (e1 + e2 + e3)
    y_att = jnp.dot(att.astype(BF16), wa_ref[...], preferred_element_type=F32)

    zu = _gelu(uv_ref[:, :GMLP_W].astype(F32))
    zv = _gelu(uv_ref[:, GMLP_W:].astype(F32))
    mu = jnp.mean(zv, axis=-1, keepdims=True)
    var = jnp.mean(jnp.square(zv - mu), axis=-1, keepdims=True)
    vn = (zv - mu) * lax.rsqrt(var + EPS) * lng_ref[...] + lnb_ref[...]
    low_half = lax.broadcasted_iota(jnp.int32, (CHUNK, 2 * GMLP_GD), 1) < GMLP_GD
    for c in range(tm // CHUNK):
        rows = slice(c * CHUNK, (c + 1) * CHUNK)
        for s in range(GMLP_W // (2 * GMLP_GD)):
            cols = slice(s * 2 * GMLP_GD, (s + 1) * 2 * GMLP_GD)
            v2 = vn[rows, cols]
            rhs = jnp.concatenate([jnp.where(low_half, v2, 0.0), jnp.where(low_half, 0.0, v2)], axis=0).astype(BF16)
            mixed = jnp.dot(wc_ref[s], rhs, preferred_element_type=F32) + bs_ref[:, cols]
            g_scr[rows, cols] = (zu[rows, cols] * mixed).astype(BF16)
    y_gm = jnp.dot(g_scr[...], wg_ref[...], preferred_element_type=F32)

    gate_a = _sigmoid(gl_ref[:, :D_MODEL].astype(F32))
    gate_g = _sigmoid(gl_ref[:, D_MODEL:].astype(F32))
    merged = (gate_a * y_att + gate_g * y_gm).astype(BF16)
    h_ref[...] = x_ref[...] + jnp.dot(merged, wo_ref[...], preferred_element_type=F32)


def _mix(x2, outs, lses, uv, gl, wa, wg, wo, wc2, bs, lng, lnb, w_expert, S):
    T = x2.shape[0]
    tm = TM_PROJ
    row = lambda w: pl.BlockSpec((tm, w), lambda i: (i, 0))
    att = [_plane_spec(d, tm, S // tm, GROUP_W) for _, d in ATT_GROUPS]
    we_spec = _expert_slice_spec(T // tm, w_expert.shape[2])
    return pl.pallas_call(
        _mix_kernel,
        grid=(T // tm,),
        in_specs=[row(D_MODEL)] + att + att + [row(2 * GMLP_W), row(N_BRANCH * D_MODEL)]
                 + [pl.BlockSpec(memory_space=pl.ANY)] * 3
                 + [_resident(wc2.shape), _resident(bs.shape), _resident(lng.shape), _resident(lnb.shape), we_spec],
        out_specs=[row(D_MODEL), we_spec],
        out_shape=[jax.ShapeDtypeStruct((T, D_MODEL), F32), jax.ShapeDtypeStruct(w_expert.shape, BF16)],
        scratch_shapes=[pltpu.VMEM((tm, GMLP_W), BF16), pltpu.VMEM((4 * N_DIL, tm, LANES), F32),
                        pltpu.VMEM(wa.shape, BF16), pltpu.VMEM(wg.shape, BF16), pltpu.VMEM(wo.shape, BF16)]
                       + _weight_stage(D_MODEL),
        compiler_params=_cparams("arbitrary"),
        name="mix",
    )(x2, *outs, *lses, uv, gl, wa, wg, wo, wc2, bs, lng, lnb, w_expert)


def _router_kernel(h_ref, g_ref, wr_ref, br_ref, upper_ref, eidx_ref, gate_ref, rank_ref, cnt_ref, carry):
    tm = h_ref.shape[0]

    @pl.when(pl.program_id(0) == 0)
    def _():
        carry[...] = jnp.zeros_like(carry)

    hn = _rms(h_ref[...], g_ref[...])
    hi = hn.astype(BF16)
    lo = (hn - hi.astype(F32)).astype(BF16)
    nt = (((1,), (1,)), ((), ()))
    by_hi = lax.dot_general(wr_ref[...], hi, nt, preferred_element_type=F32)
    by_lo = lax.dot_general(wr_ref[:N_EXPERTS, :], lo, nt, preferred_element_type=F32)
    logits = by_hi[:N_EXPERTS] + by_hi[N_EXPERTS:] + by_lo + br_ref[...]
    expert = lax.broadcasted_iota(jnp.int32, (N_EXPERTS, tm), 0)
    vals, hots = [], []
    l = logits
    for k in range(TOP_K):
        m = jnp.max(l, axis=0, keepdims=True)
        idx = jnp.min(jnp.where(l == m, expert, N_EXPERTS), axis=0, keepdims=True)
        hot = expert == idx
        eidx_ref[0, k:k + 1, :] = idx
        vals.append(m)
        hots.append(hot)
        l = jnp.where(hot, -jnp.inf, l)
    ex = [jnp.exp(v - vals[0]) for v in vals]
    tot = ex[0] + ex[1] + ex[2] + ex[3]
    for k in range(TOP_K):
        gate_ref[0, k:k + 1, :] = ex[k] / tot
    multi = jnp.zeros((N_EXPERTS, tm), F32)
    for hot in hots:
        multi = multi + hot.astype(F32)
    before = jnp.dot(multi.astype(BF16), upper_ref[...], preferred_element_type=F32) + carry[...]
    for k in range(TOP_K):
        rank_ref[0, k:k + 1, :] = jnp.sum(jnp.where(hots[k], before, 0.0), axis=0, keepdims=True).astype(jnp.int32)
    carry[...] += jnp.sum(multi, axis=1, keepdims=True)
    cnt_ref[...] = carry[...]


def _router(h1, g, wr_t, br_col):
    T = h1.shape[0]
    tm = TM_PROJ
    upper = jnp.asarray(np.triu(np.ones((tm, tm), np.float32), k=1), BF16)
    k_rows = pl.BlockSpec((1, TOP_K, tm), lambda i: (i, 0, 0))
    k_shape = lambda dt: jax.ShapeDtypeStruct((T // tm, TOP_K, tm), dt)
    return pl.pallas_call(
        _router_kernel,
        grid=(T // tm,),
        in_specs=[pl.BlockSpec((tm, D_MODEL), lambda i: (i, 0)), _resident((1, D_MODEL)),
                  _resident((2 * N_EXPERTS, D_MODEL)), _resident((N_EXPERTS, 1)), _resident((tm, tm))],
        out_specs=[k_rows, k_rows, k_rows, pl.BlockSpec((N_EXPERTS, 1), lambda i: (0, 0))],
        out_shape=[k_shape(jnp.int32), k_shape(F32), k_shape(jnp.int32),
                   jax.ShapeDtypeStruct((N_EXPERTS, 1), F32)],
        scratch_shapes=[pltpu.VMEM((N_EXPERTS, 1), F32)],
        compiler_params=_cparams("arbitrary"),
        name="router",
    )(h1, g, wr_t, br_col, upper)


def _to_row_tiles(ref, lead, value):
    n = value.shape[0]
    for c in range(ROW_SUBLANES):
        ref[(*lead, pl.ds(c, n, stride=ROW_SUBLANES), slice(None))] = value[:, c * LANES:(c + 1) * LANES]


def _from_row_tiles(ref, lead, first, n):
    return jnp.concatenate(
        [ref[(*lead, pl.ds(first * ROW_SUBLANES + c, n, stride=ROW_SUBLANES), slice(None))] for c in range(ROW_SUBLANES)],
        axis=1)


def _tile_rows(idx, n=1):
    return pl.ds(pl.multiple_of(idx * ROW_SUBLANES, ROW_SUBLANES), n * ROW_SUBLANES)


def _dispatch_kernel(last_ref, nv_ref, dest_ref, h_ref, g_ref, xs_ref, buf, sem, zero_sem):
    tm = h_ref.shape[0]
    n_blocks = xs_ref.shape[0] // (TM_EXP * ROW_SUBLANES)
    i = pl.program_id(0)
    slot = lax.rem(i, 2)

    @pl.when(i == 0)
    def _():
        buf[1] = jnp.zeros(buf.shape[1:], F32)

        def zero_block(b):
            for part in range(TM_EXP // tm):
                pltpu.make_async_copy(buf.at[1], xs_ref.at[_tile_rows(b * TM_EXP + part * tm, tm)], zero_sem).start()

        def zero_done():
            for part in range(TM_EXP // tm):
                pltpu.make_async_copy(buf.at[1], xs_ref.at[_tile_rows(0, tm)], zero_sem).wait()

        for e in range(N_EXPERTS):
            zero_block(last_ref[e])
        lax.fori_loop(nv_ref[0], n_blocks, lambda b, c: (zero_block(b), c)[1], 0)
        for e in range(N_EXPERTS):
            zero_done()
        lax.fori_loop(nv_ref[0], n_blocks, lambda b, c: (zero_done(), c)[1], 0)

    _to_row_tiles(buf, (slot,), _rms(h_ref[...], g_ref[...]))

    def issue(t, carry):
        for k in range(TOP_K):
            d = dest_ref[0, 0, k * tm + t]
            pltpu.make_async_copy(buf.at[slot, _tile_rows(t)], xs_ref.at[_tile_rows(d)],
                                  sem.at[slot]).start(priority=k % 2)
        return carry

    lax.fori_loop(0, tm, issue, 0, unroll=8)

    def wait_slot(s):
        for _ in range(TOP_K):
            pltpu.make_async_copy(buf.at[s], xs_ref.at[_tile_rows(0, tm)], sem.at[s]).wait()

    @pl.when(i > 0)
    def _():
        wait_slot(1 - slot)

    @pl.when(i == pl.num_programs(0) - 1)
    def _():
        wait_slot(slot)


def _dispatch(last_block, n_valid, dest3, h1, g, n_slots):
    T = h1.shape[0]
    tm = TM_TOK
    grid_spec = pltpu.PrefetchScalarGridSpec(
        num_scalar_prefetch=2,
        grid=(T // tm,),
        in_specs=[pl.BlockSpec((1, 1, TOP_K * tm), lambda i, lb, nv: (i, 0, 0), memory_space=pltpu.SMEM),
                  pl.BlockSpec((tm, D_MODEL), lambda i, lb, nv: (i, 0)),
                  pl.BlockSpec((1, D_MODEL), lambda i, lb, nv: (0, 0), pipeline_mode=pl.Buffered(1))],
        out_specs=pl.BlockSpec(memory_space=pl.ANY),
        scratch_shapes=[pltpu.VMEM((2, tm * ROW_SUBLANES, LANES), F32), pltpu.SemaphoreType.DMA((2,)),
                        pltpu.SemaphoreType.DMA(())],
    )
    return pl.pallas_call(
        _dispatch_kernel,
        grid_spec=grid_spec,
        out_shape=jax.ShapeDtypeStruct((n_slots * ROW_SUBLANES, LANES), F32),
        compiler_params=_cparams("arbitrary"),
        name="dispatch",
    )(last_block, n_valid, dest3, h1, g)


def _experts_kernel(be_ref, nv_ref, nr_ref, xs_hbm, wgu_ref, bgu_ref, wd_ref, bd_ref, ys_hbm, xbuf, ybuf, xsem, ysem):
    del be_ref
    tm = xbuf.shape[1]
    b = pl.program_id(0)
    n_valid = nv_ref[0]
    slot = lax.rem(b, 2)

    def x_copies(blk, s):
        return [pltpu.make_async_copy(xs_hbm.at[pl.ds(blk * tm, tm), c, :],
                                      xbuf.at[s, :, pl.ds(c * LANES, LANES)], xsem.at[s]) for c in range(ROW_SUBLANES)]

    def y_copies(blk, s):
        return [pltpu.make_async_copy(ybuf.at[s, :, pl.ds(c * LANES, LANES)],
                                      ys_hbm.at[pl.ds(blk * tm, tm), c, :], ysem.at[s]) for c in range(ROW_SUBLANES)]

    @pl.when(jnp.logical_and(b == 0, n_valid > 0))
    def _():
        for cp in x_copies(0, 0):
            cp.start()

    @pl.when(b + 1 < n_valid)
    def _():
        for cp in x_copies(b + 1, 1 - slot):
            cp.start()

    def mlp(rows):
        x = xbuf[slot, :rows, :].astype(BF16)
        gu = jnp.dot(x, wgu_ref[0], preferred_element_type=F32) + bgu_ref[0]
        glu = jnp.minimum(gu[:, :D_EXPERT], SWIGLU_LIMIT)
        lin = jnp.clip(gu[:, D_EXPERT:], -SWIGLU_LIMIT, SWIGLU_LIMIT)
        act = glu * jax.nn.sigmoid(SWIGLU_ALPHA * glu) * (lin + 1.0)
        ybuf[slot, :rows, :] = jnp.dot(act.astype(BF16), wd_ref[0], preferred_element_type=F32) + bd_ref[0]

    @pl.when(b < n_valid)
    def _():
        for cp in x_copies(b, slot):
            cp.wait()

    half = tm // 2
    needs_full = nr_ref[b] > half

    @pl.when(jnp.logical_and(b < n_valid, needs_full))
    def _():
        mlp(tm)

    @pl.when(jnp.logical_and(b < n_valid, jnp.logical_not(needs_full)))
    def _():
        mlp(half)
        ybuf[slot, half:, :] = jnp.zeros((tm - half, ybuf.shape[2]), F32)

    @pl.when(b >= n_valid)
    def _():
        ybuf[slot] = jnp.zeros(ybuf.shape[1:], F32)

    for cp in y_copies(b, slot):
        cp.start()

    @pl.when(b > 0)
    def _():
        for cp in y_copies(b - 1, 1 - slot):
            cp.wait()

    @pl.when(b == pl.num_programs(0) - 1)
    def _():
        for cp in y_copies(b, slot):
            cp.wait()


def _experts(block_expert, n_valid, block_rows, xs, wgu, bgu, wd, bd):
    tm = TM_EXP
    n_slots = xs.shape[0] // ROW_SUBLANES
    n_blocks = n_slots // tm
    any_space = pl.BlockSpec(memory_space=pl.ANY)
    grid_spec = pltpu.PrefetchScalarGridSpec(
        num_scalar_prefetch=3,
        grid=(n_blocks,),
        in_specs=[any_space,
                  pl.BlockSpec((1, D_MODEL, 2 * D_EXPERT), lambda b, be, nv, nr: (be[b], 0, 0)),
                  pl.BlockSpec((1, 1, 2 * D_EXPERT), lambda b, be, nv, nr: (be[b], 0, 0)),
                  pl.BlockSpec((1, D_EXPERT, D_MODEL), lambda b, be, nv, nr: (be[b], 0, 0)),
                  pl.BlockSpec((1, 1, D_MODEL), lambda b, be, nv, nr: (be[b], 0, 0))],
        out_specs=any_space,
        scratch_shapes=[pltpu.VMEM((2, tm, D_MODEL), F32), pltpu.VMEM((2, tm, D_MODEL), F32),
                        pltpu.SemaphoreType.DMA((2,)), pltpu.SemaphoreType.DMA((2,))],
    )
    ys = pl.pallas_call(
        _experts_kernel,
        grid_spec=grid_spec,
        out_shape=jax.ShapeDtypeStruct((n_slots, ROW_SUBLANES, LANES), F32),
        compiler_params=_cparams("arbitrary"),
        name="experts",
    )(block_expert, n_valid, block_rows, xs.reshape(n_slots, ROW_SUBLANES, LANES), wgu, bgu, wd, bd)
    return ys.reshape(xs.shape)


GATHER_SLOTS = 3


def _combine_kernel(*refs, n_tiles):
    ahead = GATHER_SLOTS - 1
    prime_refs, refs = refs[:ahead], refs[ahead:]
    (ahead_dest_ref, h_ref, gate_ref, p_ref, gp_ref, gf_ref, wpg_hbm, wpp_hbm, ys_ref, o_ref), refs = refs[:10], refs[10:]
    bufs, (sem, wpg_ref, wpp_ref, w_stage, w_sem) = refs[:GATHER_SLOTS], refs[GATHER_SLOTS:]
    tm = h_ref.shape[0]
    i = pl.program_id(0)

    def row_copy(dref, t, k, s):
        d = dref[0, 0, k * tm + t]
        return pltpu.make_async_copy(ys_ref.at[_tile_rows(d)], bufs[s].at[_tile_rows(k * tm + t)], sem.at[s])

    def wait_slot(s):
        for _ in range(TOP_K):
            pltpu.make_async_copy(ys_ref.at[_tile_rows(0, tm)], bufs[s].at[_tile_rows(0, tm)], sem.at[s]).wait()

    @pl.when(i == 0)
    def _():
        _load_weight_bf16(wpg_hbm, wpg_ref, w_stage, w_sem)
        _load_weight_bf16(wpp_hbm, wpp_ref, w_stage, w_sem)

        for s, dref in enumerate(prime_refs):
            def issue(t, carry, s=s, dref=dref):
                for k in range(TOP_K):
                    row_copy(dref, t, k, s).start(priority=k % 2)
                return carry
            lax.fori_loop(0, tm, issue, 0, unroll=8)

    def step(s):
        wait_slot(s)
        for t in range(tm):
            for k in range(TOP_K):
                row_copy(ahead_dest_ref, t, k, (s + ahead) % GATHER_SLOTS).start(priority=k % 2)
        proj = jnp.dot(p_ref[...].astype(BF16), wpp_ref[...], preferred_element_type=F32)
        h = h_ref[...]
        for k in range(TOP_K):
            h = h + gate_ref[:, k:k + 1] * _from_row_tiles(bufs[s], (), k * tm, tm)
        ple_gate = jax.nn.sigmoid(jnp.dot(_rms(h, gp_ref[...]).astype(BF16), wpg_ref[...], preferred_element_type=F32))
        h = h + ple_gate * proj
        o_ref[...] = _rms(h, gf_ref[...])

    for s in range(GATHER_SLOTS):
        pl.when(lax.rem(i, GATHER_SLOTS) == s)(functools.partial(step, s))

    @pl.when(i == n_tiles - 1)
    def _():
        for extra in range(ahead):
            wait_slot((n_tiles + extra) % GATHER_SLOTS)


def _combine(dest3, h1, gate, p2, gp, wpg, wpp, gf, ys):
    T = h1.shape[0]
    tm = TM_TOK
    n_tiles = T // tm
    row = lambda w: pl.BlockSpec((tm, w), lambda i: (i, 0))
    dest_spec = lambda ahead: pl.BlockSpec((1, 1, TOP_K * tm), lambda i: (jnp.minimum(i + ahead, n_tiles - 1), 0, 0),
                                           memory_space=pltpu.SMEM)
    gather_buf = pltpu.VMEM((TOP_K * tm * ROW_SUBLANES, LANES), F32)
    ahead = GATHER_SLOTS - 1
    assert n_tiles > ahead
    return pl.pallas_call(
        functools.partial(_combine_kernel, n_tiles=n_tiles),
        grid=(n_tiles,),
        in_specs=[dest_spec(a) for a in range(ahead)] + [dest_spec(ahead),
                  row(D_MODEL), row(TOP_K), row(PLE_DIM), _resident((1, D_MODEL)), _resident((1, D_MODEL))]
                 + [pl.BlockSpec(memory_space=pl.ANY)] * 3,
        out_specs=row(D_MODEL),
        out_shape=jax.ShapeDtypeStruct((T, D_MODEL), F32),
        scratch_shapes=[gather_buf] * GATHER_SLOTS + [pltpu.SemaphoreType.DMA((GATHER_SLOTS,)),
                        pltpu.VMEM(wpg.shape, BF16), pltpu.VMEM(wpp.shape, BF16)] + _weight_stage(D_MODEL),
        compiler_params=_cparams("arbitrary"),
        name="combine",
    )(*([dest3] * (ahead + 1)), h1, gate, p2, gp, gf, wpg, wpp, ys)


def _layer(h, p_i, g_mix, w_in, rel_bias, w_att_out, ln_v_g, ln_v_b, w_spatial, b_spatial, w_gmlp_out, w_out,
           g_moe, w_router, b_router, w_gate_up, b_gate_up, w_down, b_down, g_ple, w_ple_gate, w_ple_proj,
           g_final, B, S):
    T = B * S
    row = lambda v: v.reshape(1, -1).astype(F32)

    assert S % TM_PROJ == 0
    *att_in, uv, gl, wgu_bf = _in_proj(h, row(g_mix), w_in.astype(F32), w_gate_up.astype(F32), B, S)

    outs, lses = [], []
    for g, (window, dilation) in enumerate(ATT_GROUPS):
        assert window // dilation == BLK and S % (dilation * BLK) == 0
        bias = _bias_table(rel_bias[:, g * HEADS_PER_GROUP:(g + 1) * HEADS_PER_GROUP], dilation)
        o, lse = _attention_group(att_in[g], bias, dilation, B, S)
        outs.append(o)
        lses.append(lse)

    causal = jnp.asarray(np.tril(np.ones((CHUNK, CHUNK), np.float32)))
    w_c = (w_spatial.astype(F32) * causal[None]).astype(BF16)
    wc2 = jnp.concatenate([w_c[0::2], w_c[1::2]], axis=2)
    bs = jnp.repeat(b_spatial.astype(F32).T, GMLP_GD, axis=1)
    h1, wd_bf = _mix(h, outs, lses, uv, gl, w_att_out.astype(F32), w_gmlp_out.astype(F32), w_out.astype(F32),
                     wc2, bs, row(ln_v_g), row(ln_v_b), w_down.astype(F32), S)

    wr_hi = w_router.astype(BF16)
    wr_lo = (w_router.astype(F32) - wr_hi.astype(F32)).astype(BF16)
    eidx, gate, rank, counts = _router(h1, row(g_moe), jnp.concatenate([wr_hi, wr_lo], axis=1).T,
                                       b_router.reshape(-1, 1).astype(F32))
    cnt = counts[:, 0].astype(jnp.int32)
    blk_counts = (cnt + TM_EXP - 1) // TM_EXP
    blk_end = jnp.cumsum(blk_counts)
    pad_start = (blk_end - blk_counts) * TM_EXP
    n_blocks = T * TOP_K // TM_EXP + N_EXPERTS
    n_valid = blk_end[-1:].astype(jnp.int32)
    blk = jnp.minimum(jnp.arange(n_blocks, dtype=jnp.int32), n_valid[0] - 1)
    block_expert = jnp.minimum(jnp.sum((blk_end[None, :] <= blk[:, None]).astype(jnp.int32), axis=1), N_EXPERTS - 1)
    expert_ids = jnp.arange(N_EXPERTS, dtype=jnp.int32)
    rows_end = jnp.sum(jnp.where(block_expert[:, None] == expert_ids, pad_start + cnt, 0), axis=1)
    block_rows = jnp.clip(rows_end - jnp.arange(n_blocks, dtype=jnp.int32) * TM_EXP, 0, TM_EXP).astype(jnp.int32)
    dest = rank + jnp.sum(jnp.where(eidx[..., None] == expert_ids, pad_start, 0), axis=-1)
    split = TM_PROJ // TM_TOK
    dest3 = jnp.transpose(dest.reshape(T // TM_PROJ, TOP_K, split, TM_TOK), (0, 2, 1, 3)).reshape(
        T // TM_TOK, 1, TOP_K * TM_TOK)
    gate = jnp.transpose(gate, (0, 2, 1)).reshape(T, TOP_K)

    last_block = jnp.maximum(blk_end - 1, 0).astype(jnp.int32)
    xs = _dispatch(last_block, n_valid, dest3, h1, row(g_moe), n_blocks * TM_EXP)
    ys = _experts(block_expert, n_valid, block_rows, xs, wgu_bf, b_gate_up.reshape(N_EXPERTS, 1, -1).astype(F32),
                  wd_bf, b_down.reshape(N_EXPERTS, 1, -1).astype(F32))
    return _combine(dest3, h1, gate, p_i, row(g_ple), w_ple_gate.astype(F32), w_ple_proj.astype(F32),
                    row(g_final), ys)


def kernel(x, p, g_mix, w_in, rel_bias, w_att_out, ln_v_g, ln_v_b, w_spatial, b_spatial, w_gmlp_out, w_out, g_moe, w_router, b_router, w_gate_up, b_gate_up, w_down, b_down, g_ple, w_ple_gate, w_ple_proj, g_final):
    B, S, D = x.shape
    depth = p.shape[0]
    assert depth == 1, "the final RMSNorm is fused into the (single) layer's last kernel"
    out = _layer(x.reshape(B * S, D), p[0].reshape(B * S, PLE_DIM), g_mix[0], w_in[0], rel_bias, w_att_out[0],
                 ln_v_g[0], ln_v_b[0], w_spatial[0], b_spatial[0], w_gmlp_out[0], w_out[0], g_moe[0], w_router[0],
                 b_router[0], w_gate_up[0], b_gate_up[0], w_down[0], b_down[0], g_ple[0], w_ple_gate[0],
                 w_ple_proj[0], g_final, B, S)
    return out.reshape(B, S, D)
```

```python
import functools

import jax
import jax.numpy as jnp
import numpy as np
from jax import lax
from jax.experimental import pallas as pl
from jax.experimental.pallas import tpu as pltpu

F32 = jnp.float32
BF16 = jnp.bfloat16

D_MODEL = 1024
HEAD_DIM = 64
ATT_GROUPS = ((128, 1), (512, 4), (2048, 16))
HEADS_PER_GROUP = 4
GROUP_W = HEADS_PER_GROUP * HEAD_DIM
N_DIL = len(ATT_GROUPS)
ATT_W = N_DIL * GROUP_W
BLK = 128
REL_BUCKETS = 32
REL_MAX_DIST = 2048
CHUNK = 128
GMLP_W = 768
GMLP_GD = 64
N_BRANCH = 2
IN_W = 3 * ATT_W + 2 * GMLP_W + N_BRANCH * D_MODEL
N_EXPERTS = 32
TOP_K = 4
D_EXPERT = D_MODEL
SWIGLU_LIMIT = 7.0
SWIGLU_ALPHA = 1.702
PLE_DIM = 256
EPS = 1e-6
MASKED = -1e30
LOG2E = float(np.log2(np.e))
LN2 = float(np.log(2.0))

QKV_G = 3 * GROUP_W

LANES = 128
ROW_SUBLANES = D_MODEL // LANES
assert ROW_SUBLANES == 8
MXU_N = 256
VMEM_LIMIT = 56 * 1024 * 1024

TM_PROJ = 512
TM_TOK = 256
TM_EXP = 512
assert TM_EXP % TM_TOK == 0


def _cparams(*sem):
    return pltpu.CompilerParams(dimension_semantics=sem, vmem_limit_bytes=VMEM_LIMIT)


def _resident(shape):
    nd = len(shape)
    return pl.BlockSpec(shape, lambda *_: (0,) * nd, pipeline_mode=pl.Buffered(1))


def _rms(x, g):
    return x * lax.rsqrt(jnp.mean(x * x, axis=-1, keepdims=True) + EPS) * g


def _load_weight_bf16(w_hbm, w_bf, stage, sem):
    rows = stage.shape[1]
    n_chunks = w_hbm.shape[0] // rows
    assert n_chunks * rows == w_hbm.shape[0] and stage.shape[2] == w_hbm.shape[1]

    def chunk(c):
        return pltpu.make_async_copy(w_hbm.at[pl.ds(c * rows, rows)], stage.at[c % 2], sem.at[c % 2])

    chunk(0).start()
    for c in range(n_chunks):
        if c + 1 < n_chunks:
            chunk(c + 1).start()
        chunk(c).wait()
        w_bf[c * rows:(c + 1) * rows, :] = stage[c % 2].astype(BF16)


def _expert_slice_spec(n_steps, width):
    per_expert = n_steps // N_EXPERTS
    assert per_expert * N_EXPERTS == n_steps and D_MODEL % per_expert == 0
    return pl.BlockSpec((1, D_MODEL // per_expert, width), lambda i: (i // per_expert, i % per_expert, 0))


def _inproj_kernel(x0_ref, xn_ref, g_ref, w_hbm, we_ref, a1_ref, a2_ref, a3_ref, uv_ref, gl_ref, we_bf_ref,
                   scr, w_ref, w_stage, w_sem, n_scr, n_tmp):
    i = pl.program_id(0)
    slot = lax.rem(i, 2)

    @pl.when(i == 0)
    def _():
        _load_weight_bf16(w_hbm, w_ref, w_stage, w_sem)
        n_scr[0] = _rms(x0_ref[...], g_ref[...]).astype(BF16)

    tm = xn_ref.shape[0]
    n = n_scr[slot]
    att_refs = (a1_ref, a2_ref, a3_ref)
    n_att, n_uv = 3 * ATT_W // MXU_N, 2 * GMLP_W // MXU_N
    n_pieces = 16
    side_rows, we_rows = tm // n_pieces, we_ref.shape[1] // n_pieces
    for c in range(IN_W // MXU_N):
        if c < n_pieces:
            we_bf_ref[0, c * we_rows:(c + 1) * we_rows, :] = we_ref[0, c * we_rows:(c + 1) * we_rows, :].astype(BF16)
            rows = slice(c * side_rows, (c + 1) * side_rows)
            n_tmp[rows, :] = _rms(xn_ref[rows, :], g_ref[...]).astype(BF16)
        z = jnp.dot(n, w_ref[:, c * MXU_N:(c + 1) * MXU_N], preferred_element_type=F32)
        if c < n_att:
            which, g = divmod(c, N_DIL)
            d = ATT_GROUPS[g][1]
            dst = att_refs[g]
            cols = slice(which * GROUP_W, (which + 1) * GROUP_W)
            if d == 1:
                dst[0, 0, :, cols] = z.astype(BF16)
                continue
            scr[0] = z[:, :LANES]
            scr[1] = z[:, LANES:]
            for r in range(d):
                zr = jnp.concatenate([scr[0, pl.ds(r, tm // d, stride=d), :],
                                      scr[1, pl.ds(r, tm // d, stride=d), :]], axis=1)
                dst[0, r, :, cols] = zr.astype(BF16)
        elif c < n_att + n_uv:
            uv_ref[:, (c - n_att) * MXU_N:(c - n_att + 1) * MXU_N] = z.astype(BF16)
        else:
            gl_ref[:, (c - n_att - n_uv) * MXU_N:(c - n_att - n_uv + 1) * MXU_N] = z.astype(BF16)
    n_scr[1 - slot] = n_tmp[...]


def _plane_spec(d, tm, tiles_per_seq, width):
    return pl.BlockSpec((1, d, tm // d, width), lambda i: (i // tiles_per_seq, 0, i % tiles_per_seq, 0))


W_STAGE_ROWS = 128


def _weight_stage(width):
    return [pltpu.VMEM((2, W_STAGE_ROWS, width), F32), pltpu.SemaphoreType.DMA((2,))]


def _in_proj(x2, g, w, w_expert, B, S):
    T = x2.shape[0]
    tm = TM_PROJ
    row = lambda w: pl.BlockSpec((tm, w), lambda i: (i, 0))
    dils = [d for _, d in ATT_GROUPS]
    we_spec = _expert_slice_spec(T // tm, w_expert.shape[2])
    n_steps = T // tm
    first_tile = pl.BlockSpec((tm, D_MODEL), lambda i: (0, 0), pipeline_mode=pl.Buffered(1))
    next_tile = pl.BlockSpec((tm, D_MODEL), lambda i: (jnp.minimum(i + 1, n_steps - 1), 0))
    return pl.pallas_call(
        _inproj_kernel,
        grid=(n_steps,),
        in_specs=[first_tile, next_tile, _resident((1, D_MODEL)), pl.BlockSpec(memory_space=pl.ANY), we_spec],
        out_specs=[_plane_spec(d, tm, S // tm, QKV_G) for d in dils] + [row(2 * GMLP_W), row(N_BRANCH * D_MODEL), we_spec],
        out_shape=[jax.ShapeDtypeStruct((B, d, S // d, QKV_G), BF16) for d in dils]
                  + [jax.ShapeDtypeStruct((T, 2 * GMLP_W), BF16),
                     jax.ShapeDtypeStruct((T, N_BRANCH * D_MODEL), BF16),
                     jax.ShapeDtypeStruct(w_expert.shape, BF16)],
        scratch_shapes=[pltpu.VMEM((2, tm, LANES), F32), pltpu.VMEM((D_MODEL, IN_W), BF16)] + _weight_stage(IN_W)
                       + [pltpu.VMEM((2, tm, D_MODEL), BF16), pltpu.VMEM((tm, D_MODEL), BF16)],
        compiler_params=_cparams("arbitrary"),
        name="in_proj",
    )(x2, x2, g, w, w_expert)


def _t5_bucket(n):
    exact = REL_BUCKETS // 2
    nf = np.maximum(n, 1).astype(np.float32)
    large = exact + (np.log(nf / exact) / np.log(REL_MAX_DIST / exact) * (REL_BUCKETS - exact)).astype(np.int32)
    large = np.minimum(large, REL_BUCKETS - 1)
    return np.where(n < exact, n, large).astype(np.int32)


def _bias_table(rel_bias_g, dilation):
    n = 3 * BLK
    dist = 2 * BLK - 1 - np.arange(n)
    valid = (dist >= 0) & (dist <= BLK)
    bucket = _t5_bucket(np.clip(dist, 0, BLK) * dilation)
    c = jnp.where(jnp.asarray(valid)[None, :], rel_bias_g.astype(F32)[bucket].T * LOG2E, MASKED)
    shifted = jnp.tile(c, (1, BLK))[:, :BLK * (n - 1)].reshape(HEADS_PER_GROUP, BLK, n - 1)
    return shifted[:, :, BLK - 1:].reshape(HEADS_PER_GROUP * BLK, 2 * BLK)


def _attn_kernel(cur_ref, prev_ref, bias_ref, o_ref, lse_ref):
    rg, rb = cur_ref.shape[1], cur_ref.shape[2] // BLK
    starts_sequence = pl.program_id(2) == 0
    lane_head = lax.broadcasted_iota(jnp.int32, (1, GROUP_W), 1) // HEAD_DIM
    scale = HEAD_DIM ** -0.5
    head_bf = [jnp.where(lane_head == h, scale, 0.0).astype(BF16) for h in range(HEADS_PER_GROUP)]
    key_is_prev = lax.broadcasted_iota(jnp.int32, (1, 2 * BLK), 1) < BLK
    nt = (((1,), (1,)), ((), ()))
    qc, kc_, vc_ = slice(0, GROUP_W), slice(GROUP_W, 2 * GROUP_W), slice(2 * GROUP_W, 3 * GROUP_W)

    def by_head(x):
        sel = x[(HEADS_PER_GROUP - 1) * BLK:]
        for h in range(HEADS_PER_GROUP - 2, -1, -1):
            sel = jnp.where(lane_head == h, x[h * BLK:(h + 1) * BLK], sel)
        return sel

    for r, j in [(r, j) for r in range(rg) for j in range(rb)]:
        rows = slice(j * BLK, (j + 1) * BLK)
        prev = prev_ref if j == 0 else cur_ref
        prows = slice(0, BLK) if j == 0 else slice((j - 1) * BLK, j * BLK)
        q = cur_ref[0, r, rows, qc]
        k = jnp.concatenate([prev[0, r, prows, kc_], cur_ref[0, r, rows, kc_]], axis=0)
        v = jnp.concatenate([prev[0, r, prows, vc_], cur_ref[0, r, rows, vc_]], axis=0)
        q_bd = jnp.concatenate([q * head_bf[h] for h in range(HEADS_PER_GROUP)], axis=0)
        s = lax.dot_general(q_bd, k, nt, preferred_element_type=F32) * LOG2E + bias_ref[...]
        if j == 0:
            s = jnp.where(jnp.logical_and(starts_sequence, key_is_prev), MASKED, s)
        m = jnp.max(s, axis=-1, keepdims=True)
        p = jnp.exp2(s - m)
        den = jnp.sum(p, axis=-1, keepdims=True)
        o = jnp.dot(p.astype(BF16), v, preferred_element_type=F32)
        den_h = jnp.broadcast_to(by_head(den), (BLK, GROUP_W))
        o_ref[0, r, rows, :] = (by_head(o) / den_h).astype(BF16)
        lse_ref[0, r, rows, :] = by_head(m) * LN2 + jnp.log(den_h)


ATT_SUBBLOCKS = 16


def _attention_group(a, bias, dilation, B, S):
    sd = S // dilation
    rb = min(ATT_SUBBLOCKS, sd // BLK)
    rg = min(ATT_SUBBLOCKS // rb, dilation)
    o, lse = pl.pallas_call(
        _attn_kernel,
        grid=(B, dilation // rg, sd // (rb * BLK)),
        in_specs=[pl.BlockSpec((1, rg, rb * BLK, QKV_G), lambda b, r, n: (b, r, n, 0)),
                  pl.BlockSpec((1, rg, BLK, QKV_G), lambda b, r, n: (b, r, jnp.maximum(n * rb - 1, 0), 0)),
                  _resident((HEADS_PER_GROUP * BLK, 2 * BLK))],
        out_specs=[pl.BlockSpec((1, rg, rb * BLK, GROUP_W), lambda b, r, n: (b, r, n, 0))] * 2,
        out_shape=[jax.ShapeDtypeStruct((B, dilation, sd, GROUP_W), BF16),
                   jax.ShapeDtypeStruct((B, dilation, sd, GROUP_W), F32)],
        compiler_params=_cparams("parallel", "parallel", "parallel"),
        name=f"attn_d{dilation}",
    )(a, a, bias)
    return o, lse


def _sigmoid(x):
    return 0.5 * jnp.tanh(0.5 * x) + 0.5


def _gelu(x):
    return x * (lax.erf(x * (2.0 ** -0.5)) + 1.0) * 0.5


def _token_major(src_ref, d, scr, slot, tm):
    if d == 1:
        return src_ref[0, 0].astype(F32)
    for r in range(d):
        piece = src_ref[0, r].astype(F32)
        scr[slot, pl.ds(r, tm // d, stride=d), :] = piece[:, :LANES]
        scr[slot + 1, pl.ds(r, tm // d, stride=d), :] = piece[:, LANES:]
    return jnp.concatenate([scr[slot], scr[slot + 1]], axis=1)


def _mix_kernel(x_ref, o1_ref, o2_ref, o3_ref, l1_ref, l2_ref, l3_ref, uv_ref, gl_ref,
                wa_hbm, wg_hbm, wo_hbm, wc_ref, bs_ref, lng_ref, lnb_ref, we_ref, h_ref, we_bf_ref, g_scr, t_scr,
                wa_ref, wg_ref, wo_ref, w_stage, w_sem):
    @pl.when(pl.program_id(0) == 0)
    def _():
        for w_hbm, w_bf in ((wa_hbm, wa_ref), (wg_hbm, wg_ref), (wo_hbm, wo_ref)):
            _load_weight_bf16(w_hbm, w_bf, w_stage, w_sem)

    we_bf_ref[...] = we_ref[...].astype(BF16)

    tm = x_ref.shape[0]
    dils = [d for _, d in ATT_GROUPS]
    o1, o2, o3 = [_token_major(ref, d, t_scr, 4 * i, tm) for i, (ref, d) in enumerate(zip((o1_ref, o2_ref, o3_ref), dils))]
    l1, l2, l3 = [_token_major(ref, d, t_scr, 4 * i + 2, tm) for i, (ref, d) in enumerate(zip((l1_ref, l2_ref, l3_ref), dils))]
    lm = jnp.maximum(jnp.maximum(l1, l2), l3)
    e1, e2, e3 = jnp.exp(l1 - lm), jnp.exp(l2 - lm), jnp.exp(l3 - lm)
    att = (e1 * o1 + e2 * o2 + e3 * o3) / (e1 + e2 + e3)
    y_att = jnp.dot(att.astype(BF16), wa_ref[...], preferred_element_type=F32)

    zu = _gelu(uv_ref[:, :GMLP_W].astype(F32))
    zv = _gelu(uv_ref[:, GMLP_W:].astype(F32))
    mu = jnp.mean(zv, axis=-1, keepdims=True)
    var = jnp.mean(jnp.square(zv - mu), axis=-1, keepdims=True)
    vn = (zv - mu) * lax.rsqrt(var + EPS) * lng_ref[...] + lnb_ref[...]
    low_half = lax.broadcasted_iota(jnp.int32, (CHUNK, 2 * GMLP_GD), 1) < GMLP_GD
    for c in range(tm // CHUNK):
        rows = slice(c * CHUNK, (c + 1) * CHUNK)
        for s in range(GMLP_W // (2 * GMLP_GD)):
            cols = slice(s * 2 * GMLP_GD, (s + 1) * 2 * GMLP_GD)
            v2 = vn[rows, cols]
            rhs = jnp.concatenate([jnp.where(low_half, v2, 0.0), jnp.where(low_half, 0.0, v2)], axis=0).astype(BF16)
            mixed = jnp.dot(wc_ref[s], rhs, preferred_element_type=F32) + bs_ref[:, cols]
            g_scr[rows, cols] = (zu[rows, cols] * mixed).astype(BF16)
    y_gm = jnp.dot(g_scr[...], wg_ref[...], preferred_element_type=F32)

    gate_a = _sigmoid(gl_ref[:, :D_MODEL].astype(F32))
    gate_g = _sigmoid(gl_ref[:, D_MODEL:].astype(F32))
    merged = (gate_a * y_att + gate_g * y_gm).astype(BF16)
    h_ref[...] = x_ref[...] + jnp.dot(merged, wo_ref[...], preferred_element_type=F32)


def _mix(x2, outs, lses, uv, gl, wa, wg, wo, wc2, bs, lng, lnb, w_expert, S):
    T = x2.shape[0]
    tm = TM_PROJ
    row = lambda w: pl.BlockSpec((tm, w), lambda i: (i, 0))
    att = [_plane_spec(d, tm, S // tm, GROUP_W) for _, d in ATT_GROUPS]
    we_spec = _expert_slice_spec(T // tm, w_expert.shape[2])
    return pl.pallas_call(
        _mix_kernel,
        grid=(T // tm,),
        in_specs=[row(D_MODEL)] + att + att + [row(2 * GMLP_W), row(N_BRANCH * D_MODEL)]
                 + [pl.BlockSpec(memory_space=pl.ANY)] * 3
                 + [_resident(wc2.shape), _resident(bs.shape), _resident(lng.shape), _resident(lnb.shape), we_spec],
        out_specs=[row(D_MODEL), we_spec],
        out_shape=[jax.ShapeDtypeStruct((T, D_MODEL), F32), jax.ShapeDtypeStruct(w_expert.shape, BF16)],
        scratch_shapes=[pltpu.VMEM((tm, GMLP_W), BF16), pltpu.VMEM((4 * N_DIL, tm, LANES), F32),
                        pltpu.VMEM(wa.shape, BF16), pltpu.VMEM(wg.shape, BF16), pltpu.VMEM(wo.shape, BF16)]
                       + _weight_stage(D_MODEL),
        compiler_params=_cparams("arbitrary"),
        name="mix",
    )(x2, *outs, *lses, uv, gl, wa, wg, wo, wc2, bs, lng, lnb, w_expert)


def _router_kernel(h_ref, g_ref, wr_ref, br_ref, upper_ref, eidx_ref, gate_ref, rank_ref, cnt_ref, carry):
    tm = h_ref.shape[0]

    @pl.when(pl.program_id(0) == 0)
    def _():
        carry[...] = jnp.zeros_like(carry)

    hn = _rms(h_ref[...], g_ref[...])
    hi = hn.astype(BF16)
    lo = (hn - hi.astype(F32)).astype(BF16)
    nt = (((1,), (1,)), ((), ()))
    by_hi = lax.dot_general(wr_ref[...], hi, nt, preferred_element_type=F32)
    by_lo = lax.dot_general(wr_ref[:N_EXPERTS, :], lo, nt, preferred_element_type=F32)
    logits = by_hi[:N_EXPERTS] + by_hi[N_EXPERTS:] + by_lo + br_ref[...]
    expert = lax.broadcasted_iota(jnp.int32, (N_EXPERTS, tm), 0)
    vals, hots = [], []
    l = logits
    for k in range(TOP_K):
        m = jnp.max(l, axis=0, keepdims=True)
        idx = jnp.min(jnp.where(l == m, expert, N_EXPERTS), axis=0, keepdims=True)
        hot = expert == idx
        eidx_ref[0, k:k + 1, :] = idx
        vals.append(m)
        hots.append(hot)
        l = jnp.where(hot, -jnp.inf, l)
    ex = [jnp.exp(v - vals[0]) for v in vals]
    tot = ex[0] + ex[1] + ex[2] + ex[3]
    for k in range(TOP_K):
        gate_ref[0, k:k + 1, :] = ex[k] / tot
    multi = jnp.zeros((N_EXPERTS, tm), F32)
    for hot in hots:
        multi = multi + hot.astype(F32)
    before = jnp.dot(multi.astype(BF16), upper_ref[...], preferred_element_type=F32) + carry[...]
    for k in range(TOP_K):
        rank_ref[0, k:k + 1, :] = jnp.sum(jnp.where(hots[k], before, 0.0), axis=0, keepdims=True).astype(jnp.int32)
    carry[...] += jnp.sum(multi, axis=1, keepdims=True)
    cnt_ref[...] = carry[...]


def _router(h1, g, wr_t, br_col):
    T = h1.shape[0]
    tm = TM_PROJ
    upper = jnp.asarray(np.triu(np.ones((tm, tm), np.float32), k=1), BF16)
    k_rows = pl.BlockSpec((1, TOP_K, tm), lambda i: (i, 0, 0))
    k_shape = lambda dt: jax.ShapeDtypeStruct((T // tm, TOP_K, tm), dt)
    return pl.pallas_call(
        _router_kernel,
        grid=(T // tm,),
        in_specs=[pl.BlockSpec((tm, D_MODEL), lambda i: (i, 0)), _resident((1, D_MODEL)),
                  _resident((2 * N_EXPERTS, D_MODEL)), _resident((N_EXPERTS, 1)), _resident((tm, tm))],
        out_specs=[k_rows, k_rows, k_rows, pl.BlockSpec((N_EXPERTS, 1), lambda i: (0, 0))],
        out_shape=[k_shape(jnp.int32), k_shape(F32), k_shape(jnp.int32),
                   jax.ShapeDtypeStruct((N_EXPERTS, 1), F32)],
        scratch_shapes=[pltpu.VMEM((N_EXPERTS, 1), F32)],
        compiler_params=_cparams("arbitrary"),
        name="router",
    )(h1, g, wr_t, br_col, upper)


def _to_row_tiles(ref, lead, value):
    n = value.shape[0]
    for c in range(ROW_SUBLANES):
        ref[(*lead, pl.ds(c, n, stride=ROW_SUBLANES), slice(None))] = value[:, c * LANES:(c + 1) * LANES]


def _from_row_tiles(ref, lead, first, n):
    return jnp.concatenate(
        [ref[(*lead, pl.ds(first * ROW_SUBLANES + c, n, stride=ROW_SUBLANES), slice(None))] for c in range(ROW_SUBLANES)],
        axis=1)


def _tile_rows(idx, n=1):
    return pl.ds(pl.multiple_of(idx * ROW_SUBLANES, ROW_SUBLANES), n * ROW_SUBLANES)


def _dispatch_kernel(last_ref, nv_ref, dest_ref, h_ref, g_ref, xs_ref, buf, sem, zero_sem):
    tm = h_ref.shape[0]
    n_blocks = xs_ref.shape[0] // (TM_EXP * ROW_SUBLANES)
    i = pl.program_id(0)
    slot = lax.rem(i, 2)

    @pl.when(i == 0)
    def _():
        buf[1] = jnp.zeros(buf.shape[1:], F32)

        def zero_block(b):
            for part in range(TM_EXP // tm):
                pltpu.make_async_copy(buf.at[1], xs_ref.at[_tile_rows(b * TM_EXP + part * tm, tm)], zero_sem).start()

        def zero_done():
            for part in range(TM_EXP // tm):
                pltpu.make_async_copy(buf.at[1], xs_ref.at[_tile_rows(0, tm)], zero_sem).wait()

        for e in range(N_EXPERTS):
            zero_block(last_ref[e])
        lax.fori_loop(nv_ref[0], n_blocks, lambda b, c: (zero_block(b), c)[1], 0)
        for e in range(N_EXPERTS):
            zero_done()
        lax.fori_loop(nv_ref[0], n_blocks, lambda b, c: (zero_done(), c)[1], 0)

    _to_row_tiles(buf, (slot,), _rms(h_ref[...], g_ref[...]))

    def issue(t, carry):
        for k in range(TOP_K):
            d = dest_ref[0, 0, k * tm + t]
            pltpu.make_async_copy(buf.at[slot, _tile_rows(t)], xs_ref.at[_tile_rows(d)],
                                  sem.at[slot]).start(priority=k % 2)
        return carry

    lax.fori_loop(0, tm, issue, 0, unroll=8)

    def wait_slot(s):
        for _ in range(TOP_K):
            pltpu.make_async_copy(buf.at[s], xs_ref.at[_tile_rows(0, tm)], sem.at[s]).wait()

    @pl.when(i > 0)
    def _():
        wait_slot(1 - slot)

    @pl.when(i == pl.num_programs(0) - 1)
    def _():
        wait_slot(slot)


def _dispatch(last_block, n_valid, dest3, h1, g, n_slots):
    T = h1.shape[0]
    tm = TM_TOK
    grid_spec = pltpu.PrefetchScalarGridSpec(
        num_scalar_prefetch=2,
        grid=(T // tm,),
        in_specs=[pl.BlockSpec((1, 1, TOP_K * tm), lambda i, lb, nv: (i, 0, 0), memory_space=pltpu.SMEM),
                  pl.BlockSpec((tm, D_MODEL), lambda i, lb, nv: (i, 0)),
                  pl.BlockSpec((1, D_MODEL), lambda i, lb, nv: (0, 0), pipeline_mode=pl.Buffered(1))],
        out_specs=pl.BlockSpec(memory_space=pl.ANY),
        scratch_shapes=[pltpu.VMEM((2, tm * ROW_SUBLANES, LANES), F32), pltpu.SemaphoreType.DMA((2,)),
                        pltpu.SemaphoreType.DMA(())],
    )
    return pl.pallas_call(
        _dispatch_kernel,
        grid_spec=grid_spec,
        out_shape=jax.ShapeDtypeStruct((n_slots * ROW_SUBLANES, LANES), F32),
        compiler_params=_cparams("arbitrary"),
        name="dispatch",
    )(last_block, n_valid, dest3, h1, g)


def _experts_kernel(be_ref, nv_ref, xs_hbm, wgu_ref, bgu_ref, wd_ref, bd_ref, ys_hbm, xbuf, ybuf, xsem, ysem):
    del be_ref
    tm = xbuf.shape[1]
    b = pl.program_id(0)
    n_valid = nv_ref[0]
    slot = lax.rem(b, 2)

    def x_copies(blk, s):
        return [pltpu.make_async_copy(xs_hbm.at[pl.ds(blk * tm, tm), c, :],
                                      xbuf.at[s, :, pl.ds(c * LANES, LANES)], xsem.at[s]) for c in range(ROW_SUBLANES)]

    def y_copies(blk, s):
        return [pltpu.make_async_copy(ybuf.at[s, :, pl.ds(c * LANES, LANES)],
                                      ys_hbm.at[pl.ds(blk * tm, tm), c, :], ysem.at[s]) for c in range(ROW_SUBLANES)]

    @pl.when(jnp.logical_and(b == 0, n_valid > 0))
    def _():
        for cp in x_copies(0, 0):
            cp.start()

    @pl.when(b + 1 < n_valid)
    def _():
        for cp in x_copies(b + 1, 1 - slot):
            cp.start()

    @pl.when(b < n_valid)
    def _():
        for cp in x_copies(b, slot):
            cp.wait()
        x = xbuf[slot].astype(BF16)
        gu = jnp.dot(x, wgu_ref[0], preferred_element_type=F32) + bgu_ref[0]
        glu = jnp.minimum(gu[:, :D_EXPERT], SWIGLU_LIMIT)
        lin = jnp.clip(gu[:, D_EXPERT:], -SWIGLU_LIMIT, SWIGLU_LIMIT)
        act = glu * jax.nn.sigmoid(SWIGLU_ALPHA * glu) * (lin + 1.0)
        ybuf[slot] = jnp.dot(act.astype(BF16), wd_ref[0], preferred_element_type=F32) + bd_ref[0]

    @pl.when(b >= n_valid)
    def _():
        ybuf[slot] = jnp.zeros(ybuf.shape[1:], F32)

    for cp in y_copies(b, slot):
        cp.start()

    @pl.when(b > 0)
    def _():
        for cp in y_copies(b - 1, 1 - slot):
            cp.wait()

    @pl.when(b == pl.num_programs(0) - 1)
    def _():
        for cp in y_copies(b, slot):
            cp.wait()


def _experts(block_expert, n_valid, xs, wgu, bgu, wd, bd):
    tm = TM_EXP
    n_slots = xs.shape[0] // ROW_SUBLANES
    n_blocks = n_slots // tm
    any_space = pl.BlockSpec(memory_space=pl.ANY)
    grid_spec = pltpu.PrefetchScalarGridSpec(
        num_scalar_prefetch=2,
        grid=(n_blocks,),
        in_specs=[any_space,
                  pl.BlockSpec((1, D_MODEL, 2 * D_EXPERT), lambda b, be, nv: (be[b], 0, 0)),
                  pl.BlockSpec((1, 1, 2 * D_EXPERT), lambda b, be, nv: (be[b], 0, 0)),
                  pl.BlockSpec((1, D_EXPERT, D_MODEL), lambda b, be, nv: (be[b], 0, 0)),
                  pl.BlockSpec((1, 1, D_MODEL), lambda b, be, nv: (be[b], 0, 0))],
        out_specs=any_space,
        scratch_shapes=[pltpu.VMEM((2, tm, D_MODEL), F32), pltpu.VMEM((2, tm, D_MODEL), F32),
                        pltpu.SemaphoreType.DMA((2,)), pltpu.SemaphoreType.DMA((2,))],
    )
    ys = pl.pallas_call(
        _experts_kernel,
        grid_spec=grid_spec,
        out_shape=jax.ShapeDtypeStruct((n_slots, ROW_SUBLANES, LANES), F32),
        compiler_params=_cparams("arbitrary"),
        name="experts",
    )(block_expert, n_valid, xs.reshape(n_slots, ROW_SUBLANES, LANES), wgu, bgu, wd, bd)
    return ys.reshape(xs.shape)


GATHER_SLOTS = 3


def _combine_kernel(*refs, n_tiles):
    ahead = GATHER_SLOTS - 1
    prime_refs, refs = refs[:ahead], refs[ahead:]
    (ahead_dest_ref, h_ref, gate_ref, p_ref, gp_ref, gf_ref, wpg_hbm, wpp_hbm, ys_ref, o_ref), refs = refs[:10], refs[10:]
    bufs, (sem, wpg_ref, wpp_ref, w_stage, w_sem) = refs[:GATHER_SLOTS], refs[GATHER_SLOTS:]
    tm = h_ref.shape[0]
    i = pl.program_id(0)

    def row_copy(dref, t, k, s):
        d = dref[0, 0, k * tm + t]
        return pltpu.make_async_copy(ys_ref.at[_tile_rows(d)], bufs[s].at[_tile_rows(k * tm + t)], sem.at[s])

    def wait_slot(s):
        for _ in range(TOP_K):
            pltpu.make_async_copy(ys_ref.at[_tile_rows(0, tm)], bufs[s].at[_tile_rows(0, tm)], sem.at[s]).wait()

    @pl.when(i == 0)
    def _():
        _load_weight_bf16(wpg_hbm, wpg_ref, w_stage, w_sem)
        _load_weight_bf16(wpp_hbm, wpp_ref, w_stage, w_sem)

        for s, dref in enumerate(prime_refs):
            def issue(t, carry, s=s, dref=dref):
                for k in range(TOP_K):
                    row_copy(dref, t, k, s).start(priority=k % 2)
                return carry
            lax.fori_loop(0, tm, issue, 0, unroll=8)

    def step(s):
        wait_slot(s)
        for t in range(tm):
            for k in range(TOP_K):
                row_copy(ahead_dest_ref, t, k, (s + ahead) % GATHER_SLOTS).start(priority=k % 2)
        proj = jnp.dot(p_ref[...].astype(BF16), wpp_ref[...], preferred_element_type=F32)
        h = h_ref[...]
        for k in range(TOP_K):
            h = h + gate_ref[:, k:k + 1] * _from_row_tiles(bufs[s], (), k * tm, tm)
        ple_gate = jax.nn.sigmoid(jnp.dot(_rms(h, gp_ref[...]).astype(BF16), wpg_ref[...], preferred_element_type=F32))
        h = h + ple_gate * proj
        o_ref[...] = _rms(h, gf_ref[...])

    for s in range(GATHER_SLOTS):
        pl.when(lax.rem(i, GATHER_SLOTS) == s)(functools.partial(step, s))

    @pl.when(i == n_tiles - 1)
    def _():
        for extra in range(ahead):
            wait_slot((n_tiles + extra) % GATHER_SLOTS)


def _combine(dest3, h1, gate, p2, gp, wpg, wpp, gf, ys):
    T = h1.shape[0]
    tm = TM_TOK
    n_tiles = T // tm
    row = lambda w: pl.BlockSpec((tm, w), lambda i: (i, 0))
    dest_spec = lambda ahead: pl.BlockSpec((1, 1, TOP_K * tm), lambda i: (jnp.minimum(i + ahead, n_tiles - 1), 0, 0),
                                           memory_space=pltpu.SMEM)
    gather_buf = pltpu.VMEM((TOP_K * tm * ROW_SUBLANES, LANES), F32)
    ahead = GATHER_SLOTS - 1
    assert n_tiles > ahead
    return pl.pallas_call(
        functools.partial(_combine_kernel, n_tiles=n_tiles),
        grid=(n_tiles,),
        in_specs=[dest_spec(a) for a in range(ahead)] + [dest_spec(ahead),
                  row(D_MODEL), row(TOP_K), row(PLE_DIM), _resident((1, D_MODEL)), _resident((1, D_MODEL))]
                 + [pl.BlockSpec(memory_space=pl.ANY)] * 3,
        out_specs=row(D_MODEL),
        out_shape=jax.ShapeDtypeStruct((T, D_MODEL), F32),
        scratch_shapes=[gather_buf] * GATHER_SLOTS + [pltpu.SemaphoreType.DMA((GATHER_SLOTS,)),
                        pltpu.VMEM(wpg.shape, BF16), pltpu.VMEM(wpp.shape, BF16)] + _weight_stage(D_MODEL),
        compiler_params=_cparams("arbitrary"),
        name="combine",
    )(*([dest3] * (ahead + 1)), h1, gate, p2, gp, gf, wpg, wpp, ys)


def _layer(h, p_i, g_mix, w_in, rel_bias, w_att_out, ln_v_g, ln_v_b, w_spatial, b_spatial, w_gmlp_out, w_out,
           g_moe, w_router, b_router, w_gate_up, b_gate_up, w_down, b_down, g_ple, w_ple_gate, w_ple_proj,
           g_final, B, S):
    T = B * S
    row = lambda v: v.reshape(1, -1).astype(F32)

    assert S % TM_PROJ == 0
    *att_in, uv, gl, wgu_bf = _in_proj(h, row(g_mix), w_in.astype(F32), w_gate_up.astype(F32), B, S)

    outs, lses = [], []
    for g, (window, dilation) in enumerate(ATT_GROUPS):
        assert window // dilation == BLK and S % (dilation * BLK) == 0
        bias = _bias_table(rel_bias[:, g * HEADS_PER_GROUP:(g + 1) * HEADS_PER_GROUP], dilation)
        o, lse = _attention_group(att_in[g], bias, dilation, B, S)
        outs.append(o)
        lses.append(lse)

    causal = jnp.asarray(np.tril(np.ones((CHUNK, CHUNK), np.float32)))
    w_c = (w_spatial.astype(F32) * causal[None]).astype(BF16)
    wc2 = jnp.concatenate([w_c[0::2], w_c[1::2]], axis=2)
    bs = jnp.repeat(b_spatial.astype(F32).T, GMLP_GD, axis=1)
    h1, wd_bf = _mix(h, outs, lses, uv, gl, w_att_out.astype(F32), w_gmlp_out.astype(F32), w_out.astype(F32),
                     wc2, bs, row(ln_v_g), row(ln_v_b), w_down.astype(F32), S)

    wr_hi = w_router.astype(BF16)
    wr_lo = (w_router.astype(F32) - wr_hi.astype(F32)).astype(BF16)
    eidx, gate, rank, counts = _router(h1, row(g_moe), jnp.concatenate([wr_hi, wr_lo], axis=1).T,
                                       b_router.reshape(-1, 1).astype(F32))
    cnt = counts[:, 0].astype(jnp.int32)
    blk_counts = (cnt + TM_EXP - 1) // TM_EXP
    blk_end = jnp.cumsum(blk_counts)
    pad_start = (blk_end - blk_counts) * TM_EXP
    n_blocks = T * TOP_K // TM_EXP + N_EXPERTS
    n_valid = blk_end[-1:].astype(jnp.int32)
    blk = jnp.minimum(jnp.arange(n_blocks, dtype=jnp.int32), n_valid[0] - 1)
    block_expert = jnp.minimum(jnp.sum((blk_end[None, :] <= blk[:, None]).astype(jnp.int32), axis=1), N_EXPERTS - 1)
    expert_ids = jnp.arange(N_EXPERTS, dtype=jnp.int32)
    dest = rank + jnp.sum(jnp.where(eidx[..., None] == expert_ids, pad_start, 0), axis=-1)
    split = TM_PROJ // TM_TOK
    dest3 = jnp.transpose(dest.reshape(T // TM_PROJ, TOP_K, split, TM_TOK), (0, 2, 1, 3)).reshape(
        T // TM_TOK, 1, TOP_K * TM_TOK)
    gate = jnp.transpose(gate, (0, 2, 1)).reshape(T, TOP_K)

    last_block = jnp.maximum(blk_end - 1, 0).astype(jnp.int32)
    xs = _dispatch(last_block, n_valid, dest3, h1, row(g_moe), n_blocks * TM_EXP)
    ys = _experts(block_expert, n_valid, xs, wgu_bf, b_gate_up.reshape(N_EXPERTS, 1, -1).astype(F32),
                  wd_bf, b_down.reshape(N_EXPERTS, 1, -1).astype(F32))
    return _combine(dest3, h1, gate, p_i, row(g_ple), w_ple_gate.astype(F32), w_ple_proj.astype(F32),
                    row(g_final), ys)


def kernel(x, p, g_mix, w_in, rel_bias, w_att_out, ln_v_g, ln_v_b, w_spatial, b_spatial, w_gmlp_out, w_out, g_moe, w_router, b_router, w_gate_up, b_gate_up, w_down, b_down, g_ple, w_ple_gate, w_ple_proj, g_final):
    B, S, D = x.shape
    depth = p.shape[0]
    assert depth == 1, "the final RMSNorm is fused into the (single) layer's last kernel"
    out = _layer(x.reshape(B * S, D), p[0].reshape(B * S, PLE_DIM), g_mix[0], w_in[0], rel_bias, w_att_out[0],
                 ln_v_g[0], ln_v_b[0], w_spatial[0], b_spatial[0], w_gmlp_out[0], w_out[0], g_moe[0], w_router[0],
                 b_router[0], w_gate_up[0], b_gate_up[0], w_down[0], b_down[0], g_ple[0], w_ple_gate[0],
                 w_ple_proj[0], g_final, B, S)
    return out.reshape(B, S, D)
```

```python
import functools

import jax
import jax.numpy as jnp
import numpy as np
from jax import lax
from jax.experimental import pallas as pl
from jax.experimental.pallas import tpu as pltpu

F32 = jnp.float32
BF16 = jnp.bfloat16

D_MODEL = 1024
HEAD_DIM = 64
ATT_GROUPS = ((128, 1), (512, 4), (2048, 16))
HEADS_PER_GROUP = 4
GROUP_W = HEADS_PER_GROUP * HEAD_DIM
N_DIL = len(ATT_GROUPS)
ATT_W = N_DIL * GROUP_W
BLK = 128
REL_BUCKETS = 32
REL_MAX_DIST = 2048
CHUNK = 128
GMLP_W = 768
GMLP_GD = 64
N_BRANCH = 2
IN_W = 3 * ATT_W + 2 * GMLP_W + N_BRANCH * D_MODEL
N_EXPERTS = 32
TOP_K = 4
D_EXPERT = D_MODEL
SWIGLU_LIMIT = 7.0
SWIGLU_ALPHA = 1.702
PLE_DIM = 256
EPS = 1e-6
MASKED = -1e30
LOG2E = float(np.log2(np.e))
LN2 = float(np.log(2.0))

QKV_G = 3 * GROUP_W

LANES = 128
ROW_SUBLANES = D_MODEL // LANES
assert ROW_SUBLANES == 8
MXU_N = 256
VMEM_LIMIT = 56 * 1024 * 1024

TM_PROJ = 512
TM_TOK = 256
TM_EXP = 512
assert TM_EXP % TM_TOK == 0


def _cparams(*sem):
    return pltpu.CompilerParams(dimension_semantics=sem, vmem_limit_bytes=VMEM_LIMIT)


def _resident(shape):
    nd = len(shape)
    return pl.BlockSpec(shape, lambda *_: (0,) * nd, pipeline_mode=pl.Buffered(1))


def _rms(x, g):
    return x * lax.rsqrt(jnp.mean(x * x, axis=-1, keepdims=True) + EPS) * g


def _load_weight_bf16(w_hbm, w_bf, stage, sem):
    rows = stage.shape[1]
    n_chunks = w_hbm.shape[0] // rows
    assert n_chunks * rows == w_hbm.shape[0] and stage.shape[2] == w_hbm.shape[1]

    def chunk(c):
        return pltpu.make_async_copy(w_hbm.at[pl.ds(c * rows, rows)], stage.at[c % 2], sem.at[c % 2])

    chunk(0).start()
    for c in range(n_chunks):
        if c + 1 < n_chunks:
            chunk(c + 1).start()
        chunk(c).wait()
        w_bf[c * rows:(c + 1) * rows, :] = stage[c % 2].astype(BF16)


def _expert_slice_spec(n_steps, width):
    per_expert = n_steps // N_EXPERTS
    assert per_expert * N_EXPERTS == n_steps and D_MODEL % per_expert == 0
    return pl.BlockSpec((1, D_MODEL // per_expert, width), lambda i: (i // per_expert, i % per_expert, 0))


def _inproj_kernel(x0_ref, xn_ref, g_ref, w_hbm, we_ref, a1_ref, a2_ref, a3_ref, uv_ref, gl_ref, we_bf_ref,
                   scr, w_ref, w_stage, w_sem, n_scr, n_tmp):
    i = pl.program_id(0)
    slot = lax.rem(i, 2)

    @pl.when(i == 0)
    def _():
        _load_weight_bf16(w_hbm, w_ref, w_stage, w_sem)
        n_scr[0] = _rms(x0_ref[...], g_ref[...]).astype(BF16)

    tm = xn_ref.shape[0]
    n = n_scr[slot]
    att_refs = (a1_ref, a2_ref, a3_ref)
    n_att, n_uv = 3 * ATT_W // MXU_N, 2 * GMLP_W // MXU_N
    n_pieces = 16
    side_rows, we_rows = tm // n_pieces, we_ref.shape[1] // n_pieces
    for c in range(IN_W // MXU_N):
        if c < n_pieces:
            we_bf_ref[0, c * we_rows:(c + 1) * we_rows, :] = we_ref[0, c * we_rows:(c + 1) * we_rows, :].astype(BF16)
            rows = slice(c * side_rows, (c + 1) * side_rows)
            n_tmp[rows, :] = _rms(xn_ref[rows, :], g_ref[...]).astype(BF16)
        z = jnp.dot(n, w_ref[:, c * MXU_N:(c + 1) * MXU_N], preferred_element_type=F32)
        if c < n_att:
            which, g = divmod(c, N_DIL)
            d = ATT_GROUPS[g][1]
            dst = att_refs[g]
            cols = slice(which * GROUP_W, (which + 1) * GROUP_W)
            if d == 1:
                dst[0, 0, :, cols] = z.astype(BF16)
                continue
            scr[0] = z[:, :LANES]
            scr[1] = z[:, LANES:]
            for r in range(d):
                zr = jnp.concatenate([scr[0, pl.ds(r, tm // d, stride=d), :],
                                      scr[1, pl.ds(r, tm // d, stride=d), :]], axis=1)
                dst[0, r, :, cols] = zr.astype(BF16)
        elif c < n_att + n_uv:
            uv_ref[:, (c - n_att) * MXU_N:(c - n_att + 1) * MXU_N] = z.astype(BF16)
        else:
            gl_ref[:, (c - n_att - n_uv) * MXU_N:(c - n_att - n_uv + 1) * MXU_N] = z.astype(BF16)
    n_scr[1 - slot] = n_tmp[...]


def _plane_spec(d, tm, tiles_per_seq, width):
    return pl.BlockSpec((1, d, tm // d, width), lambda i: (i // tiles_per_seq, 0, i % tiles_per_seq, 0))


W_STAGE_ROWS = 128


def _weight_stage(width):
    return [pltpu.VMEM((2, W_STAGE_ROWS, width), F32), pltpu.SemaphoreType.DMA((2,))]


def _in_proj(x2, g, w, w_expert, B, S):
    T = x2.shape[0]
    tm = TM_PROJ
    row = lambda w: pl.BlockSpec((tm, w), lambda i: (i, 0))
    dils = [d for _, d in ATT_GROUPS]
    we_spec = _expert_slice_spec(T // tm, w_expert.shape[2])
    n_steps = T // tm
    first_tile = pl.BlockSpec((tm, D_MODEL), lambda i: (0, 0), pipeline_mode=pl.Buffered(1))
    next_tile = pl.BlockSpec((tm, D_MODEL), lambda i: (jnp.minimum(i + 1, n_steps - 1), 0))
    return pl.pallas_call(
        _inproj_kernel,
        grid=(n_steps,),
        in_specs=[first_tile, next_tile, _resident((1, D_MODEL)), pl.BlockSpec(memory_space=pl.ANY), we_spec],
        out_specs=[_plane_spec(d, tm, S // tm, QKV_G) for d in dils] + [row(2 * GMLP_W), row(N_BRANCH * D_MODEL), we_spec],
        out_shape=[jax.ShapeDtypeStruct((B, d, S // d, QKV_G), BF16) for d in dils]
                  + [jax.ShapeDtypeStruct((T, 2 * GMLP_W), BF16),
                     jax.ShapeDtypeStruct((T, N_BRANCH * D_MODEL), BF16),
                     jax.ShapeDtypeStruct(w_expert.shape, BF16)],
        scratch_shapes=[pltpu.VMEM((2, tm, LANES), F32), pltpu.VMEM((D_MODEL, IN_W), BF16)] + _weight_stage(IN_W)
                       + [pltpu.VMEM((2, tm, D_MODEL), BF16), pltpu.VMEM((tm, D_MODEL), BF16)],
        compiler_params=_cparams("arbitrary"),
        name="in_proj",
    )(x2, x2, g, w, w_expert)


def _t5_bucket(n):
    exact = REL_BUCKETS // 2
    nf = np.maximum(n, 1).astype(np.float32)
    large = exact + (np.log(nf / exact) / np.log(REL_MAX_DIST / exact) * (REL_BUCKETS - exact)).astype(np.int32)
    large = np.minimum(large, REL_BUCKETS - 1)
    return np.where(n < exact, n, large).astype(np.int32)


def _bias_table(rel_bias_g, dilation):
    n = 3 * BLK
    dist = 2 * BLK - 1 - np.arange(n)
    valid = (dist >= 0) & (dist <= BLK)
    bucket = _t5_bucket(np.clip(dist, 0, BLK) * dilation)
    c = jnp.where(jnp.asarray(valid)[None, :], rel_bias_g.astype(F32)[bucket].T * LOG2E, MASKED)
    shifted = jnp.tile(c, (1, BLK))[:, :BLK * (n - 1)].reshape(HEADS_PER_GROUP, BLK, n - 1)
    return shifted[:, :, BLK - 1:].reshape(HEADS_PER_GROUP * BLK, 2 * BLK)


def _attn_kernel(cur_ref, prev_ref, bias_ref, o_ref, lse_ref):
    rg, rb = cur_ref.shape[1], cur_ref.shape[2] // BLK
    starts_sequence = pl.program_id(2) == 0
    lane_head = lax.broadcasted_iota(jnp.int32, (1, GROUP_W), 1) // HEAD_DIM
    scale = HEAD_DIM ** -0.5
    head_bf = [jnp.where(lane_head == h, scale, 0.0).astype(BF16) for h in range(HEADS_PER_GROUP)]
    key_is_prev = lax.broadcasted_iota(jnp.int32, (1, 2 * BLK), 1) < BLK
    nt = (((1,), (1,)), ((), ()))
    qc, kc_, vc_ = slice(0, GROUP_W), slice(GROUP_W, 2 * GROUP_W), slice(2 * GROUP_W, 3 * GROUP_W)

    def by_head(x):
        sel = x[(HEADS_PER_GROUP - 1) * BLK:]
        for h in range(HEADS_PER_GROUP - 2, -1, -1):
            sel = jnp.where(lane_head == h, x[h * BLK:(h + 1) * BLK], sel)
        return sel

    for r, j in [(r, j) for r in range(rg) for j in range(rb)]:
        rows = slice(j * BLK, (j + 1) * BLK)
        prev = prev_ref if j == 0 else cur_ref
        prows = slice(0, BLK) if j == 0 else slice((j - 1) * BLK, j * BLK)
        q = cur_ref[0, r, rows, qc]
        k = jnp.concatenate([prev[0, r, prows, kc_], cur_ref[0, r, rows, kc_]], axis=0)
        v = jnp.concatenate([prev[0, r, prows, vc_], cur_ref[0, r, rows, vc_]], axis=0)
        q_bd = jnp.concatenate([q * head_bf[h] for h in range(HEADS_PER_GROUP)], axis=0)
        s = lax.dot_general(q_bd, k, nt, preferred_element_type=F32) * LOG2E + bias_ref[...]
        if j == 0:
            s = jnp.where(jnp.logical_and(starts_sequence, key_is_prev), MASKED, s)
        m = jnp.max(s, axis=-1, keepdims=True)
        p = jnp.exp2(s - m)
        den = jnp.sum(p, axis=-1, keepdims=True)
        o = jnp.dot(p.astype(BF16), v, preferred_element_type=F32)
        den_h = jnp.broadcast_to(by_head(den), (BLK, GROUP_W))
        o_ref[0, r, rows, :] = (by_head(o) / den_h).astype(BF16)
        lse_ref[0, r, rows, :] = by_head(m) * LN2 + jnp.log(den_h)


ATT_SUBBLOCKS = 32


def _attention_group(a, bias, dilation, B, S):
    sd = S // dilation
    rb = min(ATT_SUBBLOCKS, sd // BLK)
    rg = min(ATT_SUBBLOCKS // rb, dilation)
    o, lse = pl.pallas_call(
        _attn_kernel,
        grid=(B, dilation // rg, sd // (rb * BLK)),
        in_specs=[pl.BlockSpec((1, rg, rb * BLK, QKV_G), lambda b, r, n: (b, r, n, 0)),
                  pl.BlockSpec((1, rg, BLK, QKV_G), lambda b, r, n: (b, r, jnp.maximum(n * rb - 1, 0), 0)),
                  _resident((HEADS_PER_GROUP * BLK, 2 * BLK))],
        out_specs=[pl.BlockSpec((1, rg, rb * BLK, GROUP_W), lambda b, r, n: (b, r, n, 0))] * 2,
        out_shape=[jax.ShapeDtypeStruct((B, dilation, sd, GROUP_W), BF16),
                   jax.ShapeDtypeStruct((B, dilation, sd, GROUP_W), F32)],
        compiler_params=_cparams("parallel", "parallel", "parallel"),
        name=f"attn_d{dilation}",
    )(a, a, bias)
    return o, lse


def _sigmoid(x):
    return 0.5 * jnp.tanh(0.5 * x) + 0.5


def _gelu(x):
    return x * (lax.erf(x * (2.0 ** -0.5)) + 1.0) * 0.5


def _token_major(src_ref, d, scr, slot, tm):
    if d == 1:
        return src_ref[0, 0].astype(F32)
    for r in range(d):
        piece = src_ref[0, r].astype(F32)
        scr[slot, pl.ds(r, tm // d, stride=d), :] = piece[:, :LANES]
        scr[slot + 1, pl.ds(r, tm // d, stride=d), :] = piece[:, LANES:]
    return jnp.concatenate([scr[slot], scr[slot + 1]], axis=1)


def _mix_kernel(x_ref, o1_ref, o2_ref, o3_ref, l1_ref, l2_ref, l3_ref, uv_ref, gl_ref,
                wa_hbm, wg_hbm, wo_hbm, wc_ref, bs_ref, lng_ref, lnb_ref, we_ref, h_ref, we_bf_ref, g_scr, t_scr,
                wa_ref, wg_ref, wo_ref, w_stage, w_sem):
    @pl.when(pl.program_id(0) == 0)
    def _():
        for w_hbm, w_bf in ((wa_hbm, wa_ref), (wg_hbm, wg_ref), (wo_hbm, wo_ref)):
            _load_weight_bf16(w_hbm, w_bf, w_stage, w_sem)

    we_bf_ref[...] = we_ref[...].astype(BF16)

    tm = x_ref.shape[0]
    dils = [d for _, d in ATT_GROUPS]
    o1, o2, o3 = [_token_major(ref, d, t_scr, 4 * i, tm) for i, (ref, d) in enumerate(zip((o1_ref, o2_ref, o3_ref), dils))]
    l1, l2, l3 = [_token_major(ref, d, t_scr, 4 * i + 2, tm) for i, (ref, d) in enumerate(zip((l1_ref, l2_ref, l3_ref), dils))]
    lm = jnp.maximum(jnp.maximum(l1, l2), l3)
    e1, e2, e3 = jnp.exp(l1 - lm), jnp.exp(l2 - lm), jnp.exp(l3 - lm)
    att = (e1 * o1 + e2 * o2 + e3 * o3) / (e1 + e2 + e3)
    y_att = jnp.dot(att.astype(BF16), wa_ref[...], preferred_element_type=F32)

    zu = _gelu(uv_ref[:, :GMLP_W].astype(F32))
    zv = _gelu(uv_ref[:, GMLP_W:].astype(F32))
    mu = jnp.mean(zv, axis=-1, keepdims=True)
    var = jnp.mean(jnp.square(zv - mu), axis=-1, keepdims=True)
    vn = (zv - mu) * lax.rsqrt(var + EPS) * lng_ref[...] + lnb_ref[...]
    low_half = lax.broadcasted_iota(jnp.int32, (CHUNK, 2 * GMLP_GD), 1) < GMLP_GD
    for c in range(tm // CHUNK):
        rows = slice(c * CHUNK, (c + 1) * CHUNK)
        for s in range(GMLP_W // (2 * GMLP_GD)):
            cols = slice(s * 2 * GMLP_GD, (s + 1) * 2 * GMLP_GD)
            v2 = vn[rows, cols]
            rhs = jnp.concatenate([jnp.where(low_half, v2, 0.0), jnp.where(low_half, 0.0, v2)], axis=0).astype(BF16)
            mixed = jnp.dot(wc_ref[s], rhs, preferred_element_type=F32) + bs_ref[:, cols]
            g_scr[rows, cols] = (zu[rows, cols] * mixed).astype(BF16)
    y_gm = jnp.dot(g_scr[...], wg_ref[...], preferred_element_type=F32)

    gate_a = _sigmoid(gl_ref[:, :D_MODEL].astype(F32))
    gate_g = _sigmoid(gl_ref[:, D_MODEL:].astype(F32))
    merged = (gate_a * y_att + gate_g * y_gm).astype(BF16)
    h_ref[...] = x_ref[...] + jnp.dot(merged, wo_ref[...], preferred_element_type=F32)


def _mix(x2, outs, lses, uv, gl, wa, wg, wo, wc2, bs, lng, lnb, w_expert, S):
    T = x2.shape[0]
    tm = TM_PROJ
    row = lambda w: pl.BlockSpec((tm, w), lambda i: (i, 0))
    att = [_plane_spec(d, tm, S // tm, GROUP_W) for _, d in ATT_GROUPS]
    we_spec = _expert_slice_spec(T // tm, w_expert.shape[2])
    return pl.pallas_call(
        _mix_kernel,
        grid=(T // tm,),
        in_specs=[row(D_MODEL)] + att + att + [row(2 * GMLP_W), row(N_BRANCH * D_MODEL)]
                 + [pl.BlockSpec(memory_space=pl.ANY)] * 3
                 + [_resident(wc2.shape), _resident(bs.shape), _resident(lng.shape), _resident(lnb.shape), we_spec],
        out_specs=[row(D_MODEL), we_spec],
        out_shape=[jax.ShapeDtypeStruct((T, D_MODEL), F32), jax.ShapeDtypeStruct(w_expert.shape, BF16)],
        scratch_shapes=[pltpu.VMEM((tm, GMLP_W), BF16), pltpu.VMEM((4 * N_DIL, tm, LANES), F32),
                        pltpu.VMEM(wa.shape, BF16), pltpu.VMEM(wg.shape, BF16), pltpu.VMEM(wo.shape, BF16)]
                       + _weight_stage(D_MODEL),
        compiler_params=_cparams("arbitrary"),
        name="mix",
    )(x2, *outs, *lses, uv, gl, wa, wg, wo, wc2, bs, lng, lnb, w_expert)


def _router_kernel(h_ref, g_ref, wr_ref, br_ref, upper_ref, eidx_ref, gate_ref, rank_ref, cnt_ref, carry):
    tm = h_ref.shape[0]

    @pl.when(pl.program_id(0) == 0)
    def _():
        carry[...] = jnp.zeros_like(carry)

    hn = _rms(h_ref[...], g_ref[...])
    hi = hn.astype(BF16)
    lo = (hn - hi.astype(F32)).astype(BF16)
    nt = (((1,), (1,)), ((), ()))
    by_hi = lax.dot_general(wr_ref[...], hi, nt, preferred_element_type=F32)
    by_lo = lax.dot_general(wr_ref[:N_EXPERTS, :], lo, nt, preferred_element_type=F32)
    logits = by_hi[:N_EXPERTS] + by_hi[N_EXPERTS:] + by_lo + br_ref[...]
    expert = lax.broadcasted_iota(jnp.int32, (N_EXPERTS, tm), 0)
    vals, hots = [], []
    l = logits
    for k in range(TOP_K):
        m = jnp.max(l, axis=0, keepdims=True)
        idx = jnp.min(jnp.where(l == m, expert, N_EXPERTS), axis=0, keepdims=True)
        hot = expert == idx
        eidx_ref[0, k:k + 1, :] = idx
        vals.append(m)
        hots.append(hot)
        l = jnp.where(hot, -jnp.inf, l)
    ex = [jnp.exp(v - vals[0]) for v in vals]
    tot = ex[0] + ex[1] + ex[2] + ex[3]
    for k in range(TOP_K):
        gate_ref[0, k:k + 1, :] = ex[k] / tot
    multi = jnp.zeros((N_EXPERTS, tm), F32)
    for hot in hots:
        multi = multi + hot.astype(F32)
    before = jnp.dot(multi.astype(BF16), upper_ref[...], preferred_element_type=F32) + carry[...]
    for k in range(TOP_K):
        rank_ref[0, k:k + 1, :] = jnp.sum(jnp.where(hots[k], before, 0.0), axis=0, keepdims=True).astype(jnp.int32)
    carry[...] += jnp.sum(multi, axis=1, keepdims=True)
    cnt_ref[...] = carry[...]


def _router(h1, g, wr_t, br_col):
    T = h1.shape[0]
    tm = TM_PROJ
    upper = jnp.asarray(np.triu(np.ones((tm, tm), np.float32), k=1), BF16)
    k_rows = pl.BlockSpec((1, TOP_K, tm), lambda i: (i, 0, 0))
    k_shape = lambda dt: jax.ShapeDtypeStruct((T // tm, TOP_K, tm), dt)
    return pl.pallas_call(
        _router_kernel,
        grid=(T // tm,),
        in_specs=[pl.BlockSpec((tm, D_MODEL), lambda i: (i, 0)), _resident((1, D_MODEL)),
                  _resident((2 * N_EXPERTS, D_MODEL)), _resident((N_EXPERTS, 1)), _resident((tm, tm))],
        out_specs=[k_rows, k_rows, k_rows, pl.BlockSpec((N_EXPERTS, 1), lambda i: (0, 0))],
        out_shape=[k_shape(jnp.int32), k_shape(F32), k_shape(jnp.int32),
                   jax.ShapeDtypeStruct((N_EXPERTS, 1), F32)],
        scratch_shapes=[pltpu.VMEM((N_EXPERTS, 1), F32)],
        compiler_params=_cparams("arbitrary"),
        name="router",
    )(h1, g, wr_t, br_col, upper)


def _to_row_tiles(ref, lead, value):
    n = value.shape[0]
    for c in range(ROW_SUBLANES):
        ref[(*lead, pl.ds(c, n, stride=ROW_SUBLANES), slice(None))] = value[:, c * LANES:(c + 1) * LANES]


def _from_row_tiles(ref, lead, first, n):
    return jnp.concatenate(
        [ref[(*lead, pl.ds(first * ROW_SUBLANES + c, n, stride=ROW_SUBLANES), slice(None))] for c in range(ROW_SUBLANES)],
        axis=1)


def _tile_rows(idx, n=1):
    return pl.ds(pl.multiple_of(idx * ROW_SUBLANES, ROW_SUBLANES), n * ROW_SUBLANES)


def _dispatch_kernel(last_ref, nv_ref, dest_ref, h_ref, g_ref, xs_ref, buf, sem, zero_sem):
    tm = h_ref.shape[0]
    n_blocks = xs_ref.shape[0] // (TM_EXP * ROW_SUBLANES)
    i = pl.program_id(0)
    slot = lax.rem(i, 2)

    @pl.when(i == 0)
    def _():
        buf[1] = jnp.zeros(buf.shape[1:], F32)

        def zero_block(b):
            for part in range(TM_EXP // tm):
                pltpu.make_async_copy(buf.at[1], xs_ref.at[_tile_rows(b * TM_EXP + part * tm, tm)], zero_sem).start()

        def zero_done():
            for part in range(TM_EXP // tm):
                pltpu.make_async_copy(buf.at[1], xs_ref.at[_tile_rows(0, tm)], zero_sem).wait()

        for e in range(N_EXPERTS):
            zero_block(last_ref[e])
        lax.fori_loop(nv_ref[0], n_blocks, lambda b, c: (zero_block(b), c)[1], 0)
        for e in range(N_EXPERTS):
            zero_done()
        lax.fori_loop(nv_ref[0], n_blocks, lambda b, c: (zero_done(), c)[1], 0)

    _to_row_tiles(buf, (slot,), _rms(h_ref[...], g_ref[...]))

    def issue(t, carry):
        for k in range(TOP_K):
            d = dest_ref[0, 0, k * tm + t]
            pltpu.make_async_copy(buf.at[slot, _tile_rows(t)], xs_ref.at[_tile_rows(d)],
                                  sem.at[slot]).start(priority=k % 2)
        return carry

    lax.fori_loop(0, tm, issue, 0, unroll=8)

    def wait_slot(s):
        for _ in range(TOP_K):
            pltpu.make_async_copy(buf.at[s], xs_ref.at[_tile_rows(0, tm)], sem.at[s]).wait()

    @pl.when(i > 0)
    def _():
        wait_slot(1 - slot)

    @pl.when(i == pl.num_programs(0) - 1)
    def _():
        wait_slot(slot)


def _dispatch(last_block, n_valid, dest3, h1, g, n_slots):
    T = h1.shape[0]
    tm = TM_TOK
    grid_spec = pltpu.PrefetchScalarGridSpec(
        num_scalar_prefetch=2,
        grid=(T // tm,),
        in_specs=[pl.BlockSpec((1, 1, TOP_K * tm), lambda i, lb, nv: (i, 0, 0), memory_space=pltpu.SMEM),
                  pl.BlockSpec((tm, D_MODEL), lambda i, lb, nv: (i, 0)),
                  pl.BlockSpec((1, D_MODEL), lambda i, lb, nv: (0, 0), pipeline_mode=pl.Buffered(1))],
        out_specs=pl.BlockSpec(memory_space=pl.ANY),
        scratch_shapes=[pltpu.VMEM((2, tm * ROW_SUBLANES, LANES), F32), pltpu.SemaphoreType.DMA((2,)),
                        pltpu.SemaphoreType.DMA(())],
    )
    return pl.pallas_call(
        _dispatch_kernel,
        grid_spec=grid_spec,
        out_shape=jax.ShapeDtypeStruct((n_slots * ROW_SUBLANES, LANES), F32),
        compiler_params=_cparams("arbitrary"),
        name="dispatch",
    )(last_block, n_valid, dest3, h1, g)


def _experts_kernel(be_ref, nv_ref, xs_hbm, wgu_ref, bgu_ref, wd_ref, bd_ref, ys_hbm, xbuf, ybuf, xsem, ysem):
    del be_ref
    tm = xbuf.shape[1]
    b = pl.program_id(0)
    n_valid = nv_ref[0]
    slot = lax.rem(b, 2)

    def x_copies(blk, s):
        return [pltpu.make_async_copy(xs_hbm.at[pl.ds(blk * tm, tm), c, :],
                                      xbuf.at[s, :, pl.ds(c * LANES, LANES)], xsem.at[s]) for c in range(ROW_SUBLANES)]

    def y_copies(blk, s):
        return [pltpu.make_async_copy(ybuf.at[s, :, pl.ds(c * LANES, LANES)],
                                      ys_hbm.at[pl.ds(blk * tm, tm), c, :], ysem.at[s]) for c in range(ROW_SUBLANES)]

    @pl.when(jnp.logical_and(b == 0, n_valid > 0))
    def _():
        for cp in x_copies(0, 0):
            cp.start()

    @pl.when(b + 1 < n_valid)
    def _():
        for cp in x_copies(b + 1, 1 - slot):
            cp.start()

    @pl.when(b < n_valid)
    def _():
        for cp in x_copies(b, slot):
            cp.wait()
        x = xbuf[slot].astype(BF16)
        gu = jnp.dot(x, wgu_ref[0], preferred_element_type=F32) + bgu_ref[0]
        glu = jnp.minimum(gu[:, :D_EXPERT], SWIGLU_LIMIT)
        lin = jnp.clip(gu[:, D_EXPERT:], -SWIGLU_LIMIT, SWIGLU_LIMIT)
        act = glu * jax.nn.sigmoid(SWIGLU_ALPHA * glu) * (lin + 1.0)
        ybuf[slot] = jnp.dot(act.astype(BF16), wd_ref[0], preferred_element_type=F32) + bd_ref[0]

    @pl.when(b >= n_valid)
    def _():
        ybuf[slot] = jnp.zeros(ybuf.shape[1:], F32)

    for cp in y_copies(b, slot):
        cp.start()

    @pl.when(b > 0)
    def _():
        for cp in y_copies(b - 1, 1 - slot):
            cp.wait()

    @pl.when(b == pl.num_programs(0) - 1)
    def _():
        for cp in y_copies(b, slot):
            cp.wait()


def _experts(block_expert, n_valid, xs, wgu, bgu, wd, bd):
    tm = TM_EXP
    n_slots = xs.shape[0] // ROW_SUBLANES
    n_blocks = n_slots // tm
    any_space = pl.BlockSpec(memory_space=pl.ANY)
    grid_spec = pltpu.PrefetchScalarGridSpec(
        num_scalar_prefetch=2,
        grid=(n_blocks,),
        in_specs=[any_space,
                  pl.BlockSpec((1, D_MODEL, 2 * D_EXPERT), lambda b, be, nv: (be[b], 0, 0)),
                  pl.BlockSpec((1, 1, 2 * D_EXPERT), lambda b, be, nv: (be[b], 0, 0)),
                  pl.BlockSpec((1, D_EXPERT, D_MODEL), lambda b, be, nv: (be[b], 0, 0)),
                  pl.BlockSpec((1, 1, D_MODEL), lambda b, be, nv: (be[b], 0, 0))],
        out_specs=any_space,
        scratch_shapes=[pltpu.VMEM((2, tm, D_MODEL), F32), pltpu.VMEM((2, tm, D_MODEL), F32),
                        pltpu.SemaphoreType.DMA((2,)), pltpu.SemaphoreType.DMA((2,))],
    )
    ys = pl.pallas_call(
        _experts_kernel,
        grid_spec=grid_spec,
        out_shape=jax.ShapeDtypeStruct((n_slots, ROW_SUBLANES, LANES), F32),
        compiler_params=_cparams("arbitrary"),
        name="experts",
    )(block_expert, n_valid, xs.reshape(n_slots, ROW_SUBLANES, LANES), wgu, bgu, wd, bd)
    return ys.reshape(xs.shape)


GATHER_SLOTS = 3


def _combine_kernel(*refs, n_tiles):
    ahead = GATHER_SLOTS - 1
    prime_refs, refs = refs[:ahead], refs[ahead:]
    (ahead_dest_ref, h_ref, gate_ref, p_ref, gp_ref, gf_ref, wpg_hbm, wpp_hbm, ys_ref, o_ref), refs = refs[:10], refs[10:]
    bufs, (sem, wpg_ref, wpp_ref, w_stage, w_sem) = refs[:GATHER_SLOTS], refs[GATHER_SLOTS:]
    tm = h_ref.shape[0]
    i = pl.program_id(0)

    def row_copy(dref, t, k, s):
        d = dref[0, 0, k * tm + t]
        return pltpu.make_async_copy(ys_ref.at[_tile_rows(d)], bufs[s].at[_tile_rows(k * tm + t)], sem.at[s])

    def wait_slot(s):
        for _ in range(TOP_K):
            pltpu.make_async_copy(ys_ref.at[_tile_rows(0, tm)], bufs[s].at[_tile_rows(0, tm)], sem.at[s]).wait()

    @pl.when(i == 0)
    def _():
        _load_weight_bf16(wpg_hbm, wpg_ref, w_stage, w_sem)
        _load_weight_bf16(wpp_hbm, wpp_ref, w_stage, w_sem)

        for s, dref in enumerate(prime_refs):
            def issue(t, carry, s=s, dref=dref):
                for k in range(TOP_K):
                    row_copy(dref, t, k, s).start(priority=k % 2)
                return carry
            lax.fori_loop(0, tm, issue, 0, unroll=8)

    def step(s):
        wait_slot(s)
        for t in range(tm):
            for k in range(TOP_K):
                row_copy(ahead_dest_ref, t, k, (s + ahead) % GATHER_SLOTS).start(priority=k % 2)
        proj = jnp.dot(p_ref[...].astype(BF16), wpp_ref[...], preferred_element_type=F32)
        h = h_ref[...]
        for k in range(TOP_K):
            h = h + gate_ref[:, k:k + 1] * _from_row_tiles(bufs[s], (), k * tm, tm)
        ple_gate = jax.nn.sigmoid(jnp.dot(_rms(h, gp_ref[...]).astype(BF16), wpg_ref[...], preferred_element_type=F32))
        h = h + ple_gate * proj
        o_ref[...] = _rms(h, gf_ref[...])

    for s in range(GATHER_SLOTS):
        pl.when(lax.rem(i, GATHER_SLOTS) == s)(functools.partial(step, s))

    @pl.when(i == n_tiles - 1)
    def _():
        for extra in range(ahead):
            wait_slot((n_tiles + extra) % GATHER_SLOTS)


def _combine(dest3, h1, gate, p2, gp, wpg, wpp, gf, ys):
    T = h1.shape[0]
    tm = TM_TOK
    n_tiles = T // tm
    row = lambda w: pl.BlockSpec((tm, w), lambda i: (i, 0))
    dest_spec = lambda ahead: pl.BlockSpec((1, 1, TOP_K * tm), lambda i: (jnp.minimum(i + ahead, n_tiles - 1), 0, 0),
                                           memory_space=pltpu.SMEM)
    gather_buf = pltpu.VMEM((TOP_K * tm * ROW_SUBLANES, LANES), F32)
    ahead = GATHER_SLOTS - 1
    assert n_tiles > ahead
    return pl.pallas_call(
        functools.partial(_combine_kernel, n_tiles=n_tiles),
        grid=(n_tiles,),
        in_specs=[dest_spec(a) for a in range(ahead)] + [dest_spec(ahead),
                  row(D_MODEL), row(TOP_K), row(PLE_DIM), _resident((1, D_MODEL)), _resident((1, D_MODEL))]
                 + [pl.BlockSpec(memory_space=pl.ANY)] * 3,
        out_specs=row(D_MODEL),
        out_shape=jax.ShapeDtypeStruct((T, D_MODEL), F32),
        scratch_shapes=[gather_buf] * GATHER_SLOTS + [pltpu.SemaphoreType.DMA((GATHER_SLOTS,)),
                        pltpu.VMEM(wpg.shape, BF16), pltpu.VMEM(wpp.shape, BF16)] + _weight_stage(D_MODEL),
        compiler_params=_cparams("arbitrary"),
        name="combine",
    )(*([dest3] * (ahead + 1)), h1, gate, p2, gp, gf, wpg, wpp, ys)


def _layer(h, p_i, g_mix, w_in, rel_bias, w_att_out, ln_v_g, ln_v_b, w_spatial, b_spatial, w_gmlp_out, w_out,
           g_moe, w_router, b_router, w_gate_up, b_gate_up, w_down, b_down, g_ple, w_ple_gate, w_ple_proj,
           g_final, B, S):
    T = B * S
    row = lambda v: v.reshape(1, -1).astype(F32)

    assert S % TM_PROJ == 0
    *att_in, uv, gl, wgu_bf = _in_proj(h, row(g_mix), w_in.astype(F32), w_gate_up.astype(F32), B, S)

    outs, lses = [], []
    for g, (window, dilation) in enumerate(ATT_GROUPS):
        assert window // dilation == BLK and S % (dilation * BLK) == 0
        bias = _bias_table(rel_bias[:, g * HEADS_PER_GROUP:(g + 1) * HEADS_PER_GROUP], dilation)
        o, lse = _attention_group(att_in[g], bias, dilation, B, S)
        outs.append(o)
        lses.append(lse)

    causal = jnp.asarray(np.tril(np.ones((CHUNK, CHUNK), np.float32)))
    w_c = (w_spatial.astype(F32) * causal[None]).astype(BF16)
    wc2 = jnp.concatenate([w_c[0::2], w_c[1::2]], axis=2)
    bs = jnp.repeat(b_spatial.astype(F32).T, GMLP_GD, axis=1)
    h1, wd_bf = _mix(h, outs, lses, uv, gl, w_att_out.astype(F32), w_gmlp_out.astype(F32), w_out.astype(F32),
                     wc2, bs, row(ln_v_g), row(ln_v_b), w_down.astype(F32), S)

    wr_hi = w_router.astype(BF16)
    wr_lo = (w_router.astype(F32) - wr_hi.astype(F32)).astype(BF16)
    eidx, gate, rank, counts = _router(h1, row(g_moe), jnp.concatenate([wr_hi, wr_lo], axis=1).T,
                                       b_router.reshape(-1, 1).astype(F32))
    cnt = counts[:, 0].astype(jnp.int32)
    blk_counts = (cnt + TM_EXP - 1) // TM_EXP
    blk_end = jnp.cumsum(blk_counts)
    pad_start = (blk_end - blk_counts) * TM_EXP
    n_blocks = T * TOP_K // TM_EXP + N_EXPERTS
    n_valid = blk_end[-1:].astype(jnp.int32)
    blk = jnp.minimum(jnp.arange(n_blocks, dtype=jnp.int32), n_valid[0] - 1)
    block_expert = jnp.minimum(jnp.sum((blk_end[None, :] <= blk[:, None]).astype(jnp.int32), axis=1), N_EXPERTS - 1)
    expert_ids = jnp.arange(N_EXPERTS, dtype=jnp.int32)
    dest = rank + jnp.sum(jnp.where(eidx[..., None] == expert_ids, pad_start, 0), axis=-1)
    split = TM_PROJ // TM_TOK
    dest3 = jnp.transpose(dest.reshape(T // TM_PROJ, TOP_K, split, TM_TOK), (0, 2, 1, 3)).reshape(
        T // TM_TOK, 1, TOP_K * TM_TOK)
    gate = jnp.transpose(gate, (0, 2, 1)).reshape(T, TOP_K)

    last_block = jnp.maximum(blk_end - 1, 0).astype(jnp.int32)
    xs = _dispatch(last_block, n_valid, dest3, h1, row(g_moe), n_blocks * TM_EXP)
    ys = _experts(block_expert, n_valid, xs, wgu_bf, b_gate_up.reshape(N_EXPERTS, 1, -1).astype(F32),
                  wd_bf, b_down.reshape(N_EXPERTS, 1, -1).astype(F32))
    return _combine(dest3, h1, gate, p_i, row(g_ple), w_ple_gate.astype(F32), w_ple_proj.astype(F32),
                    row(g_final), ys)


def kernel(x, p, g_mix, w_in, rel_bias, w_att_out, ln_v_g, ln_v_b, w_spatial, b_spatial, w_gmlp_out, w_out, g_moe, w_router, b_router, w_gate_up, b_gate_up, w_down, b_down, g_ple, w_ple_gate, w_ple_proj, g_final):
    B, S, D = x.shape
    depth = p.shape[0]
    assert depth == 1, "the final RMSNorm is fused into the (single) layer's last kernel"
    out = _layer(x.reshape(B * S, D), p[0].reshape(B * S, PLE_DIM), g_mix[0], w_in[0], rel_bias, w_att_out[0],
                 ln_v_g[0], ln_v_b[0], w_spatial[0], b_spatial[0], w_gmlp_out[0], w_out[0], g_moe[0], w_router[0],
                 b_router[0], w_gate_up[0], b_gate_up[0], w_down[0], b_down[0], g_ple[0], w_ple_gate[0],
                 w_ple_proj[0], g_final, B, S)
    return out.reshape(B, S, D)
```

```python
import functools

import jax
import jax.numpy as jnp
import numpy as np
from jax import lax
from jax.experimental import pallas as pl
from jax.experimental.pallas import tpu as pltpu

F32 = jnp.float32
BF16 = jnp.bfloat16

D_MODEL = 1024
HEAD_DIM = 64
ATT_GROUPS = ((128, 1), (512, 4), (2048, 16))
HEADS_PER_GROUP = 4
GROUP_W = HEADS_PER_GROUP * HEAD_DIM
N_DIL = len(ATT_GROUPS)
ATT_W = N_DIL * GROUP_W
BLK = 128
REL_BUCKETS = 32
REL_MAX_DIST = 2048
CHUNK = 128
GMLP_W = 768
GMLP_GD = 64
N_BRANCH = 2
IN_W = 3 * ATT_W + 2 * GMLP_W + N_BRANCH * D_MODEL
N_EXPERTS = 32
TOP_K = 4
D_EXPERT = D_MODEL
SWIGLU_LIMIT = 7.0
SWIGLU_ALPHA = 1.702
PLE_DIM = 256
EPS = 1e-6
MASKED = -1e30
LOG2E = float(np.log2(np.e))
LN2 = float(np.log(2.0))

QKV_G = 3 * GROUP_W

LANES = 128
ROW_SUBLANES = D_MODEL // LANES
assert ROW_SUBLANES == 8
MXU_N = 256
VMEM_LIMIT = 56 * 1024 * 1024

TM_PROJ = 512
TM_ROUTER = 1024
TM_TOK = 256
TM_EXP = 512
assert TM_EXP % TM_TOK == 0


def _cparams(*sem):
    return pltpu.CompilerParams(dimension_semantics=sem, vmem_limit_bytes=VMEM_LIMIT)


def _resident(shape):
    nd = len(shape)
    return pl.BlockSpec(shape, lambda *_: (0,) * nd, pipeline_mode=pl.Buffered(1))


def _rms(x, g):
    return x * lax.rsqrt(jnp.mean(x * x, axis=-1, keepdims=True) + EPS) * g


def _load_weight_bf16(w_hbm, w_bf, stage, sem, scale=None):
    rows = stage.shape[1]
    n_chunks = w_hbm.shape[0] // rows
    assert n_chunks * rows == w_hbm.shape[0] and stage.shape[2] == w_hbm.shape[1]

    def chunk(c):
        return pltpu.make_async_copy(w_hbm.at[pl.ds(c * rows, rows)], stage.at[c % 2], sem.at[c % 2])

    chunk(0).start()
    for c in range(n_chunks):
        if c + 1 < n_chunks:
            chunk(c + 1).start()
        chunk(c).wait()
        piece = stage[c % 2] if scale is None else stage[c % 2] * scale
        w_bf[c * rows:(c + 1) * rows, :] = piece.astype(BF16)


def _expert_slice_spec(n_steps, width):
    per_expert = n_steps // N_EXPERTS
    assert per_expert * N_EXPERTS == n_steps and D_MODEL % per_expert == 0
    return pl.BlockSpec((1, D_MODEL // per_expert, width), lambda i: (i // per_expert, i % per_expert, 0))


def _inproj_kernel(x0_ref, xn_ref, g_ref, w_hbm, we_ref, a1_ref, a2_ref, a3_ref, uv_ref, gl_ref, we_bf_ref,
                   scr, w_ref, w_stage, w_sem, n_scr, n_tmp):
    i = pl.program_id(0)
    slot = lax.rem(i, 2)

    @pl.when(i == 0)
    def _():
        _load_weight_bf16(w_hbm, w_ref, w_stage, w_sem)
        n_scr[0] = _rms(x0_ref[...], g_ref[...]).astype(BF16)

    tm = xn_ref.shape[0]
    n = n_scr[slot]
    att_refs = (a1_ref, a2_ref, a3_ref)
    n_att, n_uv = 3 * ATT_W // MXU_N, 2 * GMLP_W // MXU_N
    n_pieces = 16
    side_rows, we_rows = tm // n_pieces, we_ref.shape[1] // n_pieces
    for c in range(IN_W // MXU_N):
        if c < n_pieces:
            we_bf_ref[0, c * we_rows:(c + 1) * we_rows, :] = we_ref[0, c * we_rows:(c + 1) * we_rows, :].astype(BF16)
            rows = slice(c * side_rows, (c + 1) * side_rows)
            n_tmp[rows, :] = _rms(xn_ref[rows, :], g_ref[...]).astype(BF16)
        z = jnp.dot(n, w_ref[:, c * MXU_N:(c + 1) * MXU_N], preferred_element_type=F32)
        if c < n_att:
            which, g = divmod(c, N_DIL)
            d = ATT_GROUPS[g][1]
            dst = att_refs[g]
            cols = slice(which * GROUP_W, (which + 1) * GROUP_W)
            if d == 1:
                dst[0, 0, :, cols] = z.astype(BF16)
                continue
            scr[0] = z[:, :LANES]
            scr[1] = z[:, LANES:]
            for r in range(d):
                zr = jnp.concatenate([scr[0, pl.ds(r, tm // d, stride=d), :],
                                      scr[1, pl.ds(r, tm // d, stride=d), :]], axis=1)
                dst[0, r, :, cols] = zr.astype(BF16)
        elif c < n_att + n_uv:
            uv_ref[:, (c - n_att) * MXU_N:(c - n_att + 1) * MXU_N] = z.astype(BF16)
        else:
            gl_ref[:, (c - n_att - n_uv) * MXU_N:(c - n_att - n_uv + 1) * MXU_N] = (z * 0.5).astype(BF16)
    n_scr[1 - slot] = n_tmp[...]


def _plane_spec(d, tm, tiles_per_seq, width):
    return pl.BlockSpec((1, d, tm // d, width), lambda i: (i // tiles_per_seq, 0, i % tiles_per_seq, 0))


W_STAGE_ROWS = 128


def _weight_stage(width):
    return [pltpu.VMEM((2, W_STAGE_ROWS, width), F32), pltpu.SemaphoreType.DMA((2,))]


def _in_proj(x2, g, w, w_expert, B, S):
    T = x2.shape[0]
    tm = TM_PROJ
    row = lambda w: pl.BlockSpec((tm, w), lambda i: (i, 0))
    dils = [d for _, d in ATT_GROUPS]
    we_spec = _expert_slice_spec(T // tm, w_expert.shape[2])
    n_steps = T // tm
    first_tile = pl.BlockSpec((tm, D_MODEL), lambda i: (0, 0), pipeline_mode=pl.Buffered(1))
    next_tile = pl.BlockSpec((tm, D_MODEL), lambda i: (jnp.minimum(i + 1, n_steps - 1), 0))
    return pl.pallas_call(
        _inproj_kernel,
        grid=(n_steps,),
        in_specs=[first_tile, next_tile, _resident((1, D_MODEL)), pl.BlockSpec(memory_space=pl.ANY), we_spec],
        out_specs=[_plane_spec(d, tm, S // tm, QKV_G) for d in dils] + [row(2 * GMLP_W), row(N_BRANCH * D_MODEL), we_spec],
        out_shape=[jax.ShapeDtypeStruct((B, d, S // d, QKV_G), BF16) for d in dils]
                  + [jax.ShapeDtypeStruct((T, 2 * GMLP_W), BF16),
                     jax.ShapeDtypeStruct((T, N_BRANCH * D_MODEL), BF16),
                     jax.ShapeDtypeStruct(w_expert.shape, BF16)],
        scratch_shapes=[pltpu.VMEM((2, tm, LANES), F32), pltpu.VMEM((D_MODEL, IN_W), BF16)] + _weight_stage(IN_W)
                       + [pltpu.VMEM((2, tm, D_MODEL), BF16), pltpu.VMEM((tm, D_MODEL), BF16)],
        compiler_params=_cparams("arbitrary"),
        name="in_proj",
    )(x2, x2, g, w, w_expert)


def _t5_bucket(n):
    exact = REL_BUCKETS // 2
    nf = np.maximum(n, 1).astype(np.float32)
    large = exact + (np.log(nf / exact) / np.log(REL_MAX_DIST / exact) * (REL_BUCKETS - exact)).astype(np.int32)
    large = np.minimum(large, REL_BUCKETS - 1)
    return np.where(n < exact, n, large).astype(np.int32)


def _bias_table(rel_bias_g, dilation):
    n = 3 * BLK
    dist = 2 * BLK - 1 - np.arange(n)
    valid = (dist >= 0) & (dist <= BLK)
    bucket = _t5_bucket(np.clip(dist, 0, BLK) * dilation)
    c = jnp.where(jnp.asarray(valid)[None, :], rel_bias_g.astype(F32)[bucket].T * LOG2E, MASKED)
    shifted = jnp.tile(c, (1, BLK))[:, :BLK * (n - 1)].reshape(HEADS_PER_GROUP, BLK, n - 1)
    return shifted[:, :, BLK - 1:].reshape(HEADS_PER_GROUP * BLK, 2 * BLK)


def _attn_kernel(cur_ref, prev_ref, bias_ref, o_ref, lse_ref):
    rg, rb = cur_ref.shape[1], cur_ref.shape[2] // BLK
    starts_sequence = pl.program_id(2) == 0
    lane_head = lax.broadcasted_iota(jnp.int32, (1, GROUP_W), 1) // HEAD_DIM
    scale = HEAD_DIM ** -0.5
    head_bf = [jnp.where(lane_head == h, scale, 0.0).astype(BF16) for h in range(HEADS_PER_GROUP)]
    key_is_prev = lax.broadcasted_iota(jnp.int32, (1, 2 * BLK), 1) < BLK
    nt = (((1,), (1,)), ((), ()))
    qc, kc_, vc_ = slice(0, GROUP_W), slice(GROUP_W, 2 * GROUP_W), slice(2 * GROUP_W, 3 * GROUP_W)

    def by_head(x):
        sel = x[(HEADS_PER_GROUP - 1) * BLK:]
        for h in range(HEADS_PER_GROUP - 2, -1, -1):
            sel = jnp.where(lane_head == h, x[h * BLK:(h + 1) * BLK], sel)
        return sel

    for r, j in [(r, j) for r in range(rg) for j in range(rb)]:
        rows = slice(j * BLK, (j + 1) * BLK)
        prev = prev_ref if j == 0 else cur_ref
        prows = slice(0, BLK) if j == 0 else slice((j - 1) * BLK, j * BLK)
        q = cur_ref[0, r, rows, qc]
        k = jnp.concatenate([prev[0, r, prows, kc_], cur_ref[0, r, rows, kc_]], axis=0)
        v = jnp.concatenate([prev[0, r, prows, vc_], cur_ref[0, r, rows, vc_]], axis=0)
        q_bd = jnp.concatenate([q * head_bf[h] for h in range(HEADS_PER_GROUP)], axis=0)
        s = lax.dot_general(q_bd, k, nt, preferred_element_type=F32) * LOG2E + bias_ref[...]
        if j == 0:
            s = jnp.where(jnp.logical_and(starts_sequence, key_is_prev), MASKED, s)
        m = jnp.max(s, axis=-1, keepdims=True)
        p = jnp.exp2(s - m)
        den = jnp.sum(p, axis=-1, keepdims=True)
        o = jnp.dot(p.astype(BF16), v, preferred_element_type=F32)
        den_h = jnp.broadcast_to(by_head(den), (BLK, GROUP_W))
        o_ref[0, r, rows, :] = (by_head(o) / den_h).astype(BF16)
        lse_ref[0, r, rows, :] = by_head(m) * LN2 + jnp.log(den_h)


ATT_SUBBLOCKS = 32


def _attention_group(a, bias, dilation, B, S):
    sd = S // dilation
    rb = min(ATT_SUBBLOCKS, sd // BLK)
    rg = min(ATT_SUBBLOCKS // rb, dilation)
    o, lse = pl.pallas_call(
        _attn_kernel,
        grid=(B, dilation // rg, sd // (rb * BLK)),
        in_specs=[pl.BlockSpec((1, rg, rb * BLK, QKV_G), lambda b, r, n: (b, r, n, 0)),
                  pl.BlockSpec((1, rg, BLK, QKV_G), lambda b, r, n: (b, r, jnp.maximum(n * rb - 1, 0), 0)),
                  _resident((HEADS_PER_GROUP * BLK, 2 * BLK))],
        out_specs=[pl.BlockSpec((1, rg, rb * BLK, GROUP_W), lambda b, r, n: (b, r, n, 0))] * 2,
        out_shape=[jax.ShapeDtypeStruct((B, dilation, sd, GROUP_W), BF16),
                   jax.ShapeDtypeStruct((B, dilation, sd, GROUP_W), F32)],
        compiler_params=_cparams("parallel", "parallel", "parallel"),
        name=f"attn_d{dilation}",
    )(a, a, bias)
    return o, lse


def _gelu(x):
    return x * (lax.erf(x * (2.0 ** -0.5)) + 1.0) * 0.5


def _token_major(src_ref, d, scr, slot, tm):
    if d == 1:
        return src_ref[0, 0].astype(F32)
    for r in range(d):
        piece = src_ref[0, r].astype(F32)
        scr[slot, pl.ds(r, tm // d, stride=d), :] = piece[:, :LANES]
        scr[slot + 1, pl.ds(r, tm // d, stride=d), :] = piece[:, LANES:]
    return jnp.concatenate([scr[slot], scr[slot + 1]], axis=1)


def _mix_kernel(x_ref, o1_ref, o2_ref, o3_ref, l1_ref, l2_ref, l3_ref, uv_ref, gl_ref,
                wa_hbm, wg_hbm, wo_hbm, wc_ref, bs_ref, lng_ref, lnb_ref, we_ref, h_ref, we_bf_ref, g_scr, t_scr,
                wa_ref, wg_ref, wo_ref, w_stage, w_sem):
    @pl.when(pl.program_id(0) == 0)
    def _():
        _load_weight_bf16(wa_hbm, wa_ref, w_stage, w_sem)
        _load_weight_bf16(wg_hbm, wg_ref, w_stage, w_sem)
        _load_weight_bf16(wo_hbm, wo_ref, w_stage, w_sem, scale=0.5)

    we_bf_ref[...] = we_ref[...].astype(BF16)

    tm = x_ref.shape[0]
    dils = [d for _, d in ATT_GROUPS]
    o1, o2, o3 = [_token_major(ref, d, t_scr, 4 * i, tm) for i, (ref, d) in enumerate(zip((o1_ref, o2_ref, o3_ref), dils))]
    l1, l2, l3 = [_token_major(ref, d, t_scr, 4 * i + 2, tm) for i, (ref, d) in enumerate(zip((l1_ref, l2_ref, l3_ref), dils))]
    lm = jnp.maximum(jnp.maximum(l1, l2), l3)
    e1, e2, e3 = jnp.exp(l1 - lm), jnp.exp(l2 - lm), jnp.exp(l3 - lm)
    att = (e1 * o1 + e2 * o2 + e3 * o3) / (e1 + e2 + e3)
    y_att = jnp.dot(att.astype(BF16), wa_ref[...], preferred_element_type=F32)

    zu = _gelu(uv_ref[:, :GMLP_W].astype(F32))
    zv = _gelu(uv_ref[:, GMLP_W:].astype(F32))
    mu = jnp.mean(zv, axis=-1, keepdims=True)
    var = jnp.mean(jnp.square(zv - mu), axis=-1, keepdims=True)
    vn = (zv - mu) * lax.rsqrt(var + EPS) * lng_ref[...] + lnb_ref[...]
    low_half = lax.broadcasted_iota(jnp.int32, (CHUNK, 2 * GMLP_GD), 1) < GMLP_GD
    for c in range(tm // CHUNK):
        rows = slice(c * CHUNK, (c + 1) * CHUNK)
        for s in range(GMLP_W // (2 * GMLP_GD)):
            cols = slice(s * 2 * GMLP_GD, (s + 1) * 2 * GMLP_GD)
            v2 = vn[rows, cols]
            rhs = jnp.concatenate([jnp.where(low_half, v2, 0.0), jnp.where(low_half, 0.0, v2)], axis=0).astype(BF16)
            mixed = jnp.dot(wc_ref[s], rhs, preferred_element_type=F32) + bs_ref[:, cols]
            g_scr[rows, cols] = (zu[rows, cols] * mixed).astype(BF16)
    y_gm = jnp.dot(g_scr[...], wg_ref[...], preferred_element_type=F32)

    gate_a = jnp.tanh(gl_ref[:, :D_MODEL].astype(F32)) + 1.0
    gate_g = jnp.tanh(gl_ref[:, D_MODEL:].astype(F32)) + 1.0
    merged = (gate_a * y_att + gate_g * y_gm).astype(BF16)
    h_ref[...] = x_ref[...] + jnp.dot(merged, wo_ref[...], preferred_element_type=F32)


def _mix(x2, outs, lses, uv, gl, wa, wg, wo, wc2, bs, lng, lnb, w_expert, S):
    T = x2.shape[0]
    tm = TM_PROJ
    row = lambda w: pl.BlockSpec((tm, w), lambda i: (i, 0))
    att = [_plane_spec(d, tm, S // tm, GROUP_W) for _, d in ATT_GROUPS]
    we_spec = _expert_slice_spec(T // tm, w_expert.shape[2])
    return pl.pallas_call(
        _mix_kernel,
        grid=(T // tm,),
        in_specs=[row(D_MODEL)] + att + att + [row(2 * GMLP_W), row(N_BRANCH * D_MODEL)]
                 + [pl.BlockSpec(memory_space=pl.ANY)] * 3
                 + [_resident(wc2.shape), _resident(bs.shape), _resident(lng.shape), _resident(lnb.shape), we_spec],
        out_specs=[row(D_MODEL), we_spec],
        out_shape=[jax.ShapeDtypeStruct((T, D_MODEL), F32), jax.ShapeDtypeStruct(w_expert.shape, BF16)],
        scratch_shapes=[pltpu.VMEM((tm, GMLP_W), BF16), pltpu.VMEM((4 * N_DIL, tm, LANES), F32),
                        pltpu.VMEM(wa.shape, BF16), pltpu.VMEM(wg.shape, BF16), pltpu.VMEM(wo.shape, BF16)]
                       + _weight_stage(D_MODEL),
        compiler_params=_cparams("arbitrary"),
        name="mix",
    )(x2, *outs, *lses, uv, gl, wa, wg, wo, wc2, bs, lng, lnb, w_expert)


def _router_kernel(h_ref, g_ref, wr_ref, br_ref, upper_ref, eidx_ref, gate_ref, rank_ref, cnt_ref, carry):
    tm = h_ref.shape[0]

    @pl.when(pl.program_id(0) == 0)
    def _():
        carry[...] = jnp.zeros_like(carry)

    hn = _rms(h_ref[...], g_ref[...])
    hi = hn.astype(BF16)
    lo = (hn - hi.astype(F32)).astype(BF16)
    nt = (((1,), (1,)), ((), ()))
    by_hi = lax.dot_general(wr_ref[...], hi, nt, preferred_element_type=F32)
    by_lo = lax.dot_general(wr_ref[:N_EXPERTS, :], lo, nt, preferred_element_type=F32)
    logits = by_hi[:N_EXPERTS] + by_hi[N_EXPERTS:] + by_lo + br_ref[...]
    expert = lax.broadcasted_iota(jnp.int32, (N_EXPERTS, tm), 0)
    vals, hots = [], []
    l = logits
    for k in range(TOP_K):
        m = jnp.max(l, axis=0, keepdims=True)
        idx = jnp.min(jnp.where(l == m, expert, N_EXPERTS), axis=0, keepdims=True)
        hot = expert == idx
        eidx_ref[0, k:k + 1, :] = idx
        vals.append(m)
        hots.append(hot)
        l = jnp.where(hot, -jnp.inf, l)
    ex = [jnp.exp(v - vals[0]) for v in vals]
    tot = ex[0] + ex[1] + ex[2] + ex[3]
    for k in range(TOP_K):
        gate_ref[0, k:k + 1, :] = ex[k] / tot
    multi = jnp.zeros((N_EXPERTS, tm), F32)
    for hot in hots:
        multi = multi + hot.astype(F32)
    before = jnp.dot(multi.astype(BF16), upper_ref[...], preferred_element_type=F32) + carry[...]
    for k in range(TOP_K):
        rank_ref[0, k:k + 1, :] = jnp.sum(jnp.where(hots[k], before, 0.0), axis=0, keepdims=True).astype(jnp.int32)
    carry[...] += jnp.sum(multi, axis=1, keepdims=True)
    cnt_ref[...] = carry[...]


def _router(h1, g, wr_t, br_col):
    T = h1.shape[0]
    tm = TM_ROUTER
    upper = jnp.asarray(np.triu(np.ones((tm, tm), np.float32), k=1), BF16)
    k_rows = pl.BlockSpec((1, TOP_K, tm), lambda i: (i, 0, 0))
    k_shape = lambda dt: jax.ShapeDtypeStruct((T // tm, TOP_K, tm), dt)
    return pl.pallas_call(
        _router_kernel,
        grid=(T // tm,),
        in_specs=[pl.BlockSpec((tm, D_MODEL), lambda i: (i, 0)), _resident((1, D_MODEL)),
                  _resident((2 * N_EXPERTS, D_MODEL)), _resident((N_EXPERTS, 1)), _resident((tm, tm))],
        out_specs=[k_rows, k_rows, k_rows, pl.BlockSpec((N_EXPERTS, 1), lambda i: (0, 0))],
        out_shape=[k_shape(jnp.int32), k_shape(F32), k_shape(jnp.int32),
                   jax.ShapeDtypeStruct((N_EXPERTS, 1), F32)],
        scratch_shapes=[pltpu.VMEM((N_EXPERTS, 1), F32)],
        compiler_params=_cparams("arbitrary"),
        name="router",
    )(h1, g, wr_t, br_col, upper)


def _to_row_tiles(ref, lead, value):
    n = value.shape[0]
    for c in range(ROW_SUBLANES):
        ref[(*lead, pl.ds(c, n, stride=ROW_SUBLANES), slice(None))] = value[:, c * LANES:(c + 1) * LANES]


def _from_row_tiles(ref, lead, first, n):
    return jnp.concatenate(
        [ref[(*lead, pl.ds(first * ROW_SUBLANES + c, n, stride=ROW_SUBLANES), slice(None))] for c in range(ROW_SUBLANES)],
        axis=1)


def _tile_rows(idx, n=1):
    return pl.ds(pl.multiple_of(idx * ROW_SUBLANES, ROW_SUBLANES), n * ROW_SUBLANES)


def _dispatch_kernel(last_ref, nv_ref, dest_ref, h_ref, g_ref, xs_ref, buf, sem, zero_sem):
    tm = h_ref.shape[0]
    n_blocks = xs_ref.shape[0] // (TM_EXP * ROW_SUBLANES)
    i = pl.program_id(0)
    slot = lax.rem(i, 2)

    @pl.when(i == 0)
    def _():
        buf[1] = jnp.zeros(buf.shape[1:], F32)

        def zero_block(b):
            for part in range(TM_EXP // tm):
                pltpu.make_async_copy(buf.at[1], xs_ref.at[_tile_rows(b * TM_EXP + part * tm, tm)], zero_sem).start()

        def zero_done():
            for part in range(TM_EXP // tm):
                pltpu.make_async_copy(buf.at[1], xs_ref.at[_tile_rows(0, tm)], zero_sem).wait()

        for e in range(N_EXPERTS):
            zero_block(last_ref[e])
        lax.fori_loop(nv_ref[0], n_blocks, lambda b, c: (zero_block(b), c)[1], 0)
        for e in range(N_EXPERTS):
            zero_done()
        lax.fori_loop(nv_ref[0], n_blocks, lambda b, c: (zero_done(), c)[1], 0)

    _to_row_tiles(buf, (slot,), _rms(h_ref[...], g_ref[...]))

    def issue(t, carry):
        for k in range(TOP_K):
            d = dest_ref[0, 0, k * tm + t]
            pltpu.make_async_copy(buf.at[slot, _tile_rows(t)], xs_ref.at[_tile_rows(d)],
                                  sem.at[slot]).start(priority=k % 2)
        return carry

    lax.fori_loop(0, tm, issue, 0, unroll=8)

    def wait_slot(s):
        for _ in range(TOP_K):
            pltpu.make_async_copy(buf.at[s], xs_ref.at[_tile_rows(0, tm)], sem.at[s]).wait()

    @pl.when(i > 0)
    def _():
        wait_slot(1 - slot)

    @pl.when(i == pl.num_programs(0) - 1)
    def _():
        wait_slot(slot)


def _dispatch(last_block, n_valid, dest3, h1, g, n_slots):
    T = h1.shape[0]
    tm = TM_TOK
    grid_spec = pltpu.PrefetchScalarGridSpec(
        num_scalar_prefetch=2,
        grid=(T // tm,),
        in_specs=[pl.BlockSpec((1, 1, TOP_K * tm), lambda i, lb, nv: (i, 0, 0), memory_space=pltpu.SMEM),
                  pl.BlockSpec((tm, D_MODEL), lambda i, lb, nv: (i, 0)),
                  pl.BlockSpec((1, D_MODEL), lambda i, lb, nv: (0, 0), pipeline_mode=pl.Buffered(1))],
        out_specs=pl.BlockSpec(memory_space=pl.ANY),
        scratch_shapes=[pltpu.VMEM((2, tm * ROW_SUBLANES, LANES), F32), pltpu.SemaphoreType.DMA((2,)),
                        pltpu.SemaphoreType.DMA(())],
    )
    return pl.pallas_call(
        _dispatch_kernel,
        grid_spec=grid_spec,
        out_shape=jax.ShapeDtypeStruct((n_slots * ROW_SUBLANES, LANES), F32),
        compiler_params=_cparams("arbitrary"),
        name="dispatch",
    )(last_block, n_valid, dest3, h1, g)


def _experts_kernel(be_ref, nv_ref, xs_hbm, wgu_ref, bgu_ref, wd_ref, bd_ref, ys_hbm, xbuf, ybuf, xsem, ysem):
    del be_ref
    tm = xbuf.shape[1]
    b = pl.program_id(0)
    n_valid = nv_ref[0]
    slot = lax.rem(b, 2)

    def x_copies(blk, s):
        return [pltpu.make_async_copy(xs_hbm.at[pl.ds(blk * tm, tm), c, :],
                                      xbuf.at[s, :, pl.ds(c * LANES, LANES)], xsem.at[s]) for c in range(ROW_SUBLANES)]

    def y_copies(blk, s):
        return [pltpu.make_async_copy(ybuf.at[s, :, pl.ds(c * LANES, LANES)],
                                      ys_hbm.at[pl.ds(blk * tm, tm), c, :], ysem.at[s]) for c in range(ROW_SUBLANES)]

    @pl.when(jnp.logical_and(b == 0, n_valid > 0))
    def _():
        for cp in x_copies(0, 0):
            cp.start()

    @pl.when(b + 1 < n_valid)
    def _():
        for cp in x_copies(b + 1, 1 - slot):
            cp.start()

    @pl.when(b < n_valid)
    def _():
        for cp in x_copies(b, slot):
            cp.wait()
        x = xbuf[slot].astype(BF16)
        gu = jnp.dot(x, wgu_ref[0], preferred_element_type=F32) + bgu_ref[0]
        glu = jnp.minimum(gu[:, :D_EXPERT], SWIGLU_LIMIT)
        lin = jnp.clip(gu[:, D_EXPERT:], -SWIGLU_LIMIT, SWIGLU_LIMIT)
        act = glu * jax.nn.sigmoid(SWIGLU_ALPHA * glu) * (lin + 1.0)
        ybuf[slot] = jnp.dot(act.astype(BF16), wd_ref[0], preferred_element_type=F32) + bd_ref[0]

    @pl.when(b >= n_valid)
    def _():
        ybuf[slot] = jnp.zeros(ybuf.shape[1:], F32)

    for cp in y_copies(b, slot):
        cp.start()

    @pl.when(b > 0)
    def _():
        for cp in y_copies(b - 1, 1 - slot):
            cp.wait()

    @pl.when(b == pl.num_programs(0) - 1)
    def _():
        for cp in y_copies(b, slot):
            cp.wait()


def _experts(block_expert, n_valid, xs, wgu, bgu, wd, bd):
    tm = TM_EXP
    n_slots = xs.shape[0] // ROW_SUBLANES
    n_blocks = n_slots // tm
    any_space = pl.BlockSpec(memory_space=pl.ANY)
    grid_spec = pltpu.PrefetchScalarGridSpec(
        num_scalar_prefetch=2,
        grid=(n_blocks,),
        in_specs=[any_space,
                  pl.BlockSpec((1, D_MODEL, 2 * D_EXPERT), lambda b, be, nv: (be[b], 0, 0)),
                  pl.BlockSpec((1, 1, 2 * D_EXPERT), lambda b, be, nv: (be[b], 0, 0)),
                  pl.BlockSpec((1, D_EXPERT, D_MODEL), lambda b, be, nv: (be[b], 0, 0)),
                  pl.BlockSpec((1, 1, D_MODEL), lambda b, be, nv: (be[b], 0, 0))],
        out_specs=any_space,
        scratch_shapes=[pltpu.VMEM((2, tm, D_MODEL), F32), pltpu.VMEM((2, tm, D_MODEL), F32),
                        pltpu.SemaphoreType.DMA((2,)), pltpu.SemaphoreType.DMA((2,))],
    )
    ys = pl.pallas_call(
        _experts_kernel,
        grid_spec=grid_spec,
        out_shape=jax.ShapeDtypeStruct((n_slots, ROW_SUBLANES, LANES), F32),
        compiler_params=_cparams("arbitrary"),
        name="experts",
    )(block_expert, n_valid, xs.reshape(n_slots, ROW_SUBLANES, LANES), wgu, bgu, wd, bd)
    return ys.reshape(xs.shape)


GATHER_SLOTS = 3


def _combine_kernel(*refs, n_tiles):
    ahead = GATHER_SLOTS - 1
    prime_refs, refs = refs[:ahead], refs[ahead:]
    (ahead_dest_ref, h_ref, gate_ref, p_ref, gp_ref, gf_ref, wpg_hbm, wpp_hbm, ys_ref, o_ref), refs = refs[:10], refs[10:]
    bufs, (sem, wpg_ref, wpp_ref, w_stage, w_sem) = refs[:GATHER_SLOTS], refs[GATHER_SLOTS:]
    tm = h_ref.shape[0]
    i = pl.program_id(0)

    def row_copy(dref, t, k, s):
        d = dref[0, 0, k * tm + t]
        return pltpu.make_async_copy(ys_ref.at[_tile_rows(d)], bufs[s].at[_tile_rows(k * tm + t)], sem.at[s])

    def wait_slot(s):
        for _ in range(TOP_K):
            pltpu.make_async_copy(ys_ref.at[_tile_rows(0, tm)], bufs[s].at[_tile_rows(0, tm)], sem.at[s]).wait()

    @pl.when(i == 0)
    def _():
        _load_weight_bf16(wpg_hbm, wpg_ref, w_stage, w_sem)
        _load_weight_bf16(wpp_hbm, wpp_ref, w_stage, w_sem)

        for s, dref in enumerate(prime_refs):
            def issue(t, carry, s=s, dref=dref):
                for k in range(TOP_K):
                    row_copy(dref, t, k, s).start(priority=k % 2)
                return carry
            lax.fori_loop(0, tm, issue, 0, unroll=8)

    def step(s):
        wait_slot(s)
        for t in range(tm):
            for k in range(TOP_K):
                row_copy(ahead_dest_ref, t, k, (s + ahead) % GATHER_SLOTS).start(priority=k % 2)
        proj = jnp.dot(p_ref[...].astype(BF16), wpp_ref[...], preferred_element_type=F32)
        h = h_ref[...]
        for k in range(TOP_K):
            h = h + gate_ref[:, k:k + 1] * _from_row_tiles(bufs[s], (), k * tm, tm)
        ple_gate = jax.nn.sigmoid(jnp.dot(_rms(h, gp_ref[...]).astype(BF16), wpg_ref[...], preferred_element_type=F32))
        h = h + ple_gate * proj
        o_ref[...] = _rms(h, gf_ref[...])

    for s in range(GATHER_SLOTS):
        pl.when(lax.rem(i, GATHER_SLOTS) == s)(functools.partial(step, s))

    @pl.when(i == n_tiles - 1)
    def _():
        for extra in range(ahead):
            wait_slot((n_tiles + extra) % GATHER_SLOTS)


def _combine(dest3, h1, gate, p2, gp, wpg, wpp, gf, ys):
    T = h1.shape[0]
    tm = TM_TOK
    n_tiles = T // tm
    row = lambda w: pl.BlockSpec((tm, w), lambda i: (i, 0))
    dest_spec = lambda ahead: pl.BlockSpec((1, 1, TOP_K * tm), lambda i: (jnp.minimum(i + ahead, n_tiles - 1), 0, 0),
                                           memory_space=pltpu.SMEM)
    gather_buf = pltpu.VMEM((TOP_K * tm * ROW_SUBLANES, LANES), F32)
    ahead = GATHER_SLOTS - 1
    assert n_tiles > ahead
    return pl.pallas_call(
        functools.partial(_combine_kernel, n_tiles=n_tiles),
        grid=(n_tiles,),
        in_specs=[dest_spec(a) for a in range(ahead)] + [dest_spec(ahead),
                  row(D_MODEL), row(TOP_K), row(PLE_DIM), _resident((1, D_MODEL)), _resident((1, D_MODEL))]
                 + [pl.BlockSpec(memory_space=pl.ANY)] * 3,
        out_specs=row(D_MODEL),
        out_shape=jax.ShapeDtypeStruct((T, D_MODEL), F32),
        scratch_shapes=[gather_buf] * GATHER_SLOTS + [pltpu.SemaphoreType.DMA((GATHER_SLOTS,)),
                        pltpu.VMEM(wpg.shape, BF16), pltpu.VMEM(wpp.shape, BF16)] + _weight_stage(D_MODEL),
        compiler_params=_cparams("arbitrary"),
        name="combine",
    )(*([dest3] * (ahead + 1)), h1, gate, p2, gp, gf, wpg, wpp, ys)


def _layer(h, p_i, g_mix, w_in, rel_bias, w_att_out, ln_v_g, ln_v_b, w_spatial, b_spatial, w_gmlp_out, w_out,
           g_moe, w_router, b_router, w_gate_up, b_gate_up, w_down, b_down, g_ple, w_ple_gate, w_ple_proj,
           g_final, B, S):
    T = B * S
    row = lambda v: v.reshape(1, -1).astype(F32)

    assert S % TM_PROJ == 0
    *att_in, uv, gl, wgu_bf = _in_proj(h, row(g_mix), w_in.astype(F32), w_gate_up.astype(F32), B, S)

    outs, lses = [], []
    for g, (window, dilation) in enumerate(ATT_GROUPS):
        assert window // dilation == BLK and S % (dilation * BLK) == 0
        bias = _bias_table(rel_bias[:, g * HEADS_PER_GROUP:(g + 1) * HEADS_PER_GROUP], dilation)
        o, lse = _attention_group(att_in[g], bias, dilation, B, S)
        outs.append(o)
        lses.append(lse)

    causal = jnp.asarray(np.tril(np.ones((CHUNK, CHUNK), np.float32)))
    w_c = (w_spatial.astype(F32) * causal[None]).astype(BF16)
    wc2 = jnp.concatenate([w_c[0::2], w_c[1::2]], axis=2)
    bs = jnp.repeat(b_spatial.astype(F32).T, GMLP_GD, axis=1)
    h1, wd_bf = _mix(h, outs, lses, uv, gl, w_att_out.astype(F32), w_gmlp_out.astype(F32), w_out.astype(F32),
                     wc2, bs, row(ln_v_g), row(ln_v_b), w_down.astype(F32), S)

    wr_hi = w_router.astype(BF16)
    wr_lo = (w_router.astype(F32) - wr_hi.astype(F32)).astype(BF16)
    eidx, gate, rank, counts = _router(h1, row(g_moe), jnp.concatenate([wr_hi, wr_lo], axis=1).T,
                                       b_router.reshape(-1, 1).astype(F32))
    cnt = counts[:, 0].astype(jnp.int32)
    blk_counts = (cnt + TM_EXP - 1) // TM_EXP
    blk_end = jnp.cumsum(blk_counts)
    pad_start = (blk_end - blk_counts) * TM_EXP
    n_blocks = T * TOP_K // TM_EXP + N_EXPERTS
    n_valid = blk_end[-1:].astype(jnp.int32)
    blk = jnp.minimum(jnp.arange(n_blocks, dtype=jnp.int32), n_valid[0] - 1)
    block_expert = jnp.minimum(jnp.sum((blk_end[None, :] <= blk[:, None]).astype(jnp.int32), axis=1), N_EXPERTS - 1)
    expert_ids = jnp.arange(N_EXPERTS, dtype=jnp.int32)
    dest = rank + jnp.sum(jnp.where(eidx[..., None] == expert_ids, pad_start, 0), axis=-1)
    split = TM_ROUTER // TM_TOK
    dest3 = jnp.transpose(dest.reshape(T // TM_ROUTER, TOP_K, split, TM_TOK), (0, 2, 1, 3)).reshape(
        T // TM_TOK, 1, TOP_K * TM_TOK)
    gate = jnp.transpose(gate, (0, 2, 1)).reshape(T, TOP_K)

    last_block = jnp.maximum(blk_end - 1, 0).astype(jnp.int32)
    xs = _dispatch(last_block, n_valid, dest3, h1, row(g_moe), n_blocks * TM_EXP)
    ys = _experts(block_expert, n_valid, xs, wgu_bf, b_gate_up.reshape(N_EXPERTS, 1, -1).astype(F32),
                  wd_bf, b_down.reshape(N_EXPERTS, 1, -1).astype(F32))
    return _combine(dest3, h1, gate, p_i, row(g_ple), w_ple_gate.astype(F32), w_ple_proj.astype(F32),
                    row(g_final), ys)


def kernel(x, p, g_mix, w_in, rel_bias, w_att_out, ln_v_g, ln_v_b, w_spatial, b_spatial, w_gmlp_out, w_out, g_moe, w_router, b_router, w_gate_up, b_gate_up, w_down, b_down, g_ple, w_ple_gate, w_ple_proj, g_final):
    B, S, D = x.shape
    depth = p.shape[0]
    assert depth == 1, "the final RMSNorm is fused into the (single) layer's last kernel"
    out = _layer(x.reshape(B * S, D), p[0].reshape(B * S, PLE_DIM), g_mix[0], w_in[0], rel_bias, w_att_out[0],
                 ln_v_g[0], ln_v_b[0], w_spatial[0], b_spatial[0], w_gmlp_out[0], w_out[0], g_moe[0], w_router[0],
                 b_router[0], w_gate_up[0], b_gate_up[0], w_down[0], b_down[0], g_ple[0], w_ple_gate[0],
                 w_ple_proj[0], g_final, B, S)
    return out.reshape(B, S, D)
```

```python
import functools

import jax
import jax.numpy as jnp
import numpy as np
from jax import lax
from jax.experimental import pallas as pl
from jax.experimental.pallas import tpu as pltpu

F32 = jnp.float32
BF16 = jnp.bfloat16

D_MODEL = 1024
HEAD_DIM = 64
ATT_GROUPS = ((128, 1), (512, 4), (2048, 16))
HEADS_PER_GROUP = 4
GROUP_W = HEADS_PER_GROUP * HEAD_DIM
N_DIL = len(ATT_GROUPS)
ATT_W = N_DIL * GROUP_W
BLK = 128
REL_BUCKETS = 32
REL_MAX_DIST = 2048
CHUNK = 128
GMLP_W = 768
GMLP_GD = 64
N_BRANCH = 2
IN_W = 3 * ATT_W + 2 * GMLP_W + N_BRANCH * D_MODEL
N_EXPERTS = 32
TOP_K = 4
D_EXPERT = D_MODEL
SWIGLU_LIMIT = 7.0
SWIGLU_ALPHA = 1.702
PLE_DIM = 256
EPS = 1e-6
MASKED = -1e30
LOG2E = float(np.log2(np.e))
LN2 = float(np.log(2.0))

QKV_G = 3 * GROUP_W

LANES = 128
ROW_SUBLANES = D_MODEL // LANES
assert ROW_SUBLANES == 8
MXU_N = 256
VMEM_LIMIT = 56 * 1024 * 1024

TM_PROJ = 512
TM_ROUTER = 2048
TM_TOK = 256
TM_EXP = 512
assert TM_EXP % TM_TOK == 0


def _cparams(*sem):
    return pltpu.CompilerParams(dimension_semantics=sem, vmem_limit_bytes=VMEM_LIMIT)


def _resident(shape):
    nd = len(shape)
    return pl.BlockSpec(shape, lambda *_: (0,) * nd, pipeline_mode=pl.Buffered(1))


def _rms(x, g):
    return x * lax.rsqrt(jnp.mean(x * x, axis=-1, keepdims=True) + EPS) * g


def _load_weight_bf16(w_hbm, w_bf, stage, sem, scale=None):
    rows = stage.shape[1]
    n_chunks = w_hbm.shape[0] // rows
    assert n_chunks * rows == w_hbm.shape[0] and stage.shape[2] == w_hbm.shape[1]

    def chunk(c):
        return pltpu.make_async_copy(w_hbm.at[pl.ds(c * rows, rows)], stage.at[c % 2], sem.at[c % 2])

    chunk(0).start()
    for c in range(n_chunks):
        if c + 1 < n_chunks:
            chunk(c + 1).start()
        chunk(c).wait()
        piece = stage[c % 2] if scale is None else stage[c % 2] * scale
        w_bf[c * rows:(c + 1) * rows, :] = piece.astype(BF16)


def _expert_slice_spec(n_steps, width):
    per_expert = n_steps // N_EXPERTS
    assert per_expert * N_EXPERTS == n_steps and D_MODEL % per_expert == 0
    return pl.BlockSpec((1, D_MODEL // per_expert, width), lambda i: (i // per_expert, i % per_expert, 0))


def _inproj_kernel(x0_ref, xn_ref, g_ref, w_hbm, we_ref, a1_ref, a2_ref, a3_ref, uv_ref, gl_ref, we_bf_ref,
                   scr, w_ref, w_stage, w_sem, n_scr, n_tmp):
    i = pl.program_id(0)
    slot = lax.rem(i, 2)

    @pl.when(i == 0)
    def _():
        _load_weight_bf16(w_hbm, w_ref, w_stage, w_sem)
        n_scr[0] = _rms(x0_ref[...], g_ref[...]).astype(BF16)

    tm = xn_ref.shape[0]
    n = n_scr[slot]
    att_refs = (a1_ref, a2_ref, a3_ref)
    n_att, n_uv = 3 * ATT_W // MXU_N, 2 * GMLP_W // MXU_N
    n_pieces = 16
    side_rows, we_rows = tm // n_pieces, we_ref.shape[1] // n_pieces
    for c in range(IN_W // MXU_N):
        if c < n_pieces:
            we_bf_ref[0, c * we_rows:(c + 1) * we_rows, :] = we_ref[0, c * we_rows:(c + 1) * we_rows, :].astype(BF16)
            rows = slice(c * side_rows, (c + 1) * side_rows)
            n_tmp[rows, :] = _rms(xn_ref[rows, :], g_ref[...]).astype(BF16)
        z = jnp.dot(n, w_ref[:, c * MXU_N:(c + 1) * MXU_N], preferred_element_type=F32)
        if c < n_att:
            which, g = divmod(c, N_DIL)
            d = ATT_GROUPS[g][1]
            dst = att_refs[g]
            cols = slice(which * GROUP_W, (which + 1) * GROUP_W)
            if d == 1:
                dst[0, 0, :, cols] = z.astype(BF16)
                continue
            scr[0] = z[:, :LANES]
            scr[1] = z[:, LANES:]
            for r in range(d):
                zr = jnp.concatenate([scr[0, pl.ds(r, tm // d, stride=d), :],
                                      scr[1, pl.ds(r, tm // d, stride=d), :]], axis=1)
                dst[0, r, :, cols] = zr.astype(BF16)
        elif c < n_att + n_uv:
            uv_ref[:, (c - n_att) * MXU_N:(c - n_att + 1) * MXU_N] = z.astype(BF16)
        else:
            gl_ref[:, (c - n_att - n_uv) * MXU_N:(c - n_att - n_uv + 1) * MXU_N] = (z * 0.5).astype(BF16)
    n_scr[1 - slot] = n_tmp[...]


def _plane_spec(d, tm, tiles_per_seq, width):
    return pl.BlockSpec((1, d, tm // d, width), lambda i: (i // tiles_per_seq, 0, i % tiles_per_seq, 0))


W_STAGE_ROWS = 128


def _weight_stage(width):
    return [pltpu.VMEM((2, W_STAGE_ROWS, width), F32), pltpu.SemaphoreType.DMA((2,))]


def _in_proj(x2, g, w, w_expert, B, S):
    T = x2.shape[0]
    tm = TM_PROJ
    row = lambda w: pl.BlockSpec((tm, w), lambda i: (i, 0))
    dils = [d for _, d in ATT_GROUPS]
    we_spec = _expert_slice_spec(T // tm, w_expert.shape[2])
    n_steps = T // tm
    first_tile = pl.BlockSpec((tm, D_MODEL), lambda i: (0, 0), pipeline_mode=pl.Buffered(1))
    next_tile = pl.BlockSpec((tm, D_MODEL), lambda i: (jnp.minimum(i + 1, n_steps - 1), 0))
    return pl.pallas_call(
        _inproj_kernel,
        grid=(n_steps,),
        in_specs=[first_tile, next_tile, _resident((1, D_MODEL)), pl.BlockSpec(memory_space=pl.ANY), we_spec],
        out_specs=[_plane_spec(d, tm, S // tm, QKV_G) for d in dils] + [row(2 * GMLP_W), row(N_BRANCH * D_MODEL), we_spec],
        out_shape=[jax.ShapeDtypeStruct((B, d, S // d, QKV_G), BF16) for d in dils]
                  + [jax.ShapeDtypeStruct((T, 2 * GMLP_W), BF16),
                     jax.ShapeDtypeStruct((T, N_BRANCH * D_MODEL), BF16),
                     jax.ShapeDtypeStruct(w_expert.shape, BF16)],
        scratch_shapes=[pltpu.VMEM((2, tm, LANES), F32), pltpu.VMEM((D_MODEL, IN_W), BF16)] + _weight_stage(IN_W)
                       + [pltpu.VMEM((2, tm, D_MODEL), BF16), pltpu.VMEM((tm, D_MODEL), BF16)],
        compiler_params=_cparams("arbitrary"),
        name="in_proj",
    )(x2, x2, g, w, w_expert)


def _t5_bucket(n):
    exact = REL_BUCKETS // 2
    nf = np.maximum(n, 1).astype(np.float32)
    large = exact + (np.log(nf / exact) / np.log(REL_MAX_DIST / exact) * (REL_BUCKETS - exact)).astype(np.int32)
    large = np.minimum(large, REL_BUCKETS - 1)
    return np.where(n < exact, n, large).astype(np.int32)


def _bias_table(rel_bias_g, dilation):
    n = 3 * BLK
    dist = 2 * BLK - 1 - np.arange(n)
    valid = (dist >= 0) & (dist <= BLK)
    bucket = _t5_bucket(np.clip(dist, 0, BLK) * dilation)
    c = jnp.where(jnp.asarray(valid)[None, :], rel_bias_g.astype(F32)[bucket].T * LOG2E, MASKED)
    shifted = jnp.tile(c, (1, BLK))[:, :BLK * (n - 1)].reshape(HEADS_PER_GROUP, BLK, n - 1)
    return shifted[:, :, BLK - 1:].reshape(HEADS_PER_GROUP * BLK, 2 * BLK)


def _attn_kernel(cur_ref, prev_ref, bias_ref, o_ref, lse_ref):
    rg, rb = cur_ref.shape[1], cur_ref.shape[2] // BLK
    starts_sequence = pl.program_id(2) == 0
    lane_head = lax.broadcasted_iota(jnp.int32, (1, GROUP_W), 1) // HEAD_DIM
    scale = HEAD_DIM ** -0.5
    head_bf = [jnp.where(lane_head == h, scale, 0.0).astype(BF16) for h in range(HEADS_PER_GROUP)]
    key_is_prev = lax.broadcasted_iota(jnp.int32, (1, 2 * BLK), 1) < BLK
    nt = (((1,), (1,)), ((), ()))
    qc, kc_, vc_ = slice(0, GROUP_W), slice(GROUP_W, 2 * GROUP_W), slice(2 * GROUP_W, 3 * GROUP_W)

    def by_head(x):
        sel = x[(HEADS_PER_GROUP - 1) * BLK:]
        for h in range(HEADS_PER_GROUP - 2, -1, -1):
            sel = jnp.where(lane_head == h, x[h * BLK:(h + 1) * BLK], sel)
        return sel

    for r, j in [(r, j) for r in range(rg) for j in range(rb)]:
        rows = slice(j * BLK, (j + 1) * BLK)
        prev = prev_ref if j == 0 else cur_ref
        prows = slice(0, BLK) if j == 0 else slice((j - 1) * BLK, j * BLK)
        q = cur_ref[0, r, rows, qc]
        k = jnp.concatenate([prev[0, r, prows, kc_], cur_ref[0, r, rows, kc_]], axis=0)
        v = jnp.concatenate([prev[0, r, prows, vc_], cur_ref[0, r, rows, vc_]], axis=0)
        q_bd = jnp.concatenate([q * head_bf[h] for h in range(HEADS_PER_GROUP)], axis=0)
        s = lax.dot_general(q_bd, k, nt, preferred_element_type=F32) * LOG2E + bias_ref[...]
        if j == 0:
            s = jnp.where(jnp.logical_and(starts_sequence, key_is_prev), MASKED, s)
        m = jnp.max(s, axis=-1, keepdims=True)
        p = jnp.exp2(s - m)
        den = jnp.sum(p, axis=-1, keepdims=True)
        o = jnp.dot(p.astype(BF16), v, preferred_element_type=F32)
        den_h = jnp.broadcast_to(by_head(den), (BLK, GROUP_W))
        o_ref[0, r, rows, :] = (by_head(o) / den_h).astype(BF16)
        lse_ref[0, r, rows, :] = by_head(m) * LN2 + jnp.log(den_h)


ATT_SUBBLOCKS = 32


def _attention_group(a, bias, dilation, B, S):
    sd = S // dilation
    rb = min(ATT_SUBBLOCKS, sd // BLK)
    rg = min(ATT_SUBBLOCKS // rb, dilation)
    o, lse = pl.pallas_call(
        _attn_kernel,
        grid=(B, dilation // rg, sd // (rb * BLK)),
        in_specs=[pl.BlockSpec((1, rg, rb * BLK, QKV_G), lambda b, r, n: (b, r, n, 0)),
                  pl.BlockSpec((1, rg, BLK, QKV_G), lambda b, r, n: (b, r, jnp.maximum(n * rb - 1, 0), 0)),
                  _resident((HEADS_PER_GROUP * BLK, 2 * BLK))],
        out_specs=[pl.BlockSpec((1, rg, rb * BLK, GROUP_W), lambda b, r, n: (b, r, n, 0))] * 2,
        out_shape=[jax.ShapeDtypeStruct((B, dilation, sd, GROUP_W), BF16),
                   jax.ShapeDtypeStruct((B, dilation, sd, GROUP_W), F32)],
        compiler_params=_cparams("parallel", "parallel", "parallel"),
        name=f"attn_d{dilation}",
    )(a, a, bias)
    return o, lse


def _gelu(x):
    return x * (lax.erf(x * (2.0 ** -0.5)) + 1.0) * 0.5


def _token_major(src_ref, d, scr, slot, tm):
    if d == 1:
        return src_ref[0, 0].astype(F32)
    for r in range(d):
        piece = src_ref[0, r].astype(F32)
        scr[slot, pl.ds(r, tm // d, stride=d), :] = piece[:, :LANES]
        scr[slot + 1, pl.ds(r, tm // d, stride=d), :] = piece[:, LANES:]
    return jnp.concatenate([scr[slot], scr[slot + 1]], axis=1)


def _mix_kernel(x_ref, o1_ref, o2_ref, o3_ref, l1_ref, l2_ref, l3_ref, uv_ref, gl_ref,
                wa_hbm, wg_hbm, wo_hbm, wc_ref, bs_ref, lng_ref, lnb_ref, we_ref, h_ref, we_bf_ref, g_scr, t_scr,
                wa_ref, wg_ref, wo_ref, w_stage, w_sem):
    @pl.when(pl.program_id(0) == 0)
    def _():
        _load_weight_bf16(wa_hbm, wa_ref, w_stage, w_sem)
        _load_weight_bf16(wg_hbm, wg_ref, w_stage, w_sem)
        _load_weight_bf16(wo_hbm, wo_ref, w_stage, w_sem, scale=0.5)

    we_bf_ref[...] = we_ref[...].astype(BF16)

    tm = x_ref.shape[0]
    dils = [d for _, d in ATT_GROUPS]
    o1, o2, o3 = [_token_major(ref, d, t_scr, 4 * i, tm) for i, (ref, d) in enumerate(zip((o1_ref, o2_ref, o3_ref), dils))]
    l1, l2, l3 = [_token_major(ref, d, t_scr, 4 * i + 2, tm) for i, (ref, d) in enumerate(zip((l1_ref, l2_ref, l3_ref), dils))]
    lm = jnp.maximum(jnp.maximum(l1, l2), l3)
    e1, e2, e3 = jnp.exp(l1 - lm), jnp.exp(l2 - lm), jnp.exp(l3 - lm)
    att = (e1 * o1 + e2 * o2 + e3 * o3) / (e1 + e2 + e3)
    y_att = jnp.dot(att.astype(BF16), wa_ref[...], preferred_element_type=F32)

    zu = _gelu(uv_ref[:, :GMLP_W].astype(F32))
    zv = _gelu(uv_ref[:, GMLP_W:].astype(F32))
    mu = jnp.mean(zv, axis=-1, keepdims=True)
    var = jnp.mean(jnp.square(zv - mu), axis=-1, keepdims=True)
    vn = (zv - mu) * lax.rsqrt(var + EPS) * lng_ref[...] + lnb_ref[...]
    low_half = lax.broadcasted_iota(jnp.int32, (CHUNK, 2 * GMLP_GD), 1) < GMLP_GD
    for c in range(tm // CHUNK):
        rows = slice(c * CHUNK, (c + 1) * CHUNK)
        for s in range(GMLP_W // (2 * GMLP_GD)):
            cols = slice(s * 2 * GMLP_GD, (s + 1) * 2 * GMLP_GD)
            v2 = vn[rows, cols]
            rhs = jnp.concatenate([jnp.where(low_half, v2, 0.0), jnp.where(low_half, 0.0, v2)], axis=0).astype(BF16)
            mixed = jnp.dot(wc_ref[s], rhs, preferred_element_type=F32) + bs_ref[:, cols]
            g_scr[rows, cols] = (zu[rows, cols] * mixed).astype(BF16)
    y_gm = jnp.dot(g_scr[...], wg_ref[...], preferred_element_type=F32)

    gate_a = jnp.tanh(gl_ref[:, :D_MODEL].astype(F32)) + 1.0
    gate_g = jnp.tanh(gl_ref[:, D_MODEL:].astype(F32)) + 1.0
    merged = (gate_a * y_att + gate_g * y_gm).astype(BF16)
    h_ref[...] = x_ref[...] + jnp.dot(merged, wo_ref[...], preferred_element_type=F32)


def _mix(x2, outs, lses, uv, gl, wa, wg, wo, wc2, bs, lng, lnb, w_expert, S):
    T = x2.shape[0]
    tm = TM_PROJ
    row = lambda w: pl.BlockSpec((tm, w), lambda i: (i, 0))
    att = [_plane_spec(d, tm, S // tm, GROUP_W) for _, d in ATT_GROUPS]
    we_spec = _expert_slice_spec(T // tm, w_expert.shape[2])
    return pl.pallas_call(
        _mix_kernel,
        grid=(T // tm,),
        in_specs=[row(D_MODEL)] + att + att + [row(2 * GMLP_W), row(N_BRANCH * D_MODEL)]
                 + [pl.BlockSpec(memory_space=pl.ANY)] * 3
                 + [_resident(wc2.shape), _resident(bs.shape), _resident(lng.shape), _resident(lnb.shape), we_spec],
        out_specs=[row(D_MODEL), we_spec],
        out_shape=[jax.ShapeDtypeStruct((T, D_MODEL), F32), jax.ShapeDtypeStruct(w_expert.shape, BF16)],
        scratch_shapes=[pltpu.VMEM((tm, GMLP_W), BF16), pltpu.VMEM((4 * N_DIL, tm, LANES), F32),
                        pltpu.VMEM(wa.shape, BF16), pltpu.VMEM(wg.shape, BF16), pltpu.VMEM(wo.shape, BF16)]
                       + _weight_stage(D_MODEL),
        compiler_params=_cparams("arbitrary"),
        name="mix",
    )(x2, *outs, *lses, uv, gl, wa, wg, wo, wc2, bs, lng, lnb, w_expert)


def _router_kernel(h_ref, g_ref, wr_ref, br_ref, upper_ref, eidx_ref, gate_ref, rank_ref, cnt_ref, carry):
    tm = h_ref.shape[0]

    @pl.when(pl.program_id(0) == 0)
    def _():
        carry[...] = jnp.zeros_like(carry)

    hn = _rms(h_ref[...], g_ref[...])
    hi = hn.astype(BF16)
    lo = (hn - hi.astype(F32)).astype(BF16)
    nt = (((1,), (1,)), ((), ()))
    by_hi = lax.dot_general(wr_ref[...], hi, nt, preferred_element_type=F32)
    by_lo = lax.dot_general(wr_ref[:N_EXPERTS, :], lo, nt, preferred_element_type=F32)
    logits = by_hi[:N_EXPERTS] + by_hi[N_EXPERTS:] + by_lo + br_ref[...]
    expert = lax.broadcasted_iota(jnp.int32, (N_EXPERTS, tm), 0)
    vals, hots = [], []
    l = logits
    for k in range(TOP_K):
        m = jnp.max(l, axis=0, keepdims=True)
        idx = jnp.min(jnp.where(l == m, expert, N_EXPERTS), axis=0, keepdims=True)
        hot = expert == idx
        eidx_ref[0, k:k + 1, :] = idx
        vals.append(m)
        hots.append(hot)
        l = jnp.where(hot, -jnp.inf, l)
    ex = [jnp.exp(v - vals[0]) for v in vals]
    tot = ex[0] + ex[1] + ex[2] + ex[3]
    for k in range(TOP_K):
        gate_ref[0, k:k + 1, :] = ex[k] / tot
    multi = jnp.zeros((N_EXPERTS, tm), F32)
    for hot in hots:
        multi = multi + hot.astype(F32)
    before = jnp.dot(multi.astype(BF16), upper_ref[...], preferred_element_type=F32) + carry[...]
    for k in range(TOP_K):
        rank_ref[0, k:k + 1, :] = jnp.sum(jnp.where(hots[k], before, 0.0), axis=0, keepdims=True).astype(jnp.int32)
    carry[...] += jnp.sum(multi, axis=1, keepdims=True)
    cnt_ref[...] = carry[...]


def _router(h1, g, wr_t, br_col):
    T = h1.shape[0]
    tm = TM_ROUTER
    upper = jnp.asarray(np.triu(np.ones((tm, tm), np.float32), k=1), BF16)
    k_rows = pl.BlockSpec((1, TOP_K, tm), lambda i: (i, 0, 0))
    k_shape = lambda dt: jax.ShapeDtypeStruct((T // tm, TOP_K, tm), dt)
    return pl.pallas_call(
        _router_kernel,
        grid=(T // tm,),
        in_specs=[pl.BlockSpec((tm, D_MODEL), lambda i: (i, 0)), _resident((1, D_MODEL)),
                  _resident((2 * N_EXPERTS, D_MODEL)), _resident((N_EXPERTS, 1)), _resident((tm, tm))],
        out_specs=[k_rows, k_rows, k_rows, pl.BlockSpec((N_EXPERTS, 1), lambda i: (0, 0))],
        out_shape=[k_shape(jnp.int32), k_shape(F32), k_shape(jnp.int32),
                   jax.ShapeDtypeStruct((N_EXPERTS, 1), F32)],
        scratch_shapes=[pltpu.VMEM((N_EXPERTS, 1), F32)],
        compiler_params=_cparams("arbitrary"),
        name="router",
    )(h1, g, wr_t, br_col, upper)


def _to_row_tiles(ref, lead, value):
    n = value.shape[0]
    for c in range(ROW_SUBLANES):
        ref[(*lead, pl.ds(c, n, stride=ROW_SUBLANES), slice(None))] = value[:, c * LANES:(c + 1) * LANES]


def _from_row_tiles(ref, lead, first, n):
    return jnp.concatenate(
        [ref[(*lead, pl.ds(first * ROW_SUBLANES + c, n, stride=ROW_SUBLANES), slice(None))] for c in range(ROW_SUBLANES)],
        axis=1)


def _tile_rows(idx, n=1):
    return pl.ds(pl.multiple_of(idx * ROW_SUBLANES, ROW_SUBLANES), n * ROW_SUBLANES)


def _dispatch_kernel(last_ref, nv_ref, dest_ref, h_ref, g_ref, xs_ref, buf, sem, zero_sem):
    tm = h_ref.shape[0]
    n_blocks = xs_ref.shape[0] // (TM_EXP * ROW_SUBLANES)
    i = pl.program_id(0)
    slot = lax.rem(i, 2)

    @pl.when(i == 0)
    def _():
        buf[1] = jnp.zeros(buf.shape[1:], F32)

        def zero_block(b):
            for part in range(TM_EXP // tm):
                pltpu.make_async_copy(buf.at[1], xs_ref.at[_tile_rows(b * TM_EXP + part * tm, tm)], zero_sem).start()

        def zero_done():
            for part in range(TM_EXP // tm):
                pltpu.make_async_copy(buf.at[1], xs_ref.at[_tile_rows(0, tm)], zero_sem).wait()

        for e in range(N_EXPERTS):
            zero_block(last_ref[e])
        lax.fori_loop(nv_ref[0], n_blocks, lambda b, c: (zero_block(b), c)[1], 0)
        for e in range(N_EXPERTS):
            zero_done()
        lax.fori_loop(nv_ref[0], n_blocks, lambda b, c: (zero_done(), c)[1], 0)

    _to_row_tiles(buf, (slot,), _rms(h_ref[...], g_ref[...]))

    def issue(t, carry):
        for k in range(TOP_K):
            d = dest_ref[0, 0, k * tm + t]
            pltpu.make_async_copy(buf.at[slot, _tile_rows(t)], xs_ref.at[_tile_rows(d)],
                                  sem.at[slot]).start(priority=k % 2)
        return carry

    lax.fori_loop(0, tm, issue, 0, unroll=8)

    def wait_slot(s):
        for _ in range(TOP_K):
            pltpu.make_async_copy(buf.at[s], xs_ref.at[_tile_rows(0, tm)], sem.at[s]).wait()

    @pl.when(i > 0)
    def _():
        wait_slot(1 - slot)

    @pl.when(i == pl.num_programs(0) - 1)
    def _():
        wait_slot(slot)


def _dispatch(last_block, n_valid, dest3, h1, g, n_slots):
    T = h1.shape[0]
    tm = TM_TOK
    grid_spec = pltpu.PrefetchScalarGridSpec(
        num_scalar_prefetch=2,
        grid=(T // tm,),
        in_specs=[pl.BlockSpec((1, 1, TOP_K * tm), lambda i, lb, nv: (i, 0, 0), memory_space=pltpu.SMEM),
                  pl.BlockSpec((tm, D_MODEL), lambda i, lb, nv: (i, 0)),
                  pl.BlockSpec((1, D_MODEL), lambda i, lb, nv: (0, 0), pipeline_mode=pl.Buffered(1))],
        out_specs=pl.BlockSpec(memory_space=pl.ANY),
        scratch_shapes=[pltpu.VMEM((2, tm * ROW_SUBLANES, LANES), F32), pltpu.SemaphoreType.DMA((2,)),
                        pltpu.SemaphoreType.DMA(())],
    )
    return pl.pallas_call(
        _dispatch_kernel,
        grid_spec=grid_spec,
        out_shape=jax.ShapeDtypeStruct((n_slots * ROW_SUBLANES, LANES), F32),
        compiler_params=_cparams("arbitrary"),
        name="dispatch",
    )(last_block, n_valid, dest3, h1, g)


def _experts_kernel(be_ref, nv_ref, xs_hbm, wgu_ref, bgu_ref, wd_ref, bd_ref, ys_hbm, xbuf, ybuf, xsem, ysem):
    del be_ref
    tm = xbuf.shape[1]
    b = pl.program_id(0)
    n_valid = nv_ref[0]
    slot = lax.rem(b, 2)

    def x_copies(blk, s):
        return [pltpu.make_async_copy(xs_hbm.at[pl.ds(blk * tm, tm), c, :],
                                      xbuf.at[s, :, pl.ds(c * LANES, LANES)], xsem.at[s]) for c in range(ROW_SUBLANES)]

    def y_copies(blk, s):
        return [pltpu.make_async_copy(ybuf.at[s, :, pl.ds(c * LANES, LANES)],
                                      ys_hbm.at[pl.ds(blk * tm, tm), c, :], ysem.at[s]) for c in range(ROW_SUBLANES)]

    @pl.when(jnp.logical_and(b == 0, n_valid > 0))
    def _():
        for cp in x_copies(0, 0):
            cp.start()

    @pl.when(b + 1 < n_valid)
    def _():
        for cp in x_copies(b + 1, 1 - slot):
            cp.start()

    @pl.when(b < n_valid)
    def _():
        for cp in x_copies(b, slot):
            cp.wait()
        x = xbuf[slot].astype(BF16)
        gu = jnp.dot(x, wgu_ref[0], preferred_element_type=F32) + bgu_ref[0]
        glu = jnp.minimum(gu[:, :D_EXPERT], SWIGLU_LIMIT)
        lin = jnp.clip(gu[:, D_EXPERT:], -SWIGLU_LIMIT, SWIGLU_LIMIT)
        act = glu * jax.nn.sigmoid(SWIGLU_ALPHA * glu) * (lin + 1.0)
        ybuf[slot] = jnp.dot(act.astype(BF16), wd_ref[0], preferred_element_type=F32) + bd_ref[0]

    @pl.when(b >= n_valid)
    def _():
        ybuf[slot] = jnp.zeros(ybuf.shape[1:], F32)

    for cp in y_copies(b, slot):
        cp.start()

    @pl.when(b > 0)
    def _():
        for cp in y_copies(b - 1, 1 - slot):
            cp.wait()

    @pl.when(b == pl.num_programs(0) - 1)
    def _():
        for cp in y_copies(b, slot):
            cp.wait()


def _experts(block_expert, n_valid, xs, wgu, bgu, wd, bd):
    tm = TM_EXP
    n_slots = xs.shape[0] // ROW_SUBLANES
    n_blocks = n_slots // tm
    any_space = pl.BlockSpec(memory_space=pl.ANY)
    grid_spec = pltpu.PrefetchScalarGridSpec(
        num_scalar_prefetch=2,
        grid=(n_blocks,),
        in_specs=[any_space,
                  pl.BlockSpec((1, D_MODEL, 2 * D_EXPERT), lambda b, be, nv: (be[b], 0, 0)),
                  pl.BlockSpec((1, 1, 2 * D_EXPERT), lambda b, be, nv: (be[b], 0, 0)),
                  pl.BlockSpec((1, D_EXPERT, D_MODEL), lambda b, be, nv: (be[b], 0, 0)),
                  pl.BlockSpec((1, 1, D_MODEL), lambda b, be, nv: (be[b], 0, 0))],
        out_specs=any_space,
        scratch_shapes=[pltpu.VMEM((2, tm, D_MODEL), F32), pltpu.VMEM((2, tm, D_MODEL), F32),
                        pltpu.SemaphoreType.DMA((2,)), pltpu.SemaphoreType.DMA((2,))],
    )
    ys = pl.pallas_call(
        _experts_kernel,
        grid_spec=grid_spec,
        out_shape=jax.ShapeDtypeStruct((n_slots, ROW_SUBLANES, LANES), F32),
        compiler_params=_cparams("arbitrary"),
        name="experts",
    )(block_expert, n_valid, xs.reshape(n_slots, ROW_SUBLANES, LANES), wgu, bgu, wd, bd)
    return ys.reshape(xs.shape)


GATHER_SLOTS = 3


def _combine_kernel(*refs, n_tiles):
    ahead = GATHER_SLOTS - 1
    prime_refs, refs = refs[:ahead], refs[ahead:]
    (ahead_dest_ref, h_ref, gate_ref, p_ref, gp_ref, gf_ref, wpg_hbm, wpp_hbm, ys_ref, o_ref), refs = refs[:10], refs[10:]
    bufs, (sem, wpg_ref, wpp_ref, w_stage, w_sem) = refs[:GATHER_SLOTS], refs[GATHER_SLOTS:]
    tm = h_ref.shape[0]
    i = pl.program_id(0)

    def row_copy(dref, t, k, s):
        d = dref[0, 0, k * tm + t]
        return pltpu.make_async_copy(ys_ref.at[_tile_rows(d)], bufs[s].at[_tile_rows(k * tm + t)], sem.at[s])

    def wait_slot(s):
        for _ in range(TOP_K):
            pltpu.make_async_copy(ys_ref.at[_tile_rows(0, tm)], bufs[s].at[_tile_rows(0, tm)], sem.at[s]).wait()

    @pl.when(i == 0)
    def _():
        _load_weight_bf16(wpg_hbm, wpg_ref, w_stage, w_sem)
        _load_weight_bf16(wpp_hbm, wpp_ref, w_stage, w_sem)

        for s, dref in enumerate(prime_refs):
            def issue(t, carry, s=s, dref=dref):
                for k in range(TOP_K):
                    row_copy(dref, t, k, s).start(priority=k % 2)
                return carry
            lax.fori_loop(0, tm, issue, 0, unroll=8)

    def step(s):
        wait_slot(s)
        for t in range(tm):
            for k in range(TOP_K):
                row_copy(ahead_dest_ref, t, k, (s + ahead) % GATHER_SLOTS).start(priority=k % 2)
        proj = jnp.dot(p_ref[...].astype(BF16), wpp_ref[...], preferred_element_type=F32)
        h = h_ref[...]
        for k in range(TOP_K):
            h = h + gate_ref[:, k:k + 1] * _from_row_tiles(bufs[s], (), k * tm, tm)
        ple_gate = jax.nn.sigmoid(jnp.dot(_rms(h, gp_ref[...]).astype(BF16), wpg_ref[...], preferred_element_type=F32))
        h = h + ple_gate * proj
        o_ref[...] = _rms(h, gf_ref[...])

    for s in range(GATHER_SLOTS):
        pl.when(lax.rem(i, GATHER_SLOTS) == s)(functools.partial(step, s))

    @pl.when(i == n_tiles - 1)
    def _():
        for extra in range(ahead):
            wait_slot((n_tiles + extra) % GATHER_SLOTS)


def _combine(dest3, h1, gate, p2, gp, wpg, wpp, gf, ys):
    T = h1.shape[0]
    tm = TM_TOK
    n_tiles = T // tm
    row = lambda w: pl.BlockSpec((tm, w), lambda i: (i, 0))
    dest_spec = lambda ahead: pl.BlockSpec((1, 1, TOP_K * tm), lambda i: (jnp.minimum(i + ahead, n_tiles - 1), 0, 0),
                                           memory_space=pltpu.SMEM)
    gather_buf = pltpu.VMEM((TOP_K * tm * ROW_SUBLANES, LANES), F32)
    ahead = GATHER_SLOTS - 1
    assert n_tiles > ahead
    return pl.pallas_call(
        functools.partial(_combine_kernel, n_tiles=n_tiles),
        grid=(n_tiles,),
        in_specs=[dest_spec(a) for a in range(ahead)] + [dest_spec(ahead),
                  row(D_MODEL), row(TOP_K), row(PLE_DIM), _resident((1, D_MODEL)), _resident((1, D_MODEL))]
                 + [pl.BlockSpec(memory_space=pl.ANY)] * 3,
        out_specs=row(D_MODEL),
        out_shape=jax.ShapeDtypeStruct((T, D_MODEL), F32),
        scratch_shapes=[gather_buf] * GATHER_SLOTS + [pltpu.SemaphoreType.DMA((GATHER_SLOTS,)),
                        pltpu.VMEM(wpg.shape, BF16), pltpu.VMEM(wpp.shape, BF16)] + _weight_stage(D_MODEL),
        compiler_params=_cparams("arbitrary"),
        name="combine",
    )(*([dest3] * (ahead + 1)), h1, gate, p2, gp, gf, wpg, wpp, ys)


def _layer(h, p_i, g_mix, w_in, rel_bias, w_att_out, ln_v_g, ln_v_b, w_spatial, b_spatial, w_gmlp_out, w_out,
           g_moe, w_router, b_router, w_gate_up, b_gate_up, w_down, b_down, g_ple, w_ple_gate, w_ple_proj,
           g_final, B, S):
    T = B * S
    row = lambda v: v.reshape(1, -1).astype(F32)

    assert S % TM_PROJ == 0
    *att_in, uv, gl, wgu_bf = _in_proj(h, row(g_mix), w_in.astype(F32), w_gate_up.astype(F32), B, S)

    outs, lses = [], []
    for g, (window, dilation) in enumerate(ATT_GROUPS):
        assert window // dilation == BLK and S % (dilation * BLK) == 0
        bias = _bias_table(rel_bias[:, g * HEADS_PER_GROUP:(g + 1) * HEADS_PER_GROUP], dilation)
        o, lse = _attention_group(att_in[g], bias, dilation, B, S)
        outs.append(o)
        lses.append(lse)

    causal = jnp.asarray(np.tril(np.ones((CHUNK, CHUNK), np.float32)))
    w_c = (w_spatial.astype(F32) * causal[None]).astype(BF16)
    wc2 = jnp.concatenate([w_c[0::2], w_c[1::2]], axis=2)
    bs = jnp.repeat(b_spatial.astype(F32).T, GMLP_GD, axis=1)
    h1, wd_bf = _mix(h, outs, lses, uv, gl, w_att_out.astype(F32), w_gmlp_out.astype(F32), w_out.astype(F32),
                     wc2, bs, row(ln_v_g), row(ln_v_b), w_down.astype(F32), S)

    wr_hi = w_router.astype(BF16)
    wr_lo = (w_router.astype(F32) - wr_hi.astype(F32)).astype(BF16)
    eidx, gate, rank, counts = _router(h1, row(g_moe), jnp.concatenate([wr_hi, wr_lo], axis=1).T,
                                       b_router.reshape(-1, 1).astype(F32))
    cnt = counts[:, 0].astype(jnp.int32)
    blk_counts = (cnt + TM_EXP - 1) // TM_EXP
    blk_end = jnp.cumsum(blk_counts)
    pad_start = (blk_end - blk_counts) * TM_EXP
    n_blocks = T * TOP_K // TM_EXP + N_EXPERTS
    n_valid = blk_end[-1:].astype(jnp.int32)
    blk = jnp.minimum(jnp.arange(n_blocks, dtype=jnp.int32), n_valid[0] - 1)
    block_expert = jnp.minimum(jnp.sum((blk_end[None, :] <= blk[:, None]).astype(jnp.int32), axis=1), N_EXPERTS - 1)
    expert_ids = jnp.arange(N_EXPERTS, dtype=jnp.int32)
    dest = rank + jnp.sum(jnp.where(eidx[..., None] == expert_ids, pad_start, 0), axis=-1)
    split = TM_ROUTER // TM_TOK
    dest3 = jnp.transpose(dest.reshape(T // TM_ROUTER, TOP_K, split, TM_TOK), (0, 2, 1, 3)).reshape(
        T // TM_TOK, 1, TOP_K * TM_TOK)
    gate = jnp.transpose(gate, (0, 2, 1)).reshape(T, TOP_K)

    last_block = jnp.maximum(blk_end - 1, 0).astype(jnp.int32)
    xs = _dispatch(last_block, n_valid, dest3, h1, row(g_moe), n_blocks * TM_EXP)
    ys = _experts(block_expert, n_valid, xs, wgu_bf, b_gate_up.reshape(N_EXPERTS, 1, -1).astype(F32),
                  wd_bf, b_down.reshape(N_EXPERTS, 1, -1).astype(F32))
    return _combine(dest3, h1, gate, p_i, row(g_ple), w_ple_gate.astype(F32), w_ple_proj.astype(F32),
                    row(g_final), ys)


def kernel(x, p, g_mix, w_in, rel_bias, w_att_out, ln_v_g, ln_v_b, w_spatial, b_spatial, w_gmlp_out, w_out, g_moe, w_router, b_router, w_gate_up, b_gate_up, w_down, b_down, g_ple, w_ple_gate, w_ple_proj, g_final):
    B, S, D = x.shape
    depth = p.shape[0]
    assert depth == 1, "the final RMSNorm is fused into the (single) layer's last kernel"
    out = _layer(x.reshape(B * S, D), p[0].reshape(B * S, PLE_DIM), g_mix[0], w_in[0], rel_bias, w_att_out[0],
                 ln_v_g[0], ln_v_b[0], w_spatial[0], b_spatial[0], w_gmlp_out[0], w_out[0], g_moe[0], w_router[0],
                 b_router[0], w_gate_up[0], b_gate_up[0], w_down[0], b_down[0], g_ple[0], w_ple_gate[0],
                 w_ple_proj[0], g_final, B, S)
    return out.reshape(B, S, D)
```

```python
import functools

import jax
import jax.numpy as jnp
import numpy as np
from jax import lax
from jax.experimental import pallas as pl
from jax.experimental.pallas import tpu as pltpu

F32 = jnp.float32
BF16 = jnp.bfloat16

D_MODEL = 1024
HEAD_DIM = 64
ATT_GROUPS = ((128, 1), (512, 4), (2048, 16))
HEADS_PER_GROUP = 4
GROUP_W = HEADS_PER_GROUP * HEAD_DIM
N_DIL = len(ATT_GROUPS)
ATT_W = N_DIL * GROUP_W
BLK = 128
REL_BUCKETS = 32
REL_MAX_DIST = 2048
CHUNK = 128
GMLP_W = 768
GMLP_GD = 64
N_BRANCH = 2
IN_W = 3 * ATT_W + 2 * GMLP_W + N_BRANCH * D_MODEL
N_EXPERTS = 32
TOP_K = 4
D_EXPERT = D_MODEL
SWIGLU_LIMIT = 7.0
SWIGLU_ALPHA = 1.702
PLE_DIM = 256
EPS = 1e-6
MASKED = -1e30
LOG2E = float(np.log2(np.e))
LN2 = float(np.log(2.0))

QKV_G = 3 * GROUP_W

LANES = 128
ROW_SUBLANES = D_MODEL // LANES
assert ROW_SUBLANES == 8
MXU_N = 256
VMEM_LIMIT = 56 * 1024 * 1024

TM_PROJ = 512
TM_ROUTER = 1024
TM_TOK = 256
TM_EXP = 512
assert TM_EXP % TM_TOK == 0


def _cparams(*sem):
    return pltpu.CompilerParams(dimension_semantics=sem, vmem_limit_bytes=VMEM_LIMIT)


def _resident(shape):
    nd = len(shape)
    return pl.BlockSpec(shape, lambda *_: (0,) * nd, pipeline_mode=pl.Buffered(1))


def _rms(x, g):
    return x * lax.rsqrt(jnp.mean(x * x, axis=-1, keepdims=True) + EPS) * g


def _load_weight_bf16(w_hbm, w_bf, stage, sem, scale=None):
    rows = stage.shape[1]
    n_chunks = w_hbm.shape[0] // rows
    assert n_chunks * rows == w_hbm.shape[0] and stage.shape[2] == w_hbm.shape[1]

    def chunk(c):
        return pltpu.make_async_copy(w_hbm.at[pl.ds(c * rows, rows)], stage.at[c % 2], sem.at[c % 2])

    chunk(0).start()
    for c in range(n_chunks):
        if c + 1 < n_chunks:
            chunk(c + 1).start()
        chunk(c).wait()
        piece = stage[c % 2] if scale is None else stage[c % 2] * scale
        w_bf[c * rows:(c + 1) * rows, :] = piece.astype(BF16)


def _expert_slice_spec(n_steps, width):
    per_expert = n_steps // N_EXPERTS
    assert per_expert * N_EXPERTS == n_steps and D_MODEL % per_expert == 0
    return pl.BlockSpec((1, D_MODEL // per_expert, width), lambda i: (i // per_expert, i % per_expert, 0))


def _inproj_kernel(x0_ref, xn_ref, g_ref, w_hbm, we_ref, a1_ref, a2_ref, a3_ref, uv_ref, gl_ref, we_bf_ref,
                   scr, w_ref, w_stage, w_sem, n_scr, n_tmp):
    i = pl.program_id(0)
    slot = lax.rem(i, 2)

    @pl.when(i == 0)
    def _():
        _load_weight_bf16(w_hbm, w_ref, w_stage, w_sem)
        n_scr[0] = _rms(x0_ref[...], g_ref[...]).astype(BF16)

    tm = xn_ref.shape[0]
    n = n_scr[slot]
    att_refs = (a1_ref, a2_ref, a3_ref)
    n_att, n_uv = 3 * ATT_W // MXU_N, 2 * GMLP_W // MXU_N
    n_pieces = 16
    side_rows, we_rows = tm // n_pieces, we_ref.shape[1] // n_pieces
    for c in range(IN_W // MXU_N):
        if c < n_pieces:
            we_bf_ref[0, c * we_rows:(c + 1) * we_rows, :] = we_ref[0, c * we_rows:(c + 1) * we_rows, :].astype(BF16)
            rows = slice(c * side_rows, (c + 1) * side_rows)
            n_tmp[rows, :] = _rms(xn_ref[rows, :], g_ref[...]).astype(BF16)
        z = jnp.dot(n, w_ref[:, c * MXU_N:(c + 1) * MXU_N], preferred_element_type=F32)
        if c < n_att:
            which, g = divmod(c, N_DIL)
            d = ATT_GROUPS[g][1]
            dst = att_refs[g]
            cols = slice(which * GROUP_W, (which + 1) * GROUP_W)
            if d == 1:
                dst[0, 0, :, cols] = z.astype(BF16)
                continue
            scr[0] = z[:, :LANES]
            scr[1] = z[:, LANES:]
            for r in range(d):
                zr = jnp.concatenate([scr[0, pl.ds(r, tm // d, stride=d), :],
                                      scr[1, pl.ds(r, tm // d, stride=d), :]], axis=1)
                dst[0, r, :, cols] = zr.astype(BF16)
        elif c < n_att + n_uv:
            uv_ref[:, (c - n_att) * MXU_N:(c - n_att + 1) * MXU_N] = z.astype(BF16)
        else:
            gl_ref[:, (c - n_att - n_uv) * MXU_N:(c - n_att - n_uv + 1) * MXU_N] = (z * 0.5).astype(BF16)
    n_scr[1 - slot] = n_tmp[...]


def _plane_spec(d, tm, tiles_per_seq, width):
    return pl.BlockSpec((1, d, tm // d, width), lambda i: (i // tiles_per_seq, 0, i % tiles_per_seq, 0))


W_STAGE_ROWS = 128


def _weight_stage(width):
    return [pltpu.VMEM((2, W_STAGE_ROWS, width), F32), pltpu.SemaphoreType.DMA((2,))]


def _in_proj(x2, g, w, w_expert, B, S):
    T = x2.shape[0]
    tm = TM_PROJ
    row = lambda w: pl.BlockSpec((tm, w), lambda i: (i, 0))
    dils = [d for _, d in ATT_GROUPS]
    we_spec = _expert_slice_spec(T // tm, w_expert.shape[2])
    n_steps = T // tm
    first_tile = pl.BlockSpec((tm, D_MODEL), lambda i: (0, 0), pipeline_mode=pl.Buffered(1))
    next_tile = pl.BlockSpec((tm, D_MODEL), lambda i: (jnp.minimum(i + 1, n_steps - 1), 0))
    return pl.pallas_call(
        _inproj_kernel,
        grid=(n_steps,),
        in_specs=[first_tile, next_tile, _resident((1, D_MODEL)), pl.BlockSpec(memory_space=pl.ANY), we_spec],
        out_specs=[_plane_spec(d, tm, S // tm, QKV_G) for d in dils] + [row(2 * GMLP_W), row(N_BRANCH * D_MODEL), we_spec],
        out_shape=[jax.ShapeDtypeStruct((B, d, S // d, QKV_G), BF16) for d in dils]
                  + [jax.ShapeDtypeStruct((T, 2 * GMLP_W), BF16),
                     jax.ShapeDtypeStruct((T, N_BRANCH * D_MODEL), BF16),
                     jax.ShapeDtypeStruct(w_expert.shape, BF16)],
        scratch_shapes=[pltpu.VMEM((2, tm, LANES), F32), pltpu.VMEM((D_MODEL, IN_W), BF16)] + _weight_stage(IN_W)
                       + [pltpu.VMEM((2, tm, D_MODEL), BF16), pltpu.VMEM((tm, D_MODEL), BF16)],
        compiler_params=_cparams("arbitrary"),
        name="in_proj",
    )(x2, x2, g, w, w_expert)


def _t5_bucket(n):
    exact = REL_BUCKETS // 2
    nf = np.maximum(n, 1).astype(np.float32)
    large = exact + (np.log(nf / exact) / np.log(REL_MAX_DIST / exact) * (REL_BUCKETS - exact)).astype(np.int32)
    large = np.minimum(large, REL_BUCKETS - 1)
    return np.where(n < exact, n, large).astype(np.int32)


def _bias_table(rel_bias_g, dilation):
    n = 3 * BLK
    dist = 2 * BLK - 1 - np.arange(n)
    valid = (dist >= 0) & (dist <= BLK)
    bucket = _t5_bucket(np.clip(dist, 0, BLK) * dilation)
    c = jnp.where(jnp.asarray(valid)[None, :], rel_bias_g.astype(F32)[bucket].T * LOG2E, MASKED)
    shifted = jnp.tile(c, (1, BLK))[:, :BLK * (n - 1)].reshape(HEADS_PER_GROUP, BLK, n - 1)
    return shifted[:, :, BLK - 1:].reshape(HEADS_PER_GROUP * BLK, 2 * BLK)


def _attn_kernel(cur_ref, prev_ref, bias_ref, o_ref, lse_ref):
    rg, rb = cur_ref.shape[1], cur_ref.shape[2] // BLK
    starts_sequence = pl.program_id(2) == 0
    lane_head = lax.broadcasted_iota(jnp.int32, (1, GROUP_W), 1) // HEAD_DIM
    scale = HEAD_DIM ** -0.5
    head_bf = [jnp.where(lane_head == h, scale, 0.0).astype(BF16) for h in range(HEADS_PER_GROUP)]
    key_is_prev = lax.broadcasted_iota(jnp.int32, (1, 2 * BLK), 1) < BLK
    nt = (((1,), (1,)), ((), ()))
    qc, kc_, vc_ = slice(0, GROUP_W), slice(GROUP_W, 2 * GROUP_W), slice(2 * GROUP_W, 3 * GROUP_W)

    def by_head(x):
        sel = x[(HEADS_PER_GROUP - 1) * BLK:]
        for h in range(HEADS_PER_GROUP - 2, -1, -1):
            sel = jnp.where(lane_head == h, x[h * BLK:(h + 1) * BLK], sel)
        return sel

    for r, j in [(r, j) for r in range(rg) for j in range(rb)]:
        rows = slice(j * BLK, (j + 1) * BLK)
        prev = prev_ref if j == 0 else cur_ref
        prows = slice(0, BLK) if j == 0 else slice((j - 1) * BLK, j * BLK)
        q = cur_ref[0, r, rows, qc]
        k = jnp.concatenate([prev[0, r, prows, kc_], cur_ref[0, r, rows, kc_]], axis=0)
        v = jnp.concatenate([prev[0, r, prows, vc_], cur_ref[0, r, rows, vc_]], axis=0)
        q_bd = jnp.concatenate([q * head_bf[h] for h in range(HEADS_PER_GROUP)], axis=0)
        s = lax.dot_general(q_bd, k, nt, preferred_element_type=F32) * LOG2E + bias_ref[...]
        if j == 0:
            s = jnp.where(jnp.logical_and(starts_sequence, key_is_prev), MASKED, s)
        m = jnp.max(s, axis=-1, keepdims=True)
        p = jnp.exp2(s - m)
        den = jnp.sum(p, axis=-1, keepdims=True)
        o = jnp.dot(p.astype(BF16), v, preferred_element_type=F32)
        den_h = jnp.broadcast_to(by_head(den), (BLK, GROUP_W))
        o_ref[0, r, rows, :] = (by_head(o) / den_h).astype(BF16)
        lse_ref[0, r, rows, :] = by_head(m) * LN2 + jnp.log(den_h)


ATT_SUBBLOCKS = 32


def _attention_group(a, bias, dilation, B, S):
    sd = S // dilation
    rb = min(ATT_SUBBLOCKS, sd // BLK)
    rg = min(ATT_SUBBLOCKS // rb, dilation)
    o, lse = pl.pallas_call(
        _attn_kernel,
        grid=(B, dilation // rg, sd // (rb * BLK)),
        in_specs=[pl.BlockSpec((1, rg, rb * BLK, QKV_G), lambda b, r, n: (b, r, n, 0)),
                  pl.BlockSpec((1, rg, BLK, QKV_G), lambda b, r, n: (b, r, jnp.maximum(n * rb - 1, 0), 0)),
                  _resident((HEADS_PER_GROUP * BLK, 2 * BLK))],
        out_specs=[pl.BlockSpec((1, rg, rb * BLK, GROUP_W), lambda b, r, n: (b, r, n, 0))] * 2,
        out_shape=[jax.ShapeDtypeStruct((B, dilation, sd, GROUP_W), BF16),
                   jax.ShapeDtypeStruct((B, dilation, sd, GROUP_W), F32)],
        compiler_params=_cparams("parallel", "parallel", "parallel"),
        name=f"attn_d{dilation}",
    )(a, a, bias)
    return o, lse


def _gelu(x):
    return x * (lax.erf(x * (2.0 ** -0.5)) + 1.0) * 0.5


def _token_major(src_ref, d, scr, slot, tm):
    if d == 1:
        return src_ref[0, 0].astype(F32)
    for r in range(d):
        piece = src_ref[0, r].astype(F32)
        scr[slot, pl.ds(r, tm // d, stride=d), :] = piece[:, :LANES]
        scr[slot + 1, pl.ds(r, tm // d, stride=d), :] = piece[:, LANES:]
    return jnp.concatenate([scr[slot], scr[slot + 1]], axis=1)


def _mix_kernel(x_ref, o1_ref, o2_ref, o3_ref, l1_ref, l2_ref, l3_ref, uv_ref, gl_ref,
                wa_hbm, wg_hbm, wo_hbm, wc_ref, bs_ref, lng_ref, lnb_ref, we_ref, h_ref, we_bf_ref, g_scr, t_scr,
                wa_ref, wg_ref, wo_ref, w_stage, w_sem):
    @pl.when(pl.program_id(0) == 0)
    def _():
        _load_weight_bf16(wa_hbm, wa_ref, w_stage, w_sem)
        _load_weight_bf16(wg_hbm, wg_ref, w_stage, w_sem)
        _load_weight_bf16(wo_hbm, wo_ref, w_stage, w_sem, scale=0.5)

    we_bf_ref[...] = we_ref[...].astype(BF16)

    tm = x_ref.shape[0]
    dils = [d for _, d in ATT_GROUPS]
    o1, o2, o3 = [_token_major(ref, d, t_scr, 4 * i, tm) for i, (ref, d) in enumerate(zip((o1_ref, o2_ref, o3_ref), dils))]
    l1, l2, l3 = [_token_major(ref, d, t_scr, 4 * i + 2, tm) for i, (ref, d) in enumerate(zip((l1_ref, l2_ref, l3_ref), dils))]
    lm = jnp.maximum(jnp.maximum(l1, l2), l3)
    e1, e2, e3 = jnp.exp(l1 - lm), jnp.exp(l2 - lm), jnp.exp(l3 - lm)
    att = (e1 * o1 + e2 * o2 + e3 * o3) / (e1 + e2 + e3)
    y_att = jnp.dot(att.astype(BF16), wa_ref[...], preferred_element_type=F32)

    zu = _gelu(uv_ref[:, :GMLP_W].astype(F32))
    zv = _gelu(uv_ref[:, GMLP_W:].astype(F32))
    mu = jnp.mean(zv, axis=-1, keepdims=True)
    var = jnp.mean(jnp.square(zv - mu), axis=-1, keepdims=True)
    vn = (zv - mu) * lax.rsqrt(var + EPS) * lng_ref[...] + lnb_ref[...]
    low_half = lax.broadcasted_iota(jnp.int32, (CHUNK, 2 * GMLP_GD), 1) < GMLP_GD
    for c in range(tm // CHUNK):
        rows = slice(c * CHUNK, (c + 1) * CHUNK)
        for s in range(GMLP_W // (2 * GMLP_GD)):
            cols = slice(s * 2 * GMLP_GD, (s + 1) * 2 * GMLP_GD)
            v2 = vn[rows, cols]
            rhs = jnp.concatenate([jnp.where(low_half, v2, 0.0), jnp.where(low_half, 0.0, v2)], axis=0).astype(BF16)
            mixed = jnp.dot(wc_ref[s], rhs, preferred_element_type=F32) + bs_ref[:, cols]
            g_scr[rows, cols] = (zu[rows, cols] * mixed).astype(BF16)
    y_gm = jnp.dot(g_scr[...], wg_ref[...], preferred_element_type=F32)

    gate_a = jnp.tanh(gl_ref[:, :D_MODEL].astype(F32)) + 1.0
    gate_g = jnp.tanh(gl_ref[:, D_MODEL:].astype(F32)) + 1.0
    merged = (gate_a * y_att + gate_g * y_gm).astype(BF16)
    h_ref[...] = x_ref[...] + jnp.dot(merged, wo_ref[...], preferred_element_type=F32)


def _mix(x2, outs, lses, uv, gl, wa, wg, wo, wc2, bs, lng, lnb, w_expert, S):
    T = x2.shape[0]
    tm = TM_PROJ
    row = lambda w: pl.BlockSpec((tm, w), lambda i: (i, 0))
    att = [_plane_spec(d, tm, S // tm, GROUP_W) for _, d in ATT_GROUPS]
    we_spec = _expert_slice_spec(T // tm, w_expert.shape[2])
    return pl.pallas_call(
        _mix_kernel,
        grid=(T // tm,),
        in_specs=[row(D_MODEL)] + att + att + [row(2 * GMLP_W), row(N_BRANCH * D_MODEL)]
                 + [pl.BlockSpec(memory_space=pl.ANY)] * 3
                 + [_resident(wc2.shape), _resident(bs.shape), _resident(lng.shape), _resident(lnb.shape), we_spec],
        out_specs=[row(D_MODEL), we_spec],
        out_shape=[jax.ShapeDtypeStruct((T, D_MODEL), F32), jax.ShapeDtypeStruct(w_expert.shape, BF16)],
        scratch_shapes=[pltpu.VMEM((tm, GMLP_W), BF16), pltpu.VMEM((4 * N_DIL, tm, LANES), F32),
                        pltpu.VMEM(wa.shape, BF16), pltpu.VMEM(wg.shape, BF16), pltpu.VMEM(wo.shape, BF16)]
                       + _weight_stage(D_MODEL),
        compiler_params=_cparams("arbitrary"),
        name="mix",
    )(x2, *outs, *lses, uv, gl, wa, wg, wo, wc2, bs, lng, lnb, w_expert)


def _router_kernel(h_ref, g_ref, wr_ref, br_ref, upper_ref, eidx_ref, gate_ref, rank_ref, cnt_ref, carry):
    tm = h_ref.shape[0]

    @pl.when(pl.program_id(0) == 0)
    def _():
        carry[...] = jnp.zeros_like(carry)

    hn = _rms(h_ref[...], g_ref[...])
    hi = hn.astype(BF16)
    lo = (hn - hi.astype(F32)).astype(BF16)
    nt = (((1,), (1,)), ((), ()))
    by_hi = lax.dot_general(wr_ref[...], hi, nt, preferred_element_type=F32)
    by_lo = lax.dot_general(wr_ref[:N_EXPERTS, :], lo, nt, preferred_element_type=F32)
    logits = by_hi[:N_EXPERTS] + by_hi[N_EXPERTS:] + by_lo + br_ref[...]
    expert = lax.broadcasted_iota(jnp.int32, (N_EXPERTS, tm), 0)
    vals, hots = [], []
    l = logits
    for k in range(TOP_K):
        m = jnp.max(l, axis=0, keepdims=True)
        idx = jnp.min(jnp.where(l == m, expert, N_EXPERTS), axis=0, keepdims=True)
        hot = expert == idx
        eidx_ref[0, k:k + 1, :] = idx
        vals.append(m)
        hots.append(hot)
        l = jnp.where(hot, -jnp.inf, l)
    ex = [jnp.exp(v - vals[0]) for v in vals]
    tot = ex[0] + ex[1] + ex[2] + ex[3]
    for k in range(TOP_K):
        gate_ref[0, k:k + 1, :] = ex[k] / tot
    multi = jnp.zeros((N_EXPERTS, tm), F32)
    for hot in hots:
        multi = multi + hot.astype(F32)
    before = jnp.dot(multi.astype(BF16), upper_ref[...], preferred_element_type=F32) + carry[...]
    for k in range(TOP_K):
        rank_ref[0, k:k + 1, :] = jnp.sum(jnp.where(hots[k], before, 0.0), axis=0, keepdims=True).astype(jnp.int32)
    carry[...] += jnp.sum(multi, axis=1, keepdims=True)
    cnt_ref[...] = carry[...]


def _router(h1, g, wr_t, br_col):
    T = h1.shape[0]
    tm = TM_ROUTER
    upper = jnp.asarray(np.triu(np.ones((tm, tm), np.float32), k=1), BF16)
    k_rows = pl.BlockSpec((1, TOP_K, tm), lambda i: (i, 0, 0))
    k_shape = lambda dt: jax.ShapeDtypeStruct((T // tm, TOP_K, tm), dt)
    return pl.pallas_call(
        _router_kernel,
        grid=(T // tm,),
        in_specs=[pl.BlockSpec((tm, D_MODEL), lambda i: (i, 0)), _resident((1, D_MODEL)),
                  _resident((2 * N_EXPERTS, D_MODEL)), _resident((N_EXPERTS, 1)), _resident((tm, tm))],
        out_specs=[k_rows, k_rows, k_rows, pl.BlockSpec((N_EXPERTS, 1), lambda i: (0, 0))],
        out_shape=[k_shape(jnp.int32), k_shape(F32), k_shape(jnp.int32),
                   jax.ShapeDtypeStruct((N_EXPERTS, 1), F32)],
        scratch_shapes=[pltpu.VMEM((N_EXPERTS, 1), F32)],
        compiler_params=_cparams("arbitrary"),
        name="router",
    )(h1, g, wr_t, br_col, upper)


def _to_row_tiles(ref, lead, value):
    n = value.shape[0]
    for c in range(ROW_SUBLANES):
        ref[(*lead, pl.ds(c, n, stride=ROW_SUBLANES), slice(None))] = value[:, c * LANES:(c + 1) * LANES]


def _from_row_tiles(ref, lead, first, n):
    return jnp.concatenate(
        [ref[(*lead, pl.ds(first * ROW_SUBLANES + c, n, stride=ROW_SUBLANES), slice(None))] for c in range(ROW_SUBLANES)],
        axis=1)


def _tile_rows(idx, n=1):
    return pl.ds(pl.multiple_of(idx * ROW_SUBLANES, ROW_SUBLANES), n * ROW_SUBLANES)


def _dispatch_kernel(last_ref, nv_ref, dest_ref, h_ref, g_ref, xs_ref, buf, sem, zero_sem):
    tm = h_ref.shape[0]
    n_blocks = xs_ref.shape[0] // (TM_EXP * ROW_SUBLANES)
    i = pl.program_id(0)
    slot = lax.rem(i, 2)

    @pl.when(i == 0)
    def _():
        buf[1] = jnp.zeros(buf.shape[1:], F32)

        def zero_block(b):
            for part in range(TM_EXP // tm):
                pltpu.make_async_copy(buf.at[1], xs_ref.at[_tile_rows(b * TM_EXP + part * tm, tm)], zero_sem).start()

        def zero_done():
            for part in range(TM_EXP // tm):
                pltpu.make_async_copy(buf.at[1], xs_ref.at[_tile_rows(0, tm)], zero_sem).wait()

        for e in range(N_EXPERTS):
            zero_block(last_ref[e])
        lax.fori_loop(nv_ref[0], n_blocks, lambda b, c: (zero_block(b), c)[1], 0)
        for e in range(N_EXPERTS):
            zero_done()
        lax.fori_loop(nv_ref[0], n_blocks, lambda b, c: (zero_done(), c)[1], 0)

    _to_row_tiles(buf, (slot,), _rms(h_ref[...], g_ref[...]))

    def issue(t, carry):
        for k in range(TOP_K):
            d = dest_ref[0, 0, k * tm + t]
            pltpu.make_async_copy(buf.at[slot, _tile_rows(t)], xs_ref.at[_tile_rows(d)],
                                  sem.at[slot]).start(priority=k % 2)
        return carry

    lax.fori_loop(0, tm, issue, 0, unroll=8)

    def wait_slot(s):
        for _ in range(TOP_K):
            pltpu.make_async_copy(buf.at[s], xs_ref.at[_tile_rows(0, tm)], sem.at[s]).wait()

    @pl.when(i > 0)
    def _():
        wait_slot(1 - slot)

    @pl.when(i == pl.num_programs(0) - 1)
    def _():
        wait_slot(slot)


def _dispatch(last_block, n_valid, dest3, h1, g, n_slots):
    T = h1.shape[0]
    tm = TM_TOK
    grid_spec = pltpu.PrefetchScalarGridSpec(
        num_scalar_prefetch=2,
        grid=(T // tm,),
        in_specs=[pl.BlockSpec((1, 1, TOP_K * tm), lambda i, lb, nv: (i, 0, 0), memory_space=pltpu.SMEM),
                  pl.BlockSpec((tm, D_MODEL), lambda i, lb, nv: (i, 0)),
                  pl.BlockSpec((1, D_MODEL), lambda i, lb, nv: (0, 0), pipeline_mode=pl.Buffered(1))],
        out_specs=pl.BlockSpec(memory_space=pl.ANY),
        scratch_shapes=[pltpu.VMEM((2, tm * ROW_SUBLANES, LANES), F32), pltpu.SemaphoreType.DMA((2,)),
                        pltpu.SemaphoreType.DMA(())],
    )
    return pl.pallas_call(
        _dispatch_kernel,
        grid_spec=grid_spec,
        out_shape=jax.ShapeDtypeStruct((n_slots * ROW_SUBLANES, LANES), F32),
        compiler_params=_cparams("arbitrary"),
        name="dispatch",
    )(last_block, n_valid, dest3, h1, g)


def _experts_kernel(be_ref, nv_ref, xs_hbm, wgu_ref, bgu_ref, wd_ref, bd_ref, ys_hbm, xbuf, ybuf, xsem, ysem):
    del be_ref
    tm = xbuf.shape[1]
    b = pl.program_id(0)
    n_valid = nv_ref[0]
    slot = lax.rem(b, 2)

    def x_copies(blk, s):
        return [pltpu.make_async_copy(xs_hbm.at[pl.ds(blk * tm, tm), c, :],
                                      xbuf.at[s, :, pl.ds(c * LANES, LANES)], xsem.at[s]) for c in range(ROW_SUBLANES)]

    def y_copies(blk, s):
        return [pltpu.make_async_copy(ybuf.at[s, :, pl.ds(c * LANES, LANES)],
                                      ys_hbm.at[pl.ds(blk * tm, tm), c, :], ysem.at[s]) for c in range(ROW_SUBLANES)]

    @pl.when(jnp.logical_and(b == 0, n_valid > 0))
    def _():
        for cp in x_copies(0, 0):
            cp.start()

    @pl.when(b + 1 < n_valid)
    def _():
        for cp in x_copies(b + 1, 1 - slot):
            cp.start()

    @pl.when(b < n_valid)
    def _():
        for cp in x_copies(b, slot):
            cp.wait()
        x = xbuf[slot].astype(BF16)
        gu = jnp.dot(x, wgu_ref[0], preferred_element_type=F32) + bgu_ref[0]
        glu = jnp.minimum(gu[:, :D_EXPERT], SWIGLU_LIMIT)
        lin = jnp.clip(gu[:, D_EXPERT:], -SWIGLU_LIMIT, SWIGLU_LIMIT)
        act = glu * jax.nn.sigmoid(SWIGLU_ALPHA * glu) * (lin + 1.0)
        ybuf[slot] = jnp.dot(act.astype(BF16), wd_ref[0], preferred_element_type=F32) + bd_ref[0]

    @pl.when(b >= n_valid)
    def _():
        ybuf[slot] = jnp.zeros(ybuf.shape[1:], F32)

    for cp in y_copies(b, slot):
        cp.start()

    @pl.when(b > 0)
    def _():
        for cp in y_copies(b - 1, 1 - slot):
            cp.wait()

    @pl.when(b == pl.num_programs(0) - 1)
    def _():
        for cp in y_copies(b, slot):
            cp.wait()


def _experts(block_expert, n_valid, xs, wgu, bgu, wd, bd):
    tm = TM_EXP
    n_slots = xs.shape[0] // ROW_SUBLANES
    n_blocks = n_slots // tm
    any_space = pl.BlockSpec(memory_space=pl.ANY)
    grid_spec = pltpu.PrefetchScalarGridSpec(
        num_scalar_prefetch=2,
        grid=(n_blocks,),
        in_specs=[any_space,
                  pl.BlockSpec((1, D_MODEL, 2 * D_EXPERT), lambda b, be, nv: (be[b], 0, 0)),
                  pl.BlockSpec((1, 1, 2 * D_EXPERT), lambda b, be, nv: (be[b], 0, 0)),
                  pl.BlockSpec((1, D_EXPERT, D_MODEL), lambda b, be, nv: (be[b], 0, 0)),
                  pl.BlockSpec((1, 1, D_MODEL), lambda b, be, nv: (be[b], 0, 0))],
        out_specs=any_space,
        scratch_shapes=[pltpu.VMEM((2, tm, D_MODEL), F32), pltpu.VMEM((2, tm, D_MODEL), F32),
                        pltpu.SemaphoreType.DMA((2,)), pltpu.SemaphoreType.DMA((2,))],
    )
    ys = pl.pallas_call(
        _experts_kernel,
        grid_spec=grid_spec,
        out_shape=jax.ShapeDtypeStruct((n_slots, ROW_SUBLANES, LANES), F32),
        compiler_params=_cparams("arbitrary"),
        name="experts",
    )(block_expert, n_valid, xs.reshape(n_slots, ROW_SUBLANES, LANES), wgu, bgu, wd, bd)
    return ys.reshape(xs.shape)


GATHER_SLOTS = 3


def _combine_kernel(*refs, n_tiles):
    ahead = GATHER_SLOTS - 1
    prime_refs, refs = refs[:ahead], refs[ahead:]
    (ahead_dest_ref, h_ref, gate_ref, p_ref, gp_ref, gf_ref, wpg_hbm, wpp_hbm, ys_ref, o_ref), refs = refs[:10], refs[10:]
    bufs, (sem, wpg_ref, wpp_ref, w_stage, w_sem) = refs[:GATHER_SLOTS], refs[GATHER_SLOTS:]
    tm = h_ref.shape[0]
    i = pl.program_id(0)

    def row_copy(dref, t, k, s):
        d = dref[0, 0, k * tm + t]
        return pltpu.make_async_copy(ys_ref.at[_tile_rows(d)], bufs[s].at[_tile_rows(k * tm + t)], sem.at[s])

    def wait_slot(s):
        for _ in range(TOP_K):
            pltpu.make_async_copy(ys_ref.at[_tile_rows(0, tm)], bufs[s].at[_tile_rows(0, tm)], sem.at[s]).wait()

    @pl.when(i == 0)
    def _():
        _load_weight_bf16(wpg_hbm, wpg_ref, w_stage, w_sem, scale=0.5)
        _load_weight_bf16(wpp_hbm, wpp_ref, w_stage, w_sem, scale=0.5)

        for s, dref in enumerate(prime_refs):
            def issue(t, carry, s=s, dref=dref):
                for k in range(TOP_K):
                    row_copy(dref, t, k, s).start(priority=k % 2)
                return carry
            lax.fori_loop(0, tm, issue, 0, unroll=8)

    def step(s):
        wait_slot(s)
        for t in range(tm):
            for k in range(TOP_K):
                row_copy(ahead_dest_ref, t, k, (s + ahead) % GATHER_SLOTS).start(priority=k % 2)
        proj = jnp.dot(p_ref[...].astype(BF16), wpp_ref[...], preferred_element_type=F32)
        h = h_ref[...]
        for k in range(TOP_K):
            h = h + gate_ref[:, k:k + 1] * _from_row_tiles(bufs[s], (), k * tm, tm)
        ple_gate = jnp.tanh(jnp.dot(_rms(h, gp_ref[...]).astype(BF16), wpg_ref[...], preferred_element_type=F32)) + 1.0
        h = h + ple_gate * proj
        o_ref[...] = _rms(h, gf_ref[...])

    for s in range(GATHER_SLOTS):
        pl.when(lax.rem(i, GATHER_SLOTS) == s)(functools.partial(step, s))

    @pl.when(i == n_tiles - 1)
    def _():
        for extra in range(ahead):
            wait_slot((n_tiles + extra) % GATHER_SLOTS)


def _combine(dest3, h1, gate, p2, gp, wpg, wpp, gf, ys):
    T = h1.shape[0]
    tm = TM_TOK
    n_tiles = T // tm
    row = lambda w: pl.BlockSpec((tm, w), lambda i: (i, 0))
    dest_spec = lambda ahead: pl.BlockSpec((1, 1, TOP_K * tm), lambda i: (jnp.minimum(i + ahead, n_tiles - 1), 0, 0),
                                           memory_space=pltpu.SMEM)
    gather_buf = pltpu.VMEM((TOP_K * tm * ROW_SUBLANES, LANES), F32)
    ahead = GATHER_SLOTS - 1
    assert n_tiles > ahead
    return pl.pallas_call(
        functools.partial(_combine_kernel, n_tiles=n_tiles),
        grid=(n_tiles,),
        in_specs=[dest_spec(a) for a in range(ahead)] + [dest_spec(ahead),
                  row(D_MODEL), row(TOP_K), row(PLE_DIM), _resident((1, D_MODEL)), _resident((1, D_MODEL))]
                 + [pl.BlockSpec(memory_space=pl.ANY)] * 3,
        out_specs=row(D_MODEL),
        out_shape=jax.ShapeDtypeStruct((T, D_MODEL), F32),
        scratch_shapes=[gather_buf] * GATHER_SLOTS + [pltpu.SemaphoreType.DMA((GATHER_SLOTS,)),
                        pltpu.VMEM(wpg.shape, BF16), pltpu.VMEM(wpp.shape, BF16)] + _weight_stage(D_MODEL),
        compiler_params=_cparams("arbitrary"),
        name="combine",
    )(*([dest3] * (ahead + 1)), h1, gate, p2, gp, gf, wpg, wpp, ys)


def _layer(h, p_i, g_mix, w_in, rel_bias, w_att_out, ln_v_g, ln_v_b, w_spatial, b_spatial, w_gmlp_out, w_out,
           g_moe, w_router, b_router, w_gate_up, b_gate_up, w_down, b_down, g_ple, w_ple_gate, w_ple_proj,
           g_final, B, S):
    T = B * S
    row = lambda v: v.reshape(1, -1).astype(F32)

    assert S % TM_PROJ == 0
    *att_in, uv, gl, wgu_bf = _in_proj(h, row(g_mix), w_in.astype(F32), w_gate_up.astype(F32), B, S)

    outs, lses = [], []
    for g, (window, dilation) in enumerate(ATT_GROUPS):
        assert window // dilation == BLK and S % (dilation * BLK) == 0
        bias = _bias_table(rel_bias[:, g * HEADS_PER_GROUP:(g + 1) * HEADS_PER_GROUP], dilation)
        o, lse = _attention_group(att_in[g], bias, dilation, B, S)
        outs.append(o)
        lses.append(lse)

    causal = jnp.asarray(np.tril(np.ones((CHUNK, CHUNK), np.float32)))
    w_c = (w_spatial.astype(F32) * causal[None]).astype(BF16)
    wc2 = jnp.concatenate([w_c[0::2], w_c[1::2]], axis=2)
    bs = jnp.repeat(b_spatial.astype(F32).T, GMLP_GD, axis=1)
    h1, wd_bf = _mix(h, outs, lses, uv, gl, w_att_out.astype(F32), w_gmlp_out.astype(F32), w_out.astype(F32),
                     wc2, bs, row(ln_v_g), row(ln_v_b), w_down.astype(F32), S)

    wr_hi = w_router.astype(BF16)
    wr_lo = (w_router.astype(F32) - wr_hi.astype(F32)).astype(BF16)
    eidx, gate, rank, counts = _router(h1, row(g_moe), jnp.concatenate([wr_hi, wr_lo], axis=1).T,
                                       b_router.reshape(-1, 1).astype(F32))
    cnt = counts[:, 0].astype(jnp.int32)
    blk_counts = (cnt + TM_EXP - 1) // TM_EXP
    blk_end = jnp.cumsum(blk_counts)
    pad_start = (blk_end - blk_counts) * TM_EXP
    n_blocks = T * TOP_K // TM_EXP + N_EXPERTS
    n_valid = blk_end[-1:].astype(jnp.int32)
    blk = jnp.minimum(jnp.arange(n_blocks, dtype=jnp.int32), n_valid[0] - 1)
    block_expert = jnp.minimum(jnp.sum((blk_end[None, :] <= blk[:, None]).astype(jnp.int32), axis=1), N_EXPERTS - 1)
    expert_ids = jnp.arange(N_EXPERTS, dtype=jnp.int32)
    dest = rank + jnp.sum(jnp.where(eidx[..., None] == expert_ids, pad_start, 0), axis=-1)
    split = TM_ROUTER // TM_TOK
    dest3 = jnp.transpose(dest.reshape(T // TM_ROUTER, TOP_K, split, TM_TOK), (0, 2, 1, 3)).reshape(
        T // TM_TOK, 1, TOP_K * TM_TOK)
    gate = jnp.transpose(gate, (0, 2, 1)).reshape(T, TOP_K)

    last_block = jnp.maximum(blk_end - 1, 0).astype(jnp.int32)
    xs = _dispatch(last_block, n_valid, dest3, h1, row(g_moe), n_blocks * TM_EXP)
    ys = _experts(block_expert, n_valid, xs, wgu_bf, b_gate_up.reshape(N_EXPERTS, 1, -1).astype(F32),
                  wd_bf, b_down.reshape(N_EXPERTS, 1, -1).astype(F32))
    return _combine(dest3, h1, gate, p_i, row(g_ple), w_ple_gate.astype(F32), w_ple_proj.astype(F32),
                    row(g_final), ys)


def kernel(x, p, g_mix, w_in, rel_bias, w_att_out, ln_v_g, ln_v_b, w_spatial, b_spatial, w_gmlp_out, w_out, g_moe, w_router, b_router, w_gate_up, b_gate_up, w_down, b_down, g_ple, w_ple_gate, w_ple_proj, g_final):
    B, S, D = x.shape
    depth = p.shape[0]
    assert depth == 1, "the final RMSNorm is fused into the (single) layer's last kernel"
    out = _layer(x.reshape(B * S, D), p[0].reshape(B * S, PLE_DIM), g_mix[0], w_in[0], rel_bias, w_att_out[0],
                 ln_v_g[0], ln_v_b[0], w_spatial[0], b_spatial[0], w_gmlp_out[0], w_out[0], g_moe[0], w_router[0],
                 b_router[0], w_gate_up[0], b_gate_up[0], w_down[0], b_down[0], g_ple[0], w_ple_gate[0],
                 w_ple_proj[0], g_final, B, S)
    return out.reshape(B, S, D)
```
